```python
import math
import jax, jax.numpy as jnp
from jax import lax
import numpy as np

D_MODEL = 1024
BATCH = 16
SEQ = 2048
DEPTH = 1

MEM_LEN = 256
EPS = 1e-6
GMLP_GROUPS = 6
GMLP_GROUP_DIM = 128
GMLP_WIDTH = GMLP_GROUPS * GMLP_GROUP_DIM
CHUNK = 128
MOBA_HEADS = 12
HEAD_DIM = 64
MOBA_WIDTH = MOBA_HEADS * HEAD_DIM
MOBA_BLOCK = 256
MOBA_TOPK = 3
Q_CHUNK = 64
REL_BUCKETS = 32
REL_MAX_DIST = 128
MEM_HEADS = 4
MEM_HEAD_DIM = 128
MEM_WIDTH = MEM_HEADS * MEM_HEAD_DIM
N_BRANCHES = 3
D_FF = -(-8 * D_MODEL // (3 * 256)) * 256
IN_WIDTH = 2 * GMLP_WIDTH + 3 * MOBA_WIDTH + MEM_WIDTH + N_BRANCHES * D_MODEL
NEG = -1e30

kernel_name = "hybrid_gmlp_moba_memxattn_block"


def rms_norm(x, g):
    xf = x.astype(jnp.float32)
    y = xf * lax.rsqrt(jnp.mean(xf * xf, axis=-1, keepdims=True) + EPS)
    return (y * g.astype(jnp.float32)).astype(x.dtype)


def layer_norm(x, g, b):
    xf = x.astype(jnp.float32)
    mu = jnp.mean(xf, axis=-1, keepdims=True)
    xc = xf - mu
    y = xc * lax.rsqrt(jnp.mean(xc * xc, axis=-1, keepdims=True) + EPS)
    return (y * g.astype(jnp.float32) + b.astype(jnp.float32)).astype(x.dtype)


def t5_bucket(dist):
    max_exact = REL_BUCKETS // 2
    n = jnp.maximum(dist, 0)
    nf = jnp.maximum(n, 1).astype(jnp.float32)
    large = max_exact + (jnp.log(nf / max_exact) / math.log(REL_MAX_DIST / max_exact)
                         * (REL_BUCKETS - max_exact)).astype(jnp.int32)
    large = jnp.minimum(large, REL_BUCKETS - 1)
    return jnp.where(n < max_exact, n, large)


def gmlp_branch(u, v, ln_g, ln_b, w_s, b_s):
    B, S, _ = v.shape
    u = jax.nn.gelu(u)
    v = layer_norm(jax.nn.gelu(v), ln_g, ln_b)
    vc = v.reshape(B, S // CHUNK, CHUNK, GMLP_GROUPS, GMLP_GROUP_DIM)
    causal = jnp.tril(jnp.ones((CHUNK, CHUNK), dtype=bool))
    ws = jnp.where(causal[None], w_s, jnp.zeros_like(w_s))
    mixed = jnp.einsum('gts,bcsgd->bctgd', ws, vc) + b_s.T[None, None, :, :, None]
    return u * mixed.reshape(B, S, GMLP_WIDTH)


def moba_attention(q, k, v, rel_bias):
    B, H, S, d = q.shape
    n_blk = -(-S // MOBA_BLOCK)
    pad = n_blk * MOBA_BLOCK - S
    kp = jnp.pad(k, ((0, 0), (0, 0), (0, pad), (0, 0)))
    vp = jnp.pad(v, ((0, 0), (0, 0), (0, pad), (0, 0)))
    kb = kp.reshape(B, H, n_blk, MOBA_BLOCK, d)
    vb = vp.reshape(B, H, n_blk, MOBA_BLOCK, d)
    kmean = jnp.mean(kb.astype(jnp.float32), axis=3)
    gate = jnp.einsum('bhsd,bhnd->bhsn', q.astype(jnp.float32), kmean)
    cur = jnp.arange(S) // MOBA_BLOCK
    past = jnp.arange(n_blk)[None, :] < cur[:, None]
    gate = jnp.where(past[None, None], gate, NEG)
    k_sel = min(MOBA_TOPK, n_blk)
    _, idx = lax.top_k(gate, k_sel)

    n_qc = S // Q_CHUNK
    qs = q.reshape(B, H, n_qc, Q_CHUNK, d).transpose(0, 2, 1, 3, 4).reshape(B * n_qc, H, Q_CHUNK, d)
    ids = idx.reshape(B, H, n_qc, Q_CHUNK, k_sel).transpose(0, 2, 1, 3, 4).reshape(B * n_qc, H, Q_CHUNK, k_sel)
    bids = jnp.repeat(jnp.arange(B), n_qc)
    cids = jnp.tile(jnp.arange(n_qc), B)
    bias_h = rel_bias.T.astype(jnp.float32)
    h_ix = jnp.arange(H)
    blk_ar = jnp.arange(MOBA_BLOCK)

    def one_chunk(args):
        qc, ic, b, c = args
        kb_b = kb[b]
        vb_b = vb[b]
        t = c * Q_CHUNK + jnp.arange(Q_CHUNK)
        own = (c * Q_CHUNK) // MOBA_BLOCK
        k_g = kb_b[h_ix[:, None, None], ic]
        v_g = vb_b[h_ix[:, None, None], ic]
        key_pos = ic[..., None] * MOBA_BLOCK + blk_ar
        s_sel = jnp.einsum('hqd,hqknd->hqkn', qc, k_g).astype(jnp.float32)
        s_sel = s_sel + bias_h[h_ix[:, None, None, None], t5_bucket(t[None, :, None, None] - key_pos)]
        valid = ic < own
        s_sel = jnp.where(valid[..., None], s_sel, NEG)
        k_o = lax.dynamic_index_in_dim(kb_b, own, axis=1, keepdims=False)
        v_o = lax.dynamic_index_in_dim(vb_b, own, axis=1, keepdims=False)
        dist_o = t[:, None] - (own * MOBA_BLOCK + blk_ar)[None, :]
        s_own = jnp.einsum('hqd,hnd->hqn', qc, k_o).astype(jnp.float32) + bias_h[:, t5_bucket(dist_o)]
        s_own = jnp.where((dist_o >= 0)[None], s_own, NEG)
        logits = jnp.concatenate([s_sel.reshape(H, Q_CHUNK, k_sel * MOBA_BLOCK), s_own], axis=-1)
        p = jax.nn.softmax(logits, axis=-1)
        p_sel = p[..., :k_sel * MOBA_BLOCK].reshape(H, Q_CHUNK, k_sel, MOBA_BLOCK).astype(v.dtype)
        p_own = p[..., k_sel * MOBA_BLOCK:].astype(v.dtype)
        return jnp.einsum('hqkn,hqknd->hqd', p_sel, v_g) + jnp.einsum('hqn,hnd->hqd', p_own, v_o)

    out = lax.map(one_chunk, (qs, ids, bids, cids))
    return out.reshape(B, n_qc, H, Q_CHUNK, d).transpose(0, 2, 1, 3, 4).reshape(B, H, S, d)


def mem_attention(q, mem_n, w_kv):
    B, M, _ = mem_n.shape
    kv = mem_n @ w_kv
    k, v = jnp.split(kv, 2, axis=-1)
    k = k.reshape(B, M, MEM_HEADS, MEM_HEAD_DIM)
    v = v.reshape(B, M, MEM_HEADS, MEM_HEAD_DIM)
    s = jnp.einsum('bshd,bmhd->bhsm', q, k).astype(jnp.float32) * (MEM_HEAD_DIM ** -0.5)
    p = jax.nn.softmax(s, axis=-1).astype(v.dtype)
    return jnp.einsum('bhsm,bmhd->bshd', p, v)


def setup_inputs(seed: int = 0) -> dict:
    key = jax.random.key(seed)
    ks = jax.random.split(key, 24)
    f32 = jnp.float32

    def nrm(k, shape, scale):
        return jax.random.normal(k, shape, f32) * scale

    def gain(k, shape):
        return 1.0 + 0.05 * jax.random.normal(k, shape, f32)

    L, D = DEPTH, D_MODEL
    return {
        "x": nrm(ks[0], (BATCH, SEQ, D), 1.0),
        "mem": nrm(ks[1], (BATCH, MEM_LEN, D), 1.0),
        "ln_mix_pre": gain(ks[2], (L, D)),
        "ln_mix_post": gain(ks[3], (L, D)),
        "ln_ffn_pre": gain(ks[4], (L, D)),
        "ln_ffn_post": gain(ks[5], (L, D)),
        "ln_mem": gain(ks[6], (L, D)),
        "w_in": nrm(ks[7], (L, D, IN_WIDTH), D ** -0.5),
        "ln_v_gain": gain(ks[8], (L, GMLP_WIDTH)),
        "ln_v_bias": nrm(ks[9], (L, GMLP_WIDTH), 0.05),
        "w_spatial": nrm(ks[10], (L, GMLP_GROUPS, CHUNK, CHUNK), CHUNK ** -0.5),
        "b_spatial": 1.0 + nrm(ks[11], (L, GMLP_GROUPS, CHUNK), 0.1),
        "rel_bias": nrm(ks[12], (REL_BUCKETS, MOBA_HEADS), 0.5),
        "w_mem_kv": nrm(ks[13], (L, D, 2 * MEM_WIDTH), D ** -0.5),
        "w_branch_a": nrm(ks[14], (L, GMLP_WIDTH, D), GMLP_WIDTH ** -0.5),
        "w_branch_b": nrm(ks[15], (L, MOBA_WIDTH, D), MOBA_WIDTH ** -0.5),
        "w_branch_c": nrm(ks[16], (L, MEM_WIDTH, D), MEM_WIDTH ** -0.5),
        "w_out": nrm(ks[17], (L, D, D), D ** -0.5),
        "w_ffn_gate": nrm(ks[18], (L, D, D_FF), D ** -0.5),
        "w_ffn_up": nrm(ks[19], (L, D, D_FF), D ** -0.5),
        "w_ffn_down": nrm(ks[20], (L, D_FF, D), D_FF ** -0.5),
    }


def reference(x, mem, ln_mix_pre, ln_mix_post, ln_ffn_pre, ln_ffn_post, ln_mem, w_in,
              ln_v_gain, ln_v_bias, w_spatial, b_spatial, rel_bias, w_mem_kv,
              w_branch_a, w_branch_b, w_branch_c, w_out, w_ffn_gate, w_ffn_up, w_ffn_down):
    B, S, D = x.shape
    split_at = np.cumsum([GMLP_WIDTH, GMLP_WIDTH, MOBA_WIDTH, MOBA_WIDTH, MOBA_WIDTH, MEM_WIDTH]).tolist()
    for l in range(DEPTH):
        h = rms_norm(x, ln_mix_pre[l])
        proj = h @ w_in[l]
        a_u, a_v, b_q, b_k, b_v, c_q, g_logit = jnp.split(proj, split_at, axis=-1)
        a_out = gmlp_branch(a_u, a_v, ln_v_gain[l], ln_v_bias[l], w_spatial[l], b_spatial[l])
        to_heads = lambda t: t.reshape(B, S, MOBA_HEADS, HEAD_DIM).transpose(0, 2, 1, 3)
        b_att = moba_attention(to_heads(b_q) * (HEAD_DIM ** -0.5), to_heads(b_k), to_heads(b_v), rel_bias)
        b_out = b_att.transpose(0, 2, 1, 3).reshape(B, S, MOBA_WIDTH)
        mem_n = rms_norm(mem, ln_mem[l])
        c_out = mem_attention(c_q.reshape(B, S, MEM_HEADS, MEM_HEAD_DIM), mem_n, w_mem_kv[l]).reshape(B, S, MEM_WIDTH)
        gates = jax.nn.sigmoid(g_logit).reshape(B, S, N_BRANCHES, D)
        merged = (gates[:, :, 0] * (a_out @ w_branch_a[l])
                  + gates[:, :, 1] * (b_out @ w_branch_b[l])
                  + gates[:, :, 2] * (c_out @ w_branch_c[l]))
        x = x + rms_norm(merged @ w_out[l], ln_mix_post[l])
        h2 = rms_norm(x, ln_ffn_pre[l])
        f = (jax.nn.silu(h2 @ w_ffn_gate[l]) * (h2 @ w_ffn_up[l])) @ w_ffn_down[l]
        x = x + rms_norm(f, ln_ffn_post[l])
    return x
```

```python
import functools
import math

import jax
import jax.numpy as jnp
from jax import lax
from jax.experimental import pallas as pl
from jax.experimental.pallas import tpu as pltpu

F32 = jnp.float32
BF16 = jnp.bfloat16

EPS = 1e-6
NEG = -1e30
GMLP_GROUPS = 6
GMLP_GROUP_DIM = 128
GMLP_WIDTH = GMLP_GROUPS * GMLP_GROUP_DIM
CHUNK = 128
MOBA_HEADS = 12
HEAD_DIM = 64
MOBA_WIDTH = MOBA_HEADS * HEAD_DIM
MOBA_BLOCK = 256
MOBA_TOPK = 3
REL_BUCKETS = 32
REL_MAX_DIST = 128
MEM_HEADS = 4
MEM_HEAD_DIM = 128
MEM_WIDTH = MEM_HEADS * MEM_HEAD_DIM
N_BRANCHES = 3

V7X_LANES = 128
V7X_VMEM_LIMIT = 56 * 1024 * 1024

TOKEN_TILE = 512


def _params(n_axes, vmem=V7X_VMEM_LIMIT):
    return pltpu.CompilerParams(
        dimension_semantics=("arbitrary",) * n_axes, vmem_limit_bytes=vmem)


def _resident(shape):
    zeros = (0,) * len(shape)
    return pl.BlockSpec(shape, lambda *_: zeros, pipeline_mode=pl.Buffered(1))


def _rms(x, g):
    return x * lax.rsqrt(jnp.mean(x * x, axis=-1, keepdims=True) + EPS) * g


def _prep_kernel(x_ref, g_ref, h_ref, hmean_ref):
    h = _rms(x_ref[...], g_ref[...])
    h_ref[...] = h.astype(BF16)
    hmean_ref[0] = jnp.mean(h, axis=0, keepdims=True)


def _prep(x2, g):
    n_tok, d = x2.shape
    n_blk = n_tok // MOBA_BLOCK
    return pl.pallas_call(
        _prep_kernel,
        grid=(n_blk,),
        in_specs=[pl.BlockSpec((MOBA_BLOCK, d), lambda i: (i, 0)), _resident((1, d))],
        out_specs=[pl.BlockSpec((MOBA_BLOCK, d), lambda i: (i, 0)),
                   pl.BlockSpec((1, 1, d), lambda i: (i, 0, 0))],
        out_shape=[jax.ShapeDtypeStruct((n_tok, d), BF16),
                   jax.ShapeDtypeStruct((n_blk, 1, d), F32)],
        compiler_params=_params(1),
        name="prep",
    )(x2, g)


def _gatew_kernel(hm_ref, wk_ref, wqt_ref, gt_ref):
    kmean = jnp.dot(hm_ref[...], wk_ref[...], precision=lax.Precision.HIGHEST,
                    preferred_element_type=F32)
    for h in range(MOBA_HEADS):
        sl = slice(h * HEAD_DIM, (h + 1) * HEAD_DIM)
        gt_ref[h] = jnp.dot(kmean[:, sl], wqt_ref[sl, :], precision=lax.Precision.HIGHEST,
                            preferred_element_type=F32)


def _gatew(hmean, w_k, w_qt):
    rows, d = hmean.shape
    return pl.pallas_call(
        _gatew_kernel,
        grid=(1,),
        in_specs=[_resident((rows, d)), _resident(w_k.shape), _resident(w_qt.shape)],
        out_specs=pl.BlockSpec((MOBA_HEADS, rows, d), lambda i: (0, 0, 0)),
        out_shape=jax.ShapeDtypeStruct((MOBA_HEADS, rows, d), F32),
        compiler_params=_params(1),
        name="gatew",
    )(hmean, w_k, w_qt)


def _select_kernel(x_ref, g_ref, gt_ref, mask_ref, *, n_blk, tq):
    j = pl.program_id(1)
    h = _rms(x_ref[0], g_ref[...])
    gt = gt_ref[...].reshape(MOBA_HEADS * n_blk, h.shape[-1])
    gate = lax.dot_general(gt, h, (((1,), (1,)), ((), ())), precision=lax.Precision.HIGHEST,
                           preferred_element_type=F32)
    gate = gate.reshape(MOBA_HEADS, n_blk, tq)
    pos = j * tq + lax.broadcasted_iota(jnp.int32, gate.shape, 2)
    cur = lax.shift_right_logical(pos, int(math.log2(MOBA_BLOCK)))
    blk = lax.broadcasted_iota(jnp.int32, gate.shape, 1)
    past = blk < cur
    gate = jnp.where(past, gate, NEG)
    rank = jnp.zeros(gate.shape, jnp.int32)
    for m in range(n_blk):
        gm = gate[:, m:m + 1, :]
        tie = jnp.where(blk > m, 1, 0)
        rank = rank + jnp.where(gm > gate, 1, jnp.where(gm == gate, tie, 0))
    mask = jnp.where(rank < min(MOBA_TOPK, n_blk), jnp.where(past, 0.0, NEG), NEG)
    mask_ref[0] = mask.astype(F32).reshape(MOBA_HEADS * n_blk, tq)


def _select(x, g, gt, tq=TOKEN_TILE):
    b, s, d = x.shape
    n_blk = s // MOBA_BLOCK
    return pl.pallas_call(
        functools.partial(_select_kernel, n_blk=n_blk, tq=tq),
        grid=(b, s // tq),
        in_specs=[pl.BlockSpec((1, tq, d), lambda i, j: (i, j, 0)),
                  _resident((1, d)),
                  pl.BlockSpec((MOBA_HEADS, n_blk, d), lambda i, j: (0, i, 0))],
        out_specs=pl.BlockSpec((1, MOBA_HEADS * n_blk, tq), lambda i, j: (i, 0, j)),
        out_shape=jax.ShapeDtypeStruct((b, MOBA_HEADS * n_blk, s), F32),
        compiler_params=_params(2),
        name="select",
    )(x, g, gt)


def _qkv_kernel(h_ref, wqt_ref, wk_ref, wvt_ref, qt_ref, k_ref, vt_ref):
    h = h_ref[...]
    nt = (((1,), (1,)), ((), ()))
    qt_ref[0] = lax.dot_general(wqt_ref[...], h, nt, preferred_element_type=F32).astype(BF16)
    vt_ref[0] = lax.dot_general(wvt_ref[...], h, nt, preferred_element_type=F32).astype(BF16)
    kf = jnp.dot(h, wk_ref[...], preferred_element_type=F32)
    low = lax.broadcasted_iota(jnp.int32, (h.shape[0], V7X_LANES), 1) < HEAD_DIM
    for p in range(MOBA_HEADS // 2):
        pair = kf[:, p * V7X_LANES:(p + 1) * V7X_LANES]
        even = jnp.where(low, pair, 0.0)
        odd = jnp.where(low, pltpu.roll(pair, HEAD_DIM, 1), 0.0)
        k_ref[0, :, (2 * p) * V7X_LANES:(2 * p + 1) * V7X_LANES] = even.astype(BF16)
        k_ref[0, :, (2 * p + 1) * V7X_LANES:(2 * p + 2) * V7X_LANES] = odd.astype(BF16)


def _qkv(h2, b, s, w_qt, w_k, w_vt, tm=TOKEN_TILE):
    d = h2.shape[-1]
    nt = s // tm
    return pl.pallas_call(
        _qkv_kernel,
        grid=(b, nt),
        in_specs=[pl.BlockSpec((tm, d), lambda i, j: (i * nt + j, 0)),
                  _resident(w_qt.shape), _resident(w_k.shape), _resident(w_vt.shape)],
        out_specs=[pl.BlockSpec((1, MOBA_WIDTH, tm), lambda i, j: (i, 0, j)),
                   pl.BlockSpec((1, tm, MOBA_HEADS * V7X_LANES), lambda i, j: (i, j, 0)),
                   pl.BlockSpec((1, MOBA_WIDTH, tm), lambda i, j: (i, 0, j))],
        out_shape=[jax.ShapeDtypeStruct((b, MOBA_WIDTH, s), BF16),
                   jax.ShapeDtypeStruct((b, s, MOBA_HEADS * V7X_LANES), BF16),
                   jax.ShapeDtypeStruct((b, MOBA_WIDTH, s), BF16)],
        compiler_params=_params(2),
        name="qkv",
    )(h2, w_qt, w_k, w_vt)


def _bias_kernel(rb_ref, out_ref):
    h = pl.program_id(0)
    ik = lax.broadcasted_iota(jnp.int32, (MOBA_BLOCK, MOBA_BLOCK), 0)
    iq = lax.broadcasted_iota(jnp.int32, (MOBA_BLOCK, MOBA_BLOCK), 1)
    max_exact = REL_BUCKETS // 2
    for t in range(2):
        dist = iq - ik + t * MOBA_BLOCK
        n = jnp.maximum(dist, 0)
        nf = jnp.maximum(n, 1).astype(F32)
        large = max_exact + (jnp.log(nf / max_exact) / math.log(REL_MAX_DIST / max_exact)
                             * (REL_BUCKETS - max_exact)).astype(jnp.int32)
        large = jnp.minimum(large, REL_BUCKETS - 1)
        bucket = jnp.where(n < max_exact, n, large)
        val = jnp.zeros((MOBA_BLOCK, MOBA_BLOCK), F32)
        for bk in range(REL_BUCKETS):
            val = jnp.where(bucket == bk, rb_ref[bk * MOBA_HEADS + h], val)
        if t == 0:
            val = jnp.where(dist >= 0, val, NEG)
        out_ref[0, t] = val


def _bias_tiles(rb_flat):
    return pl.pallas_call(
        _bias_kernel,
        grid=(MOBA_HEADS,),
        in_specs=[pl.BlockSpec(memory_space=pltpu.SMEM)],
        out_specs=pl.BlockSpec((1, 2, MOBA_BLOCK, MOBA_BLOCK), lambda h: (h, 0, 0, 0)),
        out_shape=jax.ShapeDtypeStruct((MOBA_HEADS, 2, MOBA_BLOCK, MOBA_BLOCK), F32),
        compiler_params=_params(1),
        name="bias",
    )(rb_flat)


def _attn_kernel(rb_ref, qt_ref, k_ref, vt_ref, mask_ref, bias_ref, o_ref, *, n_blk):
    h = pl.program_id(1)
    far = rb_ref[(REL_BUCKETS - 1) * MOBA_HEADS + h]
    for j in range(n_blk):
        cols = slice(j * MOBA_BLOCK, (j + 1) * MOBA_BLOCK)
        q = qt_ref[0, :, cols]
        qz = jnp.concatenate([q, jnp.zeros_like(q)], axis=0)
        m = l = acc = None
        for r in range(j + 1):
            n = j - r
            rows = slice(n * MOBA_BLOCK, (n + 1) * MOBA_BLOCK)
            s = jnp.dot(k_ref[0, rows, :], qz, preferred_element_type=F32)
            if r == 0:
                s = s + bias_ref[0, 0]
            elif r == 1:
                s = s + bias_ref[0, 1] + mask_ref[0, n:n + 1, cols]
            else:
                s = s + (mask_ref[0, n:n + 1, cols] + far)
            cm = jnp.max(s, axis=0, keepdims=True)
            if r == 0:
                m = cm
                p = jnp.exp(s - m)
                l = jnp.sum(p, axis=0, keepdims=True)
                acc = jnp.dot(vt_ref[0, :, rows], p.astype(BF16), preferred_element_type=F32)
            else:
                m_new = jnp.maximum(m, cm)
                alpha = jnp.exp(m - m_new)
                p = jnp.exp(s - m_new)
                l = alpha * l + jnp.sum(p, axis=0, keepdims=True)
                acc = alpha * acc + jnp.dot(vt_ref[0, :, rows], p.astype(BF16),
                                            preferred_element_type=F32)
                m = m_new
        o_ref[0, :, cols] = (acc / l).astype(BF16)


def _attn(rb_flat, qt, k, vt, mask, bias):
    b, _, s = qt.shape
    n_blk = s // MOBA_BLOCK
    return pl.pallas_call(
        functools.partial(_attn_kernel, n_blk=n_blk),
        grid=(b, MOBA_HEADS),
        in_specs=[pl.BlockSpec(memory_space=pltpu.SMEM),
                  pl.BlockSpec((1, HEAD_DIM, s), lambda i, h: (i, h, 0)),
                  pl.BlockSpec((1, s, V7X_LANES), lambda i, h: (i, 0, h)),
                  pl.BlockSpec((1, HEAD_DIM, s), lambda i, h: (i, h, 0)),
                  pl.BlockSpec((1, n_blk, s), lambda i, h: (i, h, 0)),
                  pl.BlockSpec((1, 2, MOBA_BLOCK, MOBA_BLOCK), lambda i, h: (h, 0, 0, 0))],
        out_specs=pl.BlockSpec((1, HEAD_DIM, s), lambda i, h: (i, h, 0)),
        out_shape=jax.ShapeDtypeStruct((b, MOBA_WIDTH, s), BF16),
        compiler_params=_params(2),
        name="attn",
    )(rb_flat, qt, k, vt, mask, bias)


def _gmlp_kernel(h_ref, wu_ref, wv_ref, lng_ref, lnb_ref, ws_ref, bs_ref, o_ref):
    h = h_ref[...]
    u = jax.nn.gelu(jnp.dot(h, wu_ref[...], preferred_element_type=F32))
    v = jax.nn.gelu(jnp.dot(h, wv_ref[...], preferred_element_type=F32))
    mu = jnp.mean(v, axis=-1, keepdims=True)
    vc = v - mu
    vn = vc * lax.rsqrt(jnp.mean(vc * vc, axis=-1, keepdims=True) + EPS) * lng_ref[...] + lnb_ref[...]
    vn = vn.astype(BF16)
    causal = (lax.broadcasted_iota(jnp.int32, (CHUNK, CHUNK), 0)
              >= lax.broadcasted_iota(jnp.int32, (CHUNK, CHUNK), 1))
    for g in range(GMLP_GROUPS):
        ws = jnp.where(causal, ws_ref[g], 0.0).astype(BF16)
        lanes = slice(g * GMLP_GROUP_DIM, (g + 1) * GMLP_GROUP_DIM)
        for c in range(h.shape[0] // CHUNK):
            toks = slice(c * CHUNK, (c + 1) * CHUNK)
            mixed = jnp.dot(ws, vn[toks, lanes], preferred_element_type=F32) + bs_ref[g]
            o_ref[toks, lanes] = (u[toks, lanes] * mixed).astype(BF16)


def _gmlp(h2, w_u, w_v, ln_g, ln_b, w_s, b_s_lanes, tm=TOKEN_TILE):
    n_tok, d = h2.shape
    return pl.pallas_call(
        _gmlp_kernel,
        grid=(n_tok // tm,),
        in_specs=[pl.BlockSpec((tm, d), lambda i: (i, 0)),
                  _resident(w_u.shape), _resident(w_v.shape),
                  _resident(ln_g.shape), _resident(ln_b.shape),
                  _resident(w_s.shape), _resident(b_s_lanes.shape)],
        out_specs=pl.BlockSpec((tm, GMLP_WIDTH), lambda i: (i, 0)),
        out_shape=jax.ShapeDtypeStruct((n_tok, GMLP_WIDTH), BF16),
        compiler_params=_params(1),
        name="gmlp",
    )(h2, w_u, w_v, ln_g, ln_b, w_s, b_s_lanes)


def _memkv_kernel(mem_ref, g_ref, wkt_ref, wv_ref, kt_ref, v_ref):
    mn = _rms(mem_ref[0], g_ref[...]).astype(BF16)
    kt_ref[0] = lax.dot_general(wkt_ref[...], mn, (((1,), (1,)), ((), ())),
                                preferred_element_type=F32).astype(BF16)
    v_ref[0] = jnp.dot(mn, wv_ref[...], preferred_element_type=F32).astype(BF16)


def _memkv(mem, g, w_kt, w_v):
    b, m, d = mem.shape
    return pl.pallas_call(
        _memkv_kernel,
        grid=(b,),
        in_specs=[pl.BlockSpec((1, m, d), lambda i: (i, 0, 0)), _resident((1, d)),
                  _resident(w_kt.shape), _resident(w_v.shape)],
        out_specs=[pl.BlockSpec((1, MEM_WIDTH, m), lambda i: (i, 0, 0)),
                   pl.BlockSpec((1, m, MEM_WIDTH), lambda i: (i, 0, 0))],
        out_shape=[jax.ShapeDtypeStruct((b, MEM_WIDTH, m), BF16),
                   jax.ShapeDtypeStruct((b, m, MEM_WIDTH), BF16)],
        compiler_params=_params(1),
        name="memkv",
    )(mem, g, w_kt, w_v)


def _memattn_kernel(h_ref, wq_ref, kt_ref, v_ref, o_ref):
    cq = jnp.dot(h_ref[...], wq_ref[...], preferred_element_type=F32).astype(BF16)
    for hd in range(MEM_HEADS):
        sl = slice(hd * MEM_HEAD_DIM, (hd + 1) * MEM_HEAD_DIM)
        s = jnp.dot(cq[:, sl], kt_ref[0, sl, :], preferred_element_type=F32) * (MEM_HEAD_DIM ** -0.5)
        p = jnp.exp(s - jnp.max(s, axis=-1, keepdims=True))
        l = jnp.sum(p, axis=-1, keepdims=True)
        o = jnp.dot(p.astype(BF16), v_ref[0, :, sl], preferred_element_type=F32) / l
        o_ref[:, sl] = o.astype(BF16)


def _memattn(h2, b, s, w_cq, kt, v, tm=TOKEN_TILE):
    d = h2.shape[-1]
    nt = s // tm
    m = kt.shape[-1]
    return pl.pallas_call(
        _memattn_kernel,
        grid=(b, nt),
        in_specs=[pl.BlockSpec((tm, d), lambda i, j: (i * nt + j, 0)),
                  _resident(w_cq.shape),
                  pl.BlockSpec((1, MEM_WIDTH, m), lambda i, j: (i, 0, 0)),
                  pl.BlockSpec((1, m, MEM_WIDTH), lambda i, j: (i, 0, 0))],
        out_specs=pl.BlockSpec((tm, MEM_WIDTH), lambda i, j: (i * nt + j, 0)),
        out_shape=jax.ShapeDtypeStruct((b * s, MEM_WIDTH), BF16),
        compiler_params=_params(2),
        name="memattn",
    )(h2, w_cq, kt, v)


def _merge_kernel(x_ref, h_ref, a_ref, bt_ref, c_ref, wg_ref, wa_ref, wb_ref, wc_ref, wo_ref,
                  lnpost_ref, lnpre_ref, x1_ref, h2_ref):
    d = x_ref.shape[-1]
    gates = jax.nn.sigmoid(jnp.dot(h_ref[...], wg_ref[...], preferred_element_type=F32))
    pa = jnp.dot(a_ref[...], wa_ref[...], preferred_element_type=F32)
    pb = lax.dot_general(bt_ref[0], wb_ref[...], (((0,), (0,)), ((), ())),
                         preferred_element_type=F32)
    pc = jnp.dot(c_ref[...], wc_ref[...], preferred_element_type=F32)
    merged = gates[:, :d] * pa + gates[:, d:2 * d] * pb + gates[:, 2 * d:] * pc
    mo = jnp.dot(merged.astype(BF16), wo_ref[...], preferred_element_type=F32)
    x1 = x_ref[...] + _rms(mo, lnpost_ref[...])
    x1_ref[...] = x1
    h2_ref[...] = _rms(x1, lnpre_ref[...]).astype(BF16)


def _merge(x2, h2, a_out, b_out_t, c_out, w_g, w_a, w_b, w_c, w_o, ln_post, ln_pre, tm=256):
    n_tok, d = x2.shape
    b, _, s = b_out_t.shape
    nt = s // tm
    tok = lambda i, j: (i * nt + j, 0)
    return pl.pallas_call(
        _merge_kernel,
        grid=(b, nt),
        in_specs=[pl.BlockSpec((tm, d), tok), pl.BlockSpec((tm, d), tok),
                  pl.BlockSpec((tm, GMLP_WIDTH), tok),
                  pl.BlockSpec((1, MOBA_WIDTH, tm), lambda i, j: (i, 0, j)),
                  pl.BlockSpec((tm, MEM_WIDTH), tok),
                  _resident(w_g.shape), _resident(w_a.shape), _resident(w_b.shape),
                  _resident(w_c.shape), _resident(w_o.shape),
                  _resident((1, d)), _resident((1, d))],
        out_specs=[pl.BlockSpec((tm, d), tok), pl.BlockSpec((tm, d), tok)],
        out_shape=[jax.ShapeDtypeStruct((n_tok, d), F32), jax.ShapeDtypeStruct((n_tok, d), BF16)],
        compiler_params=_params(2),
        name="merge",
    )(x2, h2, a_out, b_out_t, c_out, w_g, w_a, w_b, w_c, w_o, ln_post, ln_pre)


def _ffn_kernel(x1_ref, h2_ref, wg_ref, wu_ref, wd_ref, ln_ref, o_ref):
    h2 = h2_ref[...]
    g = jnp.dot(h2, wg_ref[...], preferred_element_type=F32)
    u = jnp.dot(h2, wu_ref[...], preferred_element_type=F32)
    act = (jax.nn.silu(g) * u).astype(BF16)
    f = jnp.dot(act, wd_ref[...], preferred_element_type=F32)
    o_ref[...] = x1_ref[...] + _rms(f, ln_ref[...])


def _ffn(x1, h2, w_g, w_u, w_d, ln_post, tm=256):
    n_tok, d = x1.shape
    return pl.pallas_call(
        _ffn_kernel,
        grid=(n_tok // tm,),
        in_specs=[pl.BlockSpec((tm, d), lambda i: (i, 0)), pl.BlockSpec((tm, d), lambda i: (i, 0)),
                  _resident(w_g.shape), _resident(w_u.shape), _resident(w_d.shape),
                  _resident((1, d))],
        out_specs=pl.BlockSpec((tm, d), lambda i: (i, 0)),
        out_shape=jax.ShapeDtypeStruct((n_tok, d), F32),
        compiler_params=_params(1),
        name="ffn",
    )(x1, h2, w_g, w_u, w_d, ln_post)


def kernel(x, mem, ln_mix_pre, ln_mix_post, ln_ffn_pre, ln_ffn_post, ln_mem, w_in, ln_v_gain, ln_v_bias,
           w_spatial, b_spatial, rel_bias, w_mem_kv, w_branch_a, w_branch_b, w_branch_c, w_out,
           w_ffn_gate, w_ffn_up, w_ffn_down):
    b, s, d = x.shape
    assert s % MOBA_BLOCK == 0 and s % TOKEN_TILE == 0 and d % V7X_LANES == 0
    depth = w_in.shape[0]
    cuts = [0, GMLP_WIDTH, 2 * GMLP_WIDTH, 2 * GMLP_WIDTH + MOBA_WIDTH, 2 * GMLP_WIDTH + 2 * MOBA_WIDTH,
            2 * GMLP_WIDTH + 3 * MOBA_WIDTH, 2 * GMLP_WIDTH + 3 * MOBA_WIDTH + MEM_WIDTH]
    rb_flat = rel_bias.astype(F32).reshape(-1)
    bias = _bias_tiles(rb_flat)
    row = lambda v: v.reshape(1, -1).astype(F32)
    for l in range(depth):
        wi = w_in[l]
        w_u, w_v, w_q, w_k, w_v2, w_cq = (wi[:, cuts[i]:cuts[i + 1]] for i in range(6))
        w_g = wi[:, cuts[6]:]
        x2 = x.reshape(b * s, d)

        h2, hmean = _prep(x2, row(ln_mix_pre[l]))
        w_qt_scaled = (w_q * (HEAD_DIM ** -0.5)).T
        gt = _gatew(hmean.reshape(-1, d), w_k, w_qt_scaled)
        mask = _select(x, row(ln_mix_pre[l]), gt)
        qt, k, vt = _qkv(h2, b, s, w_qt_scaled.astype(BF16), w_k.astype(BF16), w_v2.T.astype(BF16))
        b_out_t = _attn(rb_flat, qt, k, vt, mask, bias)

        b_s_lanes = jnp.broadcast_to(b_spatial[l][:, :, None], (GMLP_GROUPS, CHUNK, GMLP_GROUP_DIM))
        a_out = _gmlp(h2, w_u.astype(BF16), w_v.astype(BF16), row(ln_v_gain[l]), row(ln_v_bias[l]),
                      w_spatial[l], b_s_lanes.astype(F32))

        wkv = w_mem_kv[l]
        kt_mem, v_mem = _memkv(mem, row(ln_mem[l]), wkv[:, :MEM_WIDTH].T.astype(BF16),
                               wkv[:, MEM_WIDTH:].astype(BF16))
        c_out = _memattn(h2, b, s, w_cq.astype(BF16), kt_mem, v_mem)

        x1, hn = _merge(x2, h2, a_out, b_out_t, c_out, w_g.astype(BF16), w_branch_a[l].astype(BF16),
                        w_branch_b[l].astype(BF16), w_branch_c[l].astype(BF16), w_out[l].astype(BF16),
                        row(ln_mix_post[l]), row(ln_ffn_pre[l]))
        out = _ffn(x1, hn, w_ffn_gate[l].astype(BF16), w_ffn_up[l].astype(BF16),
                   w_ffn_down[l].astype(BF16), row(ln_ffn_post[l]))
        x = out.reshape(b, s, d)
    return x
```

```python
import functools
import math

import jax
import jax.numpy as jnp
from jax import lax
from jax.experimental import pallas as pl
from jax.experimental.pallas import tpu as pltpu

F32 = jnp.float32
BF16 = jnp.bfloat16

EPS = 1e-6
NEG = -1e30
GMLP_GROUPS = 6
GMLP_GROUP_DIM = 128
GMLP_WIDTH = GMLP_GROUPS * GMLP_GROUP_DIM
CHUNK = 128
MOBA_HEADS = 12
HEAD_DIM = 64
MOBA_WIDTH = MOBA_HEADS * HEAD_DIM
MOBA_BLOCK = 256
MOBA_TOPK = 3
REL_BUCKETS = 32
REL_MAX_DIST = 128
MEM_HEADS = 4
MEM_HEAD_DIM = 128
MEM_WIDTH = MEM_HEADS * MEM_HEAD_DIM
N_BRANCHES = 3

V7X_LANES = 128
V7X_VMEM_LIMIT = 56 * 1024 * 1024

TOKEN_TILE = 512


def _params(n_axes, vmem=V7X_VMEM_LIMIT):
    return pltpu.CompilerParams(
        dimension_semantics=("arbitrary",) * n_axes, vmem_limit_bytes=vmem)


def _resident(shape):
    zeros = (0,) * len(shape)
    return pl.BlockSpec(shape, lambda *_: zeros, pipeline_mode=pl.Buffered(1))


def _rms(x, g):
    return x * lax.rsqrt(jnp.mean(x * x, axis=-1, keepdims=True) + EPS) * g


def _prep_kernel(x_ref, g_ref, h_ref, hmean_ref):
    h = _rms(x_ref[...], g_ref[...])
    h_ref[...] = h.astype(BF16)
    hmean_ref[0] = jnp.mean(h, axis=0, keepdims=True)


def _prep(x2, g):
    n_tok, d = x2.shape
    n_blk = n_tok // MOBA_BLOCK
    return pl.pallas_call(
        _prep_kernel,
        grid=(n_blk,),
        in_specs=[pl.BlockSpec((MOBA_BLOCK, d), lambda i: (i, 0)), _resident((1, d))],
        out_specs=[pl.BlockSpec((MOBA_BLOCK, d), lambda i: (i, 0)),
                   pl.BlockSpec((1, 1, d), lambda i: (i, 0, 0))],
        out_shape=[jax.ShapeDtypeStruct((n_tok, d), BF16),
                   jax.ShapeDtypeStruct((n_blk, 1, d), F32)],
        compiler_params=_params(1),
        name="prep",
    )(x2, g)


def _gatew_kernel(hm_ref, wk_ref, wqt_ref, gt_ref):
    kmean = jnp.dot(hm_ref[...], wk_ref[...], precision=lax.Precision.HIGHEST,
                    preferred_element_type=F32)
    for h in range(MOBA_HEADS):
        sl = slice(h * HEAD_DIM, (h + 1) * HEAD_DIM)
        gt_ref[h] = jnp.dot(kmean[:, sl], wqt_ref[sl, :], precision=lax.Precision.HIGHEST,
                            preferred_element_type=F32)


def _gatew(hmean, w_k, w_qt):
    rows, d = hmean.shape
    return pl.pallas_call(
        _gatew_kernel,
        grid=(1,),
        in_specs=[_resident((rows, d)), _resident(w_k.shape), _resident(w_qt.shape)],
        out_specs=pl.BlockSpec((MOBA_HEADS, rows, d), lambda i: (0, 0, 0)),
        out_shape=jax.ShapeDtypeStruct((MOBA_HEADS, rows, d), F32),
        compiler_params=_params(1),
        name="gatew",
    )(hmean, w_k, w_qt)


def _select_kernel(x_ref, g_ref, gt_ref, mask_ref, *, n_blk, tq):
    j = pl.program_id(1)
    h = _rms(x_ref[0], g_ref[...])
    gt = gt_ref[...].reshape(MOBA_HEADS * n_blk, h.shape[-1])
    gate = lax.dot_general(gt, h, (((1,), (1,)), ((), ())), precision=lax.Precision.HIGHEST,
                           preferred_element_type=F32)
    gate = gate.reshape(MOBA_HEADS, n_blk, tq)
    pos = j * tq + lax.broadcasted_iota(jnp.int32, gate.shape, 2)
    cur = lax.shift_right_logical(pos, int(math.log2(MOBA_BLOCK)))
    blk = lax.broadcasted_iota(jnp.int32, gate.shape, 1)
    past = blk < cur
    gate = jnp.where(past, gate, NEG)
    rank = jnp.zeros(gate.shape, jnp.int32)
    for m in range(n_blk):
        gm = gate[:, m:m + 1, :]
        tie = jnp.where(blk > m, 1, 0)
        rank = rank + jnp.where(gm > gate, 1, jnp.where(gm == gate, tie, 0))
    mask = jnp.where(rank < min(MOBA_TOPK, n_blk), jnp.where(past, 0.0, NEG), NEG)
    mask_ref[0] = mask.astype(F32).reshape(MOBA_HEADS * n_blk, tq)


def _select(x, g, gt, tq=TOKEN_TILE):
    b, s, d = x.shape
    n_blk = s // MOBA_BLOCK
    return pl.pallas_call(
        functools.partial(_select_kernel, n_blk=n_blk, tq=tq),
        grid=(b, s // tq),
        in_specs=[pl.BlockSpec((1, tq, d), lambda i, j: (i, j, 0)),
                  _resident((1, d)),
                  pl.BlockSpec((MOBA_HEADS, n_blk, d), lambda i, j: (0, i, 0))],
        out_specs=pl.BlockSpec((1, MOBA_HEADS * n_blk, tq), lambda i, j: (i, 0, j)),
        out_shape=jax.ShapeDtypeStruct((b, MOBA_HEADS * n_blk, s), F32),
        compiler_params=_params(2),
        name="select",
    )(x, g, gt)


def _qkv_kernel(h_ref, wqt_ref, wk_ref, wvt_ref, qt_ref, k_ref, vt_ref):
    h = h_ref[...]
    nt = (((1,), (1,)), ((), ()))
    qt_ref[0] = lax.dot_general(wqt_ref[...], h, nt, preferred_element_type=F32).astype(BF16)
    vt_ref[0] = lax.dot_general(wvt_ref[...], h, nt, preferred_element_type=F32).astype(BF16)
    kf = jnp.dot(h, wk_ref[...], preferred_element_type=F32)
    low = lax.broadcasted_iota(jnp.int32, (h.shape[0], V7X_LANES), 1) < HEAD_DIM
    for p in range(MOBA_HEADS // 2):
        pair = kf[:, p * V7X_LANES:(p + 1) * V7X_LANES]
        even = jnp.where(low, pair, 0.0)
        odd = jnp.where(low, pltpu.roll(pair, HEAD_DIM, 1), 0.0)
        k_ref[0, :, (2 * p) * V7X_LANES:(2 * p + 1) * V7X_LANES] = even.astype(BF16)
        k_ref[0, :, (2 * p + 1) * V7X_LANES:(2 * p + 2) * V7X_LANES] = odd.astype(BF16)


def _qkv(h2, b, s, w_qt, w_k, w_vt, tm=TOKEN_TILE):
    d = h2.shape[-1]
    nt = s // tm
    return pl.pallas_call(
        _qkv_kernel,
        grid=(b, nt),
        in_specs=[pl.BlockSpec((tm, d), lambda i, j: (i * nt + j, 0)),
                  _resident(w_qt.shape), _resident(w_k.shape), _resident(w_vt.shape)],
        out_specs=[pl.BlockSpec((1, MOBA_WIDTH, tm), lambda i, j: (i, 0, j)),
                   pl.BlockSpec((1, tm, MOBA_HEADS * V7X_LANES), lambda i, j: (i, j, 0)),
                   pl.BlockSpec((1, MOBA_WIDTH, tm), lambda i, j: (i, 0, j))],
        out_shape=[jax.ShapeDtypeStruct((b, MOBA_WIDTH, s), BF16),
                   jax.ShapeDtypeStruct((b, s, MOBA_HEADS * V7X_LANES), BF16),
                   jax.ShapeDtypeStruct((b, MOBA_WIDTH, s), BF16)],
        compiler_params=_params(2),
        name="qkv",
    )(h2, w_qt, w_k, w_vt)


def _bias_kernel(rb_ref, out_ref):
    h = pl.program_id(0)
    ik = lax.broadcasted_iota(jnp.int32, (MOBA_BLOCK, MOBA_BLOCK), 0)
    iq = lax.broadcasted_iota(jnp.int32, (MOBA_BLOCK, MOBA_BLOCK), 1)
    max_exact = REL_BUCKETS // 2
    for t in range(2):
        dist = iq - ik + t * MOBA_BLOCK
        n = jnp.maximum(dist, 0)
        nf = jnp.maximum(n, 1).astype(F32)
        large = max_exact + (jnp.log(nf / max_exact) / math.log(REL_MAX_DIST / max_exact)
                             * (REL_BUCKETS - max_exact)).astype(jnp.int32)
        large = jnp.minimum(large, REL_BUCKETS - 1)
        bucket = jnp.where(n < max_exact, n, large)
        val = jnp.zeros((MOBA_BLOCK, MOBA_BLOCK), F32)
        for bk in range(REL_BUCKETS):
            val = jnp.where(bucket == bk, rb_ref[bk * MOBA_HEADS + h], val)
        if t == 0:
            val = jnp.where(dist >= 0, val, NEG)
        out_ref[0, t] = val


def _bias_tiles(rb_flat):
    return pl.pallas_call(
        _bias_kernel,
        grid=(MOBA_HEADS,),
        in_specs=[pl.BlockSpec(memory_space=pltpu.SMEM)],
        out_specs=pl.BlockSpec((1, 2, MOBA_BLOCK, MOBA_BLOCK), lambda h: (h, 0, 0, 0)),
        out_shape=jax.ShapeDtypeStruct((MOBA_HEADS, 2, MOBA_BLOCK, MOBA_BLOCK), F32),
        compiler_params=_params(1),
        name="bias",
    )(rb_flat)


def _attn_kernel(rb_ref, qt_ref, k_ref, vt_ref, mask_ref, bias_ref, o_ref, s_scr, *, n_blk):
    h = pl.program_id(1)
    far = rb_ref[(REL_BUCKETS - 1) * MOBA_HEADS + h]

    def scores(j):
        cols = slice(j * MOBA_BLOCK, (j + 1) * MOBA_BLOCK)
        q = qt_ref[0, :, cols]
        qz = jnp.concatenate([q, jnp.zeros_like(q)], axis=0)
        s = jnp.dot(k_ref[0, 0:(j + 1) * MOBA_BLOCK, :], qz, preferred_element_type=F32)
        m = None
        for n in range(j + 1):
            rows = slice(n * MOBA_BLOCK, (n + 1) * MOBA_BLOCK)
            if n == j:
                sn = s[rows] + bias_ref[0, 0]
            elif n == j - 1:
                sn = s[rows] + bias_ref[0, 1] + mask_ref[0, n:n + 1, cols]
            else:
                sn = s[rows] + (mask_ref[0, n:n + 1, cols] + far)
            s_scr[j % 2, rows, :] = sn
            cm = jnp.max(sn, axis=0, keepdims=True)
            m = cm if m is None else jnp.maximum(m, cm)
        return m

    def finish(j, m):
        cols = slice(j * MOBA_BLOCK, (j + 1) * MOBA_BLOCK)
        nk = (j + 1) * MOBA_BLOCK
        p = jnp.exp(s_scr[j % 2, 0:nk, :] - m)
        l = jnp.sum(p, axis=0, keepdims=True)
        acc = jnp.dot(vt_ref[0, :, 0:nk], p.astype(BF16), preferred_element_type=F32)
        o_ref[0, :, cols] = (acc / l).astype(BF16)

    m_next = scores(0)
    for j in range(n_blk):
        m_cur = m_next
        if j + 1 < n_blk:
            m_next = scores(j + 1)
        finish(j, m_cur)


def _attn(rb_flat, qt, k, vt, mask, bias):
    b, _, s = qt.shape
    n_blk = s // MOBA_BLOCK
    return pl.pallas_call(
        functools.partial(_attn_kernel, n_blk=n_blk),
        grid=(b, MOBA_HEADS),
        in_specs=[pl.BlockSpec(memory_space=pltpu.SMEM),
                  pl.BlockSpec((1, HEAD_DIM, s), lambda i, h: (i, h, 0)),
                  pl.BlockSpec((1, s, V7X_LANES), lambda i, h: (i, 0, h)),
                  pl.BlockSpec((1, HEAD_DIM, s), lambda i, h: (i, h, 0)),
                  pl.BlockSpec((1, n_blk, s), lambda i, h: (i, h, 0)),
                  pl.BlockSpec((1, 2, MOBA_BLOCK, MOBA_BLOCK), lambda i, h: (h, 0, 0, 0))],
        out_specs=pl.BlockSpec((1, HEAD_DIM, s), lambda i, h: (i, h, 0)),
        out_shape=jax.ShapeDtypeStruct((b, MOBA_WIDTH, s), BF16),
        scratch_shapes=[pltpu.VMEM((2, s, MOBA_BLOCK), F32)],
        compiler_params=_params(2),
        name="attn",
    )(rb_flat, qt, k, vt, mask, bias)


def _gmlp_kernel(h_ref, wu_ref, wv_ref, lng_ref, lnb_ref, ws_ref, bs_ref, o_ref):
    h = h_ref[...]
    u = jax.nn.gelu(jnp.dot(h, wu_ref[...], preferred_element_type=F32))
    v = jax.nn.gelu(jnp.dot(h, wv_ref[...], preferred_element_type=F32))
    mu = jnp.mean(v, axis=-1, keepdims=True)
    vc = v - mu
    vn = vc * lax.rsqrt(jnp.mean(vc * vc, axis=-1, keepdims=True) + EPS) * lng_ref[...] + lnb_ref[...]
    vn = vn.astype(BF16)
    causal = (lax.broadcasted_iota(jnp.int32, (CHUNK, CHUNK), 0)
              >= lax.broadcasted_iota(jnp.int32, (CHUNK, CHUNK), 1))
    for g in range(GMLP_GROUPS):
        ws = jnp.where(causal, ws_ref[g], 0.0).astype(BF16)
        lanes = slice(g * GMLP_GROUP_DIM, (g + 1) * GMLP_GROUP_DIM)
        for c in range(h.shape[0] // CHUNK):
            toks = slice(c * CHUNK, (c + 1) * CHUNK)
            mixed = jnp.dot(ws, vn[toks, lanes], preferred_element_type=F32) + bs_ref[g]
            o_ref[toks, lanes] = (u[toks, lanes] * mixed).astype(BF16)


def _gmlp(h2, w_u, w_v, ln_g, ln_b, w_s, b_s_lanes, tm=TOKEN_TILE):
    n_tok, d = h2.shape
    return pl.pallas_call(
        _gmlp_kernel,
        grid=(n_tok // tm,),
        in_specs=[pl.BlockSpec((tm, d), lambda i: (i, 0)),
                  _resident(w_u.shape), _resident(w_v.shape),
                  _resident(ln_g.shape), _resident(ln_b.shape),
                  _resident(w_s.shape), _resident(b_s_lanes.shape)],
        out_specs=pl.BlockSpec((tm, GMLP_WIDTH), lambda i: (i, 0)),
        out_shape=jax.ShapeDtypeStruct((n_tok, GMLP_WIDTH), BF16),
        compiler_params=_params(1),
        name="gmlp",
    )(h2, w_u, w_v, ln_g, ln_b, w_s, b_s_lanes)


def _memkv_kernel(mem_ref, g_ref, wkt_ref, wv_ref, kt_ref, v_ref):
    mn = _rms(mem_ref[0], g_ref[...]).astype(BF16)
    kt_ref[0] = lax.dot_general(wkt_ref[...], mn, (((1,), (1,)), ((), ())),
                                preferred_element_type=F32).astype(BF16)
    v_ref[0] = jnp.dot(mn, wv_ref[...], preferred_element_type=F32).astype(BF16)


def _memkv(mem, g, w_kt, w_v):
    b, m, d = mem.shape
    return pl.pallas_call(
        _memkv_kernel,
        grid=(b,),
        in_specs=[pl.BlockSpec((1, m, d), lambda i: (i, 0, 0)), _resident((1, d)),
                  _resident(w_kt.shape), _resident(w_v.shape)],
        out_specs=[pl.BlockSpec((1, MEM_WIDTH, m), lambda i: (i, 0, 0)),
                   pl.BlockSpec((1, m, MEM_WIDTH), lambda i: (i, 0, 0))],
        out_shape=[jax.ShapeDtypeStruct((b, MEM_WIDTH, m), BF16),
                   jax.ShapeDtypeStruct((b, m, MEM_WIDTH), BF16)],
        compiler_params=_params(1),
        name="memkv",
    )(mem, g, w_kt, w_v)


def _memattn_kernel(h_ref, wq_ref, kt_ref, v_ref, o_ref):
    cq = jnp.dot(h_ref[...], wq_ref[...], preferred_element_type=F32).astype(BF16)
    for hd in range(MEM_HEADS):
        sl = slice(hd * MEM_HEAD_DIM, (hd + 1) * MEM_HEAD_DIM)
        s = jnp.dot(cq[:, sl], kt_ref[0, sl, :], preferred_element_type=F32) * (MEM_HEAD_DIM ** -0.5)
        p = jnp.exp(s - jnp.max(s, axis=-1, keepdims=True))
        l = jnp.sum(p, axis=-1, keepdims=True)
        o = jnp.dot(p.astype(BF16), v_ref[0, :, sl], preferred_element_type=F32) / l
        o_ref[:, sl] = o.astype(BF16)


def _memattn(h2, b, s, w_cq, kt, v, tm=TOKEN_TILE):
    d = h2.shape[-1]
    nt = s // tm
    m = kt.shape[-1]
    return pl.pallas_call(
        _memattn_kernel,
        grid=(b, nt),
        in_specs=[pl.BlockSpec((tm, d), lambda i, j: (i * nt + j, 0)),
                  _resident(w_cq.shape),
                  pl.BlockSpec((1, MEM_WIDTH, m), lambda i, j: (i, 0, 0)),
                  pl.BlockSpec((1, m, MEM_WIDTH), lambda i, j: (i, 0, 0))],
        out_specs=pl.BlockSpec((tm, MEM_WIDTH), lambda i, j: (i * nt + j, 0)),
        out_shape=jax.ShapeDtypeStruct((b * s, MEM_WIDTH), BF16),
        compiler_params=_params(2),
        name="memattn",
    )(h2, w_cq, kt, v)


def _merge_kernel(x_ref, h_ref, a_ref, bt_ref, c_ref, wg_ref, wa_ref, wb_ref, wc_ref, wo_ref,
                  lnpost_ref, lnpre_ref, x1_ref, h2_ref):
    d = x_ref.shape[-1]
    gates = jax.nn.sigmoid(jnp.dot(h_ref[...], wg_ref[...], preferred_element_type=F32))
    pa = jnp.dot(a_ref[...], wa_ref[...], preferred_element_type=F32)
    pb = lax.dot_general(bt_ref[0], wb_ref[...], (((0,), (0,)), ((), ())),
                         preferred_element_type=F32)
    pc = jnp.dot(c_ref[...], wc_ref[...], preferred_element_type=F32)
    merged = gates[:, :d] * pa + gates[:, d:2 * d] * pb + gates[:, 2 * d:] * pc
    mo = jnp.dot(merged.astype(BF16), wo_ref[...], preferred_element_type=F32)
    x1 = x_ref[...] + _rms(mo, lnpost_ref[...])
    x1_ref[...] = x1
    h2_ref[...] = _rms(x1, lnpre_ref[...]).astype(BF16)


def _merge(x2, h2, a_out, b_out_t, c_out, w_g, w_a, w_b, w_c, w_o, ln_post, ln_pre, tm=256):
    n_tok, d = x2.shape
    b, _, s = b_out_t.shape
    nt = s // tm
    tok = lambda i, j: (i * nt + j, 0)
    return pl.pallas_call(
        _merge_kernel,
        grid=(b, nt),
        in_specs=[pl.BlockSpec((tm, d), tok), pl.BlockSpec((tm, d), tok),
                  pl.BlockSpec((tm, GMLP_WIDTH), tok),
                  pl.BlockSpec((1, MOBA_WIDTH, tm), lambda i, j: (i, 0, j)),
                  pl.BlockSpec((tm, MEM_WIDTH), tok),
                  _resident(w_g.shape), _resident(w_a.shape), _resident(w_b.shape),
                  _resident(w_c.shape), _resident(w_o.shape),
                  _resident((1, d)), _resident((1, d))],
        out_specs=[pl.BlockSpec((tm, d), tok), pl.BlockSpec((tm, d), tok)],
        out_shape=[jax.ShapeDtypeStruct((n_tok, d), F32), jax.ShapeDtypeStruct((n_tok, d), BF16)],
        compiler_params=_params(2),
        name="merge",
    )(x2, h2, a_out, b_out_t, c_out, w_g, w_a, w_b, w_c, w_o, ln_post, ln_pre)


def _ffn_kernel(x1_ref, h2_ref, wg_ref, wu_ref, wd_ref, ln_ref, o_ref):
    h2 = h2_ref[...]
    g = jnp.dot(h2, wg_ref[...], preferred_element_type=F32)
    u = jnp.dot(h2, wu_ref[...], preferred_element_type=F32)
    act = (jax.nn.silu(g) * u).astype(BF16)
    f = jnp.dot(act, wd_ref[...], preferred_element_type=F32)
    o_ref[...] = x1_ref[...] + _rms(f, ln_ref[...])


def _ffn(x1, h2, w_g, w_u, w_d, ln_post, tm=256):
    n_tok, d = x1.shape
    return pl.pallas_call(
        _ffn_kernel,
        grid=(n_tok // tm,),
        in_specs=[pl.BlockSpec((tm, d), lambda i: (i, 0)), pl.BlockSpec((tm, d), lambda i: (i, 0)),
                  _resident(w_g.shape), _resident(w_u.shape), _resident(w_d.shape),
                  _resident((1, d))],
        out_specs=pl.BlockSpec((tm, d), lambda i: (i, 0)),
        out_shape=jax.ShapeDtypeStruct((n_tok, d), F32),
        compiler_params=_params(1),
        name="ffn",
    )(x1, h2, w_g, w_u, w_d, ln_post)


def kernel(x, mem, ln_mix_pre, ln_mix_post, ln_ffn_pre, ln_ffn_post, ln_mem, w_in, ln_v_gain, ln_v_bias,
           w_spatial, b_spatial, rel_bias, w_mem_kv, w_branch_a, w_branch_b, w_branch_c, w_out,
           w_ffn_gate, w_ffn_up, w_ffn_down):
    b, s, d = x.shape
    assert s % MOBA_BLOCK == 0 and s % TOKEN_TILE == 0 and d % V7X_LANES == 0
    depth = w_in.shape[0]
    cuts = [0, GMLP_WIDTH, 2 * GMLP_WIDTH, 2 * GMLP_WIDTH + MOBA_WIDTH, 2 * GMLP_WIDTH + 2 * MOBA_WIDTH,
            2 * GMLP_WIDTH + 3 * MOBA_WIDTH, 2 * GMLP_WIDTH + 3 * MOBA_WIDTH + MEM_WIDTH]
    rb_flat = rel_bias.astype(F32).reshape(-1)
    bias = _bias_tiles(rb_flat)
    row = lambda v: v.reshape(1, -1).astype(F32)
    for l in range(depth):
        wi = w_in[l]
        w_u, w_v, w_q, w_k, w_v2, w_cq = (wi[:, cuts[i]:cuts[i + 1]] for i in range(6))
        w_g = wi[:, cuts[6]:]
        x2 = x.reshape(b * s, d)

        h2, hmean = _prep(x2, row(ln_mix_pre[l]))
        w_qt_scaled = (w_q * (HEAD_DIM ** -0.5)).T
        gt = _gatew(hmean.reshape(-1, d), w_k, w_qt_scaled)
        mask = _select(x, row(ln_mix_pre[l]), gt)
        qt, k, vt = _qkv(h2, b, s, w_qt_scaled.astype(BF16), w_k.astype(BF16), w_v2.T.astype(BF16))
        b_out_t = _attn(rb_flat, qt, k, vt, mask, bias)

        b_s_lanes = jnp.broadcast_to(b_spatial[l][:, :, None], (GMLP_GROUPS, CHUNK, GMLP_GROUP_DIM))
        a_out = _gmlp(h2, w_u.astype(BF16), w_v.astype(BF16), row(ln_v_gain[l]), row(ln_v_bias[l]),
                      w_spatial[l], b_s_lanes.astype(F32))

        wkv = w_mem_kv[l]
        kt_mem, v_mem = _memkv(mem, row(ln_mem[l]), wkv[:, :MEM_WIDTH].T.astype(BF16),
                               wkv[:, MEM_WIDTH:].astype(BF16))
        c_out = _memattn(h2, b, s, w_cq.astype(BF16), kt_mem, v_mem)

        x1, hn = _merge(x2, h2, a_out, b_out_t, c_out, w_g.astype(BF16), w_branch_a[l].astype(BF16),
                        w_branch_b[l].astype(BF16), w_branch_c[l].astype(BF16), w_out[l].astype(BF16),
                        row(ln_mix_post[l]), row(ln_ffn_pre[l]))
        out = _ffn(x1, hn, w_ffn_gate[l].astype(BF16), w_ffn_up[l].astype(BF16),
                   w_ffn_down[l].astype(BF16), row(ln_ffn_post[l]))
        x = out.reshape(b, s, d)
    return x
```

```python
import functools
import math

import jax
import jax.numpy as jnp
from jax import lax
from jax.experimental import pallas as pl
from jax.experimental.pallas import tpu as pltpu

F32 = jnp.float32
BF16 = jnp.bfloat16

EPS = 1e-6
NEG = -1e30
GMLP_GROUPS = 6
GMLP_GROUP_DIM = 128
GMLP_WIDTH = GMLP_GROUPS * GMLP_GROUP_DIM
CHUNK = 128
MOBA_HEADS = 12
HEAD_DIM = 64
MOBA_WIDTH = MOBA_HEADS * HEAD_DIM
MOBA_BLOCK = 256
MOBA_TOPK = 3
REL_BUCKETS = 32
REL_MAX_DIST = 128
LOG2E = math.log2(math.e)
ONES_ROWS = 16
MEM_HEADS = 4
MEM_HEAD_DIM = 128
MEM_WIDTH = MEM_HEADS * MEM_HEAD_DIM
N_BRANCHES = 3

V7X_LANES = 128
V7X_VMEM_LIMIT = 56 * 1024 * 1024

TOKEN_TILE = 512
PREP_BLOCKS = 4
GATE_ROWS = 128


def _params(n_axes, vmem=V7X_VMEM_LIMIT):
    return pltpu.CompilerParams(
        dimension_semantics=("arbitrary",) * n_axes, vmem_limit_bytes=vmem)


def _resident(shape):
    zeros = (0,) * len(shape)
    return pl.BlockSpec(shape, lambda *_: zeros, pipeline_mode=pl.Buffered(1))


def _rms(x, g):
    return x * lax.rsqrt(jnp.mean(x * x, axis=-1, keepdims=True) + EPS) * g


def _prep_kernel(x_ref, g_ref, h_ref, hmean_ref):
    h = _rms(x_ref[...], g_ref[...])
    h_ref[...] = h.astype(BF16)
    for i in range(PREP_BLOCKS):
        hmean_ref[i] = jnp.mean(h[i * MOBA_BLOCK:(i + 1) * MOBA_BLOCK], axis=0, keepdims=True)


def _prep(x2, g):
    n_tok, d = x2.shape
    n_blk = n_tok // MOBA_BLOCK
    tm = PREP_BLOCKS * MOBA_BLOCK
    return pl.pallas_call(
        _prep_kernel,
        grid=(n_tok // tm,),
        in_specs=[pl.BlockSpec((tm, d), lambda i: (i, 0)), _resident((1, d))],
        out_specs=[pl.BlockSpec((tm, d), lambda i: (i, 0)),
                   pl.BlockSpec((PREP_BLOCKS, 1, d), lambda i: (i, 0, 0))],
        out_shape=[jax.ShapeDtypeStruct((n_tok, d), BF16),
                   jax.ShapeDtypeStruct((n_blk, 1, d), F32)],
        compiler_params=_params(1),
        name="prep",
    )(x2, g)


def _split_bf16(x):
    hi = x.astype(BF16)
    return hi, (x - hi.astype(F32)).astype(BF16)


def _gatew_kernel(hm_ref, wk_ref, wqt_ref, g_ref, gt_scr, *, n_batch, n_blk):
    kmean = jnp.dot(hm_ref[...], wk_ref[...], precision=lax.Precision.HIGHEST,
                    preferred_element_type=F32)
    used = MOBA_HEADS * n_blk
    gt_scr[:, used:, :] = jnp.zeros((n_batch, GATE_ROWS - used, gt_scr.shape[-1]), F32)
    for h in range(MOBA_HEADS):
        sl = slice(h * HEAD_DIM, (h + 1) * HEAD_DIM)
        res = jnp.dot(kmean[:, sl], wqt_ref[sl, :], precision=lax.Precision.HIGHEST,
                      preferred_element_type=F32)
        for b in range(n_batch):
            gt_scr[b, h * n_blk:(h + 1) * n_blk, :] = res[b * n_blk:(b + 1) * n_blk, :]
    for b in range(n_batch):
        hi, lo = _split_bf16(gt_scr[b].T)
        g_ref[b, :, 0:GATE_ROWS] = hi
        g_ref[b, :, GATE_ROWS:] = lo


def _gatew(hmean, w_k, w_qt, n_batch):
    rows, d = hmean.shape
    n_blk = rows // n_batch
    assert MOBA_HEADS * n_blk <= GATE_ROWS
    return pl.pallas_call(
        functools.partial(_gatew_kernel, n_batch=n_batch, n_blk=n_blk),
        grid=(1,),
        in_specs=[_resident((rows, d)), _resident(w_k.shape), _resident(w_qt.shape)],
        out_specs=pl.BlockSpec((n_batch, d, 2 * GATE_ROWS), lambda i: (0, 0, 0)),
        out_shape=jax.ShapeDtypeStruct((n_batch, d, 2 * GATE_ROWS), BF16),
        scratch_shapes=[pltpu.VMEM((n_batch, GATE_ROWS, d), F32)],
        compiler_params=_params(1),
        name="gatew",
    )(hmean, w_k, w_qt)


def _select_kernel(x_ref, g_ref, gw_ref, mask_ref, *, n_blk, tq):
    j = pl.program_id(1)
    h = _rms(x_ref[0], g_ref[...])
    res = jnp.dot(jnp.concatenate(_split_bf16(h), axis=0), gw_ref[0], preferred_element_type=F32)
    gate = ((res[:tq, :GATE_ROWS] + (res[:tq, GATE_ROWS:] + res[tq:, :GATE_ROWS])) + res[tq:, GATE_ROWS:])
    gate = gate.T[0:MOBA_HEADS * n_blk].reshape(MOBA_HEADS, n_blk, tq)
    pos = j * tq + lax.broadcasted_iota(jnp.int32, gate.shape, 2)
    cur = lax.shift_right_logical(pos, int(math.log2(MOBA_BLOCK)))
    blk = lax.broadcasted_iota(jnp.int32, gate.shape, 1)
    past = blk < cur
    gate = jnp.where(past, gate, NEG)
    rank = jnp.zeros(gate.shape, jnp.int32)
    for m in range(n_blk):
        gm = gate[:, m:m + 1, :]
        tie = jnp.where(blk > m, 1, 0)
        rank = rank + jnp.where(gm > gate, 1, jnp.where(gm == gate, tie, 0))
    mask = jnp.where(rank < min(MOBA_TOPK, n_blk), jnp.where(past, 0.0, NEG), NEG)
    mask_ref[0] = mask.astype(F32).reshape(MOBA_HEADS * n_blk, tq)


def _select(x, g, gt, tq=TOKEN_TILE):
    b, s, d = x.shape
    n_blk = s // MOBA_BLOCK
    return pl.pallas_call(
        functools.partial(_select_kernel, n_blk=n_blk, tq=tq),
        grid=(b, s // tq),
        in_specs=[pl.BlockSpec((1, tq, d), lambda i, j: (i, j, 0)),
                  _resident((1, d)),
                  pl.BlockSpec((1, d, 2 * GATE_ROWS), lambda i, j: (i, 0, 0))],
        out_specs=pl.BlockSpec((1, MOBA_HEADS * n_blk, tq), lambda i, j: (i, 0, j)),
        out_shape=jax.ShapeDtypeStruct((b, MOBA_HEADS * n_blk, s), F32),
        compiler_params=_params(2),
        name="select",
    )(x, g, gt)


def _qkv_kernel(h_ref, wqt_ref, wk_ref, wvt_ref, qt_ref, k_ref, vt_ref):
    h = h_ref[...]
    nt = (((1,), (1,)), ((), ()))
    qt_ref[0] = lax.dot_general(wqt_ref[...], h, nt, preferred_element_type=F32).astype(BF16)
    vt_ref[0] = lax.dot_general(wvt_ref[...], h, nt, preferred_element_type=F32).astype(BF16)
    kf = jnp.dot(h, wk_ref[...], preferred_element_type=F32)
    low = lax.broadcasted_iota(jnp.int32, (h.shape[0], V7X_LANES), 1) < HEAD_DIM
    for p in range(MOBA_HEADS // 2):
        pair = kf[:, p * V7X_LANES:(p + 1) * V7X_LANES]
        even = jnp.where(low, pair, 0.0)
        odd = jnp.where(low, pltpu.roll(pair, HEAD_DIM, 1), 0.0)
        k_ref[0, :, (2 * p) * V7X_LANES:(2 * p + 1) * V7X_LANES] = even.astype(BF16)
        k_ref[0, :, (2 * p + 1) * V7X_LANES:(2 * p + 2) * V7X_LANES] = odd.astype(BF16)


def _qkv(h2, b, s, w_qt, w_k, w_vt, tm=TOKEN_TILE):
    d = h2.shape[-1]
    nt = s // tm
    return pl.pallas_call(
        _qkv_kernel,
        grid=(b, nt),
        in_specs=[pl.BlockSpec((tm, d), lambda i, j: (i * nt + j, 0)),
                  _resident(w_qt.shape), _resident(w_k.shape), _resident(w_vt.shape)],
        out_specs=[pl.BlockSpec((1, MOBA_WIDTH, tm), lambda i, j: (i, 0, j)),
                   pl.BlockSpec((1, tm, MOBA_HEADS * V7X_LANES), lambda i, j: (i, j, 0)),
                   pl.BlockSpec((1, MOBA_WIDTH, tm), lambda i, j: (i, 0, j))],
        out_shape=[jax.ShapeDtypeStruct((b, MOBA_WIDTH, s), BF16),
                   jax.ShapeDtypeStruct((b, s, MOBA_HEADS * V7X_LANES), BF16),
                   jax.ShapeDtypeStruct((b, MOBA_WIDTH, s), BF16)],
        compiler_params=_params(2),
        name="qkv",
    )(h2, w_qt, w_k, w_vt)


def _bias_kernel(rb_ref, out_ref):
    h = pl.program_id(0)
    ik = lax.broadcasted_iota(jnp.int32, (MOBA_BLOCK, MOBA_BLOCK), 0)
    iq = lax.broadcasted_iota(jnp.int32, (MOBA_BLOCK, MOBA_BLOCK), 1)
    max_exact = REL_BUCKETS // 2
    for t in range(2):
        dist = iq - ik + t * MOBA_BLOCK
        n = jnp.maximum(dist, 0)
        nf = jnp.maximum(n, 1).astype(F32)
        large = max_exact + (jnp.log(nf / max_exact) / math.log(REL_MAX_DIST / max_exact)
                             * (REL_BUCKETS - max_exact)).astype(jnp.int32)
        large = jnp.minimum(large, REL_BUCKETS - 1)
        bucket = jnp.where(n < max_exact, n, large)
        val = jnp.zeros((MOBA_BLOCK, MOBA_BLOCK), F32)
        for bk in range(REL_BUCKETS):
            val = jnp.where(bucket == bk, rb_ref[bk * MOBA_HEADS + h], val)
        val = val * LOG2E
        if t == 0:
            val = jnp.where(dist >= 0, val, NEG)
        out_ref[0, t] = val


def _bias_tiles(rb_flat):
    return pl.pallas_call(
        _bias_kernel,
        grid=(MOBA_HEADS,),
        in_specs=[pl.BlockSpec(memory_space=pltpu.SMEM)],
        out_specs=pl.BlockSpec((1, 2, MOBA_BLOCK, MOBA_BLOCK), lambda h: (h, 0, 0, 0)),
        out_shape=jax.ShapeDtypeStruct((MOBA_HEADS, 2, MOBA_BLOCK, MOBA_BLOCK), F32),
        compiler_params=_params(1),
        name="bias",
    )(rb_flat)


def _attn_kernel(rb_ref, qt_ref, k_ref, vt_ref, mask_ref, bias_ref, o_ref, s_scr, *, n_blk):
    h = pl.program_id(1)
    far = rb_ref[(REL_BUCKETS - 1) * MOBA_HEADS + h] * LOG2E

    def scores(j):
        cols = slice(j * MOBA_BLOCK, (j + 1) * MOBA_BLOCK)
        q = qt_ref[0, :, cols]
        qz = jnp.concatenate([q, jnp.zeros_like(q)], axis=0)
        s = jnp.dot(k_ref[0, 0:(j + 1) * MOBA_BLOCK, :], qz, preferred_element_type=F32)
        m = None
        for n in range(j + 1):
            rows = slice(n * MOBA_BLOCK, (n + 1) * MOBA_BLOCK)
            if n == j:
                sn = s[rows] + bias_ref[0, 0]
            elif n == j - 1:
                sn = s[rows] + bias_ref[0, 1] + mask_ref[0, n:n + 1, cols]
            else:
                sn = s[rows] + (mask_ref[0, n:n + 1, cols] + far)
            s_scr[j % 2, rows, :] = sn
            cm = jnp.max(sn, axis=0, keepdims=True)
            m = cm if m is None else jnp.maximum(m, cm)
        return m

    def finish(j, m):
        cols = slice(j * MOBA_BLOCK, (j + 1) * MOBA_BLOCK)
        nk = (j + 1) * MOBA_BLOCK
        p = jnp.exp2(s_scr[j % 2, 0:nk, :] - m).astype(BF16)
        vt1 = jnp.concatenate([vt_ref[0, :, 0:nk], jnp.ones((ONES_ROWS, nk), BF16)], axis=0)
        acc = jnp.dot(vt1, p, preferred_element_type=F32)
        o_ref[0, :, cols] = (acc[0:HEAD_DIM] / acc[HEAD_DIM:HEAD_DIM + 1]).astype(BF16)

    m_next = scores(0)
    for j in range(n_blk):
        m_cur = m_next
        if j + 1 < n_blk:
            m_next = scores(j + 1)
        finish(j, m_cur)


def _attn(rb_flat, qt, k, vt, mask, bias):
    b, _, s = qt.shape
    n_blk = s // MOBA_BLOCK
    return pl.pallas_call(
        functools.partial(_attn_kernel, n_blk=n_blk),
        grid=(b, MOBA_HEADS),
        in_specs=[pl.BlockSpec(memory_space=pltpu.SMEM),
                  pl.BlockSpec((1, HEAD_DIM, s), lambda i, h: (i, h, 0)),
                  pl.BlockSpec((1, s, V7X_LANES), lambda i, h: (i, 0, h)),
                  pl.BlockSpec((1, HEAD_DIM, s), lambda i, h: (i, h, 0)),
                  pl.BlockSpec((1, n_blk, s), lambda i, h: (i, h, 0)),
                  pl.BlockSpec((1, 2, MOBA_BLOCK, MOBA_BLOCK), lambda i, h: (h, 0, 0, 0))],
        out_specs=pl.BlockSpec((1, HEAD_DIM, s), lambda i, h: (i, h, 0)),
        out_shape=jax.ShapeDtypeStruct((b, MOBA_WIDTH, s), BF16),
        scratch_shapes=[pltpu.VMEM((2, s, MOBA_BLOCK), F32)],
        compiler_params=_params(2),
        name="attn",
    )(rb_flat, qt, k, vt, mask, bias)


def _gmlp_kernel(h_ref, wu_ref, wv_ref, lng_ref, lnb_ref, ws_ref, bs_ref, o_ref):
    h = h_ref[...]
    u = jax.nn.gelu(jnp.dot(h, wu_ref[...], preferred_element_type=F32))
    v = jax.nn.gelu(jnp.dot(h, wv_ref[...], preferred_element_type=F32))
    mu = jnp.mean(v, axis=-1, keepdims=True)
    vc = v - mu
    vn = vc * lax.rsqrt(jnp.mean(vc * vc, axis=-1, keepdims=True) + EPS) * lng_ref[...] + lnb_ref[...]
    vn = vn.astype(BF16)
    causal = (lax.broadcasted_iota(jnp.int32, (CHUNK, CHUNK), 0)
              >= lax.broadcasted_iota(jnp.int32, (CHUNK, CHUNK), 1))
    for g in range(GMLP_GROUPS):
        ws = jnp.where(causal, ws_ref[g], 0.0).astype(BF16)
        lanes = slice(g * GMLP_GROUP_DIM, (g + 1) * GMLP_GROUP_DIM)
        for c in range(h.shape[0] // CHUNK):
            toks = slice(c * CHUNK, (c + 1) * CHUNK)
            mixed = jnp.dot(ws, vn[toks, lanes], preferred_element_type=F32) + bs_ref[g]
            o_ref[toks, lanes] = (u[toks, lanes] * mixed).astype(BF16)


def _gmlp(h2, w_u, w_v, ln_g, ln_b, w_s, b_s_lanes, tm=TOKEN_TILE):
    n_tok, d = h2.shape
    return pl.pallas_call(
        _gmlp_kernel,
        grid=(n_tok // tm,),
        in_specs=[pl.BlockSpec((tm, d), lambda i: (i, 0)),
                  _resident(w_u.shape), _resident(w_v.shape),
                  _resident(ln_g.shape), _resident(ln_b.shape),
                  _resident(w_s.shape), _resident(b_s_lanes.shape)],
        out_specs=pl.BlockSpec((tm, GMLP_WIDTH), lambda i: (i, 0)),
        out_shape=jax.ShapeDtypeStruct((n_tok, GMLP_WIDTH), BF16),
        compiler_params=_params(1),
        name="gmlp",
    )(h2, w_u, w_v, ln_g, ln_b, w_s, b_s_lanes)


def _memkv_kernel(mem_ref, g_ref, wkt_ref, wv_ref, kt_ref, v_ref):
    mn = _rms(mem_ref[0], g_ref[...]).astype(BF16)
    kt_ref[0] = lax.dot_general(wkt_ref[...], mn, (((1,), (1,)), ((), ())),
                                preferred_element_type=F32).astype(BF16)
    v_ref[0] = jnp.dot(mn, wv_ref[...], preferred_element_type=F32).astype(BF16)


def _memkv(mem, g, w_kt, w_v):
    b, m, d = mem.shape
    return pl.pallas_call(
        _memkv_kernel,
        grid=(b,),
        in_specs=[pl.BlockSpec((1, m, d), lambda i: (i, 0, 0)), _resident((1, d)),
                  _resident(w_kt.shape), _resident(w_v.shape)],
        out_specs=[pl.BlockSpec((1, MEM_WIDTH, m), lambda i: (i, 0, 0)),
                   pl.BlockSpec((1, m, MEM_WIDTH), lambda i: (i, 0, 0))],
        out_shape=[jax.ShapeDtypeStruct((b, MEM_WIDTH, m), BF16),
                   jax.ShapeDtypeStruct((b, m, MEM_WIDTH), BF16)],
        compiler_params=_params(1),
        name="memkv",
    )(mem, g, w_kt, w_v)


def _memattn_kernel(h_ref, wq_ref, kt_ref, v_ref, o_ref):
    cq = jnp.dot(h_ref[...], wq_ref[...], preferred_element_type=F32).astype(BF16)
    for hd in range(MEM_HEADS):
        sl = slice(hd * MEM_HEAD_DIM, (hd + 1) * MEM_HEAD_DIM)
        s = jnp.dot(cq[:, sl], kt_ref[0, sl, :], preferred_element_type=F32) * (MEM_HEAD_DIM ** -0.5)
        p = jnp.exp(s - jnp.max(s, axis=-1, keepdims=True))
        l = jnp.sum(p, axis=-1, keepdims=True)
        o = jnp.dot(p.astype(BF16), v_ref[0, :, sl], preferred_element_type=F32) / l
        o_ref[:, sl] = o.astype(BF16)


def _memattn(h2, b, s, w_cq, kt, v, tm=TOKEN_TILE):
    d = h2.shape[-1]
    nt = s // tm
    m = kt.shape[-1]
    return pl.pallas_call(
        _memattn_kernel,
        grid=(b, nt),
        in_specs=[pl.BlockSpec((tm, d), lambda i, j: (i * nt + j, 0)),
                  _resident(w_cq.shape),
                  pl.BlockSpec((1, MEM_WIDTH, m), lambda i, j: (i, 0, 0)),
                  pl.BlockSpec((1, m, MEM_WIDTH), lambda i, j: (i, 0, 0))],
        out_specs=pl.BlockSpec((tm, MEM_WIDTH), lambda i, j: (i * nt + j, 0)),
        out_shape=jax.ShapeDtypeStruct((b * s, MEM_WIDTH), BF16),
        compiler_params=_params(2),
        name="memattn",
    )(h2, w_cq, kt, v)


def _merge_kernel(x_ref, h_ref, a_ref, bt_ref, c_ref, wg_ref, wa_ref, wb_ref, wc_ref, wo_ref,
                  lnpost_ref, lnpre_ref, x1_ref, h2_ref):
    d = x_ref.shape[-1]
    gates = jax.nn.sigmoid(jnp.dot(h_ref[...], wg_ref[...], preferred_element_type=F32))
    pa = jnp.dot(a_ref[...], wa_ref[...], preferred_element_type=F32)
    pb = lax.dot_general(bt_ref[0], wb_ref[...], (((0,), (0,)), ((), ())),
                         preferred_element_type=F32)
    pc = jnp.dot(c_ref[...], wc_ref[...], preferred_element_type=F32)
    merged = gates[:, :d] * pa + gates[:, d:2 * d] * pb + gates[:, 2 * d:] * pc
    mo = jnp.dot(merged.astype(BF16), wo_ref[...], preferred_element_type=F32)
    x1 = x_ref[...] + _rms(mo, lnpost_ref[...])
    x1_ref[...] = x1
    h2_ref[...] = _rms(x1, lnpre_ref[...]).astype(BF16)


def _merge(x2, h2, a_out, b_out_t, c_out, w_g, w_a, w_b, w_c, w_o, ln_post, ln_pre, tm=256):
    n_tok, d = x2.shape
    b, _, s = b_out_t.shape
    nt = s // tm
    tok = lambda i, j: (i * nt + j, 0)
    return pl.pallas_call(
        _merge_kernel,
        grid=(b, nt),
        in_specs=[pl.BlockSpec((tm, d), tok), pl.BlockSpec((tm, d), tok),
                  pl.BlockSpec((tm, GMLP_WIDTH), tok),
                  pl.BlockSpec((1, MOBA_WIDTH, tm), lambda i, j: (i, 0, j)),
                  pl.BlockSpec((tm, MEM_WIDTH), tok),
                  _resident(w_g.shape), _resident(w_a.shape), _resident(w_b.shape),
                  _resident(w_c.shape), _resident(w_o.shape),
                  _resident((1, d)), _resident((1, d))],
        out_specs=[pl.BlockSpec((tm, d), tok), pl.BlockSpec((tm, d), tok)],
        out_shape=[jax.ShapeDtypeStruct((n_tok, d), F32), jax.ShapeDtypeStruct((n_tok, d), BF16)],
        compiler_params=_params(2),
        name="merge",
    )(x2, h2, a_out, b_out_t, c_out, w_g, w_a, w_b, w_c, w_o, ln_post, ln_pre)


def _ffn_kernel(x1_ref, h2_ref, wg_ref, wu_ref, wd_ref, ln_ref, o_ref):
    h2 = h2_ref[...]
    g = jnp.dot(h2, wg_ref[...], preferred_element_type=F32)
    u = jnp.dot(h2, wu_ref[...], preferred_element_type=F32)
    act = (jax.nn.silu(g) * u).astype(BF16)
    f = jnp.dot(act, wd_ref[...], preferred_element_type=F32)
    o_ref[...] = x1_ref[...] + _rms(f, ln_ref[...])


def _ffn(x1, h2, w_g, w_u, w_d, ln_post, tm=256):
    n_tok, d = x1.shape
    return pl.pallas_call(
        _ffn_kernel,
        grid=(n_tok // tm,),
        in_specs=[pl.BlockSpec((tm, d), lambda i: (i, 0)), pl.BlockSpec((tm, d), lambda i: (i, 0)),
                  _resident(w_g.shape), _resident(w_u.shape), _resident(w_d.shape),
                  _resident((1, d))],
        out_specs=pl.BlockSpec((tm, d), lambda i: (i, 0)),
        out_shape=jax.ShapeDtypeStruct((n_tok, d), F32),
        compiler_params=_params(1),
        name="ffn",
    )(x1, h2, w_g, w_u, w_d, ln_post)


def kernel(x, mem, ln_mix_pre, ln_mix_post, ln_ffn_pre, ln_ffn_post, ln_mem, w_in, ln_v_gain, ln_v_bias,
           w_spatial, b_spatial, rel_bias, w_mem_kv, w_branch_a, w_branch_b, w_branch_c, w_out,
           w_ffn_gate, w_ffn_up, w_ffn_down):
    b, s, d = x.shape
    assert s % MOBA_BLOCK == 0 and s % TOKEN_TILE == 0 and d % V7X_LANES == 0
    depth = w_in.shape[0]
    cuts = [0, GMLP_WIDTH, 2 * GMLP_WIDTH, 2 * GMLP_WIDTH + MOBA_WIDTH, 2 * GMLP_WIDTH + 2 * MOBA_WIDTH,
            2 * GMLP_WIDTH + 3 * MOBA_WIDTH, 2 * GMLP_WIDTH + 3 * MOBA_WIDTH + MEM_WIDTH]
    rb_flat = rel_bias.astype(F32).reshape(-1)
    bias = _bias_tiles(rb_flat)
    row = lambda v: v.reshape(1, -1).astype(F32)
    for l in range(depth):
        wi = w_in[l]
        w_u, w_v, w_q, w_k, w_v2, w_cq = (wi[:, cuts[i]:cuts[i + 1]] for i in range(6))
        w_g = wi[:, cuts[6]:]
        x2 = x.reshape(b * s, d)

        h2, hmean = _prep(x2, row(ln_mix_pre[l]))
        w_qt_scaled = (w_q * (HEAD_DIM ** -0.5)).T
        gt = _gatew(hmean.reshape(-1, d), w_k, w_qt_scaled, b)
        mask = _select(x, row(ln_mix_pre[l]), gt)
        qt, k, vt = _qkv(h2, b, s, (w_qt_scaled * LOG2E).astype(BF16), w_k.astype(BF16),
                         w_v2.T.astype(BF16))
        b_out_t = _attn(rb_flat, qt, k, vt, mask, bias)

        b_s_lanes = jnp.broadcast_to(b_spatial[l][:, :, None], (GMLP_GROUPS, CHUNK, GMLP_GROUP_DIM))
        a_out = _gmlp(h2, w_u.astype(BF16), w_v.astype(BF16), row(ln_v_gain[l]), row(ln_v_bias[l]),
                      w_spatial[l], b_s_lanes.astype(F32))

        wkv = w_mem_kv[l]
        kt_mem, v_mem = _memkv(mem, row(ln_mem[l]), wkv[:, :MEM_WIDTH].T.astype(BF16),
                               wkv[:, MEM_WIDTH:].astype(BF16))
        c_out = _memattn(h2, b, s, w_cq.astype(BF16), kt_mem, v_mem)

        x1, hn = _merge(x2, h2, a_out, b_out_t, c_out, w_g.astype(BF16), w_branch_a[l].astype(BF16),
                        w_branch_b[l].astype(BF16), w_branch_c[l].astype(BF16), w_out[l].astype(BF16),
                        row(ln_mix_post[l]), row(ln_ffn_pre[l]))
        out = _ffn(x1, hn, w_ffn_gate[l].astype(BF16), w_ffn_up[l].astype(BF16),
                   w_ffn_down[l].astype(BF16), row(ln_ffn_post[l]))
        x = out.reshape(b, s, d)
    return x
```

```python
import functools
import math

import jax
import jax.numpy as jnp
from jax import lax
from jax.experimental import pallas as pl
from jax.experimental.pallas import tpu as pltpu

F32 = jnp.float32
BF16 = jnp.bfloat16

EPS = 1e-6
NEG = -1e30
GMLP_GROUPS = 6
GMLP_GROUP_DIM = 128
GMLP_WIDTH = GMLP_GROUPS * GMLP_GROUP_DIM
CHUNK = 128
MOBA_HEADS = 12
HEAD_DIM = 64
MOBA_WIDTH = MOBA_HEADS * HEAD_DIM
MOBA_BLOCK = 256
MOBA_TOPK = 3
REL_BUCKETS = 32
REL_MAX_DIST = 128
LOG2E = math.log2(math.e)
ONES_ROWS = 16
SCORE_LOOKAHEAD = 4
MEM_HEADS = 4
MEM_HEAD_DIM = 128
MEM_WIDTH = MEM_HEADS * MEM_HEAD_DIM
N_BRANCHES = 3

V7X_LANES = 128
V7X_VMEM_LIMIT = 56 * 1024 * 1024

TOKEN_TILE = 512
PREP_BLOCKS = 4
GATE_ROWS = 128


def _params(n_axes, vmem=V7X_VMEM_LIMIT):
    return pltpu.CompilerParams(
        dimension_semantics=("arbitrary",) * n_axes, vmem_limit_bytes=vmem)


def _resident(shape):
    zeros = (0,) * len(shape)
    return pl.BlockSpec(shape, lambda *_: zeros, pipeline_mode=pl.Buffered(1))


def _rms(x, g):
    return x * lax.rsqrt(jnp.mean(x * x, axis=-1, keepdims=True) + EPS) * g


def _prep_kernel(x_ref, g_ref, h_ref, hmean_ref):
    h = _rms(x_ref[...], g_ref[...])
    h_ref[...] = h.astype(BF16)
    for i in range(PREP_BLOCKS):
        hmean_ref[i] = jnp.mean(h[i * MOBA_BLOCK:(i + 1) * MOBA_BLOCK], axis=0, keepdims=True)


def _prep(x2, g):
    n_tok, d = x2.shape
    n_blk = n_tok // MOBA_BLOCK
    tm = PREP_BLOCKS * MOBA_BLOCK
    return pl.pallas_call(
        _prep_kernel,
        grid=(n_tok // tm,),
        in_specs=[pl.BlockSpec((tm, d), lambda i: (i, 0)), _resident((1, d))],
        out_specs=[pl.BlockSpec((tm, d), lambda i: (i, 0)),
                   pl.BlockSpec((PREP_BLOCKS, 1, d), lambda i: (i, 0, 0))],
        out_shape=[jax.ShapeDtypeStruct((n_tok, d), BF16),
                   jax.ShapeDtypeStruct((n_blk, 1, d), F32)],
        compiler_params=_params(1),
        name="prep",
    )(x2, g)


def _split_bf16(x):
    hi = x.astype(BF16)
    return hi, (x - hi.astype(F32)).astype(BF16)


def _gatew_kernel(hm_ref, wk_ref, wqt_ref, g_ref, gt_scr, *, n_batch, n_blk):
    kmean = jnp.dot(hm_ref[...], wk_ref[...], precision=lax.Precision.HIGHEST,
                    preferred_element_type=F32)
    used = MOBA_HEADS * n_blk
    gt_scr[:, used:, :] = jnp.zeros((n_batch, GATE_ROWS - used, gt_scr.shape[-1]), F32)
    for h in range(MOBA_HEADS):
        sl = slice(h * HEAD_DIM, (h + 1) * HEAD_DIM)
        res = jnp.dot(kmean[:, sl], wqt_ref[sl, :], precision=lax.Precision.HIGHEST,
                      preferred_element_type=F32)
        for b in range(n_batch):
            gt_scr[b, h * n_blk:(h + 1) * n_blk, :] = res[b * n_blk:(b + 1) * n_blk, :]
    for b in range(n_batch):
        hi, lo = _split_bf16(gt_scr[b].T)
        g_ref[b, :, 0:GATE_ROWS] = hi
        g_ref[b, :, GATE_ROWS:] = lo


def _gatew(hmean, w_k, w_qt, n_batch):
    rows, d = hmean.shape
    n_blk = rows // n_batch
    assert MOBA_HEADS * n_blk <= GATE_ROWS
    return pl.pallas_call(
        functools.partial(_gatew_kernel, n_batch=n_batch, n_blk=n_blk),
        grid=(1,),
        in_specs=[_resident((rows, d)), _resident(w_k.shape), _resident(w_qt.shape)],
        out_specs=pl.BlockSpec((n_batch, d, 2 * GATE_ROWS), lambda i: (0, 0, 0)),
        out_shape=jax.ShapeDtypeStruct((n_batch, d, 2 * GATE_ROWS), BF16),
        scratch_shapes=[pltpu.VMEM((n_batch, GATE_ROWS, d), F32)],
        compiler_params=_params(1),
        name="gatew",
    )(hmean, w_k, w_qt)


def _select_kernel(x_ref, g_ref, gw_ref, mask_ref, *, n_blk, tq):
    j = pl.program_id(1)
    h = _rms(x_ref[0], g_ref[...])
    res = jnp.dot(jnp.concatenate(_split_bf16(h), axis=0), gw_ref[0], preferred_element_type=F32)
    gate = ((res[:tq, :GATE_ROWS] + (res[:tq, GATE_ROWS:] + res[tq:, :GATE_ROWS])) + res[tq:, GATE_ROWS:])
    gate = gate.T[0:MOBA_HEADS * n_blk].reshape(MOBA_HEADS, n_blk, tq)
    pos = j * tq + lax.broadcasted_iota(jnp.int32, gate.shape, 2)
    cur = lax.shift_right_logical(pos, int(math.log2(MOBA_BLOCK)))
    blk = lax.broadcasted_iota(jnp.int32, gate.shape, 1)
    past = blk < cur
    gate = jnp.where(past, gate, NEG)
    rank = jnp.zeros(gate.shape, jnp.int32)
    for m in range(n_blk):
        gm = gate[:, m:m + 1, :]
        tie = jnp.where(blk > m, 1, 0)
        rank = rank + jnp.where(gm > gate, 1, jnp.where(gm == gate, tie, 0))
    mask = jnp.where(rank < min(MOBA_TOPK, n_blk), jnp.where(past, 0.0, NEG), NEG)
    mask_ref[0] = mask.astype(F32).reshape(MOBA_HEADS * n_blk, tq)


def _select(x, g, gt, tq=TOKEN_TILE):
    b, s, d = x.shape
    n_blk = s // MOBA_BLOCK
    return pl.pallas_call(
        functools.partial(_select_kernel, n_blk=n_blk, tq=tq),
        grid=(b, s // tq),
        in_specs=[pl.BlockSpec((1, tq, d), lambda i, j: (i, j, 0)),
                  _resident((1, d)),
                  pl.BlockSpec((1, d, 2 * GATE_ROWS), lambda i, j: (i, 0, 0))],
        out_specs=pl.BlockSpec((1, MOBA_HEADS * n_blk, tq), lambda i, j: (i, 0, j)),
        out_shape=jax.ShapeDtypeStruct((b, MOBA_HEADS * n_blk, s), F32),
        compiler_params=_params(2),
        name="select",
    )(x, g, gt)


def _qkv_kernel(h_ref, wqt_ref, wk_ref, wvt_ref, qt_ref, k_ref, vt_ref):
    h = h_ref[...]
    nt = (((1,), (1,)), ((), ()))
    qt_ref[0] = lax.dot_general(wqt_ref[...], h, nt, preferred_element_type=F32).astype(BF16)
    vt_ref[0] = lax.dot_general(wvt_ref[...], h, nt, preferred_element_type=F32).astype(BF16)
    kf = jnp.dot(h, wk_ref[...], preferred_element_type=F32)
    low = lax.broadcasted_iota(jnp.int32, (h.shape[0], V7X_LANES), 1) < HEAD_DIM
    for p in range(MOBA_HEADS // 2):
        pair = kf[:, p * V7X_LANES:(p + 1) * V7X_LANES]
        even = jnp.where(low, pair, 0.0)
        odd = jnp.where(low, pltpu.roll(pair, HEAD_DIM, 1), 0.0)
        k_ref[0, :, (2 * p) * V7X_LANES:(2 * p + 1) * V7X_LANES] = even.astype(BF16)
        k_ref[0, :, (2 * p + 1) * V7X_LANES:(2 * p + 2) * V7X_LANES] = odd.astype(BF16)


def _qkv(h2, b, s, w_qt, w_k, w_vt, tm=TOKEN_TILE):
    d = h2.shape[-1]
    nt = s // tm
    return pl.pallas_call(
        _qkv_kernel,
        grid=(b, nt),
        in_specs=[pl.BlockSpec((tm, d), lambda i, j: (i * nt + j, 0)),
                  _resident(w_qt.shape), _resident(w_k.shape), _resident(w_vt.shape)],
        out_specs=[pl.BlockSpec((1, MOBA_WIDTH, tm), lambda i, j: (i, 0, j)),
                   pl.BlockSpec((1, tm, MOBA_HEADS * V7X_LANES), lambda i, j: (i, j, 0)),
                   pl.BlockSpec((1, MOBA_WIDTH, tm), lambda i, j: (i, 0, j))],
        out_shape=[jax.ShapeDtypeStruct((b, MOBA_WIDTH, s), BF16),
                   jax.ShapeDtypeStruct((b, s, MOBA_HEADS * V7X_LANES), BF16),
                   jax.ShapeDtypeStruct((b, MOBA_WIDTH, s), BF16)],
        compiler_params=_params(2),
        name="qkv",
    )(h2, w_qt, w_k, w_vt)


def _bias_kernel(rb_ref, out_ref):
    h = pl.program_id(0)
    ik = lax.broadcasted_iota(jnp.int32, (MOBA_BLOCK, MOBA_BLOCK), 0)
    iq = lax.broadcasted_iota(jnp.int32, (MOBA_BLOCK, MOBA_BLOCK), 1)
    max_exact = REL_BUCKETS // 2
    for t in range(2):
        dist = iq - ik + t * MOBA_BLOCK
        n = jnp.maximum(dist, 0)
        nf = jnp.maximum(n, 1).astype(F32)
        large = max_exact + (jnp.log(nf / max_exact) / math.log(REL_MAX_DIST / max_exact)
                             * (REL_BUCKETS - max_exact)).astype(jnp.int32)
        large = jnp.minimum(large, REL_BUCKETS - 1)
        bucket = jnp.where(n < max_exact, n, large)
        val = jnp.zeros((MOBA_BLOCK, MOBA_BLOCK), F32)
        for bk in range(REL_BUCKETS):
            val = jnp.where(bucket == bk, rb_ref[bk * MOBA_HEADS + h], val)
        val = val * LOG2E
        if t == 0:
            val = jnp.where(dist >= 0, val, NEG)
        out_ref[0, t] = val


def _bias_tiles(rb_flat):
    return pl.pallas_call(
        _bias_kernel,
        grid=(MOBA_HEADS,),
        in_specs=[pl.BlockSpec(memory_space=pltpu.SMEM)],
        out_specs=pl.BlockSpec((1, 2, MOBA_BLOCK, MOBA_BLOCK), lambda h: (h, 0, 0, 0)),
        out_shape=jax.ShapeDtypeStruct((MOBA_HEADS, 2, MOBA_BLOCK, MOBA_BLOCK), F32),
        compiler_params=_params(1),
        name="bias",
    )(rb_flat)


def _attn_kernel(rb_ref, qt_ref, k_ref, vt_ref, mask_ref, bias_ref, o_ref, s_scr, *, n_blk):
    h = pl.program_id(1)
    far = rb_ref[(REL_BUCKETS - 1) * MOBA_HEADS + h] * LOG2E

    def scores(j, buf):
        cols = slice(j * MOBA_BLOCK, (j + 1) * MOBA_BLOCK)
        q = qt_ref[0, :, cols]
        qz = jnp.concatenate([q, jnp.zeros_like(q)], axis=0)
        s = jnp.dot(k_ref[0, 0:(j + 1) * MOBA_BLOCK, :], qz, preferred_element_type=F32)
        m = None
        for n in range(j + 1):
            rows = slice(n * MOBA_BLOCK, (n + 1) * MOBA_BLOCK)
            if n == j:
                sn = s[rows] + bias_ref[0, 0]
            elif n == j - 1:
                sn = s[rows] + bias_ref[0, 1] + mask_ref[0, n:n + 1, cols]
            else:
                sn = s[rows] + (mask_ref[0, n:n + 1, cols] + far)
            s_scr[buf, rows, :] = sn
            cm = jnp.max(sn, axis=0, keepdims=True)
            m = cm if m is None else jnp.maximum(m, cm)
        return m

    def finish(j, buf, m):
        cols = slice(j * MOBA_BLOCK, (j + 1) * MOBA_BLOCK)
        nk = (j + 1) * MOBA_BLOCK
        p = jnp.exp2(s_scr[buf, 0:nk, :] - m).astype(BF16)
        vt1 = jnp.concatenate([vt_ref[0, :, 0:nk], jnp.ones((ONES_ROWS, nk), BF16)], axis=0)
        acc = jnp.dot(vt1, p, preferred_element_type=F32)
        o_ref[0, :, cols] = (acc[0:HEAD_DIM] / acc[HEAD_DIM:HEAD_DIM + 1]).astype(BF16)

    pending = []
    for i, j in enumerate(reversed(range(n_blk))):
        buf = i % (SCORE_LOOKAHEAD + 1)
        pending.append((j, buf, scores(j, buf)))
        if len(pending) > SCORE_LOOKAHEAD:
            finish(*pending.pop(0))
    for item in pending:
        finish(*item)


def _attn(rb_flat, qt, k, vt, mask, bias):
    b, _, s = qt.shape
    n_blk = s // MOBA_BLOCK
    return pl.pallas_call(
        functools.partial(_attn_kernel, n_blk=n_blk),
        grid=(b, MOBA_HEADS),
        in_specs=[pl.BlockSpec(memory_space=pltpu.SMEM),
                  pl.BlockSpec((1, HEAD_DIM, s), lambda i, h: (i, h, 0)),
                  pl.BlockSpec((1, s, V7X_LANES), lambda i, h: (i, 0, h)),
                  pl.BlockSpec((1, HEAD_DIM, s), lambda i, h: (i, h, 0)),
                  pl.BlockSpec((1, n_blk, s), lambda i, h: (i, h, 0)),
                  pl.BlockSpec((1, 2, MOBA_BLOCK, MOBA_BLOCK), lambda i, h: (h, 0, 0, 0))],
        out_specs=pl.BlockSpec((1, HEAD_DIM, s), lambda i, h: (i, h, 0)),
        out_shape=jax.ShapeDtypeStruct((b, MOBA_WIDTH, s), BF16),
        scratch_shapes=[pltpu.VMEM((SCORE_LOOKAHEAD + 1, s, MOBA_BLOCK), F32)],
        compiler_params=_params(2),
        name="attn",
    )(rb_flat, qt, k, vt, mask, bias)


def _gmlp_kernel(h_ref, wu_ref, wv_ref, lng_ref, lnb_ref, ws_ref, bs_ref, o_ref):
    h = h_ref[...]
    u = jax.nn.gelu(jnp.dot(h, wu_ref[...], preferred_element_type=F32))
    v = jax.nn.gelu(jnp.dot(h, wv_ref[...], preferred_element_type=F32))
    mu = jnp.mean(v, axis=-1, keepdims=True)
    vc = v - mu
    vn = vc * lax.rsqrt(jnp.mean(vc * vc, axis=-1, keepdims=True) + EPS) * lng_ref[...] + lnb_ref[...]
    vn = vn.astype(BF16)
    causal = (lax.broadcasted_iota(jnp.int32, (CHUNK, CHUNK), 0)
              >= lax.broadcasted_iota(jnp.int32, (CHUNK, CHUNK), 1))
    for g in range(GMLP_GROUPS):
        ws = jnp.where(causal, ws_ref[g], 0.0).astype(BF16)
        lanes = slice(g * GMLP_GROUP_DIM, (g + 1) * GMLP_GROUP_DIM)
        for c in range(h.shape[0] // CHUNK):
            toks = slice(c * CHUNK, (c + 1) * CHUNK)
            mixed = jnp.dot(ws, vn[toks, lanes], preferred_element_type=F32) + bs_ref[g]
            o_ref[toks, lanes] = (u[toks, lanes] * mixed).astype(BF16)


def _gmlp(h2, w_u, w_v, ln_g, ln_b, w_s, b_s_lanes, tm=TOKEN_TILE):
    n_tok, d = h2.shape
    return pl.pallas_call(
        _gmlp_kernel,
        grid=(n_tok // tm,),
        in_specs=[pl.BlockSpec((tm, d), lambda i: (i, 0)),
                  _resident(w_u.shape), _resident(w_v.shape),
                  _resident(ln_g.shape), _resident(ln_b.shape),
                  _resident(w_s.shape), _resident(b_s_lanes.shape)],
        out_specs=pl.BlockSpec((tm, GMLP_WIDTH), lambda i: (i, 0)),
        out_shape=jax.ShapeDtypeStruct((n_tok, GMLP_WIDTH), BF16),
        compiler_params=_params(1),
        name="gmlp",
    )(h2, w_u, w_v, ln_g, ln_b, w_s, b_s_lanes)


def _memkv_kernel(mem_ref, g_ref, wkt_ref, wv_ref, kt_ref, v_ref):
    mn = _rms(mem_ref[0], g_ref[...]).astype(BF16)
    kt_ref[0] = lax.dot_general(wkt_ref[...], mn, (((1,), (1,)), ((), ())),
                                preferred_element_type=F32).astype(BF16)
    v_ref[0] = jnp.dot(mn, wv_ref[...], preferred_element_type=F32).astype(BF16)


def _memkv(mem, g, w_kt, w_v):
    b, m, d = mem.shape
    return pl.pallas_call(
        _memkv_kernel,
        grid=(b,),
        in_specs=[pl.BlockSpec((1, m, d), lambda i: (i, 0, 0)), _resident((1, d)),
                  _resident(w_kt.shape), _resident(w_v.shape)],
        out_specs=[pl.BlockSpec((1, MEM_WIDTH, m), lambda i: (i, 0, 0)),
                   pl.BlockSpec((1, m, MEM_WIDTH), lambda i: (i, 0, 0))],
        out_shape=[jax.ShapeDtypeStruct((b, MEM_WIDTH, m), BF16),
                   jax.ShapeDtypeStruct((b, m, MEM_WIDTH), BF16)],
        compiler_params=_params(1),
        name="memkv",
    )(mem, g, w_kt, w_v)


def _memattn_kernel(h_ref, wq_ref, kt_ref, v_ref, o_ref):
    cq = jnp.dot(h_ref[...], wq_ref[...], preferred_element_type=F32).astype(BF16)
    for hd in range(MEM_HEADS):
        sl = slice(hd * MEM_HEAD_DIM, (hd + 1) * MEM_HEAD_DIM)
        s = jnp.dot(cq[:, sl], kt_ref[0, sl, :], preferred_element_type=F32) * (MEM_HEAD_DIM ** -0.5)
        p = jnp.exp(s - jnp.max(s, axis=-1, keepdims=True))
        l = jnp.sum(p, axis=-1, keepdims=True)
        o = jnp.dot(p.astype(BF16), v_ref[0, :, sl], preferred_element_type=F32) / l
        o_ref[:, sl] = o.astype(BF16)


def _memattn(h2, b, s, w_cq, kt, v, tm=TOKEN_TILE):
    d = h2.shape[-1]
    nt = s // tm
    m = kt.shape[-1]
    return pl.pallas_call(
        _memattn_kernel,
        grid=(b, nt),
        in_specs=[pl.BlockSpec((tm, d), lambda i, j: (i * nt + j, 0)),
                  _resident(w_cq.shape),
                  pl.BlockSpec((1, MEM_WIDTH, m), lambda i, j: (i, 0, 0)),
                  pl.BlockSpec((1, m, MEM_WIDTH), lambda i, j: (i, 0, 0))],
        out_specs=pl.BlockSpec((tm, MEM_WIDTH), lambda i, j: (i * nt + j, 0)),
        out_shape=jax.ShapeDtypeStruct((b * s, MEM_WIDTH), BF16),
        compiler_params=_params(2),
        name="memattn",
    )(h2, w_cq, kt, v)


def _merge_kernel(x_ref, h_ref, a_ref, bt_ref, c_ref, wg_ref, wa_ref, wb_ref, wc_ref, wo_ref,
                  lnpost_ref, lnpre_ref, x1_ref, h2_ref):
    d = x_ref.shape[-1]
    gates = jax.nn.sigmoid(jnp.dot(h_ref[...], wg_ref[...], preferred_element_type=F32))
    pa = jnp.dot(a_ref[...], wa_ref[...], preferred_element_type=F32)
    pb = lax.dot_general(bt_ref[0], wb_ref[...], (((0,), (0,)), ((), ())),
                         preferred_element_type=F32)
    pc = jnp.dot(c_ref[...], wc_ref[...], preferred_element_type=F32)
    merged = gates[:, :d] * pa + gates[:, d:2 * d] * pb + gates[:, 2 * d:] * pc
    mo = jnp.dot(merged.astype(BF16), wo_ref[...], preferred_element_type=F32)
    x1 = x_ref[...] + _rms(mo, lnpost_ref[...])
    x1_ref[...] = x1
    h2_ref[...] = _rms(x1, lnpre_ref[...]).astype(BF16)


def _merge(x2, h2, a_out, b_out_t, c_out, w_g, w_a, w_b, w_c, w_o, ln_post, ln_pre, tm=256):
    n_tok, d = x2.shape
    b, _, s = b_out_t.shape
    nt = s // tm
    tok = lambda i, j: (i * nt + j, 0)
    return pl.pallas_call(
        _merge_kernel,
        grid=(b, nt),
        in_specs=[pl.BlockSpec((tm, d), tok), pl.BlockSpec((tm, d), tok),
                  pl.BlockSpec((tm, GMLP_WIDTH), tok),
                  pl.BlockSpec((1, MOBA_WIDTH, tm), lambda i, j: (i, 0, j)),
                  pl.BlockSpec((tm, MEM_WIDTH), tok),
                  _resident(w_g.shape), _resident(w_a.shape), _resident(w_b.shape),
                  _resident(w_c.shape), _resident(w_o.shape),
                  _resident((1, d)), _resident((1, d))],
        out_specs=[pl.BlockSpec((tm, d), tok), pl.BlockSpec((tm, d), tok)],
        out_shape=[jax.ShapeDtypeStruct((n_tok, d), F32), jax.ShapeDtypeStruct((n_tok, d), BF16)],
        compiler_params=_params(2),
        name="merge",
    )(x2, h2, a_out, b_out_t, c_out, w_g, w_a, w_b, w_c, w_o, ln_post, ln_pre)


def _ffn_kernel(x1_ref, h2_ref, wg_ref, wu_ref, wd_ref, ln_ref, o_ref):
    h2 = h2_ref[...]
    g = jnp.dot(h2, wg_ref[...], preferred_element_type=F32)
    u = jnp.dot(h2, wu_ref[...], preferred_element_type=F32)
    act = (jax.nn.silu(g) * u).astype(BF16)
    f = jnp.dot(act, wd_ref[...], preferred_element_type=F32)
    o_ref[...] = x1_ref[...] + _rms(f, ln_ref[...])


def _ffn(x1, h2, w_g, w_u, w_d, ln_post, tm=256):
    n_tok, d = x1.shape
    return pl.pallas_call(
        _ffn_kernel,
        grid=(n_tok // tm,),
        in_specs=[pl.BlockSpec((tm, d), lambda i: (i, 0)), pl.BlockSpec((tm, d), lambda i: (i, 0)),
                  _resident(w_g.shape), _resident(w_u.shape), _resident(w_d.shape),
                  _resident((1, d))],
        out_specs=pl.BlockSpec((tm, d), lambda i: (i, 0)),
        out_shape=jax.ShapeDtypeStruct((n_tok, d), F32),
        compiler_params=_params(1),
        name="ffn",
    )(x1, h2, w_g, w_u, w_d, ln_post)


def kernel(x, mem, ln_mix_pre, ln_mix_post, ln_ffn_pre, ln_ffn_post, ln_mem, w_in, ln_v_gain, ln_v_bias,
           w_spatial, b_spatial, rel_bias, w_mem_kv, w_branch_a, w_branch_b, w_branch_c, w_out,
           w_ffn_gate, w_ffn_up, w_ffn_down):
    b, s, d = x.shape
    assert s % MOBA_BLOCK == 0 and s % TOKEN_TILE == 0 and d % V7X_LANES == 0
    depth = w_in.shape[0]
    cuts = [0, GMLP_WIDTH, 2 * GMLP_WIDTH, 2 * GMLP_WIDTH + MOBA_WIDTH, 2 * GMLP_WIDTH + 2 * MOBA_WIDTH,
            2 * GMLP_WIDTH + 3 * MOBA_WIDTH, 2 * GMLP_WIDTH + 3 * MOBA_WIDTH + MEM_WIDTH]
    rb_flat = rel_bias.astype(F32).reshape(-1)
    bias = _bias_tiles(rb_flat)
    row = lambda v: v.reshape(1, -1).astype(F32)
    for l in range(depth):
        wi = w_in[l]
        w_u, w_v, w_q, w_k, w_v2, w_cq = (wi[:, cuts[i]:cuts[i + 1]] for i in range(6))
        w_g = wi[:, cuts[6]:]
        x2 = x.reshape(b * s, d)

        h2, hmean = _prep(x2, row(ln_mix_pre[l]))
        w_qt_scaled = (w_q * (HEAD_DIM ** -0.5)).T
        gt = _gatew(hmean.reshape(-1, d), w_k, w_qt_scaled, b)
        mask = _select(x, row(ln_mix_pre[l]), gt)
        qt, k, vt = _qkv(h2, b, s, (w_qt_scaled * LOG2E).astype(BF16), w_k.astype(BF16),
                         w_v2.T.astype(BF16))
        b_out_t = _attn(rb_flat, qt, k, vt, mask, bias)

        b_s_lanes = jnp.broadcast_to(b_spatial[l][:, :, None], (GMLP_GROUPS, CHUNK, GMLP_GROUP_DIM))
        a_out = _gmlp(h2, w_u.astype(BF16), w_v.astype(BF16), row(ln_v_gain[l]), row(ln_v_bias[l]),
                      w_spatial[l], b_s_lanes.astype(F32))

        wkv = w_mem_kv[l]
        kt_mem, v_mem = _memkv(mem, row(ln_mem[l]), wkv[:, :MEM_WIDTH].T.astype(BF16),
                               wkv[:, MEM_WIDTH:].astype(BF16))
        c_out = _memattn(h2, b, s, w_cq.astype(BF16), kt_mem, v_mem)

        x1, hn = _merge(x2, h2, a_out, b_out_t, c_out, w_g.astype(BF16), w_branch_a[l].astype(BF16),
                        w_branch_b[l].astype(BF16), w_branch_c[l].astype(BF16), w_out[l].astype(BF16),
                        row(ln_mix_post[l]), row(ln_ffn_pre[l]))
        out = _ffn(x1, hn, w_ffn_gate[l].astype(BF16), w_ffn_up[l].astype(BF16),
                   w_ffn_down[l].astype(BF16), row(ln_ffn_post[l]))
        x = out.reshape(b, s, d)
    return x
```

```python
import functools
import math

import jax
import jax.numpy as jnp
from jax import lax
from jax.experimental import pallas as pl
from jax.experimental.pallas import tpu as pltpu

F32 = jnp.float32
BF16 = jnp.bfloat16

EPS = 1e-6
NEG = -1e30
GMLP_GROUPS = 6
GMLP_GROUP_DIM = 128
GMLP_WIDTH = GMLP_GROUPS * GMLP_GROUP_DIM
CHUNK = 128
MOBA_HEADS = 12
HEAD_DIM = 64
MOBA_WIDTH = MOBA_HEADS * HEAD_DIM
MOBA_BLOCK = 256
MOBA_TOPK = 3
REL_BUCKETS = 32
REL_MAX_DIST = 128
LOG2E = math.log2(math.e)
ONES_ROWS = 16
ATTN_HEADS = 2
SCORE_LOOKAHEAD = 4
MEM_HEADS = 4
MEM_HEAD_DIM = 128
MEM_WIDTH = MEM_HEADS * MEM_HEAD_DIM
N_BRANCHES = 3

V7X_LANES = 128
V7X_VMEM_LIMIT = 56 * 1024 * 1024

TOKEN_TILE = 512
PREP_BLOCKS = 4
GATE_ROWS = 128


def _params(n_axes, vmem=V7X_VMEM_LIMIT):
    return pltpu.CompilerParams(
        dimension_semantics=("arbitrary",) * n_axes, vmem_limit_bytes=vmem)


def _resident(shape):
    zeros = (0,) * len(shape)
    return pl.BlockSpec(shape, lambda *_: zeros, pipeline_mode=pl.Buffered(1))


def _rms(x, g):
    return x * lax.rsqrt(jnp.mean(x * x, axis=-1, keepdims=True) + EPS) * g


def _prep_kernel(x_ref, g_ref, h_ref, hmean_ref):
    h = _rms(x_ref[...], g_ref[...])
    h_ref[...] = h.astype(BF16)
    for i in range(PREP_BLOCKS):
        hmean_ref[i] = jnp.mean(h[i * MOBA_BLOCK:(i + 1) * MOBA_BLOCK], axis=0, keepdims=True)


def _prep(x2, g):
    n_tok, d = x2.shape
    n_blk = n_tok // MOBA_BLOCK
    tm = PREP_BLOCKS * MOBA_BLOCK
    return pl.pallas_call(
        _prep_kernel,
        grid=(n_tok // tm,),
        in_specs=[pl.BlockSpec((tm, d), lambda i: (i, 0)), _resident((1, d))],
        out_specs=[pl.BlockSpec((tm, d), lambda i: (i, 0)),
                   pl.BlockSpec((PREP_BLOCKS, 1, d), lambda i: (i, 0, 0))],
        out_shape=[jax.ShapeDtypeStruct((n_tok, d), BF16),
                   jax.ShapeDtypeStruct((n_blk, 1, d), F32)],
        compiler_params=_params(1),
        name="prep",
    )(x2, g)


def _split_bf16(x):
    hi = x.astype(BF16)
    return hi, (x - hi.astype(F32)).astype(BF16)


def _gatew_kernel(hm_ref, wk_ref, wqt_ref, g_ref, gt_scr, *, n_batch, n_blk):
    kmean = jnp.dot(hm_ref[...], wk_ref[...], precision=lax.Precision.HIGHEST,
                    preferred_element_type=F32)
    used = MOBA_HEADS * n_blk
    gt_scr[:, used:, :] = jnp.zeros((n_batch, GATE_ROWS - used, gt_scr.shape[-1]), F32)
    for h in range(MOBA_HEADS):
        sl = slice(h * HEAD_DIM, (h + 1) * HEAD_DIM)
        res = jnp.dot(kmean[:, sl], wqt_ref[sl, :], precision=lax.Precision.HIGHEST,
                      preferred_element_type=F32)
        for b in range(n_batch):
            gt_scr[b, h * n_blk:(h + 1) * n_blk, :] = res[b * n_blk:(b + 1) * n_blk, :]
    for b in range(n_batch):
        hi, lo = _split_bf16(gt_scr[b].T)
        g_ref[b, :, 0:GATE_ROWS] = hi
        g_ref[b, :, GATE_ROWS:] = lo


def _gatew(hmean, w_k, w_qt, n_batch):
    rows, d = hmean.shape
    n_blk = rows // n_batch
    assert MOBA_HEADS * n_blk <= GATE_ROWS
    return pl.pallas_call(
        functools.partial(_gatew_kernel, n_batch=n_batch, n_blk=n_blk),
        grid=(1,),
        in_specs=[_resident((rows, d)), _resident(w_k.shape), _resident(w_qt.shape)],
        out_specs=pl.BlockSpec((n_batch, d, 2 * GATE_ROWS), lambda i: (0, 0, 0)),
        out_shape=jax.ShapeDtypeStruct((n_batch, d, 2 * GATE_ROWS), BF16),
        scratch_shapes=[pltpu.VMEM((n_batch, GATE_ROWS, d), F32)],
        compiler_params=_params(1),
        name="gatew",
    )(hmean, w_k, w_qt)


def _select_kernel(x_ref, g_ref, gw_ref, mask_ref, *, n_blk, tq):
    j = pl.program_id(1)
    h = _rms(x_ref[0], g_ref[...])
    res = jnp.dot(jnp.concatenate(_split_bf16(h), axis=0), gw_ref[0], preferred_element_type=F32)
    gate = ((res[:tq, :GATE_ROWS] + (res[:tq, GATE_ROWS:] + res[tq:, :GATE_ROWS])) + res[tq:, GATE_ROWS:])
    gate = gate.T[0:MOBA_HEADS * n_blk].reshape(MOBA_HEADS, n_blk, tq)
    pos = j * tq + lax.broadcasted_iota(jnp.int32, gate.shape, 2)
    cur = lax.shift_right_logical(pos, int(math.log2(MOBA_BLOCK)))
    blk = lax.broadcasted_iota(jnp.int32, gate.shape, 1)
    past = blk < cur
    gate = jnp.where(past, gate, NEG)
    rank = jnp.zeros(gate.shape, jnp.int32)
    for m in range(n_blk):
        gm = gate[:, m:m + 1, :]
        tie = jnp.where(blk > m, 1, 0)
        rank = rank + jnp.where(gm > gate, 1, jnp.where(gm == gate, tie, 0))
    mask = jnp.where(rank < min(MOBA_TOPK, n_blk), jnp.where(past, 0.0, NEG), NEG)
    mask_ref[0] = mask.astype(F32).reshape(MOBA_HEADS * n_blk, tq)


def _select(x, g, gt, tq=TOKEN_TILE):
    b, s, d = x.shape
    n_blk = s // MOBA_BLOCK
    return pl.pallas_call(
        functools.partial(_select_kernel, n_blk=n_blk, tq=tq),
        grid=(b, s // tq),
        in_specs=[pl.BlockSpec((1, tq, d), lambda i, j: (i, j, 0)),
                  _resident((1, d)),
                  pl.BlockSpec((1, d, 2 * GATE_ROWS), lambda i, j: (i, 0, 0))],
        out_specs=pl.BlockSpec((1, MOBA_HEADS * n_blk, tq), lambda i, j: (i, 0, j)),
        out_shape=jax.ShapeDtypeStruct((b, MOBA_HEADS * n_blk, s), F32),
        compiler_params=_params(2),
        name="select",
    )(x, g, gt)


def _qkv_kernel(h_ref, wqt_ref, wk_ref, wvt_ref, qt_ref, k_ref, vt_ref):
    h = h_ref[...]
    nt = (((1,), (1,)), ((), ()))
    qt_ref[0] = lax.dot_general(wqt_ref[...], h, nt, preferred_element_type=F32).astype(BF16)
    vt_ref[0] = lax.dot_general(wvt_ref[...], h, nt, preferred_element_type=F32).astype(BF16)
    kf = jnp.dot(h, wk_ref[...], preferred_element_type=F32)
    low = lax.broadcasted_iota(jnp.int32, (h.shape[0], V7X_LANES), 1) < HEAD_DIM
    for p in range(MOBA_HEADS // 2):
        pair = kf[:, p * V7X_LANES:(p + 1) * V7X_LANES]
        even = jnp.where(low, pair, 0.0)
        odd = jnp.where(low, pltpu.roll(pair, HEAD_DIM, 1), 0.0)
        k_ref[0, :, (2 * p) * V7X_LANES:(2 * p + 1) * V7X_LANES] = even.astype(BF16)
        k_ref[0, :, (2 * p + 1) * V7X_LANES:(2 * p + 2) * V7X_LANES] = odd.astype(BF16)


def _qkv(h2, b, s, w_qt, w_k, w_vt, tm=TOKEN_TILE):
    d = h2.shape[-1]
    nt = s // tm
    return pl.pallas_call(
        _qkv_kernel,
        grid=(b, nt),
        in_specs=[pl.BlockSpec((tm, d), lambda i, j: (i * nt + j, 0)),
                  _resident(w_qt.shape), _resident(w_k.shape), _resident(w_vt.shape)],
        out_specs=[pl.BlockSpec((1, MOBA_WIDTH, tm), lambda i, j: (i, 0, j)),
                   pl.BlockSpec((1, tm, MOBA_HEADS * V7X_LANES), lambda i, j: (i, j, 0)),
                   pl.BlockSpec((1, MOBA_WIDTH, tm), lambda i, j: (i, 0, j))],
        out_shape=[jax.ShapeDtypeStruct((b, MOBA_WIDTH, s), BF16),
                   jax.ShapeDtypeStruct((b, s, MOBA_HEADS * V7X_LANES), BF16),
                   jax.ShapeDtypeStruct((b, MOBA_WIDTH, s), BF16)],
        compiler_params=_params(2),
        name="qkv",
    )(h2, w_qt, w_k, w_vt)


def _bias_kernel(rb_ref, out_ref):
    h = pl.program_id(0)
    ik = lax.broadcasted_iota(jnp.int32, (MOBA_BLOCK, MOBA_BLOCK), 0)
    iq = lax.broadcasted_iota(jnp.int32, (MOBA_BLOCK, MOBA_BLOCK), 1)
    max_exact = REL_BUCKETS // 2
    for t in range(2):
        dist = iq - ik + t * MOBA_BLOCK
        n = jnp.maximum(dist, 0)
        nf = jnp.maximum(n, 1).astype(F32)
        large = max_exact + (jnp.log(nf / max_exact) / math.log(REL_MAX_DIST / max_exact)
                             * (REL_BUCKETS - max_exact)).astype(jnp.int32)
        large = jnp.minimum(large, REL_BUCKETS - 1)
        bucket = jnp.where(n < max_exact, n, large)
        val = jnp.zeros((MOBA_BLOCK, MOBA_BLOCK), F32)
        for bk in range(REL_BUCKETS):
            val = jnp.where(bucket == bk, rb_ref[bk * MOBA_HEADS + h], val)
        val = val * LOG2E
        if t == 0:
            val = jnp.where(dist >= 0, val, NEG)
        out_ref[0, t] = val


def _bias_tiles(rb_flat):
    return pl.pallas_call(
        _bias_kernel,
        grid=(MOBA_HEADS,),
        in_specs=[pl.BlockSpec(memory_space=pltpu.SMEM)],
        out_specs=pl.BlockSpec((1, 2, MOBA_BLOCK, MOBA_BLOCK), lambda h: (h, 0, 0, 0)),
        out_shape=jax.ShapeDtypeStruct((MOBA_HEADS, 2, MOBA_BLOCK, MOBA_BLOCK), F32),
        compiler_params=_params(1),
        name="bias",
    )(rb_flat)


def _attn_kernel(rb_ref, qt_ref, k_ref, vt_ref, mask_ref, bias_ref, o_ref, s_scr, *, n_blk):
    first_head = pl.program_id(1) * ATTN_HEADS
    far = [rb_ref[(REL_BUCKETS - 1) * MOBA_HEADS + first_head + hh] * LOG2E for hh in range(ATTN_HEADS)]

    def scores(hh, j, buf):
        cols = slice(j * MOBA_BLOCK, (j + 1) * MOBA_BLOCK)
        q = qt_ref[0, hh * HEAD_DIM:(hh + 1) * HEAD_DIM, cols]
        qz = jnp.concatenate([q, jnp.zeros_like(q)], axis=0)
        k = k_ref[0, 0:(j + 1) * MOBA_BLOCK, hh * V7X_LANES:(hh + 1) * V7X_LANES]
        s = jnp.dot(k, qz, preferred_element_type=F32)
        for n in range(j + 1):
            rows = slice(n * MOBA_BLOCK, (n + 1) * MOBA_BLOCK)
            mrow = mask_ref[0, hh * n_blk + n:hh * n_blk + n + 1, cols]
            if n == j:
                sn = s[rows] + bias_ref[hh, 0]
            elif n == j - 1:
                sn = s[rows] + bias_ref[hh, 1] + mrow
            else:
                sn = s[rows] + (mrow + far[hh])
            s_scr[buf, rows, :] = sn

    def finish(hh, j, buf):
        cols = slice(j * MOBA_BLOCK, (j + 1) * MOBA_BLOCK)
        heads = slice(hh * HEAD_DIM, (hh + 1) * HEAD_DIM)
        nk = (j + 1) * MOBA_BLOCK
        m = jnp.max(s_scr[buf, 0:nk, :], axis=0, keepdims=True)
        p = jnp.exp2(s_scr[buf, 0:nk, :] - m).astype(BF16)
        vt1 = jnp.concatenate([vt_ref[0, heads, 0:nk], jnp.ones((ONES_ROWS, nk), BF16)], axis=0)
        acc = jnp.dot(vt1, p, preferred_element_type=F32)
        o_ref[0, heads, cols] = (acc[0:HEAD_DIM] / acc[HEAD_DIM:HEAD_DIM + 1]).astype(BF16)

    tasks = [(hh, j) for j in reversed(range(n_blk)) for hh in range(ATTN_HEADS)]
    pending = []
    for i, (hh, j) in enumerate(tasks):
        buf = i % (SCORE_LOOKAHEAD + 1)
        scores(hh, j, buf)
        pending.append((hh, j, buf))
        if len(pending) > SCORE_LOOKAHEAD:
            finish(*pending.pop(0))
    for item in pending:
        finish(*item)


def _attn(rb_flat, qt, k, vt, mask, bias):
    b, _, s = qt.shape
    n_blk = s // MOBA_BLOCK
    nh = ATTN_HEADS
    return pl.pallas_call(
        functools.partial(_attn_kernel, n_blk=n_blk),
        grid=(b, MOBA_HEADS // nh),
        in_specs=[pl.BlockSpec(memory_space=pltpu.SMEM),
                  pl.BlockSpec((1, nh * HEAD_DIM, s), lambda i, h: (i, h, 0)),
                  pl.BlockSpec((1, s, nh * V7X_LANES), lambda i, h: (i, 0, h)),
                  pl.BlockSpec((1, nh * HEAD_DIM, s), lambda i, h: (i, h, 0)),
                  pl.BlockSpec((1, nh * n_blk, s), lambda i, h: (i, h, 0)),
                  pl.BlockSpec((nh, 2, MOBA_BLOCK, MOBA_BLOCK), lambda i, h: (h, 0, 0, 0))],
        out_specs=pl.BlockSpec((1, nh * HEAD_DIM, s), lambda i, h: (i, h, 0)),
        out_shape=jax.ShapeDtypeStruct((b, MOBA_WIDTH, s), BF16),
        scratch_shapes=[pltpu.VMEM((SCORE_LOOKAHEAD + 1, s, MOBA_BLOCK), F32)],
        compiler_params=_params(2),
        name="attn",
    )(rb_flat, qt, k, vt, mask, bias)


def _gmlp_kernel(h_ref, wu_ref, wv_ref, lng_ref, lnb_ref, ws_ref, bs_ref, o_ref):
    h = h_ref[...]
    u = jax.nn.gelu(jnp.dot(h, wu_ref[...], preferred_element_type=F32))
    v = jax.nn.gelu(jnp.dot(h, wv_ref[...], preferred_element_type=F32))
    mu = jnp.mean(v, axis=-1, keepdims=True)
    vc = v - mu
    vn = vc * lax.rsqrt(jnp.mean(vc * vc, axis=-1, keepdims=True) + EPS) * lng_ref[...] + lnb_ref[...]
    vn = vn.astype(BF16)
    causal = (lax.broadcasted_iota(jnp.int32, (CHUNK, CHUNK), 0)
              >= lax.broadcasted_iota(jnp.int32, (CHUNK, CHUNK), 1))
    for g in range(GMLP_GROUPS):
        ws = jnp.where(causal, ws_ref[g], 0.0).astype(BF16)
        lanes = slice(g * GMLP_GROUP_DIM, (g + 1) * GMLP_GROUP_DIM)
        for c in range(h.shape[0] // CHUNK):
            toks = slice(c * CHUNK, (c + 1) * CHUNK)
            mixed = jnp.dot(ws, vn[toks, lanes], preferred_element_type=F32) + bs_ref[g]
            o_ref[toks, lanes] = (u[toks, lanes] * mixed).astype(BF16)


def _gmlp(h2, w_u, w_v, ln_g, ln_b, w_s, b_s_lanes, tm=TOKEN_TILE):
    n_tok, d = h2.shape
    return pl.pallas_call(
        _gmlp_kernel,
        grid=(n_tok // tm,),
        in_specs=[pl.BlockSpec((tm, d), lambda i: (i, 0)),
                  _resident(w_u.shape), _resident(w_v.shape),
                  _resident(ln_g.shape), _resident(ln_b.shape),
                  _resident(w_s.shape), _resident(b_s_lanes.shape)],
        out_specs=pl.BlockSpec((tm, GMLP_WIDTH), lambda i: (i, 0)),
        out_shape=jax.ShapeDtypeStruct((n_tok, GMLP_WIDTH), BF16),
        compiler_params=_params(1),
        name="gmlp",
    )(h2, w_u, w_v, ln_g, ln_b, w_s, b_s_lanes)


def _memkv_kernel(mem_ref, g_ref, wkt_ref, wv_ref, kt_ref, v_ref):
    mn = _rms(mem_ref[0], g_ref[...]).astype(BF16)
    kt_ref[0] = lax.dot_general(wkt_ref[...], mn, (((1,), (1,)), ((), ())),
                                preferred_element_type=F32).astype(BF16)
    v_ref[0] = jnp.dot(mn, wv_ref[...], preferred_element_type=F32).astype(BF16)


def _memkv(mem, g, w_kt, w_v):
    b, m, d = mem.shape
    return pl.pallas_call(
        _memkv_kernel,
        grid=(b,),
        in_specs=[pl.BlockSpec((1, m, d), lambda i: (i, 0, 0)), _resident((1, d)),
                  _resident(w_kt.shape), _resident(w_v.shape)],
        out_specs=[pl.BlockSpec((1, MEM_WIDTH, m), lambda i: (i, 0, 0)),
                   pl.BlockSpec((1, m, MEM_WIDTH), lambda i: (i, 0, 0))],
        out_shape=[jax.ShapeDtypeStruct((b, MEM_WIDTH, m), BF16),
                   jax.ShapeDtypeStruct((b, m, MEM_WIDTH), BF16)],
        compiler_params=_params(1),
        name="memkv",
    )(mem, g, w_kt, w_v)


def _memattn_kernel(h_ref, wq_ref, kt_ref, v_ref, o_ref):
    cq = jnp.dot(h_ref[...], wq_ref[...], preferred_element_type=F32).astype(BF16)
    for hd in range(MEM_HEADS):
        sl = slice(hd * MEM_HEAD_DIM, (hd + 1) * MEM_HEAD_DIM)
        s = jnp.dot(cq[:, sl], kt_ref[0, sl, :], preferred_element_type=F32) * (MEM_HEAD_DIM ** -0.5)
        p = jnp.exp(s - jnp.max(s, axis=-1, keepdims=True))
        l = jnp.sum(p, axis=-1, keepdims=True)
        o = jnp.dot(p.astype(BF16), v_ref[0, :, sl], preferred_element_type=F32) / l
        o_ref[:, sl] = o.astype(BF16)


def _memattn(h2, b, s, w_cq, kt, v, tm=TOKEN_TILE):
    d = h2.shape[-1]
    nt = s // tm
    m = kt.shape[-1]
    return pl.pallas_call(
        _memattn_kernel,
        grid=(b, nt),
        in_specs=[pl.BlockSpec((tm, d), lambda i, j: (i * nt + j, 0)),
                  _resident(w_cq.shape),
                  pl.BlockSpec((1, MEM_WIDTH, m), lambda i, j: (i, 0, 0)),
                  pl.BlockSpec((1, m, MEM_WIDTH), lambda i, j: (i, 0, 0))],
        out_specs=pl.BlockSpec((tm, MEM_WIDTH), lambda i, j: (i * nt + j, 0)),
        out_shape=jax.ShapeDtypeStruct((b * s, MEM_WIDTH), BF16),
        compiler_params=_params(2),
        name="memattn",
    )(h2, w_cq, kt, v)


def _merge_kernel(x_ref, h_ref, a_ref, bt_ref, c_ref, wg_ref, wa_ref, wb_ref, wc_ref, wo_ref,
                  lnpost_ref, lnpre_ref, x1_ref, h2_ref):
    d = x_ref.shape[-1]
    gates = jax.nn.sigmoid(jnp.dot(h_ref[...], wg_ref[...], preferred_element_type=F32))
    pa = jnp.dot(a_ref[...], wa_ref[...], preferred_element_type=F32)
    pb = lax.dot_general(bt_ref[0], wb_ref[...], (((0,), (0,)), ((), ())),
                         preferred_element_type=F32)
    pc = jnp.dot(c_ref[...], wc_ref[...], preferred_element_type=F32)
    merged = gates[:, :d] * pa + gates[:, d:2 * d] * pb + gates[:, 2 * d:] * pc
    mo = jnp.dot(merged.astype(BF16), wo_ref[...], preferred_element_type=F32)
    x1 = x_ref[...] + _rms(mo, lnpost_ref[...])
    x1_ref[...] = x1
    h2_ref[...] = _rms(x1, lnpre_ref[...]).astype(BF16)


def _merge(x2, h2, a_out, b_out_t, c_out, w_g, w_a, w_b, w_c, w_o, ln_post, ln_pre, tm=256):
    n_tok, d = x2.shape
    b, _, s = b_out_t.shape
    nt = s // tm
    tok = lambda i, j: (i * nt + j, 0)
    return pl.pallas_call(
        _merge_kernel,
        grid=(b, nt),
        in_specs=[pl.BlockSpec((tm, d), tok), pl.BlockSpec((tm, d), tok),
                  pl.BlockSpec((tm, GMLP_WIDTH), tok),
                  pl.BlockSpec((1, MOBA_WIDTH, tm), lambda i, j: (i, 0, j)),
                  pl.BlockSpec((tm, MEM_WIDTH), tok),
                  _resident(w_g.shape), _resident(w_a.shape), _resident(w_b.shape),
                  _resident(w_c.shape), _resident(w_o.shape),
                  _resident((1, d)), _resident((1, d))],
        out_specs=[pl.BlockSpec((tm, d), tok), pl.BlockSpec((tm, d), tok)],
        out_shape=[jax.ShapeDtypeStruct((n_tok, d), F32), jax.ShapeDtypeStruct((n_tok, d), BF16)],
        compiler_params=_params(2),
        name="merge",
    )(x2, h2, a_out, b_out_t, c_out, w_g, w_a, w_b, w_c, w_o, ln_post, ln_pre)


def _ffn_kernel(x1_ref, h2_ref, wg_ref, wu_ref, wd_ref, ln_ref, o_ref):
    h2 = h2_ref[...]
    g = jnp.dot(h2, wg_ref[...], preferred_element_type=F32)
    u = jnp.dot(h2, wu_ref[...], preferred_element_type=F32)
    act = (jax.nn.silu(g) * u).astype(BF16)
    f = jnp.dot(act, wd_ref[...], preferred_element_type=F32)
    o_ref[...] = x1_ref[...] + _rms(f, ln_ref[...])


def _ffn(x1, h2, w_g, w_u, w_d, ln_post, tm=256):
    n_tok, d = x1.shape
    return pl.pallas_call(
        _ffn_kernel,
        grid=(n_tok // tm,),
        in_specs=[pl.BlockSpec((tm, d), lambda i: (i, 0)), pl.BlockSpec((tm, d), lambda i: (i, 0)),
                  _resident(w_g.shape), _resident(w_u.shape), _resident(w_d.shape),
                  _resident((1, d))],
        out_specs=pl.BlockSpec((tm, d), lambda i: (i, 0)),
        out_shape=jax.ShapeDtypeStruct((n_tok, d), F32),
        compiler_params=_params(1),
        name="ffn",
    )(x1, h2, w_g, w_u, w_d, ln_post)


def kernel(x, mem, ln_mix_pre, ln_mix_post, ln_ffn_pre, ln_ffn_post, ln_mem, w_in, ln_v_gain, ln_v_bias,
           w_spatial, b_spatial, rel_bias, w_mem_kv, w_branch_a, w_branch_b, w_branch_c, w_out,
           w_ffn_gate, w_ffn_up, w_ffn_down):
    b, s, d = x.shape
    assert s % MOBA_BLOCK == 0 and s % TOKEN_TILE == 0 and d % V7X_LANES == 0
    depth = w_in.shape[0]
    cuts = [0, GMLP_WIDTH, 2 * GMLP_WIDTH, 2 * GMLP_WIDTH + MOBA_WIDTH, 2 * GMLP_WIDTH + 2 * MOBA_WIDTH,
            2 * GMLP_WIDTH + 3 * MOBA_WIDTH, 2 * GMLP_WIDTH + 3 * MOBA_WIDTH + MEM_WIDTH]
    rb_flat = rel_bias.astype(F32).reshape(-1)
    bias = _bias_tiles(rb_flat)
    row = lambda v: v.reshape(1, -1).astype(F32)
    for l in range(depth):
        wi = w_in[l]
        w_u, w_v, w_q, w_k, w_v2, w_cq = (wi[:, cuts[i]:cuts[i + 1]] for i in range(6))
        w_g = wi[:, cuts[6]:]
        x2 = x.reshape(b * s, d)

        h2, hmean = _prep(x2, row(ln_mix_pre[l]))
        w_qt_scaled = (w_q * (HEAD_DIM ** -0.5)).T
        gt = _gatew(hmean.reshape(-1, d), w_k, w_qt_scaled, b)
        mask = _select(x, row(ln_mix_pre[l]), gt)
        qt, k, vt = _qkv(h2, b, s, (w_qt_scaled * LOG2E).astype(BF16), w_k.astype(BF16),
                         w_v2.T.astype(BF16))
        b_out_t = _attn(rb_flat, qt, k, vt, mask, bias)

        b_s_lanes = jnp.broadcast_to(b_spatial[l][:, :, None], (GMLP_GROUPS, CHUNK, GMLP_GROUP_DIM))
        a_out = _gmlp(h2, w_u.astype(BF16), w_v.astype(BF16), row(ln_v_gain[l]), row(ln_v_bias[l]),
                      w_spatial[l], b_s_lanes.astype(F32))

        wkv = w_mem_kv[l]
        kt_mem, v_mem = _memkv(mem, row(ln_mem[l]), wkv[:, :MEM_WIDTH].T.astype(BF16),
                               wkv[:, MEM_WIDTH:].astype(BF16))
        c_out = _memattn(h2, b, s, w_cq.astype(BF16), kt_mem, v_mem)

        x1, hn = _merge(x2, h2, a_out, b_out_t, c_out, w_g.astype(BF16), w_branch_a[l].astype(BF16),
                        w_branch_b[l].astype(BF16), w_branch_c[l].astype(BF16), w_out[l].astype(BF16),
                        row(ln_mix_post[l]), row(ln_ffn_pre[l]))
        out = _ffn(x1, hn, w_ffn_gate[l].astype(BF16), w_ffn_up[l].astype(BF16),
                   w_ffn_down[l].astype(BF16), row(ln_ffn_post[l]))
        x = out.reshape(b, s, d)
    return x
```

```python
import functools
import math

import jax
import jax.numpy as jnp
from jax import lax
from jax.experimental import pallas as pl
from jax.experimental.pallas import tpu as pltpu

F32 = jnp.float32
BF16 = jnp.bfloat16

EPS = 1e-6
NEG = -1e30
GMLP_GROUPS = 6
GMLP_GROUP_DIM = 128
GMLP_WIDTH = GMLP_GROUPS * GMLP_GROUP_DIM
CHUNK = 128
GMLP_SLAB = 256
MOBA_HEADS = 12
HEAD_DIM = 64
MOBA_WIDTH = MOBA_HEADS * HEAD_DIM
MOBA_BLOCK = 256
MOBA_TOPK = 3
REL_BUCKETS = 32
REL_MAX_DIST = 128
LOG2E = math.log2(math.e)
ONES_ROWS = 16
ATTN_HEADS = 2
SCORE_LOOKAHEAD = 4
MEM_HEADS = 4
MEM_HEAD_DIM = 128
MEM_WIDTH = MEM_HEADS * MEM_HEAD_DIM
MEM_SLAB = 256
N_BRANCHES = 3

V7X_LANES = 128
V7X_VMEM_LIMIT = 56 * 1024 * 1024

TOKEN_TILE = 512
PREP_BLOCKS = 4
GATE_ROWS = 128


def _params(n_axes, vmem=V7X_VMEM_LIMIT):
    return pltpu.CompilerParams(
        dimension_semantics=("arbitrary",) * n_axes, vmem_limit_bytes=vmem)


def _resident(shape):
    zeros = (0,) * len(shape)
    return pl.BlockSpec(shape, lambda *_: zeros, pipeline_mode=pl.Buffered(1))


def _rms(x, g):
    return x * lax.rsqrt(jnp.mean(x * x, axis=-1, keepdims=True) + EPS) * g


def _prep_kernel(x_ref, g_ref, h_ref, hmean_ref):
    h = _rms(x_ref[...], g_ref[...])
    h_ref[...] = h.astype(BF16)
    for i in range(PREP_BLOCKS):
        hmean_ref[i] = jnp.mean(h[i * MOBA_BLOCK:(i + 1) * MOBA_BLOCK], axis=0, keepdims=True)


def _prep(x2, g):
    n_tok, d = x2.shape
    n_blk = n_tok // MOBA_BLOCK
    tm = PREP_BLOCKS * MOBA_BLOCK
    return pl.pallas_call(
        _prep_kernel,
        grid=(n_tok // tm,),
        in_specs=[pl.BlockSpec((tm, d), lambda i: (i, 0)), _resident((1, d))],
        out_specs=[pl.BlockSpec((tm, d), lambda i: (i, 0)),
                   pl.BlockSpec((PREP_BLOCKS, 1, d), lambda i: (i, 0, 0))],
        out_shape=[jax.ShapeDtypeStruct((n_tok, d), BF16),
                   jax.ShapeDtypeStruct((n_blk, 1, d), F32)],
        compiler_params=_params(1),
        name="prep",
    )(x2, g)


def _split_bf16(x):
    hi = x.astype(BF16)
    return hi, (x - hi.astype(F32)).astype(BF16)


def _gatew_kernel(hm_ref, wk_ref, wqt_ref, g_ref, gt_scr, *, n_batch, n_blk):
    kmean = jnp.dot(hm_ref[...], wk_ref[...], precision=lax.Precision.HIGHEST,
                    preferred_element_type=F32)
    used = MOBA_HEADS * n_blk
    gt_scr[:, used:, :] = jnp.zeros((n_batch, GATE_ROWS - used, gt_scr.shape[-1]), F32)
    for h in range(MOBA_HEADS):
        sl = slice(h * HEAD_DIM, (h + 1) * HEAD_DIM)
        res = jnp.dot(kmean[:, sl], wqt_ref[sl, :], precision=lax.Precision.HIGHEST,
                      preferred_element_type=F32)
        for b in range(n_batch):
            gt_scr[b, h * n_blk:(h + 1) * n_blk, :] = res[b * n_blk:(b + 1) * n_blk, :]
    for b in range(n_batch):
        hi, lo = _split_bf16(gt_scr[b].T)
        g_ref[b, :, 0:GATE_ROWS] = hi
        g_ref[b, :, GATE_ROWS:] = lo


def _gatew(hmean, w_k, w_qt, n_batch):
    rows, d = hmean.shape
    n_blk = rows // n_batch
    assert MOBA_HEADS * n_blk <= GATE_ROWS
    return pl.pallas_call(
        functools.partial(_gatew_kernel, n_batch=n_batch, n_blk=n_blk),
        grid=(1,),
        in_specs=[_resident((rows, d)), _resident(w_k.shape), _resident(w_qt.shape)],
        out_specs=pl.BlockSpec((n_batch, d, 2 * GATE_ROWS), lambda i: (0, 0, 0)),
        out_shape=jax.ShapeDtypeStruct((n_batch, d, 2 * GATE_ROWS), BF16),
        scratch_shapes=[pltpu.VMEM((n_batch, GATE_ROWS, d), F32)],
        compiler_params=_params(1),
        name="gatew",
    )(hmean, w_k, w_qt)


def _select_kernel(x_ref, g_ref, gw_ref, mask_ref, *, n_blk, tq):
    j = pl.program_id(1)
    h = _rms(x_ref[0], g_ref[...])
    res = jnp.dot(jnp.concatenate(_split_bf16(h), axis=0), gw_ref[0], preferred_element_type=F32)
    gate = ((res[:tq, :GATE_ROWS] + (res[:tq, GATE_ROWS:] + res[tq:, :GATE_ROWS])) + res[tq:, GATE_ROWS:])
    gate = gate.T[0:MOBA_HEADS * n_blk].reshape(MOBA_HEADS, n_blk, tq)
    pos = j * tq + lax.broadcasted_iota(jnp.int32, gate.shape, 2)
    cur = lax.shift_right_logical(pos, int(math.log2(MOBA_BLOCK)))
    blk = lax.broadcasted_iota(jnp.int32, gate.shape, 1)
    past = blk < cur
    gate = jnp.where(past, gate, NEG)
    rank = jnp.zeros(gate.shape, jnp.int32)
    for m in range(n_blk):
        gm = gate[:, m:m + 1, :]
        tie = jnp.where(blk > m, 1, 0)
        rank = rank + jnp.where(gm > gate, 1, jnp.where(gm == gate, tie, 0))
    mask = jnp.where(rank < min(MOBA_TOPK, n_blk), jnp.where(past, 0.0, NEG), NEG)
    mask_ref[0] = mask.astype(F32).reshape(MOBA_HEADS * n_blk, tq)


def _select(x, g, gt, tq=TOKEN_TILE):
    b, s, d = x.shape
    n_blk = s // MOBA_BLOCK
    return pl.pallas_call(
        functools.partial(_select_kernel, n_blk=n_blk, tq=tq),
        grid=(b, s // tq),
        in_specs=[pl.BlockSpec((1, tq, d), lambda i, j: (i, j, 0)),
                  _resident((1, d)),
                  pl.BlockSpec((1, d, 2 * GATE_ROWS), lambda i, j: (i, 0, 0))],
        out_specs=pl.BlockSpec((1, MOBA_HEADS * n_blk, tq), lambda i, j: (i, 0, j)),
        out_shape=jax.ShapeDtypeStruct((b, MOBA_HEADS * n_blk, s), F32),
        compiler_params=_params(2),
        name="select",
    )(x, g, gt)


def _qkv_kernel(h_ref, wqt_ref, wk_ref, wvt_ref, qt_ref, k_ref, vt_ref):
    h = h_ref[...]
    nt = (((1,), (1,)), ((), ()))
    qt_ref[0] = lax.dot_general(wqt_ref[...], h, nt, preferred_element_type=F32).astype(BF16)
    vt_ref[0] = lax.dot_general(wvt_ref[...], h, nt, preferred_element_type=F32).astype(BF16)
    kf = jnp.dot(h, wk_ref[...], preferred_element_type=F32)
    low = lax.broadcasted_iota(jnp.int32, (h.shape[0], V7X_LANES), 1) < HEAD_DIM
    for p in range(MOBA_HEADS // 2):
        pair = kf[:, p * V7X_LANES:(p + 1) * V7X_LANES]
        even = jnp.where(low, pair, 0.0)
        odd = jnp.where(low, pltpu.roll(pair, HEAD_DIM, 1), 0.0)
        k_ref[0, :, (2 * p) * V7X_LANES:(2 * p + 1) * V7X_LANES] = even.astype(BF16)
        k_ref[0, :, (2 * p + 1) * V7X_LANES:(2 * p + 2) * V7X_LANES] = odd.astype(BF16)


def _qkv(h2, b, s, w_qt, w_k, w_vt, tm=TOKEN_TILE):
    d = h2.shape[-1]
    nt = s // tm
    return pl.pallas_call(
        _qkv_kernel,
        grid=(b, nt),
        in_specs=[pl.BlockSpec((tm, d), lambda i, j: (i * nt + j, 0)),
                  _resident(w_qt.shape), _resident(w_k.shape), _resident(w_vt.shape)],
        out_specs=[pl.BlockSpec((1, MOBA_WIDTH, tm), lambda i, j: (i, 0, j)),
                   pl.BlockSpec((1, tm, MOBA_HEADS * V7X_LANES), lambda i, j: (i, j, 0)),
                   pl.BlockSpec((1, MOBA_WIDTH, tm), lambda i, j: (i, 0, j))],
        out_shape=[jax.ShapeDtypeStruct((b, MOBA_WIDTH, s), BF16),
                   jax.ShapeDtypeStruct((b, s, MOBA_HEADS * V7X_LANES), BF16),
                   jax.ShapeDtypeStruct((b, MOBA_WIDTH, s), BF16)],
        compiler_params=_params(2),
        name="qkv",
    )(h2, w_qt, w_k, w_vt)


def _bias_kernel(rb_ref, out_ref):
    h = pl.program_id(0)
    ik = lax.broadcasted_iota(jnp.int32, (MOBA_BLOCK, MOBA_BLOCK), 0)
    iq = lax.broadcasted_iota(jnp.int32, (MOBA_BLOCK, MOBA_BLOCK), 1)
    max_exact = REL_BUCKETS // 2
    for t in range(2):
        dist = iq - ik + t * MOBA_BLOCK
        n = jnp.maximum(dist, 0)
        nf = jnp.maximum(n, 1).astype(F32)
        large = max_exact + (jnp.log(nf / max_exact) / math.log(REL_MAX_DIST / max_exact)
                             * (REL_BUCKETS - max_exact)).astype(jnp.int32)
        large = jnp.minimum(large, REL_BUCKETS - 1)
        bucket = jnp.where(n < max_exact, n, large)
        val = jnp.zeros((MOBA_BLOCK, MOBA_BLOCK), F32)
        for bk in range(REL_BUCKETS):
            val = jnp.where(bucket == bk, rb_ref[bk * MOBA_HEADS + h], val)
        val = val * LOG2E
        if t == 0:
            val = jnp.where(dist >= 0, val, NEG)
        out_ref[0, t] = val


def _bias_tiles(rb_flat):
    return pl.pallas_call(
        _bias_kernel,
        grid=(MOBA_HEADS,),
        in_specs=[pl.BlockSpec(memory_space=pltpu.SMEM)],
        out_specs=pl.BlockSpec((1, 2, MOBA_BLOCK, MOBA_BLOCK), lambda h: (h, 0, 0, 0)),
        out_shape=jax.ShapeDtypeStruct((MOBA_HEADS, 2, MOBA_BLOCK, MOBA_BLOCK), F32),
        compiler_params=_params(1),
        name="bias",
    )(rb_flat)


def _attn_kernel(rb_ref, qt_ref, k_ref, vt_ref, mask_ref, bias_ref, o_ref, s_scr, *, n_blk):
    first_head = pl.program_id(1) * ATTN_HEADS
    far = [rb_ref[(REL_BUCKETS - 1) * MOBA_HEADS + first_head + hh] * LOG2E for hh in range(ATTN_HEADS)]

    def scores(hh, j, buf):
        cols = slice(j * MOBA_BLOCK, (j + 1) * MOBA_BLOCK)
        q = qt_ref[0, hh * HEAD_DIM:(hh + 1) * HEAD_DIM, cols]
        qz = jnp.concatenate([q, jnp.zeros_like(q)], axis=0)
        k = k_ref[0, 0:(j + 1) * MOBA_BLOCK, hh * V7X_LANES:(hh + 1) * V7X_LANES]
        s = jnp.dot(k, qz, preferred_element_type=F32)
        for n in range(j + 1):
            rows = slice(n * MOBA_BLOCK, (n + 1) * MOBA_BLOCK)
            mrow = mask_ref[0, hh * n_blk + n:hh * n_blk + n + 1, cols]
            if n == j:
                sn = s[rows] + bias_ref[hh, 0]
            elif n == j - 1:
                sn = s[rows] + bias_ref[hh, 1] + mrow
            else:
                sn = s[rows] + (mrow + far[hh])
            s_scr[buf, rows, :] = sn

    def finish(hh, j, buf):
        cols = slice(j * MOBA_BLOCK, (j + 1) * MOBA_BLOCK)
        heads = slice(hh * HEAD_DIM, (hh + 1) * HEAD_DIM)
        nk = (j + 1) * MOBA_BLOCK
        m = jnp.max(s_scr[buf, 0:nk, :], axis=0, keepdims=True)
        p = jnp.exp2(s_scr[buf, 0:nk, :] - m).astype(BF16)
        vt1 = jnp.concatenate([vt_ref[0, heads, 0:nk], jnp.ones((ONES_ROWS, nk), BF16)], axis=0)
        acc = jnp.dot(vt1, p, preferred_element_type=F32)
        o_ref[0, heads, cols] = (acc[0:HEAD_DIM] / acc[HEAD_DIM:HEAD_DIM + 1]).astype(BF16)

    tasks = [(hh, j) for j in reversed(range(n_blk)) for hh in range(ATTN_HEADS)]
    pending = []
    for i, (hh, j) in enumerate(tasks):
        buf = i % (SCORE_LOOKAHEAD + 1)
        scores(hh, j, buf)
        pending.append((hh, j, buf))
        if len(pending) > SCORE_LOOKAHEAD:
            finish(*pending.pop(0))
    for item in pending:
        finish(*item)


def _attn(rb_flat, qt, k, vt, mask, bias):
    b, _, s = qt.shape
    n_blk = s // MOBA_BLOCK
    nh = ATTN_HEADS
    return pl.pallas_call(
        functools.partial(_attn_kernel, n_blk=n_blk),
        grid=(b, MOBA_HEADS // nh),
        in_specs=[pl.BlockSpec(memory_space=pltpu.SMEM),
                  pl.BlockSpec((1, nh * HEAD_DIM, s), lambda i, h: (i, h, 0)),
                  pl.BlockSpec((1, s, nh * V7X_LANES), lambda i, h: (i, 0, h)),
                  pl.BlockSpec((1, nh * HEAD_DIM, s), lambda i, h: (i, h, 0)),
                  pl.BlockSpec((1, nh * n_blk, s), lambda i, h: (i, h, 0)),
                  pl.BlockSpec((nh, 2, MOBA_BLOCK, MOBA_BLOCK), lambda i, h: (h, 0, 0, 0))],
        out_specs=pl.BlockSpec((1, nh * HEAD_DIM, s), lambda i, h: (i, h, 0)),
        out_shape=jax.ShapeDtypeStruct((b, MOBA_WIDTH, s), BF16),
        scratch_shapes=[pltpu.VMEM((SCORE_LOOKAHEAD + 1, s, MOBA_BLOCK), F32)],
        compiler_params=_params(2),
        name="attn",
    )(rb_flat, qt, k, vt, mask, bias)


def _gmlp_kernel(h_ref, wu_ref, wv_ref, lng_ref, lnb_ref, ws_ref, bs_ref, o_ref):
    causal = (lax.broadcasted_iota(jnp.int32, (CHUNK, CHUNK), 0)
              >= lax.broadcasted_iota(jnp.int32, (CHUNK, CHUNK), 1))
    ws = [jnp.where(causal, ws_ref[g], 0.0).astype(BF16) for g in range(GMLP_GROUPS)]
    slabs = [slice(i * GMLP_SLAB, (i + 1) * GMLP_SLAB) for i in range(h_ref.shape[0] // GMLP_SLAB)]
    vs = [jnp.dot(h_ref[rows, :], wv_ref[...], preferred_element_type=F32) for rows in slabs]
    us = [jnp.dot(h_ref[rows, :], wu_ref[...], preferred_element_type=F32) for rows in slabs]
    for rows, v, u in zip(slabs, vs, us):
        v = jax.nn.gelu(v)
        vc = v - jnp.mean(v, axis=-1, keepdims=True)
        vn = vc * lax.rsqrt(jnp.mean(vc * vc, axis=-1, keepdims=True) + EPS) * lng_ref[...] + lnb_ref[...]
        vn = vn.astype(BF16)
        u = jax.nn.gelu(u)
        for g in range(GMLP_GROUPS):
            lanes = slice(g * GMLP_GROUP_DIM, (g + 1) * GMLP_GROUP_DIM)
            for c in range(GMLP_SLAB // CHUNK):
                toks = slice(c * CHUNK, (c + 1) * CHUNK)
                mixed = jnp.dot(ws[g], vn[toks, lanes], preferred_element_type=F32) + bs_ref[g]
                o_ref[rows.start + c * CHUNK:rows.start + (c + 1) * CHUNK, lanes] = (
                    u[toks, lanes] * mixed).astype(BF16)


def _gmlp(h2, w_u, w_v, ln_g, ln_b, w_s, b_s_lanes, tm=2 * TOKEN_TILE):
    n_tok, d = h2.shape
    return pl.pallas_call(
        _gmlp_kernel,
        grid=(n_tok // tm,),
        in_specs=[pl.BlockSpec((tm, d), lambda i: (i, 0)),
                  _resident(w_u.shape), _resident(w_v.shape),
                  _resident(ln_g.shape), _resident(ln_b.shape),
                  _resident(w_s.shape), _resident(b_s_lanes.shape)],
        out_specs=pl.BlockSpec((tm, GMLP_WIDTH), lambda i: (i, 0)),
        out_shape=jax.ShapeDtypeStruct((n_tok, GMLP_WIDTH), BF16),
        compiler_params=_params(1),
        name="gmlp",
    )(h2, w_u, w_v, ln_g, ln_b, w_s, b_s_lanes)


def _memkv_kernel(mem_ref, g_ref, wkt_ref, wv_ref, kt_ref, v_ref):
    mn = _rms(mem_ref[0], g_ref[...]).astype(BF16)
    kt_ref[0] = lax.dot_general(wkt_ref[...], mn, (((1,), (1,)), ((), ())),
                                preferred_element_type=F32).astype(BF16)
    v_ref[0] = jnp.dot(mn, wv_ref[...], preferred_element_type=F32).astype(BF16)


def _memkv(mem, g, w_kt, w_v):
    b, m, d = mem.shape
    return pl.pallas_call(
        _memkv_kernel,
        grid=(b,),
        in_specs=[pl.BlockSpec((1, m, d), lambda i: (i, 0, 0)), _resident((1, d)),
                  _resident(w_kt.shape), _resident(w_v.shape)],
        out_specs=[pl.BlockSpec((1, MEM_WIDTH, m), lambda i: (i, 0, 0)),
                   pl.BlockSpec((1, m, MEM_WIDTH), lambda i: (i, 0, 0))],
        out_shape=[jax.ShapeDtypeStruct((b, MEM_WIDTH, m), BF16),
                   jax.ShapeDtypeStruct((b, m, MEM_WIDTH), BF16)],
        compiler_params=_params(1),
        name="memkv",
    )(mem, g, w_kt, w_v)


def _memattn_kernel(h_ref, wq_ref, kt_ref, v_ref, o_ref):
    slabs = [slice(i * MEM_SLAB, (i + 1) * MEM_SLAB) for i in range(h_ref.shape[0] // MEM_SLAB)]
    cqs = [jnp.dot(h_ref[rows, :], wq_ref[...], preferred_element_type=F32).astype(BF16) for rows in slabs]
    ones = jnp.ones((v_ref.shape[1], MEM_HEAD_DIM), BF16)
    for rows, cq in zip(slabs, cqs):
        for hd in range(MEM_HEADS):
            sl = slice(hd * MEM_HEAD_DIM, (hd + 1) * MEM_HEAD_DIM)
            s = jnp.dot(cq[:, sl], kt_ref[0, sl, :], preferred_element_type=F32) * (MEM_HEAD_DIM ** -0.5 * LOG2E)
            p = jnp.exp2(s - jnp.max(s, axis=-1, keepdims=True)).astype(BF16)
            acc = jnp.dot(p, jnp.concatenate([v_ref[0, :, sl], ones], axis=1), preferred_element_type=F32)
            o_ref[rows, sl] = (acc[:, :MEM_HEAD_DIM] / acc[:, MEM_HEAD_DIM:]).astype(BF16)


def _memattn(h2, b, s, w_cq, kt, v, tm=TOKEN_TILE):
    d = h2.shape[-1]
    nt = s // tm
    m = kt.shape[-1]
    return pl.pallas_call(
        _memattn_kernel,
        grid=(b, nt),
        in_specs=[pl.BlockSpec((tm, d), lambda i, j: (i * nt + j, 0)),
                  _resident(w_cq.shape),
                  pl.BlockSpec((1, MEM_WIDTH, m), lambda i, j: (i, 0, 0)),
                  pl.BlockSpec((1, m, MEM_WIDTH), lambda i, j: (i, 0, 0))],
        out_specs=pl.BlockSpec((tm, MEM_WIDTH), lambda i, j: (i * nt + j, 0)),
        out_shape=jax.ShapeDtypeStruct((b * s, MEM_WIDTH), BF16),
        compiler_params=_params(2),
        name="memattn",
    )(h2, w_cq, kt, v)


def _merge_kernel(x_ref, h_ref, a_ref, bt_ref, c_ref, wg_ref, wa_ref, wb_ref, wc_ref, wo_ref,
                  lnpost_ref, lnpre_ref, x1_ref, h2_ref):
    d = x_ref.shape[-1]
    gates = jax.nn.sigmoid(jnp.dot(h_ref[...], wg_ref[...], preferred_element_type=F32))
    pa = jnp.dot(a_ref[...], wa_ref[...], preferred_element_type=F32)
    pb = lax.dot_general(bt_ref[0], wb_ref[...], (((0,), (0,)), ((), ())),
                         preferred_element_type=F32)
    pc = jnp.dot(c_ref[...], wc_ref[...], preferred_element_type=F32)
    merged = gates[:, :d] * pa + gates[:, d:2 * d] * pb + gates[:, 2 * d:] * pc
    mo = jnp.dot(merged.astype(BF16), wo_ref[...], preferred_element_type=F32)
    x1 = x_ref[...] + _rms(mo, lnpost_ref[...])
    x1_ref[...] = x1
    h2_ref[...] = _rms(x1, lnpre_ref[...]).astype(BF16)


def _merge(x2, h2, a_out, b_out_t, c_out, w_g, w_a, w_b, w_c, w_o, ln_post, ln_pre, tm=TOKEN_TILE):
    n_tok, d = x2.shape
    b, _, s = b_out_t.shape
    nt = s // tm
    tok = lambda i, j: (i * nt + j, 0)
    return pl.pallas_call(
        _merge_kernel,
        grid=(b, nt),
        in_specs=[pl.BlockSpec((tm, d), tok), pl.BlockSpec((tm, d), tok),
                  pl.BlockSpec((tm, GMLP_WIDTH), tok),
                  pl.BlockSpec((1, MOBA_WIDTH, tm), lambda i, j: (i, 0, j)),
                  pl.BlockSpec((tm, MEM_WIDTH), tok),
                  _resident(w_g.shape), _resident(w_a.shape), _resident(w_b.shape),
                  _resident(w_c.shape), _resident(w_o.shape),
                  _resident((1, d)), _resident((1, d))],
        out_specs=[pl.BlockSpec((tm, d), tok), pl.BlockSpec((tm, d), tok)],
        out_shape=[jax.ShapeDtypeStruct((n_tok, d), F32), jax.ShapeDtypeStruct((n_tok, d), BF16)],
        compiler_params=_params(2),
        name="merge",
    )(x2, h2, a_out, b_out_t, c_out, w_g, w_a, w_b, w_c, w_o, ln_post, ln_pre)


def _ffn_kernel(x1_ref, h2_ref, wg_ref, wu_ref, wd_ref, ln_ref, o_ref):
    h2 = h2_ref[...]
    g = jnp.dot(h2, wg_ref[...], preferred_element_type=F32)
    u = jnp.dot(h2, wu_ref[...], preferred_element_type=F32)
    act = (jax.nn.silu(g) * u).astype(BF16)
    f = jnp.dot(act, wd_ref[...], preferred_element_type=F32)
    o_ref[...] = x1_ref[...] + _rms(f, ln_ref[...])


def _ffn(x1, h2, w_g, w_u, w_d, ln_post, tm=TOKEN_TILE):
    n_tok, d = x1.shape
    return pl.pallas_call(
        _ffn_kernel,
        grid=(n_tok // tm,),
        in_specs=[pl.BlockSpec((tm, d), lambda i: (i, 0)), pl.BlockSpec((tm, d), lambda i: (i, 0)),
                  _resident(w_g.shape), _resident(w_u.shape), _resident(w_d.shape),
                  _resident((1, d))],
        out_specs=pl.BlockSpec((tm, d), lambda i: (i, 0)),
        out_shape=jax.ShapeDtypeStruct((n_tok, d), F32),
        compiler_params=_params(1),
        name="ffn",
    )(x1, h2, w_g, w_u, w_d, ln_post)


def kernel(x, mem, ln_mix_pre, ln_mix_post, ln_ffn_pre, ln_ffn_post, ln_mem, w_in, ln_v_gain, ln_v_bias,
           w_spatial, b_spatial, rel_bias, w_mem_kv, w_branch_a, w_branch_b, w_branch_c, w_out,
           w_ffn_gate, w_ffn_up, w_ffn_down):
    b, s, d = x.shape
    assert s % MOBA_BLOCK == 0 and s % TOKEN_TILE == 0 and d % V7X_LANES == 0
    depth = w_in.shape[0]
    cuts = [0, GMLP_WIDTH, 2 * GMLP_WIDTH, 2 * GMLP_WIDTH + MOBA_WIDTH, 2 * GMLP_WIDTH + 2 * MOBA_WIDTH,
            2 * GMLP_WIDTH + 3 * MOBA_WIDTH, 2 * GMLP_WIDTH + 3 * MOBA_WIDTH + MEM_WIDTH]
    rb_flat = rel_bias.astype(F32).reshape(-1)
    bias = _bias_tiles(rb_flat)
    row = lambda v: v.reshape(1, -1).astype(F32)
    for l in range(depth):
        wi = w_in[l]
        w_u, w_v, w_q, w_k, w_v2, w_cq = (wi[:, cuts[i]:cuts[i + 1]] for i in range(6))
        w_g = wi[:, cuts[6]:]
        x2 = x.reshape(b * s, d)

        h2, hmean = _prep(x2, row(ln_mix_pre[l]))
        w_qt_scaled = (w_q * (HEAD_DIM ** -0.5)).T
        gt = _gatew(hmean.reshape(-1, d), w_k, w_qt_scaled, b)
        mask = _select(x, row(ln_mix_pre[l]), gt)
        qt, k, vt = _qkv(h2, b, s, (w_qt_scaled * LOG2E).astype(BF16), w_k.astype(BF16),
                         w_v2.T.astype(BF16))
        b_out_t = _attn(rb_flat, qt, k, vt, mask, bias)

        b_s_lanes = jnp.broadcast_to(b_spatial[l][:, :, None], (GMLP_GROUPS, CHUNK, GMLP_GROUP_DIM))
        a_out = _gmlp(h2, w_u.astype(BF16), w_v.astype(BF16), row(ln_v_gain[l]), row(ln_v_bias[l]),
                      w_spatial[l], b_s_lanes.astype(F32))

        wkv = w_mem_kv[l]
        kt_mem, v_mem = _memkv(mem, row(ln_mem[l]), wkv[:, :MEM_WIDTH].T.astype(BF16),
                               wkv[:, MEM_WIDTH:].astype(BF16))
        c_out = _memattn(h2, b, s, w_cq.astype(BF16), kt_mem, v_mem)

        x1, hn = _merge(x2, h2, a_out, b_out_t, c_out, w_g.astype(BF16), w_branch_a[l].astype(BF16),
                        w_branch_b[l].astype(BF16), w_branch_c[l].astype(BF16), w_out[l].astype(BF16),
                        row(ln_mix_post[l]), row(ln_ffn_pre[l]))
        out = _ffn(x1, hn, w_ffn_gate[l].astype(BF16), w_ffn_up[l].astype(BF16),
                   w_ffn_down[l].astype(BF16), row(ln_ffn_post[l]))
        x = out.reshape(b, s, d)
    return x
```

```python
import functools
import math

import jax
import jax.numpy as jnp
from jax import lax
from jax.experimental import pallas as pl
from jax.experimental.pallas import tpu as pltpu

F32 = jnp.float32
BF16 = jnp.bfloat16

EPS = 1e-6
NEG = -1e30
GMLP_GROUPS = 6
GMLP_GROUP_DIM = 128
GMLP_WIDTH = GMLP_GROUPS * GMLP_GROUP_DIM
CHUNK = 128
GMLP_SLAB = 256
MOBA_HEADS = 12
HEAD_DIM = 64
MOBA_WIDTH = MOBA_HEADS * HEAD_DIM
MOBA_BLOCK = 256
MOBA_TOPK = 3
REL_BUCKETS = 32
REL_MAX_DIST = 128
LOG2E = math.log2(math.e)
ONES_ROWS = 16
ATTN_HEADS = 2
SCORE_LOOKAHEAD = 4
MEM_HEADS = 4
MEM_HEAD_DIM = 128
MEM_WIDTH = MEM_HEADS * MEM_HEAD_DIM
MEM_SLAB = 256
N_BRANCHES = 3

V7X_LANES = 128
V7X_VMEM_LIMIT = 56 * 1024 * 1024

TOKEN_TILE = 512
PREP_BLOCKS = 4
GATE_ROWS = 128


def _params(n_axes, vmem=V7X_VMEM_LIMIT):
    return pltpu.CompilerParams(
        dimension_semantics=("arbitrary",) * n_axes, vmem_limit_bytes=vmem)


def _resident(shape):
    zeros = (0,) * len(shape)
    return pl.BlockSpec(shape, lambda *_: zeros, pipeline_mode=pl.Buffered(1))


def _rms(x, g):
    return x * lax.rsqrt(jnp.mean(x * x, axis=-1, keepdims=True) + EPS) * g


def _split_bf16(x):
    hi = x.astype(BF16)
    return hi, (x - hi.astype(F32)).astype(BF16)


def _prep_kernel(x_ref, g_ref, h_ref, hlo_ref, hmean_ref):
    h = _rms(x_ref[...], g_ref[...])
    h_ref[...], hlo_ref[...] = _split_bf16(h)
    for i in range(PREP_BLOCKS):
        hmean_ref[i] = jnp.mean(h[i * MOBA_BLOCK:(i + 1) * MOBA_BLOCK], axis=0, keepdims=True)


def _prep(x2, g):
    n_tok, d = x2.shape
    n_blk = n_tok // MOBA_BLOCK
    tm = PREP_BLOCKS * MOBA_BLOCK
    return pl.pallas_call(
        _prep_kernel,
        grid=(n_tok // tm,),
        in_specs=[pl.BlockSpec((tm, d), lambda i: (i, 0)), _resident((1, d))],
        out_specs=[pl.BlockSpec((tm, d), lambda i: (i, 0)),
                   pl.BlockSpec((tm, d), lambda i: (i, 0)),
                   pl.BlockSpec((PREP_BLOCKS, 1, d), lambda i: (i, 0, 0))],
        out_shape=[jax.ShapeDtypeStruct((n_tok, d), BF16),
                   jax.ShapeDtypeStruct((n_tok, d), BF16),
                   jax.ShapeDtypeStruct((n_blk, 1, d), F32)],
        compiler_params=_params(1),
        name="prep",
    )(x2, g)


def _gatew_kernel(hm_ref, wk_ref, wqt_ref, g_ref, gt_scr, *, n_batch, n_blk):
    kmean = jnp.dot(hm_ref[...], wk_ref[...], precision=lax.Precision.HIGHEST,
                    preferred_element_type=F32)
    used = MOBA_HEADS * n_blk
    gt_scr[:, used:, :] = jnp.zeros((n_batch, GATE_ROWS - used, gt_scr.shape[-1]), F32)
    for h in range(MOBA_HEADS):
        sl = slice(h * HEAD_DIM, (h + 1) * HEAD_DIM)
        res = jnp.dot(kmean[:, sl], wqt_ref[sl, :], precision=lax.Precision.HIGHEST,
                      preferred_element_type=F32)
        for b in range(n_batch):
            gt_scr[b, h * n_blk:(h + 1) * n_blk, :] = res[b * n_blk:(b + 1) * n_blk, :]
    for b in range(n_batch):
        hi, lo = _split_bf16(gt_scr[b].T)
        g_ref[b, :, 0:GATE_ROWS] = hi
        g_ref[b, :, GATE_ROWS:] = lo


def _gatew(hmean, w_k, w_qt, n_batch):
    rows, d = hmean.shape
    n_blk = rows // n_batch
    assert MOBA_HEADS * n_blk <= GATE_ROWS
    return pl.pallas_call(
        functools.partial(_gatew_kernel, n_batch=n_batch, n_blk=n_blk),
        grid=(1,),
        in_specs=[_resident((rows, d)), _resident(w_k.shape), _resident(w_qt.shape)],
        out_specs=pl.BlockSpec((n_batch, d, 2 * GATE_ROWS), lambda i: (0, 0, 0)),
        out_shape=jax.ShapeDtypeStruct((n_batch, d, 2 * GATE_ROWS), BF16),
        scratch_shapes=[pltpu.VMEM((n_batch, GATE_ROWS, d), F32)],
        compiler_params=_params(1),
        name="gatew",
    )(hmean, w_k, w_qt)


def _select_kernel(hhi_ref, hlo_ref, gw_ref, mask_ref, *, n_blk, tq):
    for i in range(tq // MOBA_BLOCK):
        rows = slice(i * MOBA_BLOCK, (i + 1) * MOBA_BLOCK)
        cur = pl.program_id(1) * (tq // MOBA_BLOCK) + i
        res = jnp.dot(jnp.concatenate([hhi_ref[rows, :], hlo_ref[rows, :]], axis=0), gw_ref[0],
                      preferred_element_type=F32)
        hi, lo = res[:MOBA_BLOCK], res[MOBA_BLOCK:]
        gate = (hi[:, :GATE_ROWS] + (hi[:, GATE_ROWS:] + lo[:, :GATE_ROWS])) + lo[:, GATE_ROWS:]
        gate = gate.T[0:MOBA_HEADS * n_blk].reshape(MOBA_HEADS, n_blk, MOBA_BLOCK)
        blk = lax.broadcasted_iota(jnp.int32, gate.shape, 1)
        past = blk < cur
        gate = jnp.where(past, gate, NEG)
        rank = jnp.zeros(gate.shape, jnp.int32)
        for m in range(n_blk):
            gm = gate[:, m:m + 1, :]
            tie = jnp.where(blk > m, 1, 0)
            rank = rank + jnp.where(gm > gate, 1, jnp.where(gm == gate, tie, 0))
        mask = jnp.where(rank < min(MOBA_TOPK, n_blk), jnp.where(past, 0.0, NEG), NEG)
        mask_ref[0, :, rows] = mask.astype(F32).reshape(MOBA_HEADS * n_blk, MOBA_BLOCK)


def _select(h_hi, h_lo, gt, b, s, tq=2 * TOKEN_TILE):
    d = h_hi.shape[-1]
    n_blk = s // MOBA_BLOCK
    nt = s // tq
    return pl.pallas_call(
        functools.partial(_select_kernel, n_blk=n_blk, tq=tq),
        grid=(b, nt),
        in_specs=[pl.BlockSpec((tq, d), lambda i, j: (i * nt + j, 0)),
                  pl.BlockSpec((tq, d), lambda i, j: (i * nt + j, 0)),
                  pl.BlockSpec((1, d, 2 * GATE_ROWS), lambda i, j: (i, 0, 0))],
        out_specs=pl.BlockSpec((1, MOBA_HEADS * n_blk, tq), lambda i, j: (i, 0, j)),
        out_shape=jax.ShapeDtypeStruct((b, MOBA_HEADS * n_blk, s), F32),
        compiler_params=_params(2),
        name="select",
    )(h_hi, h_lo, gt)


def _qkv_kernel(h_ref, wqt_ref, wk_ref, wvt_ref, qt_ref, k_ref, vt_ref):
    h = h_ref[...]
    nt = (((1,), (1,)), ((), ()))
    qt_ref[0] = lax.dot_general(wqt_ref[...], h, nt, preferred_element_type=F32).astype(BF16)
    vt_ref[0] = lax.dot_general(wvt_ref[...], h, nt, preferred_element_type=F32).astype(BF16)
    kf = jnp.dot(h, wk_ref[...], preferred_element_type=F32)
    low = lax.broadcasted_iota(jnp.int32, (h.shape[0], V7X_LANES), 1) < HEAD_DIM
    for p in range(MOBA_HEADS // 2):
        pair = kf[:, p * V7X_LANES:(p + 1) * V7X_LANES]
        even = jnp.where(low, pair, 0.0)
        odd = jnp.where(low, pltpu.roll(pair, HEAD_DIM, 1), 0.0)
        k_ref[0, :, (2 * p) * V7X_LANES:(2 * p + 1) * V7X_LANES] = even.astype(BF16)
        k_ref[0, :, (2 * p + 1) * V7X_LANES:(2 * p + 2) * V7X_LANES] = odd.astype(BF16)


def _qkv(h2, b, s, w_qt, w_k, w_vt, tm=TOKEN_TILE):
    d = h2.shape[-1]
    nt = s // tm
    return pl.pallas_call(
        _qkv_kernel,
        grid=(b, nt),
        in_specs=[pl.BlockSpec((tm, d), lambda i, j: (i * nt + j, 0)),
                  _resident(w_qt.shape), _resident(w_k.shape), _resident(w_vt.shape)],
        out_specs=[pl.BlockSpec((1, MOBA_WIDTH, tm), lambda i, j: (i, 0, j)),
                   pl.BlockSpec((1, tm, MOBA_HEADS * V7X_LANES), lambda i, j: (i, j, 0)),
                   pl.BlockSpec((1, MOBA_WIDTH, tm), lambda i, j: (i, 0, j))],
        out_shape=[jax.ShapeDtypeStruct((b, MOBA_WIDTH, s), BF16),
                   jax.ShapeDtypeStruct((b, s, MOBA_HEADS * V7X_LANES), BF16),
                   jax.ShapeDtypeStruct((b, MOBA_WIDTH, s), BF16)],
        compiler_params=_params(2),
        name="qkv",
    )(h2, w_qt, w_k, w_vt)


def _bias_kernel(rb_ref, out_ref):
    h = pl.program_id(0)
    ik = lax.broadcasted_iota(jnp.int32, (MOBA_BLOCK, MOBA_BLOCK), 0)
    iq = lax.broadcasted_iota(jnp.int32, (MOBA_BLOCK, MOBA_BLOCK), 1)
    max_exact = REL_BUCKETS // 2
    for t in range(2):
        dist = iq - ik + t * MOBA_BLOCK
        n = jnp.maximum(dist, 0)
        nf = jnp.maximum(n, 1).astype(F32)
        large = max_exact + (jnp.log(nf / max_exact) / math.log(REL_MAX_DIST / max_exact)
                             * (REL_BUCKETS - max_exact)).astype(jnp.int32)
        large = jnp.minimum(large, REL_BUCKETS - 1)
        bucket = jnp.where(n < max_exact, n, large)
        val = jnp.zeros((MOBA_BLOCK, MOBA_BLOCK), F32)
        for bk in range(REL_BUCKETS):
            val = jnp.where(bucket == bk, rb_ref[bk * MOBA_HEADS + h], val)
        val = val * LOG2E
        if t == 0:
            val = jnp.where(dist >= 0, val, NEG)
        out_ref[0, t] = val


def _bias_tiles(rb_flat):
    return pl.pallas_call(
        _bias_kernel,
        grid=(MOBA_HEADS,),
        in_specs=[pl.BlockSpec(memory_space=pltpu.SMEM)],
        out_specs=pl.BlockSpec((1, 2, MOBA_BLOCK, MOBA_BLOCK), lambda h: (h, 0, 0, 0)),
        out_shape=jax.ShapeDtypeStruct((MOBA_HEADS, 2, MOBA_BLOCK, MOBA_BLOCK), F32),
        compiler_params=_params(1),
        name="bias",
    )(rb_flat)


def _attn_kernel(rb_ref, qt_ref, k_ref, vt_ref, mask_ref, bias_ref, o_ref, s_scr, *, n_blk):
    first_head = pl.program_id(1) * ATTN_HEADS
    far = [rb_ref[(REL_BUCKETS - 1) * MOBA_HEADS + first_head + hh] * LOG2E for hh in range(ATTN_HEADS)]

    def scores(hh, j, buf):
        cols = slice(j * MOBA_BLOCK, (j + 1) * MOBA_BLOCK)
        q = qt_ref[0, hh * HEAD_DIM:(hh + 1) * HEAD_DIM, cols]
        qz = jnp.concatenate([q, jnp.zeros_like(q)], axis=0)
        k = k_ref[0, 0:(j + 1) * MOBA_BLOCK, hh * V7X_LANES:(hh + 1) * V7X_LANES]
        s = jnp.dot(k, qz, preferred_element_type=F32)
        for n in range(j + 1):
            rows = slice(n * MOBA_BLOCK, (n + 1) * MOBA_BLOCK)
            mrow = mask_ref[0, hh * n_blk + n:hh * n_blk + n + 1, cols]
            if n == j:
                sn = s[rows] + bias_ref[hh, 0]
            elif n == j - 1:
                sn = s[rows] + bias_ref[hh, 1] + mrow
            else:
                sn = s[rows] + (mrow + far[hh])
            s_scr[buf, rows, :] = sn

    def finish(hh, j, buf):
        cols = slice(j * MOBA_BLOCK, (j + 1) * MOBA_BLOCK)
        heads = slice(hh * HEAD_DIM, (hh + 1) * HEAD_DIM)
        nk = (j + 1) * MOBA_BLOCK
        m = jnp.max(s_scr[buf, 0:nk, :], axis=0, keepdims=True)
        p = jnp.exp2(s_scr[buf, 0:nk, :] - m).astype(BF16)
        vt1 = jnp.concatenate([vt_ref[0, heads, 0:nk], jnp.ones((ONES_ROWS, nk), BF16)], axis=0)
        acc = jnp.dot(vt1, p, preferred_element_type=F32)
        o_ref[0, heads, cols] = (acc[0:HEAD_DIM] / acc[HEAD_DIM:HEAD_DIM + 1]).astype(BF16)

    tasks = [(hh, j) for j in reversed(range(n_blk)) for hh in range(ATTN_HEADS)]
    pending = []
    for i, (hh, j) in enumerate(tasks):
        buf = i % (SCORE_LOOKAHEAD + 1)
        scores(hh, j, buf)
        pending.append((hh, j, buf))
        if len(pending) > SCORE_LOOKAHEAD:
            finish(*pending.pop(0))
    for item in pending:
        finish(*item)


def _attn(rb_flat, qt, k, vt, mask, bias):
    b, _, s = qt.shape
    n_blk = s // MOBA_BLOCK
    nh = ATTN_HEADS
    return pl.pallas_call(
        functools.partial(_attn_kernel, n_blk=n_blk),
        grid=(b, MOBA_HEADS // nh),
        in_specs=[pl.BlockSpec(memory_space=pltpu.SMEM),
                  pl.BlockSpec((1, nh * HEAD_DIM, s), lambda i, h: (i, h, 0)),
                  pl.BlockSpec((1, s, nh * V7X_LANES), lambda i, h: (i, 0, h)),
                  pl.BlockSpec((1, nh * HEAD_DIM, s), lambda i, h: (i, h, 0)),
                  pl.BlockSpec((1, nh * n_blk, s), lambda i, h: (i, h, 0)),
                  pl.BlockSpec((nh, 2, MOBA_BLOCK, MOBA_BLOCK), lambda i, h: (h, 0, 0, 0))],
        out_specs=pl.BlockSpec((1, nh * HEAD_DIM, s), lambda i, h: (i, h, 0)),
        out_shape=jax.ShapeDtypeStruct((b, MOBA_WIDTH, s), BF16),
        scratch_shapes=[pltpu.VMEM((SCORE_LOOKAHEAD + 1, s, MOBA_BLOCK), F32)],
        compiler_params=_params(2),
        name="attn",
    )(rb_flat, qt, k, vt, mask, bias)


def _gmlp_kernel(h_ref, wu_ref, wv_ref, lng_ref, lnb_ref, ws_ref, bs_ref, o_ref):
    causal = (lax.broadcasted_iota(jnp.int32, (CHUNK, CHUNK), 0)
              >= lax.broadcasted_iota(jnp.int32, (CHUNK, CHUNK), 1))
    ws = [jnp.where(causal, ws_ref[g], 0.0).astype(BF16) for g in range(GMLP_GROUPS)]
    slabs = [slice(i * GMLP_SLAB, (i + 1) * GMLP_SLAB) for i in range(h_ref.shape[0] // GMLP_SLAB)]
    vs = [jnp.dot(h_ref[rows, :], wv_ref[...], preferred_element_type=F32) for rows in slabs]
    us = [jnp.dot(h_ref[rows, :], wu_ref[...], preferred_element_type=F32) for rows in slabs]
    for rows, v, u in zip(slabs, vs, us):
        v = jax.nn.gelu(v)
        vc = v - jnp.mean(v, axis=-1, keepdims=True)
        vn = vc * lax.rsqrt(jnp.mean(vc * vc, axis=-1, keepdims=True) + EPS) * lng_ref[...] + lnb_ref[...]
        vn = vn.astype(BF16)
        u = jax.nn.gelu(u)
        for g in range(GMLP_GROUPS):
            lanes = slice(g * GMLP_GROUP_DIM, (g + 1) * GMLP_GROUP_DIM)
            for c in range(GMLP_SLAB // CHUNK):
                toks = slice(c * CHUNK, (c + 1) * CHUNK)
                mixed = jnp.dot(ws[g], vn[toks, lanes], preferred_element_type=F32) + bs_ref[g]
                o_ref[rows.start + c * CHUNK:rows.start + (c + 1) * CHUNK, lanes] = (
                    u[toks, lanes] * mixed).astype(BF16)


def _gmlp(h2, w_u, w_v, ln_g, ln_b, w_s, b_s_lanes, tm=2 * TOKEN_TILE):
    n_tok, d = h2.shape
    return pl.pallas_call(
        _gmlp_kernel,
        grid=(n_tok // tm,),
        in_specs=[pl.BlockSpec((tm, d), lambda i: (i, 0)),
                  _resident(w_u.shape), _resident(w_v.shape),
                  _resident(ln_g.shape), _resident(ln_b.shape),
                  _resident(w_s.shape), _resident(b_s_lanes.shape)],
        out_specs=pl.BlockSpec((tm, GMLP_WIDTH), lambda i: (i, 0)),
        out_shape=jax.ShapeDtypeStruct((n_tok, GMLP_WIDTH), BF16),
        compiler_params=_params(1),
        name="gmlp",
    )(h2, w_u, w_v, ln_g, ln_b, w_s, b_s_lanes)


def _memkv_kernel(mem_ref, g_ref, wkt_ref, wv_ref, kt_ref, v_ref):
    mn = _rms(mem_ref[0], g_ref[...]).astype(BF16)
    kt_ref[0] = lax.dot_general(wkt_ref[...], mn, (((1,), (1,)), ((), ())),
                                preferred_element_type=F32).astype(BF16)
    v_ref[0] = jnp.dot(mn, wv_ref[...], preferred_element_type=F32).astype(BF16)


def _memkv(mem, g, w_kt, w_v):
    b, m, d = mem.shape
    return pl.pallas_call(
        _memkv_kernel,
        grid=(b,),
        in_specs=[pl.BlockSpec((1, m, d), lambda i: (i, 0, 0)), _resident((1, d)),
                  _resident(w_kt.shape), _resident(w_v.shape)],
        out_specs=[pl.BlockSpec((1, MEM_WIDTH, m), lambda i: (i, 0, 0)),
                   pl.BlockSpec((1, m, MEM_WIDTH), lambda i: (i, 0, 0))],
        out_shape=[jax.ShapeDtypeStruct((b, MEM_WIDTH, m), BF16),
                   jax.ShapeDtypeStruct((b, m, MEM_WIDTH), BF16)],
        compiler_params=_params(1),
        name="memkv",
    )(mem, g, w_kt, w_v)


def _memattn_kernel(h_ref, wq_ref, kt_ref, v_ref, o_ref):
    slabs = [slice(i * MEM_SLAB, (i + 1) * MEM_SLAB) for i in range(h_ref.shape[0] // MEM_SLAB)]
    cqs = [jnp.dot(h_ref[rows, :], wq_ref[...], preferred_element_type=F32).astype(BF16) for rows in slabs]
    ones = jnp.ones((v_ref.shape[1], MEM_HEAD_DIM), BF16)
    for rows, cq in zip(slabs, cqs):
        for hd in range(MEM_HEADS):
            sl = slice(hd * MEM_HEAD_DIM, (hd + 1) * MEM_HEAD_DIM)
            s = jnp.dot(cq[:, sl], kt_ref[0, sl, :], preferred_element_type=F32) * (MEM_HEAD_DIM ** -0.5 * LOG2E)
            p = jnp.exp2(s - jnp.max(s, axis=-1, keepdims=True)).astype(BF16)
            acc = jnp.dot(p, jnp.concatenate([v_ref[0, :, sl], ones], axis=1), preferred_element_type=F32)
            o_ref[rows, sl] = (acc[:, :MEM_HEAD_DIM] / acc[:, MEM_HEAD_DIM:]).astype(BF16)


def _memattn(h2, b, s, w_cq, kt, v, tm=TOKEN_TILE):
    d = h2.shape[-1]
    nt = s // tm
    m = kt.shape[-1]
    return pl.pallas_call(
        _memattn_kernel,
        grid=(b, nt),
        in_specs=[pl.BlockSpec((tm, d), lambda i, j: (i * nt + j, 0)),
                  _resident(w_cq.shape),
                  pl.BlockSpec((1, MEM_WIDTH, m), lambda i, j: (i, 0, 0)),
                  pl.BlockSpec((1, m, MEM_WIDTH), lambda i, j: (i, 0, 0))],
        out_specs=pl.BlockSpec((tm, MEM_WIDTH), lambda i, j: (i * nt + j, 0)),
        out_shape=jax.ShapeDtypeStruct((b * s, MEM_WIDTH), BF16),
        compiler_params=_params(2),
        name="memattn",
    )(h2, w_cq, kt, v)


def _merge_kernel(x_ref, h_ref, a_ref, bt_ref, c_ref, wg_ref, wa_ref, wb_ref, wc_ref, wo_ref,
                  lnpost_ref, lnpre_ref, x1_ref, h2_ref):
    d = x_ref.shape[-1]
    gates = jax.nn.sigmoid(jnp.dot(h_ref[...], wg_ref[...], preferred_element_type=F32))
    pa = jnp.dot(a_ref[...], wa_ref[...], preferred_element_type=F32)
    pb = lax.dot_general(bt_ref[0], wb_ref[...], (((0,), (0,)), ((), ())),
                         preferred_element_type=F32)
    pc = jnp.dot(c_ref[...], wc_ref[...], preferred_element_type=F32)
    merged = gates[:, :d] * pa + gates[:, d:2 * d] * pb + gates[:, 2 * d:] * pc
    mo = jnp.dot(merged.astype(BF16), wo_ref[...], preferred_element_type=F32)
    x1 = x_ref[...] + _rms(mo, lnpost_ref[...])
    x1_ref[...] = x1
    h2_ref[...] = _rms(x1, lnpre_ref[...]).astype(BF16)


def _merge(x2, h2, a_out, b_out_t, c_out, w_g, w_a, w_b, w_c, w_o, ln_post, ln_pre, tm=TOKEN_TILE):
    n_tok, d = x2.shape
    b, _, s = b_out_t.shape
    nt = s // tm
    tok = lambda i, j: (i * nt + j, 0)
    return pl.pallas_call(
        _merge_kernel,
        grid=(b, nt),
        in_specs=[pl.BlockSpec((tm, d), tok), pl.BlockSpec((tm, d), tok),
                  pl.BlockSpec((tm, GMLP_WIDTH), tok),
                  pl.BlockSpec((1, MOBA_WIDTH, tm), lambda i, j: (i, 0, j)),
                  pl.BlockSpec((tm, MEM_WIDTH), tok),
                  _resident(w_g.shape), _resident(w_a.shape), _resident(w_b.shape),
                  _resident(w_c.shape), _resident(w_o.shape),
                  _resident((1, d)), _resident((1, d))],
        out_specs=[pl.BlockSpec((tm, d), tok), pl.BlockSpec((tm, d), tok)],
        out_shape=[jax.ShapeDtypeStruct((n_tok, d), F32), jax.ShapeDtypeStruct((n_tok, d), BF16)],
        compiler_params=_params(2),
        name="merge",
    )(x2, h2, a_out, b_out_t, c_out, w_g, w_a, w_b, w_c, w_o, ln_post, ln_pre)


def _ffn_kernel(x1_ref, h2_ref, wg_ref, wu_ref, wd_ref, ln_ref, o_ref):
    h2 = h2_ref[...]
    g = jnp.dot(h2, wg_ref[...], preferred_element_type=F32)
    u = jnp.dot(h2, wu_ref[...], preferred_element_type=F32)
    act = (jax.nn.silu(g) * u).astype(BF16)
    f = jnp.dot(act, wd_ref[...], preferred_element_type=F32)
    o_ref[...] = x1_ref[...] + _rms(f, ln_ref[...])


def _ffn(x1, h2, w_g, w_u, w_d, ln_post, tm=TOKEN_TILE):
    n_tok, d = x1.shape
    return pl.pallas_call(
        _ffn_kernel,
        grid=(n_tok // tm,),
        in_specs=[pl.BlockSpec((tm, d), lambda i: (i, 0)), pl.BlockSpec((tm, d), lambda i: (i, 0)),
                  _resident(w_g.shape), _resident(w_u.shape), _resident(w_d.shape),
                  _resident((1, d))],
        out_specs=pl.BlockSpec((tm, d), lambda i: (i, 0)),
        out_shape=jax.ShapeDtypeStruct((n_tok, d), F32),
        compiler_params=_params(1),
        name="ffn",
    )(x1, h2, w_g, w_u, w_d, ln_post)


def kernel(x, mem, ln_mix_pre, ln_mix_post, ln_ffn_pre, ln_ffn_post, ln_mem, w_in, ln_v_gain, ln_v_bias,
           w_spatial, b_spatial, rel_bias, w_mem_kv, w_branch_a, w_branch_b, w_branch_c, w_out,
           w_ffn_gate, w_ffn_up, w_ffn_down):
    b, s, d = x.shape
    assert s % MOBA_BLOCK == 0 and s % TOKEN_TILE == 0 and d % V7X_LANES == 0
    depth = w_in.shape[0]
    cuts = [0, GMLP_WIDTH, 2 * GMLP_WIDTH, 2 * GMLP_WIDTH + MOBA_WIDTH, 2 * GMLP_WIDTH + 2 * MOBA_WIDTH,
            2 * GMLP_WIDTH + 3 * MOBA_WIDTH, 2 * GMLP_WIDTH + 3 * MOBA_WIDTH + MEM_WIDTH]
    rb_flat = rel_bias.astype(F32).reshape(-1)
    bias = _bias_tiles(rb_flat)
    row = lambda v: v.reshape(1, -1).astype(F32)
    for l in range(depth):
        wi = w_in[l]
        w_u, w_v, w_q, w_k, w_v2, w_cq = (wi[:, cuts[i]:cuts[i + 1]] for i in range(6))
        w_g = wi[:, cuts[6]:]
        x2 = x.reshape(b * s, d)

        h2, h_lo, hmean = _prep(x2, row(ln_mix_pre[l]))
        w_qt_scaled = (w_q * (HEAD_DIM ** -0.5)).T
        gt = _gatew(hmean.reshape(-1, d), w_k, w_qt_scaled, b)
        mask = _select(h2, h_lo, gt, b, s)
        qt, k, vt = _qkv(h2, b, s, (w_qt_scaled * LOG2E).astype(BF16), w_k.astype(BF16),
                         w_v2.T.astype(BF16))
        b_out_t = _attn(rb_flat, qt, k, vt, mask, bias)

        b_s_lanes = jnp.broadcast_to(b_spatial[l][:, :, None], (GMLP_GROUPS, CHUNK, GMLP_GROUP_DIM))
        a_out = _gmlp(h2, w_u.astype(BF16), w_v.astype(BF16), row(ln_v_gain[l]), row(ln_v_bias[l]),
                      w_spatial[l], b_s_lanes.astype(F32))

        wkv = w_mem_kv[l]
        kt_mem, v_mem = _memkv(mem, row(ln_mem[l]), wkv[:, :MEM_WIDTH].T.astype(BF16),
                               wkv[:, MEM_WIDTH:].astype(BF16))
        c_out = _memattn(h2, b, s, w_cq.astype(BF16), kt_mem, v_mem)

        x1, hn = _merge(x2, h2, a_out, b_out_t, c_out, w_g.astype(BF16), w_branch_a[l].astype(BF16),
                        w_branch_b[l].astype(BF16), w_branch_c[l].astype(BF16), w_out[l].astype(BF16),
                        row(ln_mix_post[l]), row(ln_ffn_pre[l]))
        out = _ffn(x1, hn, w_ffn_gate[l].astype(BF16), w_ffn_up[l].astype(BF16),
                   w_ffn_down[l].astype(BF16), row(ln_ffn_post[l]))
        x = out.reshape(b, s, d)
    return x
```

```python
import functools
import math

import jax
import jax.numpy as jnp
from jax import lax
from jax.experimental import pallas as pl
from jax.experimental.pallas import tpu as pltpu

F32 = jnp.float32
BF16 = jnp.bfloat16

EPS = 1e-6
NEG = -1e30
GMLP_GROUPS = 6
GMLP_GROUP_DIM = 128
GMLP_WIDTH = GMLP_GROUPS * GMLP_GROUP_DIM
CHUNK = 128
GMLP_SLAB = 256
MOBA_HEADS = 12
HEAD_DIM = 64
MOBA_WIDTH = MOBA_HEADS * HEAD_DIM
MOBA_BLOCK = 256
MOBA_TOPK = 3
REL_BUCKETS = 32
REL_MAX_DIST = 128
LOG2E = math.log2(math.e)
ONES_ROWS = 16
ATTN_HEADS = 2
SCORE_LOOKAHEAD = 4
MEM_HEADS = 4
MEM_HEAD_DIM = 128
MEM_WIDTH = MEM_HEADS * MEM_HEAD_DIM
MEM_SLAB = 256
N_BRANCHES = 3

V7X_LANES = 128
V7X_VMEM_LIMIT = 56 * 1024 * 1024

TOKEN_TILE = 512
QKV_SLAB = 256
GATE_ROWS = 128


def _params(n_axes, vmem=V7X_VMEM_LIMIT):
    return pltpu.CompilerParams(
        dimension_semantics=("arbitrary",) * n_axes, vmem_limit_bytes=vmem)


def _resident(shape):
    zeros = (0,) * len(shape)
    return pl.BlockSpec(shape, lambda *_: zeros, pipeline_mode=pl.Buffered(1))


def _rms(x, g):
    return x * lax.rsqrt(jnp.mean(x * x, axis=-1, keepdims=True) + EPS) * g


def _split_bf16(x):
    hi = x.astype(BF16)
    return hi, (x - hi.astype(F32)).astype(BF16)


def _gatew_kernel(hm_ref, wk_ref, wqt_ref, g_ref, gt_scr, *, n_batch, n_blk):
    kmean = jnp.dot(hm_ref[...], wk_ref[...], precision=lax.Precision.HIGHEST,
                    preferred_element_type=F32)
    used = MOBA_HEADS * n_blk
    gt_scr[:, used:, :] = jnp.zeros((n_batch, GATE_ROWS - used, gt_scr.shape[-1]), F32)
    for h in range(MOBA_HEADS):
        sl = slice(h * HEAD_DIM, (h + 1) * HEAD_DIM)
        res = jnp.dot(kmean[:, sl], wqt_ref[sl, :], precision=lax.Precision.HIGHEST,
                      preferred_element_type=F32)
        for b in range(n_batch):
            gt_scr[b, h * n_blk:(h + 1) * n_blk, :] = res[b * n_blk:(b + 1) * n_blk, :]
    for b in range(n_batch):
        hi, lo = _split_bf16(gt_scr[b].T)
        g_ref[b, :, 0:GATE_ROWS] = hi
        g_ref[b, :, GATE_ROWS:] = lo


def _gatew(hmean, w_k, w_qt, n_batch):
    rows, d = hmean.shape
    n_blk = rows // n_batch
    assert MOBA_HEADS * n_blk <= GATE_ROWS
    return pl.pallas_call(
        functools.partial(_gatew_kernel, n_batch=n_batch, n_blk=n_blk),
        grid=(1,),
        in_specs=[_resident((rows, d)), _resident(w_k.shape), _resident(w_qt.shape)],
        out_specs=pl.BlockSpec((n_batch, d, 2 * GATE_ROWS), lambda i: (0, 0, 0)),
        out_shape=jax.ShapeDtypeStruct((n_batch, d, 2 * GATE_ROWS), BF16),
        scratch_shapes=[pltpu.VMEM((n_batch, GATE_ROWS, d), F32)],
        compiler_params=_params(1),
        name="gatew",
    )(hmean, w_k, w_qt)


def _select_kernel(hhi_ref, hlo_ref, gw_ref, mask_ref, *, n_blk, tq):
    for i in range(tq // MOBA_BLOCK):
        rows = slice(i * MOBA_BLOCK, (i + 1) * MOBA_BLOCK)
        cur = pl.program_id(1) * (tq // MOBA_BLOCK) + i
        res = jnp.dot(jnp.concatenate([hhi_ref[rows, :], hlo_ref[rows, :]], axis=0), gw_ref[0],
                      preferred_element_type=F32)
        hi, lo = res[:MOBA_BLOCK], res[MOBA_BLOCK:]
        gate = (hi[:, :GATE_ROWS] + (hi[:, GATE_ROWS:] + lo[:, :GATE_ROWS])) + lo[:, GATE_ROWS:]
        gate = gate.T[0:MOBA_HEADS * n_blk].reshape(MOBA_HEADS, n_blk, MOBA_BLOCK)
        blk = lax.broadcasted_iota(jnp.int32, gate.shape, 1)
        past = blk < cur
        gate = jnp.where(past, gate, NEG)
        rank = jnp.zeros(gate.shape, jnp.int32)
        for m in range(n_blk):
            gm = gate[:, m:m + 1, :]
            tie = jnp.where(blk > m, 1, 0)
            rank = rank + jnp.where(gm > gate, 1, jnp.where(gm == gate, tie, 0))
        mask = jnp.where(rank < min(MOBA_TOPK, n_blk), jnp.where(past, 0.0, NEG), NEG)
        mask_ref[0, :, rows] = mask.astype(F32).reshape(MOBA_HEADS * n_blk, MOBA_BLOCK)


def _select(h_hi, h_lo, gt, b, s, tq=2 * TOKEN_TILE):
    d = h_hi.shape[-1]
    n_blk = s // MOBA_BLOCK
    nt = s // tq
    return pl.pallas_call(
        functools.partial(_select_kernel, n_blk=n_blk, tq=tq),
        grid=(b, nt),
        in_specs=[pl.BlockSpec((tq, d), lambda i, j: (i * nt + j, 0)),
                  pl.BlockSpec((tq, d), lambda i, j: (i * nt + j, 0)),
                  pl.BlockSpec((1, d, 2 * GATE_ROWS), lambda i, j: (i, 0, 0))],
        out_specs=pl.BlockSpec((1, MOBA_HEADS * n_blk, tq), lambda i, j: (i, 0, j)),
        out_shape=jax.ShapeDtypeStruct((b, MOBA_HEADS * n_blk, s), F32),
        compiler_params=_params(2),
        name="select",
    )(h_hi, h_lo, gt)


def _qkv_kernel(x_ref, g_ref, wqt_ref, wk_ref, wvt_ref, h_ref, hlo_ref, hmean_ref, qt_ref, k_ref, vt_ref):
    nt = (((1,), (1,)), ((), ()))
    low = lax.broadcasted_iota(jnp.int32, (QKV_SLAB, V7X_LANES), 1) < HEAD_DIM
    for i in range(x_ref.shape[0] // QKV_SLAB):
        rows = slice(i * QKV_SLAB, (i + 1) * QKV_SLAB)
        hf = _rms(x_ref[rows, :], g_ref[...])
        h, hlo_ref[rows, :] = _split_bf16(hf)
        h_ref[rows, :] = h
        for c in range(QKV_SLAB // MOBA_BLOCK):
            hmean_ref[i * (QKV_SLAB // MOBA_BLOCK) + c] = jnp.mean(
                hf[c * MOBA_BLOCK:(c + 1) * MOBA_BLOCK], axis=0, keepdims=True)
        qt_ref[0, :, rows] = lax.dot_general(wqt_ref[...], h, nt, preferred_element_type=F32).astype(BF16)
        vt_ref[0, :, rows] = lax.dot_general(wvt_ref[...], h, nt, preferred_element_type=F32).astype(BF16)
        kf = jnp.dot(h, wk_ref[...], preferred_element_type=F32)
        for p in range(MOBA_HEADS // 2):
            pair = kf[:, p * V7X_LANES:(p + 1) * V7X_LANES]
            even = jnp.where(low, pair, 0.0)
            odd = jnp.where(low, pltpu.roll(pair, HEAD_DIM, 1), 0.0)
            k_ref[0, rows, (2 * p) * V7X_LANES:(2 * p + 1) * V7X_LANES] = even.astype(BF16)
            k_ref[0, rows, (2 * p + 1) * V7X_LANES:(2 * p + 2) * V7X_LANES] = odd.astype(BF16)


def _qkv(x2, g, b, s, w_qt, w_k, w_vt, tm=2 * TOKEN_TILE):
    d = x2.shape[-1]
    nt = s // tm
    blocks = tm // MOBA_BLOCK
    tok = lambda i, j: (i * nt + j, 0)
    return pl.pallas_call(
        _qkv_kernel,
        grid=(b, nt),
        in_specs=[pl.BlockSpec((tm, d), tok), _resident((1, d)),
                  _resident(w_qt.shape), _resident(w_k.shape), _resident(w_vt.shape)],
        out_specs=[pl.BlockSpec((tm, d), tok), pl.BlockSpec((tm, d), tok),
                   pl.BlockSpec((blocks, 1, d), lambda i, j: (i * nt + j, 0, 0)),
                   pl.BlockSpec((1, MOBA_WIDTH, tm), lambda i, j: (i, 0, j)),
                   pl.BlockSpec((1, tm, MOBA_HEADS * V7X_LANES), lambda i, j: (i, j, 0)),
                   pl.BlockSpec((1, MOBA_WIDTH, tm), lambda i, j: (i, 0, j))],
        out_shape=[jax.ShapeDtypeStruct((b * s, d), BF16), jax.ShapeDtypeStruct((b * s, d), BF16),
                   jax.ShapeDtypeStruct((b * s // MOBA_BLOCK, 1, d), F32),
                   jax.ShapeDtypeStruct((b, MOBA_WIDTH, s), BF16),
                   jax.ShapeDtypeStruct((b, s, MOBA_HEADS * V7X_LANES), BF16),
                   jax.ShapeDtypeStruct((b, MOBA_WIDTH, s), BF16)],
        compiler_params=_params(2),
        name="qkv",
    )(x2, g, w_qt, w_k, w_vt)


def _bias_kernel(rb_ref, out_ref):
    h = pl.program_id(0)
    ik = lax.broadcasted_iota(jnp.int32, (MOBA_BLOCK, MOBA_BLOCK), 0)
    iq = lax.broadcasted_iota(jnp.int32, (MOBA_BLOCK, MOBA_BLOCK), 1)
    max_exact = REL_BUCKETS // 2
    for t in range(2):
        dist = iq - ik + t * MOBA_BLOCK
        n = jnp.maximum(dist, 0)
        nf = jnp.maximum(n, 1).astype(F32)
        large = max_exact + (jnp.log(nf / max_exact) / math.log(REL_MAX_DIST / max_exact)
                             * (REL_BUCKETS - max_exact)).astype(jnp.int32)
        large = jnp.minimum(large, REL_BUCKETS - 1)
        bucket = jnp.where(n < max_exact, n, large)
        val = jnp.zeros((MOBA_BLOCK, MOBA_BLOCK), F32)
        for bk in range(REL_BUCKETS):
            val = jnp.where(bucket == bk, rb_ref[bk * MOBA_HEADS + h], val)
        val = val * LOG2E
        if t == 0:
            val = jnp.where(dist >= 0, val, NEG)
        out_ref[0, t] = val


def _bias_tiles(rb_flat):
    return pl.pallas_call(
        _bias_kernel,
        grid=(MOBA_HEADS,),
        in_specs=[pl.BlockSpec(memory_space=pltpu.SMEM)],
        out_specs=pl.BlockSpec((1, 2, MOBA_BLOCK, MOBA_BLOCK), lambda h: (h, 0, 0, 0)),
        out_shape=jax.ShapeDtypeStruct((MOBA_HEADS, 2, MOBA_BLOCK, MOBA_BLOCK), F32),
        compiler_params=_params(1),
        name="bias",
    )(rb_flat)


def _attn_kernel(rb_ref, qt_ref, k_ref, vt_ref, mask_ref, bias_ref, o_ref, s_scr, *, n_blk):
    first_head = pl.program_id(1) * ATTN_HEADS
    far = [rb_ref[(REL_BUCKETS - 1) * MOBA_HEADS + first_head + hh] * LOG2E for hh in range(ATTN_HEADS)]

    def scores(hh, j, buf):
        cols = slice(j * MOBA_BLOCK, (j + 1) * MOBA_BLOCK)
        q = qt_ref[0, hh * HEAD_DIM:(hh + 1) * HEAD_DIM, cols]
        qz = jnp.concatenate([q, jnp.zeros_like(q)], axis=0)
        k = k_ref[0, 0:(j + 1) * MOBA_BLOCK, hh * V7X_LANES:(hh + 1) * V7X_LANES]
        s = jnp.dot(k, qz, preferred_element_type=F32)
        for n in range(j + 1):
            rows = slice(n * MOBA_BLOCK, (n + 1) * MOBA_BLOCK)
            mrow = mask_ref[0, hh * n_blk + n:hh * n_blk + n + 1, cols]
            if n == j:
                sn = s[rows] + bias_ref[hh, 0]
            elif n == j - 1:
                sn = s[rows] + bias_ref[hh, 1] + mrow
            else:
                sn = s[rows] + (mrow + far[hh])
            s_scr[buf, rows, :] = sn

    def finish(hh, j, buf):
        cols = slice(j * MOBA_BLOCK, (j + 1) * MOBA_BLOCK)
        heads = slice(hh * HEAD_DIM, (hh + 1) * HEAD_DIM)
        nk = (j + 1) * MOBA_BLOCK
        m = jnp.max(s_scr[buf, 0:nk, :], axis=0, keepdims=True)
        p = jnp.exp2(s_scr[buf, 0:nk, :] - m).astype(BF16)
        vt1 = jnp.concatenate([vt_ref[0, heads, 0:nk], jnp.ones((ONES_ROWS, nk), BF16)], axis=0)
        acc = jnp.dot(vt1, p, preferred_element_type=F32)
        o_ref[0, heads, cols] = (acc[0:HEAD_DIM] / acc[HEAD_DIM:HEAD_DIM + 1]).astype(BF16)

    tasks = [(hh, j) for j in reversed(range(n_blk)) for hh in range(ATTN_HEADS)]
    pending = []
    for i, (hh, j) in enumerate(tasks):
        buf = i % (SCORE_LOOKAHEAD + 1)
        scores(hh, j, buf)
        pending.append((hh, j, buf))
        if len(pending) > SCORE_LOOKAHEAD:
            finish(*pending.pop(0))
    for item in pending:
        finish(*item)


def _attn(rb_flat, qt, k, vt, mask, bias):
    b, _, s = qt.shape
    n_blk = s // MOBA_BLOCK
    nh = ATTN_HEADS
    return pl.pallas_call(
        functools.partial(_attn_kernel, n_blk=n_blk),
        grid=(b, MOBA_HEADS // nh),
        in_specs=[pl.BlockSpec(memory_space=pltpu.SMEM),
                  pl.BlockSpec((1, nh * HEAD_DIM, s), lambda i, h: (i, h, 0)),
                  pl.BlockSpec((1, s, nh * V7X_LANES), lambda i, h: (i, 0, h)),
                  pl.BlockSpec((1, nh * HEAD_DIM, s), lambda i, h: (i, h, 0)),
                  pl.BlockSpec((1, nh * n_blk, s), lambda i, h: (i, h, 0)),
                  pl.BlockSpec((nh, 2, MOBA_BLOCK, MOBA_BLOCK), lambda i, h: (h, 0, 0, 0))],
        out_specs=pl.BlockSpec((1, nh * HEAD_DIM, s), lambda i, h: (i, h, 0)),
        out_shape=jax.ShapeDtypeStruct((b, MOBA_WIDTH, s), BF16),
        scratch_shapes=[pltpu.VMEM((SCORE_LOOKAHEAD + 1, s, MOBA_BLOCK), F32)],
        compiler_params=_params(2),
        name="attn",
    )(rb_flat, qt, k, vt, mask, bias)


def _gmlp_kernel(h_ref, wu_ref, wv_ref, lng_ref, lnb_ref, ws_ref, bs_ref, o_ref):
    causal = (lax.broadcasted_iota(jnp.int32, (CHUNK, CHUNK), 0)
              >= lax.broadcasted_iota(jnp.int32, (CHUNK, CHUNK), 1))
    ws = [jnp.where(causal, ws_ref[g], 0.0).astype(BF16) for g in range(GMLP_GROUPS)]
    slabs = [slice(i * GMLP_SLAB, (i + 1) * GMLP_SLAB) for i in range(h_ref.shape[0] // GMLP_SLAB)]
    vs = [jnp.dot(h_ref[rows, :], wv_ref[...], preferred_element_type=F32) for rows in slabs]
    us = [jnp.dot(h_ref[rows, :], wu_ref[...], preferred_element_type=F32) for rows in slabs]
    for rows, v, u in zip(slabs, vs, us):
        v = jax.nn.gelu(v)
        vc = v - jnp.mean(v, axis=-1, keepdims=True)
        vn = vc * lax.rsqrt(jnp.mean(vc * vc, axis=-1, keepdims=True) + EPS) * lng_ref[...] + lnb_ref[...]
        vn = vn.astype(BF16)
        u = jax.nn.gelu(u)
        for g in range(GMLP_GROUPS):
            lanes = slice(g * GMLP_GROUP_DIM, (g + 1) * GMLP_GROUP_DIM)
            for c in range(GMLP_SLAB // CHUNK):
                toks = slice(c * CHUNK, (c + 1) * CHUNK)
                mixed = jnp.dot(ws[g], vn[toks, lanes], preferred_element_type=F32) + bs_ref[g]
                o_ref[rows.start + c * CHUNK:rows.start + (c + 1) * CHUNK, lanes] = (
                    u[toks, lanes] * mixed).astype(BF16)


def _gmlp(h2, w_u, w_v, ln_g, ln_b, w_s, b_s_lanes, tm=2 * TOKEN_TILE):
    n_tok, d = h2.shape
    return pl.pallas_call(
        _gmlp_kernel,
        grid=(n_tok // tm,),
        in_specs=[pl.BlockSpec((tm, d), lambda i: (i, 0)),
                  _resident(w_u.shape), _resident(w_v.shape),
                  _resident(ln_g.shape), _resident(ln_b.shape),
                  _resident(w_s.shape), _resident(b_s_lanes.shape)],
        out_specs=pl.BlockSpec((tm, GMLP_WIDTH), lambda i: (i, 0)),
        out_shape=jax.ShapeDtypeStruct((n_tok, GMLP_WIDTH), BF16),
        compiler_params=_params(1),
        name="gmlp",
    )(h2, w_u, w_v, ln_g, ln_b, w_s, b_s_lanes)


def _memkv_kernel(mem_ref, g_ref, wkt_ref, wv_ref, kt_ref, v_ref):
    mn = _rms(mem_ref[0], g_ref[...]).astype(BF16)
    kt_ref[0] = lax.dot_general(wkt_ref[...], mn, (((1,), (1,)), ((), ())),
                                preferred_element_type=F32).astype(BF16)
    v_ref[0] = jnp.dot(mn, wv_ref[...], preferred_element_type=F32).astype(BF16)


def _memkv(mem, g, w_kt, w_v):
    b, m, d = mem.shape
    return pl.pallas_call(
        _memkv_kernel,
        grid=(b,),
        in_specs=[pl.BlockSpec((1, m, d), lambda i: (i, 0, 0)), _resident((1, d)),
                  _resident(w_kt.shape), _resident(w_v.shape)],
        out_specs=[pl.BlockSpec((1, MEM_WIDTH, m), lambda i: (i, 0, 0)),
                   pl.BlockSpec((1, m, MEM_WIDTH), lambda i: (i, 0, 0))],
        out_shape=[jax.ShapeDtypeStruct((b, MEM_WIDTH, m), BF16),
                   jax.ShapeDtypeStruct((b, m, MEM_WIDTH), BF16)],
        compiler_params=_params(1),
        name="memkv",
    )(mem, g, w_kt, w_v)


def _memattn_kernel(h_ref, wq_ref, kt_ref, v_ref, o_ref):
    slabs = [slice(i * MEM_SLAB, (i + 1) * MEM_SLAB) for i in range(h_ref.shape[0] // MEM_SLAB)]
    cqs = [jnp.dot(h_ref[rows, :], wq_ref[...], preferred_element_type=F32).astype(BF16) for rows in slabs]
    ones = jnp.ones((v_ref.shape[1], MEM_HEAD_DIM), BF16)
    for rows, cq in zip(slabs, cqs):
        for hd in range(MEM_HEADS):
            sl = slice(hd * MEM_HEAD_DIM, (hd + 1) * MEM_HEAD_DIM)
            s = jnp.dot(cq[:, sl], kt_ref[0, sl, :], preferred_element_type=F32) * (MEM_HEAD_DIM ** -0.5 * LOG2E)
            p = jnp.exp2(s - jnp.max(s, axis=-1, keepdims=True)).astype(BF16)
            acc = jnp.dot(p, jnp.concatenate([v_ref[0, :, sl], ones], axis=1), preferred_element_type=F32)
            o_ref[rows, sl] = (acc[:, :MEM_HEAD_DIM] / acc[:, MEM_HEAD_DIM:]).astype(BF16)


def _memattn(h2, b, s, w_cq, kt, v, tm=TOKEN_TILE):
    d = h2.shape[-1]
    nt = s // tm
    m = kt.shape[-1]
    return pl.pallas_call(
        _memattn_kernel,
        grid=(b, nt),
        in_specs=[pl.BlockSpec((tm, d), lambda i, j: (i * nt + j, 0)),
                  _resident(w_cq.shape),
                  pl.BlockSpec((1, MEM_WIDTH, m), lambda i, j: (i, 0, 0)),
                  pl.BlockSpec((1, m, MEM_WIDTH), lambda i, j: (i, 0, 0))],
        out_specs=pl.BlockSpec((tm, MEM_WIDTH), lambda i, j: (i * nt + j, 0)),
        out_shape=jax.ShapeDtypeStruct((b * s, MEM_WIDTH), BF16),
        compiler_params=_params(2),
        name="memattn",
    )(h2, w_cq, kt, v)


def _merge_kernel(x_ref, h_ref, a_ref, bt_ref, c_ref, wg_ref, wa_ref, wb_ref, wc_ref, wo_ref,
                  lnpost_ref, lnpre_ref, x1_ref, h2_ref):
    d = x_ref.shape[-1]
    gates = jax.nn.sigmoid(jnp.dot(h_ref[...], wg_ref[...], preferred_element_type=F32))
    pa = jnp.dot(a_ref[...], wa_ref[...], preferred_element_type=F32)
    pb = lax.dot_general(bt_ref[0], wb_ref[...], (((0,), (0,)), ((), ())),
                         preferred_element_type=F32)
    pc = jnp.dot(c_ref[...], wc_ref[...], preferred_element_type=F32)
    merged = gates[:, :d] * pa + gates[:, d:2 * d] * pb + gates[:, 2 * d:] * pc
    mo = jnp.dot(merged.astype(BF16), wo_ref[...], preferred_element_type=F32)
    x1 = x_ref[...] + _rms(mo, lnpost_ref[...])
    x1_ref[...] = x1
    h2_ref[...] = _rms(x1, lnpre_ref[...]).astype(BF16)


def _merge(x2, h2, a_out, b_out_t, c_out, w_g, w_a, w_b, w_c, w_o, ln_post, ln_pre, tm=TOKEN_TILE):
    n_tok, d = x2.shape
    b, _, s = b_out_t.shape
    nt = s // tm
    tok = lambda i, j: (i * nt + j, 0)
    return pl.pallas_call(
        _merge_kernel,
        grid=(b, nt),
        in_specs=[pl.BlockSpec((tm, d), tok), pl.BlockSpec((tm, d), tok),
                  pl.BlockSpec((tm, GMLP_WIDTH), tok),
                  pl.BlockSpec((1, MOBA_WIDTH, tm), lambda i, j: (i, 0, j)),
                  pl.BlockSpec((tm, MEM_WIDTH), tok),
                  _resident(w_g.shape), _resident(w_a.shape), _resident(w_b.shape),
                  _resident(w_c.shape), _resident(w_o.shape),
                  _resident((1, d)), _resident((1, d))],
        out_specs=[pl.BlockSpec((tm, d), tok), pl.BlockSpec((tm, d), tok)],
        out_shape=[jax.ShapeDtypeStruct((n_tok, d), F32), jax.ShapeDtypeStruct((n_tok, d), BF16)],
        compiler_params=_params(2),
        name="merge",
    )(x2, h2, a_out, b_out_t, c_out, w_g, w_a, w_b, w_c, w_o, ln_post, ln_pre)


def _ffn_kernel(x1_ref, h2_ref, wg_ref, wu_ref, wd_ref, ln_ref, o_ref):
    h2 = h2_ref[...]
    g = jnp.dot(h2, wg_ref[...], preferred_element_type=F32)
    u = jnp.dot(h2, wu_ref[...], preferred_element_type=F32)
    act = (jax.nn.silu(g) * u).astype(BF16)
    f = jnp.dot(act, wd_ref[...], preferred_element_type=F32)
    o_ref[...] = x1_ref[...] + _rms(f, ln_ref[...])


def _ffn(x1, h2, w_g, w_u, w_d, ln_post, tm=TOKEN_TILE):
    n_tok, d = x1.shape
    return pl.pallas_call(
        _ffn_kernel,
        grid=(n_tok // tm,),
        in_specs=[pl.BlockSpec((tm, d), lambda i: (i, 0)), pl.BlockSpec((tm, d), lambda i: (i, 0)),
                  _resident(w_g.shape), _resident(w_u.shape), _resident(w_d.shape),
                  _resident((1, d))],
        out_specs=pl.BlockSpec((tm, d), lambda i: (i, 0)),
        out_shape=jax.ShapeDtypeStruct((n_tok, d), F32),
        compiler_params=_params(1),
        name="ffn",
    )(x1, h2, w_g, w_u, w_d, ln_post)


def kernel(x, mem, ln_mix_pre, ln_mix_post, ln_ffn_pre, ln_ffn_post, ln_mem, w_in, ln_v_gain, ln_v_bias,
           w_spatial, b_spatial, rel_bias, w_mem_kv, w_branch_a, w_branch_b, w_branch_c, w_out,
           w_ffn_gate, w_ffn_up, w_ffn_down):
    b, s, d = x.shape
    assert s % MOBA_BLOCK == 0 and s % TOKEN_TILE == 0 and d % V7X_LANES == 0
    depth = w_in.shape[0]
    cuts = [0, GMLP_WIDTH, 2 * GMLP_WIDTH, 2 * GMLP_WIDTH + MOBA_WIDTH, 2 * GMLP_WIDTH + 2 * MOBA_WIDTH,
            2 * GMLP_WIDTH + 3 * MOBA_WIDTH, 2 * GMLP_WIDTH + 3 * MOBA_WIDTH + MEM_WIDTH]
    rb_flat = rel_bias.astype(F32).reshape(-1)
    bias = _bias_tiles(rb_flat)
    row = lambda v: v.reshape(1, -1).astype(F32)
    for l in range(depth):
        wi = w_in[l]
        w_u, w_v, w_q, w_k, w_v2, w_cq = (wi[:, cuts[i]:cuts[i + 1]] for i in range(6))
        w_g = wi[:, cuts[6]:]
        x2 = x.reshape(b * s, d)

        w_qt_scaled = (w_q * (HEAD_DIM ** -0.5)).T
        h2, h_lo, hmean, qt, k, vt = _qkv(x2, row(ln_mix_pre[l]), b, s, (w_qt_scaled * LOG2E).astype(BF16),
                                          w_k.astype(BF16), w_v2.T.astype(BF16))
        gt = _gatew(hmean.reshape(-1, d), w_k, w_qt_scaled, b)
        mask = _select(h2, h_lo, gt, b, s)
        b_out_t = _attn(rb_flat, qt, k, vt, mask, bias)

        b_s_lanes = jnp.broadcast_to(b_spatial[l][:, :, None], (GMLP_GROUPS, CHUNK, GMLP_GROUP_DIM))
        a_out = _gmlp(h2, w_u.astype(BF16), w_v.astype(BF16), row(ln_v_gain[l]), row(ln_v_bias[l]),
                      w_spatial[l], b_s_lanes.astype(F32))

        wkv = w_mem_kv[l]
        kt_mem, v_mem = _memkv(mem, row(ln_mem[l]), wkv[:, :MEM_WIDTH].T.astype(BF16),
                               wkv[:, MEM_WIDTH:].astype(BF16))
        c_out = _memattn(h2, b, s, w_cq.astype(BF16), kt_mem, v_mem)

        x1, hn = _merge(x2, h2, a_out, b_out_t, c_out, w_g.astype(BF16), w_branch_a[l].astype(BF16),
                        w_branch_b[l].astype(BF16), w_branch_c[l].astype(BF16), w_out[l].astype(BF16),
                        row(ln_mix_post[l]), row(ln_ffn_pre[l]))
        out = _ffn(x1, hn, w_ffn_gate[l].astype(BF16), w_ffn_up[l].astype(BF16),
                   w_ffn_down[l].astype(BF16), row(ln_ffn_post[l]))
        x = out.reshape(b, s, d)
    return x
```

```python
import functools
import math

import jax
import jax.numpy as jnp
from jax import lax
from jax.experimental import pallas as pl
from jax.experimental.pallas import tpu as pltpu

F32 = jnp.float32
BF16 = jnp.bfloat16

EPS = 1e-6
NEG = -1e30
GMLP_GROUPS = 6
GMLP_GROUP_DIM = 128
GMLP_WIDTH = GMLP_GROUPS * GMLP_GROUP_DIM
CHUNK = 128
GMLP_SLAB = 256
MOBA_HEADS = 12
HEAD_DIM = 64
MOBA_WIDTH = MOBA_HEADS * HEAD_DIM
MOBA_BLOCK = 256
MOBA_TOPK = 3
REL_BUCKETS = 32
REL_MAX_DIST = 128
LOG2E = math.log2(math.e)
ONES_ROWS = 16
ATTN_HEADS = 2
SCORE_LOOKAHEAD = 4
MEM_HEADS = 4
MEM_HEAD_DIM = 128
MEM_WIDTH = MEM_HEADS * MEM_HEAD_DIM
MEM_SLAB = 256
N_BRANCHES = 3

V7X_LANES = 128
V7X_VMEM_LIMIT = 56 * 1024 * 1024

TOKEN_TILE = 512
MLP_SLAB = 256
QKV_SLAB = 256
GATE_ROWS = 128


def _params(n_axes, vmem=V7X_VMEM_LIMIT):
    return pltpu.CompilerParams(
        dimension_semantics=("arbitrary",) * n_axes, vmem_limit_bytes=vmem)


def _resident(shape):
    zeros = (0,) * len(shape)
    return pl.BlockSpec(shape, lambda *_: zeros, pipeline_mode=pl.Buffered(1))


def _rms(x, g):
    return x * lax.rsqrt(jnp.mean(x * x, axis=-1, keepdims=True) + EPS) * g


def _split_bf16(x):
    hi = x.astype(BF16)
    return hi, (x - hi.astype(F32)).astype(BF16)


def _gatew_kernel(hm_ref, wk_ref, wqt_ref, g_ref, gt_scr, *, n_batch, n_blk):
    kmean = jnp.dot(hm_ref[...], wk_ref[...], precision=lax.Precision.HIGHEST,
                    preferred_element_type=F32)
    used = MOBA_HEADS * n_blk
    gt_scr[:, used:, :] = jnp.zeros((n_batch, GATE_ROWS - used, gt_scr.shape[-1]), F32)
    for h in range(MOBA_HEADS):
        sl = slice(h * HEAD_DIM, (h + 1) * HEAD_DIM)
        res = jnp.dot(kmean[:, sl], wqt_ref[sl, :], precision=lax.Precision.HIGHEST,
                      preferred_element_type=F32)
        for b in range(n_batch):
            gt_scr[b, h * n_blk:(h + 1) * n_blk, :] = res[b * n_blk:(b + 1) * n_blk, :]
    for b in range(n_batch):
        hi, lo = _split_bf16(gt_scr[b].T)
        g_ref[b, :, 0:GATE_ROWS] = hi
        g_ref[b, :, GATE_ROWS:] = lo


def _gatew(hmean, w_k, w_qt, n_batch):
    rows, d = hmean.shape
    n_blk = rows // n_batch
    assert MOBA_HEADS * n_blk <= GATE_ROWS
    return pl.pallas_call(
        functools.partial(_gatew_kernel, n_batch=n_batch, n_blk=n_blk),
        grid=(1,),
        in_specs=[_resident((rows, d)), _resident(w_k.shape), _resident(w_qt.shape)],
        out_specs=pl.BlockSpec((n_batch, d, 2 * GATE_ROWS), lambda i: (0, 0, 0)),
        out_shape=jax.ShapeDtypeStruct((n_batch, d, 2 * GATE_ROWS), BF16),
        scratch_shapes=[pltpu.VMEM((n_batch, GATE_ROWS, d), F32)],
        compiler_params=_params(1),
        name="gatew",
    )(hmean, w_k, w_qt)


def _select_kernel(hhi_ref, hlo_ref, gw_ref, mask_ref, *, n_blk, tq):
    for i in range(tq // MOBA_BLOCK):
        rows = slice(i * MOBA_BLOCK, (i + 1) * MOBA_BLOCK)
        cur = pl.program_id(1) * (tq // MOBA_BLOCK) + i
        res = jnp.dot(jnp.concatenate([hhi_ref[rows, :], hlo_ref[rows, :]], axis=0), gw_ref[0],
                      preferred_element_type=F32)
        hi, lo = res[:MOBA_BLOCK], res[MOBA_BLOCK:]
        gate = (hi[:, :GATE_ROWS] + (hi[:, GATE_ROWS:] + lo[:, :GATE_ROWS])) + lo[:, GATE_ROWS:]
        gate = gate.T[0:MOBA_HEADS * n_blk].reshape(MOBA_HEADS, n_blk, MOBA_BLOCK)
        blk = lax.broadcasted_iota(jnp.int32, gate.shape, 1)
        past = blk < cur
        gate = jnp.where(past, gate, NEG)
        rank = jnp.zeros(gate.shape, jnp.int32)
        for m in range(n_blk):
            gm = gate[:, m:m + 1, :]
            tie = jnp.where(blk > m, 1, 0)
            rank = rank + jnp.where(gm > gate, 1, jnp.where(gm == gate, tie, 0))
        mask = jnp.where(rank < min(MOBA_TOPK, n_blk), jnp.where(past, 0.0, NEG), NEG)
        mask_ref[0, :, rows] = mask.astype(F32).reshape(MOBA_HEADS * n_blk, MOBA_BLOCK)


def _select(h_hi, h_lo, gt, b, s, tq=2 * TOKEN_TILE):
    d = h_hi.shape[-1]
    n_blk = s // MOBA_BLOCK
    nt = s // tq
    return pl.pallas_call(
        functools.partial(_select_kernel, n_blk=n_blk, tq=tq),
        grid=(b, nt),
        in_specs=[pl.BlockSpec((tq, d), lambda i, j: (i * nt + j, 0)),
                  pl.BlockSpec((tq, d), lambda i, j: (i * nt + j, 0)),
                  pl.BlockSpec((1, d, 2 * GATE_ROWS), lambda i, j: (i, 0, 0))],
        out_specs=pl.BlockSpec((1, MOBA_HEADS * n_blk, tq), lambda i, j: (i, 0, j)),
        out_shape=jax.ShapeDtypeStruct((b, MOBA_HEADS * n_blk, s), F32),
        compiler_params=_params(2),
        name="select",
    )(h_hi, h_lo, gt)


def _qkv_kernel(x_ref, g_ref, wqt_ref, wk_ref, wvt_ref, h_ref, hlo_ref, hmean_ref, qt_ref, k_ref, vt_ref):
    nt = (((1,), (1,)), ((), ()))
    low = lax.broadcasted_iota(jnp.int32, (QKV_SLAB, V7X_LANES), 1) < HEAD_DIM
    for i in range(x_ref.shape[0] // QKV_SLAB):
        rows = slice(i * QKV_SLAB, (i + 1) * QKV_SLAB)
        hf = _rms(x_ref[rows, :], g_ref[...])
        h, hlo_ref[rows, :] = _split_bf16(hf)
        h_ref[rows, :] = h
        for c in range(QKV_SLAB // MOBA_BLOCK):
            hmean_ref[i * (QKV_SLAB // MOBA_BLOCK) + c] = jnp.mean(
                hf[c * MOBA_BLOCK:(c + 1) * MOBA_BLOCK], axis=0, keepdims=True)
        qt_ref[0, :, rows] = lax.dot_general(wqt_ref[...], h, nt, preferred_element_type=F32).astype(BF16)
        vt_ref[0, :, rows] = lax.dot_general(wvt_ref[...], h, nt, preferred_element_type=F32).astype(BF16)
        kf = jnp.dot(h, wk_ref[...], preferred_element_type=F32)
        for p in range(MOBA_HEADS // 2):
            pair = kf[:, p * V7X_LANES:(p + 1) * V7X_LANES]
            even = jnp.where(low, pair, 0.0)
            odd = jnp.where(low, pltpu.roll(pair, HEAD_DIM, 1), 0.0)
            k_ref[0, rows, (2 * p) * V7X_LANES:(2 * p + 1) * V7X_LANES] = even.astype(BF16)
            k_ref[0, rows, (2 * p + 1) * V7X_LANES:(2 * p + 2) * V7X_LANES] = odd.astype(BF16)


def _qkv(x2, g, b, s, w_qt, w_k, w_vt, tm=2 * TOKEN_TILE):
    d = x2.shape[-1]
    nt = s // tm
    blocks = tm // MOBA_BLOCK
    tok = lambda i, j: (i * nt + j, 0)
    return pl.pallas_call(
        _qkv_kernel,
        grid=(b, nt),
        in_specs=[pl.BlockSpec((tm, d), tok), _resident((1, d)),
                  _resident(w_qt.shape), _resident(w_k.shape), _resident(w_vt.shape)],
        out_specs=[pl.BlockSpec((tm, d), tok), pl.BlockSpec((tm, d), tok),
                   pl.BlockSpec((blocks, 1, d), lambda i, j: (i * nt + j, 0, 0)),
                   pl.BlockSpec((1, MOBA_WIDTH, tm), lambda i, j: (i, 0, j)),
                   pl.BlockSpec((1, tm, MOBA_HEADS * V7X_LANES), lambda i, j: (i, j, 0)),
                   pl.BlockSpec((1, MOBA_WIDTH, tm), lambda i, j: (i, 0, j))],
        out_shape=[jax.ShapeDtypeStruct((b * s, d), BF16), jax.ShapeDtypeStruct((b * s, d), BF16),
                   jax.ShapeDtypeStruct((b * s // MOBA_BLOCK, 1, d), F32),
                   jax.ShapeDtypeStruct((b, MOBA_WIDTH, s), BF16),
                   jax.ShapeDtypeStruct((b, s, MOBA_HEADS * V7X_LANES), BF16),
                   jax.ShapeDtypeStruct((b, MOBA_WIDTH, s), BF16)],
        compiler_params=_params(2),
        name="qkv",
    )(x2, g, w_qt, w_k, w_vt)


def _bias_kernel(rb_ref, out_ref):
    h = pl.program_id(0)
    ik = lax.broadcasted_iota(jnp.int32, (MOBA_BLOCK, MOBA_BLOCK), 0)
    iq = lax.broadcasted_iota(jnp.int32, (MOBA_BLOCK, MOBA_BLOCK), 1)
    max_exact = REL_BUCKETS // 2
    for t in range(2):
        dist = iq - ik + t * MOBA_BLOCK
        n = jnp.maximum(dist, 0)
        nf = jnp.maximum(n, 1).astype(F32)
        large = max_exact + (jnp.log(nf / max_exact) / math.log(REL_MAX_DIST / max_exact)
                             * (REL_BUCKETS - max_exact)).astype(jnp.int32)
        large = jnp.minimum(large, REL_BUCKETS - 1)
        bucket = jnp.where(n < max_exact, n, large)
        val = jnp.zeros((MOBA_BLOCK, MOBA_BLOCK), F32)
        for bk in range(REL_BUCKETS):
            val = jnp.where(bucket == bk, rb_ref[bk * MOBA_HEADS + h], val)
        val = val * LOG2E
        if t == 0:
            val = jnp.where(dist >= 0, val, NEG)
        out_ref[0, t] = val


def _bias_tiles(rb_flat):
    return pl.pallas_call(
        _bias_kernel,
        grid=(MOBA_HEADS,),
        in_specs=[pl.BlockSpec(memory_space=pltpu.SMEM)],
        out_specs=pl.BlockSpec((1, 2, MOBA_BLOCK, MOBA_BLOCK), lambda h: (h, 0, 0, 0)),
        out_shape=jax.ShapeDtypeStruct((MOBA_HEADS, 2, MOBA_BLOCK, MOBA_BLOCK), F32),
        compiler_params=_params(1),
        name="bias",
    )(rb_flat)


def _attn_kernel(rb_ref, qt_ref, k_ref, vt_ref, mask_ref, bias_ref, o_ref, s_scr, *, n_blk):
    first_head = pl.program_id(1) * ATTN_HEADS
    far = [rb_ref[(REL_BUCKETS - 1) * MOBA_HEADS + first_head + hh] * LOG2E for hh in range(ATTN_HEADS)]

    def scores(hh, j, buf):
        cols = slice(j * MOBA_BLOCK, (j + 1) * MOBA_BLOCK)
        q = qt_ref[0, hh * HEAD_DIM:(hh + 1) * HEAD_DIM, cols]
        qz = jnp.concatenate([q, jnp.zeros_like(q)], axis=0)
        k = k_ref[0, 0:(j + 1) * MOBA_BLOCK, hh * V7X_LANES:(hh + 1) * V7X_LANES]
        s = jnp.dot(k, qz, preferred_element_type=F32)
        for n in range(j + 1):
            rows = slice(n * MOBA_BLOCK, (n + 1) * MOBA_BLOCK)
            mrow = mask_ref[0, hh * n_blk + n:hh * n_blk + n + 1, cols]
            if n == j:
                sn = s[rows] + bias_ref[hh, 0]
            elif n == j - 1:
                sn = s[rows] + bias_ref[hh, 1] + mrow
            else:
                sn = s[rows] + (mrow + far[hh])
            s_scr[buf, rows, :] = sn

    def finish(hh, j, buf):
        cols = slice(j * MOBA_BLOCK, (j + 1) * MOBA_BLOCK)
        heads = slice(hh * HEAD_DIM, (hh + 1) * HEAD_DIM)
        nk = (j + 1) * MOBA_BLOCK
        m = jnp.max(s_scr[buf, 0:nk, :], axis=0, keepdims=True)
        p = jnp.exp2(s_scr[buf, 0:nk, :] - m).astype(BF16)
        vt1 = jnp.concatenate([vt_ref[0, heads, 0:nk], jnp.ones((ONES_ROWS, nk), BF16)], axis=0)
        acc = jnp.dot(vt1, p, preferred_element_type=F32)
        o_ref[0, heads, cols] = (acc[0:HEAD_DIM] / acc[HEAD_DIM:HEAD_DIM + 1]).astype(BF16)

    tasks = [(hh, j) for j in reversed(range(n_blk)) for hh in range(ATTN_HEADS)]
    pending = []
    for i, (hh, j) in enumerate(tasks):
        buf = i % (SCORE_LOOKAHEAD + 1)
        scores(hh, j, buf)
        pending.append((hh, j, buf))
        if len(pending) > SCORE_LOOKAHEAD:
            finish(*pending.pop(0))
    for item in pending:
        finish(*item)


def _attn(rb_flat, qt, k, vt, mask, bias):
    b, _, s = qt.shape
    n_blk = s // MOBA_BLOCK
    nh = ATTN_HEADS
    return pl.pallas_call(
        functools.partial(_attn_kernel, n_blk=n_blk),
        grid=(b, MOBA_HEADS // nh),
        in_specs=[pl.BlockSpec(memory_space=pltpu.SMEM),
                  pl.BlockSpec((1, nh * HEAD_DIM, s), lambda i, h: (i, h, 0)),
                  pl.BlockSpec((1, s, nh * V7X_LANES), lambda i, h: (i, 0, h)),
                  pl.BlockSpec((1, nh * HEAD_DIM, s), lambda i, h: (i, h, 0)),
                  pl.BlockSpec((1, nh * n_blk, s), lambda i, h: (i, h, 0)),
                  pl.BlockSpec((nh, 2, MOBA_BLOCK, MOBA_BLOCK), lambda i, h: (h, 0, 0, 0))],
        out_specs=pl.BlockSpec((1, nh * HEAD_DIM, s), lambda i, h: (i, h, 0)),
        out_shape=jax.ShapeDtypeStruct((b, MOBA_WIDTH, s), BF16),
        scratch_shapes=[pltpu.VMEM((SCORE_LOOKAHEAD + 1, s, MOBA_BLOCK), F32)],
        compiler_params=_params(2),
        name="attn",
    )(rb_flat, qt, k, vt, mask, bias)


def _gmlp_kernel(h_ref, wu_ref, wv_ref, lng_ref, lnb_ref, ws_ref, bs_ref, o_ref):
    causal = (lax.broadcasted_iota(jnp.int32, (CHUNK, CHUNK), 0)
              >= lax.broadcasted_iota(jnp.int32, (CHUNK, CHUNK), 1))
    ws = [jnp.where(causal, ws_ref[g], 0.0).astype(BF16) for g in range(GMLP_GROUPS)]
    slabs = [slice(i * GMLP_SLAB, (i + 1) * GMLP_SLAB) for i in range(h_ref.shape[0] // GMLP_SLAB)]
    vs = [jnp.dot(h_ref[rows, :], wv_ref[...], preferred_element_type=F32) for rows in slabs]
    us = [jnp.dot(h_ref[rows, :], wu_ref[...], preferred_element_type=F32) for rows in slabs]
    for rows, v, u in zip(slabs, vs, us):
        v = jax.nn.gelu(v)
        vc = v - jnp.mean(v, axis=-1, keepdims=True)
        vn = vc * lax.rsqrt(jnp.mean(vc * vc, axis=-1, keepdims=True) + EPS) * lng_ref[...] + lnb_ref[...]
        vn = vn.astype(BF16)
        u = jax.nn.gelu(u)
        for g in range(GMLP_GROUPS):
            lanes = slice(g * GMLP_GROUP_DIM, (g + 1) * GMLP_GROUP_DIM)
            for c in range(GMLP_SLAB // CHUNK):
                toks = slice(c * CHUNK, (c + 1) * CHUNK)
                mixed = jnp.dot(ws[g], vn[toks, lanes], preferred_element_type=F32) + bs_ref[g]
                o_ref[rows.start + c * CHUNK:rows.start + (c + 1) * CHUNK, lanes] = (
                    u[toks, lanes] * mixed).astype(BF16)


def _gmlp(h2, w_u, w_v, ln_g, ln_b, w_s, b_s_lanes, tm=2 * TOKEN_TILE):
    n_tok, d = h2.shape
    return pl.pallas_call(
        _gmlp_kernel,
        grid=(n_tok // tm,),
        in_specs=[pl.BlockSpec((tm, d), lambda i: (i, 0)),
                  _resident(w_u.shape), _resident(w_v.shape),
                  _resident(ln_g.shape), _resident(ln_b.shape),
                  _resident(w_s.shape), _resident(b_s_lanes.shape)],
        out_specs=pl.BlockSpec((tm, GMLP_WIDTH), lambda i: (i, 0)),
        out_shape=jax.ShapeDtypeStruct((n_tok, GMLP_WIDTH), BF16),
        compiler_params=_params(1),
        name="gmlp",
    )(h2, w_u, w_v, ln_g, ln_b, w_s, b_s_lanes)


def _memkv_kernel(mem_ref, g_ref, wkt_ref, wv_ref, kt_ref, v_ref):
    mn = _rms(mem_ref[0], g_ref[...]).astype(BF16)
    kt_ref[0] = lax.dot_general(wkt_ref[...], mn, (((1,), (1,)), ((), ())),
                                preferred_element_type=F32).astype(BF16)
    v_ref[0] = jnp.dot(mn, wv_ref[...], preferred_element_type=F32).astype(BF16)


def _memkv(mem, g, w_kt, w_v):
    b, m, d = mem.shape
    return pl.pallas_call(
        _memkv_kernel,
        grid=(b,),
        in_specs=[pl.BlockSpec((1, m, d), lambda i: (i, 0, 0)), _resident((1, d)),
                  _resident(w_kt.shape), _resident(w_v.shape)],
        out_specs=[pl.BlockSpec((1, MEM_WIDTH, m), lambda i: (i, 0, 0)),
                   pl.BlockSpec((1, m, MEM_WIDTH), lambda i: (i, 0, 0))],
        out_shape=[jax.ShapeDtypeStruct((b, MEM_WIDTH, m), BF16),
                   jax.ShapeDtypeStruct((b, m, MEM_WIDTH), BF16)],
        compiler_params=_params(1),
        name="memkv",
    )(mem, g, w_kt, w_v)


def _memattn_kernel(h_ref, wq_ref, kt_ref, v_ref, o_ref):
    slabs = [slice(i * MEM_SLAB, (i + 1) * MEM_SLAB) for i in range(h_ref.shape[0] // MEM_SLAB)]
    cqs = [jnp.dot(h_ref[rows, :], wq_ref[...], preferred_element_type=F32).astype(BF16) for rows in slabs]
    ones = jnp.ones((v_ref.shape[1], MEM_HEAD_DIM), BF16)
    for rows, cq in zip(slabs, cqs):
        for hd in range(MEM_HEADS):
            sl = slice(hd * MEM_HEAD_DIM, (hd + 1) * MEM_HEAD_DIM)
            s = jnp.dot(cq[:, sl], kt_ref[0, sl, :], preferred_element_type=F32) * (MEM_HEAD_DIM ** -0.5 * LOG2E)
            p = jnp.exp2(s - jnp.max(s, axis=-1, keepdims=True)).astype(BF16)
            acc = jnp.dot(p, jnp.concatenate([v_ref[0, :, sl], ones], axis=1), preferred_element_type=F32)
            o_ref[rows, sl] = (acc[:, :MEM_HEAD_DIM] / acc[:, MEM_HEAD_DIM:]).astype(BF16)


def _memattn(h2, b, s, w_cq, kt, v, tm=TOKEN_TILE):
    d = h2.shape[-1]
    nt = s // tm
    m = kt.shape[-1]
    return pl.pallas_call(
        _memattn_kernel,
        grid=(b, nt),
        in_specs=[pl.BlockSpec((tm, d), lambda i, j: (i * nt + j, 0)),
                  _resident(w_cq.shape),
                  pl.BlockSpec((1, MEM_WIDTH, m), lambda i, j: (i, 0, 0)),
                  pl.BlockSpec((1, m, MEM_WIDTH), lambda i, j: (i, 0, 0))],
        out_specs=pl.BlockSpec((tm, MEM_WIDTH), lambda i, j: (i * nt + j, 0)),
        out_shape=jax.ShapeDtypeStruct((b * s, MEM_WIDTH), BF16),
        compiler_params=_params(2),
        name="memattn",
    )(h2, w_cq, kt, v)


def _merge_kernel(x_ref, h_ref, a_ref, bt_ref, c_ref, wg_ref, wa_ref, wb_ref, wc_ref, wo_ref,
                  lnpost_ref, lnpre_ref, x1_ref, h2_ref):
    d = x_ref.shape[-1]
    for i in range(x_ref.shape[0] // MLP_SLAB):
        rows = slice(i * MLP_SLAB, (i + 1) * MLP_SLAB)
        gates = jax.nn.sigmoid(jnp.dot(h_ref[rows, :], wg_ref[...], preferred_element_type=F32))
        pa = jnp.dot(a_ref[rows, :], wa_ref[...], preferred_element_type=F32)
        pb = lax.dot_general(bt_ref[0, :, rows], wb_ref[...], (((0,), (0,)), ((), ())),
                             preferred_element_type=F32)
        pc = jnp.dot(c_ref[rows, :], wc_ref[...], preferred_element_type=F32)
        merged = gates[:, :d] * pa + gates[:, d:2 * d] * pb + gates[:, 2 * d:] * pc
        mo = jnp.dot(merged.astype(BF16), wo_ref[...], preferred_element_type=F32)
        x1 = x_ref[rows, :] + _rms(mo, lnpost_ref[...])
        x1_ref[rows, :] = x1
        h2_ref[rows, :] = _rms(x1, lnpre_ref[...]).astype(BF16)


def _merge(x2, h2, a_out, b_out_t, c_out, w_g, w_a, w_b, w_c, w_o, ln_post, ln_pre, tm=2 * TOKEN_TILE):
    n_tok, d = x2.shape
    b, _, s = b_out_t.shape
    nt = s // tm
    tok = lambda i, j: (i * nt + j, 0)
    return pl.pallas_call(
        _merge_kernel,
        grid=(b, nt),
        in_specs=[pl.BlockSpec((tm, d), tok), pl.BlockSpec((tm, d), tok),
                  pl.BlockSpec((tm, GMLP_WIDTH), tok),
                  pl.BlockSpec((1, MOBA_WIDTH, tm), lambda i, j: (i, 0, j)),
                  pl.BlockSpec((tm, MEM_WIDTH), tok),
                  _resident(w_g.shape), _resident(w_a.shape), _resident(w_b.shape),
                  _resident(w_c.shape), _resident(w_o.shape),
                  _resident((1, d)), _resident((1, d))],
        out_specs=[pl.BlockSpec((tm, d), tok), pl.BlockSpec((tm, d), tok)],
        out_shape=[jax.ShapeDtypeStruct((n_tok, d), F32), jax.ShapeDtypeStruct((n_tok, d), BF16)],
        compiler_params=_params(2),
        name="merge",
    )(x2, h2, a_out, b_out_t, c_out, w_g, w_a, w_b, w_c, w_o, ln_post, ln_pre)


def _ffn_kernel(x1_ref, h2_ref, wg_ref, wu_ref, wd_ref, ln_ref, o_ref):
    for i in range(x1_ref.shape[0] // MLP_SLAB):
        rows = slice(i * MLP_SLAB, (i + 1) * MLP_SLAB)
        h2 = h2_ref[rows, :]
        g = jnp.dot(h2, wg_ref[...], preferred_element_type=F32)
        u = jnp.dot(h2, wu_ref[...], preferred_element_type=F32)
        act = (jax.nn.silu(g) * u).astype(BF16)
        f = jnp.dot(act, wd_ref[...], preferred_element_type=F32)
        o_ref[rows, :] = x1_ref[rows, :] + _rms(f, ln_ref[...])


def _ffn(x1, h2, w_g, w_u, w_d, ln_post, tm=2 * TOKEN_TILE):
    n_tok, d = x1.shape
    return pl.pallas_call(
        _ffn_kernel,
        grid=(n_tok // tm,),
        in_specs=[pl.BlockSpec((tm, d), lambda i: (i, 0)), pl.BlockSpec((tm, d), lambda i: (i, 0)),
                  _resident(w_g.shape), _resident(w_u.shape), _resident(w_d.shape),
                  _resident((1, d))],
        out_specs=pl.BlockSpec((tm, d), lambda i: (i, 0)),
        out_shape=jax.ShapeDtypeStruct((n_tok, d), F32),
        compiler_params=_params(1),
        name="ffn",
    )(x1, h2, w_g, w_u, w_d, ln_post)


def kernel(x, mem, ln_mix_pre, ln_mix_post, ln_ffn_pre, ln_ffn_post, ln_mem, w_in, ln_v_gain, ln_v_bias,
           w_spatial, b_spatial, rel_bias, w_mem_kv, w_branch_a, w_branch_b, w_branch_c, w_out,
           w_ffn_gate, w_ffn_up, w_ffn_down):
    b, s, d = x.shape
    assert s % MOBA_BLOCK == 0 and s % TOKEN_TILE == 0 and d % V7X_LANES == 0
    depth = w_in.shape[0]
    cuts = [0, GMLP_WIDTH, 2 * GMLP_WIDTH, 2 * GMLP_WIDTH + MOBA_WIDTH, 2 * GMLP_WIDTH + 2 * MOBA_WIDTH,
            2 * GMLP_WIDTH + 3 * MOBA_WIDTH, 2 * GMLP_WIDTH + 3 * MOBA_WIDTH + MEM_WIDTH]
    rb_flat = rel_bias.astype(F32).reshape(-1)
    bias = _bias_tiles(rb_flat)
    row = lambda v: v.reshape(1, -1).astype(F32)
    for l in range(depth):
        wi = w_in[l]
        w_u, w_v, w_q, w_k, w_v2, w_cq = (wi[:, cuts[i]:cuts[i + 1]] for i in range(6))
        w_g = wi[:, cuts[6]:]
        x2 = x.reshape(b * s, d)

        w_qt_scaled = (w_q * (HEAD_DIM ** -0.5)).T
        h2, h_lo, hmean, qt, k, vt = _qkv(x2, row(ln_mix_pre[l]), b, s, (w_qt_scaled * LOG2E).astype(BF16),
                                          w_k.astype(BF16), w_v2.T.astype(BF16))
        gt = _gatew(hmean.reshape(-1, d), w_k, w_qt_scaled, b)
        mask = _select(h2, h_lo, gt, b, s)
        b_out_t = _attn(rb_flat, qt, k, vt, mask, bias)

        b_s_lanes = jnp.broadcast_to(b_spatial[l][:, :, None], (GMLP_GROUPS, CHUNK, GMLP_GROUP_DIM))
        a_out = _gmlp(h2, w_u.astype(BF16), w_v.astype(BF16), row(ln_v_gain[l]), row(ln_v_bias[l]),
                      w_spatial[l], b_s_lanes.astype(F32))

        wkv = w_mem_kv[l]
        kt_mem, v_mem = _memkv(mem, row(ln_mem[l]), wkv[:, :MEM_WIDTH].T.astype(BF16),
                               wkv[:, MEM_WIDTH:].astype(BF16))
        c_out = _memattn(h2, b, s, w_cq.astype(BF16), kt_mem, v_mem)

        x1, hn = _merge(x2, h2, a_out, b_out_t, c_out, w_g.astype(BF16), w_branch_a[l].astype(BF16),
                        w_branch_b[l].astype(BF16), w_branch_c[l].astype(BF16), w_out[l].astype(BF16),
                        row(ln_mix_post[l]), row(ln_ffn_pre[l]))
        out = _ffn(x1, hn, w_ffn_gate[l].astype(BF16), w_ffn_up[l].astype(BF16),
                   w_ffn_down[l].astype(BF16), row(ln_ffn_post[l]))
        x = out.reshape(b, s, d)
    return x
```

```python
import functools
import math

import jax
import jax.numpy as jnp
from jax import lax
from jax.experimental import pallas as pl
from jax.experimental.pallas import tpu as pltpu

F32 = jnp.float32
BF16 = jnp.bfloat16

EPS = 1e-6
NEG = -1e30
GMLP_GROUPS = 6
GMLP_GROUP_DIM = 128
GMLP_WIDTH = GMLP_GROUPS * GMLP_GROUP_DIM
CHUNK = 128
GMLP_SLAB = 256
MOBA_HEADS = 12
HEAD_DIM = 64
MOBA_WIDTH = MOBA_HEADS * HEAD_DIM
MOBA_BLOCK = 256
MOBA_TOPK = 3
REL_BUCKETS = 32
REL_MAX_DIST = 128
LOG2E = math.log2(math.e)
ONES_ROWS = 16
ATTN_HEADS = 2
SCORE_LOOKAHEAD = 3
SCORE_BUFFERS = SCORE_LOOKAHEAD + 1
GATE_BACK = 1
MEM_HEADS = 4
MEM_HEAD_DIM = 128
MEM_WIDTH = MEM_HEADS * MEM_HEAD_DIM
MEM_SLAB = 256
N_BRANCHES = 3

V7X_LANES = 128
V7X_VMEM_LIMIT = 56 * 1024 * 1024

TOKEN_TILE = 512
MLP_TILE = 1024
MLP_SLAB = 256
QKV_SLAB = 256
GATE_ROWS = 128


def _params(n_axes, vmem=V7X_VMEM_LIMIT):
    return pltpu.CompilerParams(
        dimension_semantics=("arbitrary",) * n_axes, vmem_limit_bytes=vmem)


def _resident(shape):
    zeros = (0,) * len(shape)
    return pl.BlockSpec(shape, lambda *_: zeros, pipeline_mode=pl.Buffered(1))


def _rms(x, g):
    return x * lax.rsqrt(jnp.mean(x * x, axis=-1, keepdims=True) + EPS) * g


def _split_bf16(x):
    hi = x.astype(BF16)
    return hi, (x - hi.astype(F32)).astype(BF16)


def _gatew_kernel(hm_ref, wk_ref, wqt_ref, g_ref, gt_scr, *, n_batch, n_blk):
    kmean = jnp.dot(hm_ref[...], wk_ref[...], precision=lax.Precision.HIGHEST,
                    preferred_element_type=F32)
    used = MOBA_HEADS * n_blk
    gt_scr[:, used:, :] = jnp.zeros((n_batch, GATE_ROWS - used, gt_scr.shape[-1]), F32)
    for h in range(MOBA_HEADS):
        sl = slice(h * HEAD_DIM, (h + 1) * HEAD_DIM)
        res = jnp.dot(kmean[:, sl], wqt_ref[sl, :], precision=lax.Precision.HIGHEST,
                      preferred_element_type=F32)
        for b in range(n_batch):
            gt_scr[b, h * n_blk:(h + 1) * n_blk, :] = res[b * n_blk:(b + 1) * n_blk, :]
    for b in range(n_batch):
        hi, lo = _split_bf16(gt_scr[b].T)
        g_ref[b, :, 0:GATE_ROWS] = hi
        g_ref[b, :, GATE_ROWS:] = lo


def _gatew(hmean, w_k, w_qt, n_batch):
    rows, d = hmean.shape
    n_blk = rows // n_batch
    assert MOBA_HEADS * n_blk <= GATE_ROWS
    return pl.pallas_call(
        functools.partial(_gatew_kernel, n_batch=n_batch, n_blk=n_blk),
        grid=(1,),
        in_specs=[_resident((rows, d)), _resident(w_k.shape), _resident(w_qt.shape)],
        out_specs=pl.BlockSpec((n_batch, d, 2 * GATE_ROWS), lambda i: (0, 0, 0)),
        out_shape=jax.ShapeDtypeStruct((n_batch, d, 2 * GATE_ROWS), BF16),
        scratch_shapes=[pltpu.VMEM((n_batch, GATE_ROWS, d), F32)],
        compiler_params=_params(1),
        name="gatew",
    )(hmean, w_k, w_qt)


def _select_kernel(hhi_ref, hlo_ref, gw_ref, far_ref, qadd_ref, *, n_blk, tq):
    far = jnp.concatenate([far_ref[...]] * (MOBA_BLOCK // V7X_LANES), axis=-1)
    for i in range(tq // MOBA_BLOCK):
        rows = slice(i * MOBA_BLOCK, (i + 1) * MOBA_BLOCK)
        cur = pl.program_id(1) * (tq // MOBA_BLOCK) + i
        res = jnp.dot(jnp.concatenate([hhi_ref[rows, :], hlo_ref[rows, :]], axis=0), gw_ref[0],
                      preferred_element_type=F32)
        hi, lo = res[:MOBA_BLOCK], res[MOBA_BLOCK:]
        gate = (hi[:, :GATE_ROWS] + (hi[:, GATE_ROWS:] + lo[:, :GATE_ROWS])) + lo[:, GATE_ROWS:]
        gate = gate.T[0:MOBA_HEADS * n_blk].reshape(MOBA_HEADS, n_blk, MOBA_BLOCK)
        blk = lax.broadcasted_iota(jnp.int32, gate.shape, 1)
        past = blk < cur
        gate = jnp.where(past, gate, NEG)
        rank = jnp.zeros(gate.shape, jnp.int32)
        for m in range(n_blk):
            gm = gate[:, m:m + 1, :]
            tie = jnp.where(blk > m, 1, 0)
            rank = rank + jnp.where(gm > gate, 1, jnp.where(gm == gate, tie, 0))
        mask = jnp.where(rank < min(MOBA_TOPK, n_blk), jnp.where(past, 0.0, NEG), NEG)
        add = jnp.where(blk < cur - 1, mask + far, jnp.where(blk == cur - 1, mask, 0.0))
        hi, lo = _split_bf16(add)
        qadd_ref[0, :, :, rows] = jnp.concatenate([hi.astype(F32), lo.astype(F32)], axis=1).astype(BF16)


def _select(h_hi, h_lo, gt, far_lanes, b, s, tq=2 * TOKEN_TILE):
    d = h_hi.shape[-1]
    n_blk = s // MOBA_BLOCK
    nt = s // tq
    return pl.pallas_call(
        functools.partial(_select_kernel, n_blk=n_blk, tq=tq),
        grid=(b, nt),
        in_specs=[pl.BlockSpec((tq, d), lambda i, j: (i * nt + j, 0)),
                  pl.BlockSpec((tq, d), lambda i, j: (i * nt + j, 0)),
                  pl.BlockSpec((1, d, 2 * GATE_ROWS), lambda i, j: (i, 0, 0)),
                  _resident(far_lanes.shape)],
        out_specs=pl.BlockSpec((1, MOBA_HEADS, 2 * n_blk, tq), lambda i, j: (i, 0, 0, j)),
        out_shape=jax.ShapeDtypeStruct((b, MOBA_HEADS, 2 * n_blk, s), BF16),
        compiler_params=_params(2),
        name="select",
    )(h_hi, h_lo, gt, far_lanes)


def _qkv_kernel(x_ref, g_ref, wqt_ref, wk_ref, wvt_ref, h_ref, hlo_ref, hmean_ref, qt_ref, k_ref, vt_ref,
                *, n_blk):
    nt = (((1,), (1,)), ((), ()))
    n_slabs = x_ref.shape[0] // QKV_SLAB
    lane =lax.broadcasted_iota(jnp.int32, (QKV_SLAB, V7X_LANES), 1)
    low = lane < HEAD_DIM
    for i in range(n_slabs):
        rows = slice(i * QKV_SLAB, (i + 1) * QKV_SLAB)
        blk = pl.program_id(1) * n_slabs + i
        onehot = jnp.where((lane == HEAD_DIM + blk) | (lane == HEAD_DIM + n_blk + blk), 1.0, 0.0)
        hf = _rms(x_ref[rows, :], g_ref[...])
        h, hlo_ref[rows, :] = _split_bf16(hf)
        h_ref[rows, :] = h
        for c in range(QKV_SLAB // MOBA_BLOCK):
            hmean_ref[i * (QKV_SLAB // MOBA_BLOCK) + c] = jnp.mean(
                hf[c * MOBA_BLOCK:(c + 1) * MOBA_BLOCK], axis=0, keepdims=True)
        qt_ref[0, :, rows] = lax.dot_general(wqt_ref[...], h, nt, preferred_element_type=F32).astype(BF16)
        vt_ref[0, :, rows] = lax.dot_general(wvt_ref[...], h, nt, preferred_element_type=F32).astype(BF16)
        kf = jnp.dot(h, wk_ref[...], preferred_element_type=F32)
        for p in range(MOBA_HEADS // 2):
            pair = kf[:, p * V7X_LANES:(p + 1) * V7X_LANES]
            even = jnp.where(low, pair, onehot)
            odd = jnp.where(low, pltpu.roll(pair, HEAD_DIM, 1), onehot)
            k_ref[0, rows, (2 * p) * V7X_LANES:(2 * p + 1) * V7X_LANES] = even.astype(BF16)
            k_ref[0, rows, (2 * p + 1) * V7X_LANES:(2 * p + 2) * V7X_LANES] = odd.astype(BF16)


def _qkv(x2, g, b, s, w_qt, w_k, w_vt, tm=2 * TOKEN_TILE):
    d = x2.shape[-1]
    nt = s // tm
    blocks = tm // MOBA_BLOCK
    n_blk = s // MOBA_BLOCK
    assert QKV_SLAB == MOBA_BLOCK and HEAD_DIM + 2 * n_blk <= V7X_LANES
    tok = lambda i, j: (i * nt + j, 0)
    return pl.pallas_call(
        functools.partial(_qkv_kernel, n_blk=n_blk),
        grid=(b, nt),
        in_specs=[pl.BlockSpec((tm, d), tok), _resident((1, d)),
                  _resident(w_qt.shape), _resident(w_k.shape), _resident(w_vt.shape)],
        out_specs=[pl.BlockSpec((tm, d), tok), pl.BlockSpec((tm, d), tok),
                   pl.BlockSpec((blocks, 1, d), lambda i, j: (i * nt + j, 0, 0)),
                   pl.BlockSpec((1, MOBA_WIDTH, tm), lambda i, j: (i, 0, j)),
                   pl.BlockSpec((1, tm, MOBA_HEADS * V7X_LANES), lambda i, j: (i, j, 0)),
                   pl.BlockSpec((1, MOBA_WIDTH, tm), lambda i, j: (i, 0, j))],
        out_shape=[jax.ShapeDtypeStruct((b * s, d), BF16), jax.ShapeDtypeStruct((b * s, d), BF16),
                   jax.ShapeDtypeStruct((b * s // MOBA_BLOCK, 1, d), F32),
                   jax.ShapeDtypeStruct((b, MOBA_WIDTH, s), BF16),
                   jax.ShapeDtypeStruct((b, s, MOBA_HEADS * V7X_LANES), BF16),
                   jax.ShapeDtypeStruct((b, MOBA_WIDTH, s), BF16)],
        compiler_params=_params(2),
        name="qkv",
    )(x2, g, w_qt, w_k, w_vt)


def _bias_kernel(rb_ref, out_ref):
    h = pl.program_id(0)
    ik = lax.broadcasted_iota(jnp.int32, (MOBA_BLOCK, MOBA_BLOCK), 0)
    iq = lax.broadcasted_iota(jnp.int32, (MOBA_BLOCK, MOBA_BLOCK), 1)
    max_exact = REL_BUCKETS // 2
    for t in range(2):
        dist = iq - ik + t * MOBA_BLOCK
        n = jnp.maximum(dist, 0)
        nf = jnp.maximum(n, 1).astype(F32)
        large = max_exact + (jnp.log(nf / max_exact) / math.log(REL_MAX_DIST / max_exact)
                             * (REL_BUCKETS - max_exact)).astype(jnp.int32)
        large = jnp.minimum(large, REL_BUCKETS - 1)
        bucket = jnp.where(n < max_exact, n, large)
        val = jnp.zeros((MOBA_BLOCK, MOBA_BLOCK), F32)
        for bk in range(REL_BUCKETS):
            val = jnp.where(bucket == bk, rb_ref[bk * MOBA_HEADS + h], val)
        val = val * LOG2E
        if t == 0:
            val = jnp.where(dist >= 0, val, NEG)
        out_ref[0, t] = val


def _bias_tiles(rb_flat):
    return pl.pallas_call(
        _bias_kernel,
        grid=(MOBA_HEADS,),
        in_specs=[pl.BlockSpec(memory_space=pltpu.SMEM)],
        out_specs=pl.BlockSpec((1, 2, MOBA_BLOCK, MOBA_BLOCK), lambda h: (h, 0, 0, 0)),
        out_shape=jax.ShapeDtypeStruct((MOBA_HEADS, 2, MOBA_BLOCK, MOBA_BLOCK), F32),
        compiler_params=_params(1),
        name="bias",
    )(rb_flat)


def _zero_after(x):
    bits = lax.bitcast_convert_type(x, jnp.uint32)
    sixteen = jnp.uint32(16)
    return lax.bitcast_convert_type(
        lax.shift_right_logical(lax.shift_right_logical(bits, sixteen), sixteen), F32)


def _attn_kernel(qt_ref, qadd_ref, k_ref, vt_ref, bias_ref, o_ref, s_scr, *, n_blk):
    pad = jnp.zeros((V7X_LANES - HEAD_DIM - 2 * n_blk, MOBA_BLOCK), BF16)

    def scores(hh, j, buf):
        cols = slice(j * MOBA_BLOCK, (j + 1) * MOBA_BLOCK)
        q = qt_ref[0, hh * HEAD_DIM:(hh + 1) * HEAD_DIM, cols]
        qz = jnp.concatenate([q, qadd_ref[0, hh, :, cols], pad], axis=0)
        k = k_ref[0, 0:(j + 1) * MOBA_BLOCK, hh * V7X_LANES:(hh + 1) * V7X_LANES]
        s = jnp.dot(k, qz, preferred_element_type=F32)
        gates = []
        for n in range(j + 1):
            rows = slice(n * MOBA_BLOCK, (n + 1) * MOBA_BLOCK)
            if n == j:
                sn = s[rows] + bias_ref[hh, 0]
            elif n == j - 1:
                sn = s[rows] + bias_ref[hh, 1]
            else:
                sn = s[rows]
            s_scr[buf, rows, :] = sn
            gates.append(_zero_after(sn[0:1, :]))
        return gates

    def finish(hh, j, buf, gates):
        cols = slice(j * MOBA_BLOCK, (j + 1) * MOBA_BLOCK)
        heads = slice(hh * HEAD_DIM, (hh + 1) * HEAD_DIM)
        m = jnp.max(s_scr[buf, 0:(j + 1) * MOBA_BLOCK, :], axis=0, keepdims=True)
        ones = jnp.ones((ONES_ROWS, MOBA_BLOCK), BF16)
        acc = None
        for n in range(j + 1):
            rows = slice(n * MOBA_BLOCK, (n + 1) * MOBA_BLOCK)
            mn = m + gates[min(n, len(gates) - 1)] if gates else m
            p = jnp.exp2(s_scr[buf, rows, :] - mn).astype(BF16)
            part = jnp.dot(jnp.concatenate([vt_ref[0, heads, rows], ones], axis=0), p,
                           preferred_element_type=F32)
            acc = part if acc is None else acc + part
        o_ref[0, heads, cols] = (acc[0:HEAD_DIM] / acc[HEAD_DIM:HEAD_DIM + 1]).astype(BF16)

    tasks = [(hh, j) for j in reversed(range(n_blk)) for hh in range(ATTN_HEADS)]
    pending, issued = [], []
    for i, (hh, j) in enumerate(tasks):
        buf = i % SCORE_BUFFERS
        issued.append(scores(hh, j, buf))
        pending.append((hh, j, buf))
        if len(pending) > SCORE_LOOKAHEAD:
            finish(*pending.pop(0), issued[i - GATE_BACK])
    for item in pending:
        finish(*item, None)


def _attn(qt, qadd, k, vt, bias):
    b, _, s = qt.shape
    n_blk = s // MOBA_BLOCK
    nh = ATTN_HEADS
    return pl.pallas_call(
        functools.partial(_attn_kernel, n_blk=n_blk),
        grid=(b, MOBA_HEADS // nh),
        in_specs=[pl.BlockSpec((1, nh * HEAD_DIM, s), lambda i, h: (i, h, 0)),
                  pl.BlockSpec((1, nh, 2 * n_blk, s), lambda i, h: (i, h, 0, 0)),
                  pl.BlockSpec((1, s, nh * V7X_LANES), lambda i, h: (i, 0, h)),
                  pl.BlockSpec((1, nh * HEAD_DIM, s), lambda i, h: (i, h, 0)),
                  pl.BlockSpec((nh, 2, MOBA_BLOCK, MOBA_BLOCK), lambda i, h: (h, 0, 0, 0))],
        out_specs=pl.BlockSpec((1, nh * HEAD_DIM, s), lambda i, h: (i, h, 0)),
        out_shape=jax.ShapeDtypeStruct((b, MOBA_WIDTH, s), BF16),
        scratch_shapes=[pltpu.VMEM((SCORE_BUFFERS, s, MOBA_BLOCK), F32)],
        compiler_params=_params(2),
        name="attn",
    )(qt, qadd, k, vt, bias)


def _gmlp_kernel(h_ref, wu_ref, wv_ref, lng_ref, lnb_ref, ws_ref, bs_ref, o_ref):
    causal = (lax.broadcasted_iota(jnp.int32, (CHUNK, CHUNK), 0)
              >= lax.broadcasted_iota(jnp.int32, (CHUNK, CHUNK), 1))
    ws = [jnp.where(causal, ws_ref[g], 0.0).astype(BF16) for g in range(GMLP_GROUPS)]
    slabs = [slice(i * GMLP_SLAB, (i + 1) * GMLP_SLAB) for i in range(h_ref.shape[0] // GMLP_SLAB)]
    vs = [jnp.dot(h_ref[rows, :], wv_ref[...], preferred_element_type=F32) for rows in slabs]
    us = [jnp.dot(h_ref[rows, :], wu_ref[...], preferred_element_type=F32) for rows in slabs]
    for rows, v, u in zip(slabs, vs, us):
        v = jax.nn.gelu(v)
        vc = v - jnp.mean(v, axis=-1, keepdims=True)
        vn = vc * lax.rsqrt(jnp.mean(vc * vc, axis=-1, keepdims=True) + EPS) * lng_ref[...] + lnb_ref[...]
        vn = vn.astype(BF16)
        u = jax.nn.gelu(u)
        for g in range(GMLP_GROUPS):
            lanes = slice(g * GMLP_GROUP_DIM, (g + 1) * GMLP_GROUP_DIM)
            for c in range(GMLP_SLAB // CHUNK):
                toks = slice(c * CHUNK, (c + 1) * CHUNK)
                mixed = jnp.dot(ws[g], vn[toks, lanes], preferred_element_type=F32) + bs_ref[g]
                o_ref[rows.start + c * CHUNK:rows.start + (c + 1) * CHUNK, lanes] = (
                    u[toks, lanes] * mixed).astype(BF16)


def _gmlp(h2, w_u, w_v, ln_g, ln_b, w_s, b_s_lanes, tm=2 * TOKEN_TILE):
    n_tok, d = h2.shape
    return pl.pallas_call(
        _gmlp_kernel,
        grid=(n_tok // tm,),
        in_specs=[pl.BlockSpec((tm, d), lambda i: (i, 0)),
                  _resident(w_u.shape), _resident(w_v.shape),
                  _resident(ln_g.shape), _resident(ln_b.shape),
                  _resident(w_s.shape), _resident(b_s_lanes.shape)],
        out_specs=pl.BlockSpec((tm, GMLP_WIDTH), lambda i: (i, 0)),
        out_shape=jax.ShapeDtypeStruct((n_tok, GMLP_WIDTH), BF16),
        compiler_params=_params(1),
        name="gmlp",
    )(h2, w_u, w_v, ln_g, ln_b, w_s, b_s_lanes)


def _memkv_kernel(mem_ref, g_ref, wkt_ref, wv_ref, kt_ref, v_ref):
    mn = _rms(mem_ref[0], g_ref[...]).astype(BF16)
    kt_ref[0] = lax.dot_general(wkt_ref[...], mn, (((1,), (1,)), ((), ())),
                                preferred_element_type=F32).astype(BF16)
    v_ref[0] = jnp.dot(mn, wv_ref[...], preferred_element_type=F32).astype(BF16)


def _memkv(mem, g, w_kt, w_v):
    b, m, d = mem.shape
    return pl.pallas_call(
        _memkv_kernel,
        grid=(b,),
        in_specs=[pl.BlockSpec((1, m, d), lambda i: (i, 0, 0)), _resident((1, d)),
                  _resident(w_kt.shape), _resident(w_v.shape)],
        out_specs=[pl.BlockSpec((1, MEM_WIDTH, m), lambda i: (i, 0, 0)),
                   pl.BlockSpec((1, m, MEM_WIDTH), lambda i: (i, 0, 0))],
        out_shape=[jax.ShapeDtypeStruct((b, MEM_WIDTH, m), BF16),
                   jax.ShapeDtypeStruct((b, m, MEM_WIDTH), BF16)],
        compiler_params=_params(1),
        name="memkv",
    )(mem, g, w_kt, w_v)


def _memattn_kernel(h_ref, wq_ref, kt_ref, v_ref, o_ref):
    slabs = [slice(i * MEM_SLAB, (i + 1) * MEM_SLAB) for i in range(h_ref.shape[0] // MEM_SLAB)]
    cqs = [jnp.dot(h_ref[rows, :], wq_ref[...], preferred_element_type=F32).astype(BF16) for rows in slabs]
    ones = jnp.ones((v_ref.shape[1], MEM_HEAD_DIM), BF16)
    for rows, cq in zip(slabs, cqs):
        for hd in range(MEM_HEADS):
            sl = slice(hd * MEM_HEAD_DIM, (hd + 1) * MEM_HEAD_DIM)
            s = jnp.dot(cq[:, sl], kt_ref[0, sl, :], preferred_element_type=F32) * (MEM_HEAD_DIM ** -0.5 * LOG2E)
            p = jnp.exp2(s - jnp.max(s, axis=-1, keepdims=True)).astype(BF16)
            acc = jnp.dot(p, jnp.concatenate([v_ref[0, :, sl], ones], axis=1), preferred_element_type=F32)
            o_ref[rows, sl] = (acc[:, :MEM_HEAD_DIM] / acc[:, MEM_HEAD_DIM:]).astype(BF16)


def _memattn(h2, b, s, w_cq, kt, v, tm=TOKEN_TILE):
    d = h2.shape[-1]
    nt = s // tm
    m = kt.shape[-1]
    return pl.pallas_call(
        _memattn_kernel,
        grid=(b, nt),
        in_specs=[pl.BlockSpec((tm, d), lambda i, j: (i * nt + j, 0)),
                  _resident(w_cq.shape),
                  pl.BlockSpec((1, MEM_WIDTH, m), lambda i, j: (i, 0, 0)),
                  pl.BlockSpec((1, m, MEM_WIDTH), lambda i, j: (i, 0, 0))],
        out_specs=pl.BlockSpec((tm, MEM_WIDTH), lambda i, j: (i * nt + j, 0)),
        out_shape=jax.ShapeDtypeStruct((b * s, MEM_WIDTH), BF16),
        compiler_params=_params(2),
        name="memattn",
    )(h2, w_cq, kt, v)


def _merge_kernel(x_ref, h_ref, a_ref, bt_ref, c_ref, wg_ref, wa_ref, wb_ref, wc_ref, wo_ref,
                  lnpost_ref, lnpre_ref, x1_ref, h2_ref):
    d = x_ref.shape[-1]
    for i in range(x_ref.shape[0] // MLP_SLAB):
        rows = slice(i * MLP_SLAB, (i + 1) * MLP_SLAB)
        gates = jax.nn.sigmoid(jnp.dot(h_ref[rows, :], wg_ref[...], preferred_element_type=F32))
        pa = jnp.dot(a_ref[rows, :], wa_ref[...], preferred_element_type=F32)
        pb = lax.dot_general(bt_ref[0, :, rows], wb_ref[...], (((0,), (0,)), ((), ())),
                             preferred_element_type=F32)
        pc = jnp.dot(c_ref[rows, :], wc_ref[...], preferred_element_type=F32)
        merged = gates[:, :d] * pa + gates[:, d:2 * d] * pb + gates[:, 2 * d:] * pc
        mo = jnp.dot(merged.astype(BF16), wo_ref[...], preferred_element_type=F32)
        x1 = x_ref[rows, :] + _rms(mo, lnpost_ref[...])
        x1_ref[rows, :] = x1
        h2_ref[rows, :] = _rms(x1, lnpre_ref[...]).astype(BF16)


def _merge(x2, h2, a_out, b_out_t, c_out, w_g, w_a, w_b, w_c, w_o, ln_post, ln_pre, tm=MLP_TILE):
    n_tok, d = x2.shape
    b, _, s = b_out_t.shape
    nt = s // tm
    tok = lambda i, j: (i * nt + j, 0)
    return pl.pallas_call(
        _merge_kernel,
        grid=(b, nt),
        in_specs=[pl.BlockSpec((tm, d), tok), pl.BlockSpec((tm, d), tok),
                  pl.BlockSpec((tm, GMLP_WIDTH), tok),
                  pl.BlockSpec((1, MOBA_WIDTH, tm), lambda i, j: (i, 0, j)),
                  pl.BlockSpec((tm, MEM_WIDTH), tok),
                  _resident(w_g.shape), _resident(w_a.shape), _resident(w_b.shape),
                  _resident(w_c.shape), _resident(w_o.shape),
                  _resident((1, d)), _resident((1, d))],
        out_specs=[pl.BlockSpec((tm, d), tok), pl.BlockSpec((tm, d), tok)],
        out_shape=[jax.ShapeDtypeStruct((n_tok, d), F32), jax.ShapeDtypeStruct((n_tok, d), BF16)],
        compiler_params=_params(2),
        name="merge",
    )(x2, h2, a_out, b_out_t, c_out, w_g, w_a, w_b, w_c, w_o, ln_post, ln_pre)


def _ffn_kernel(x1_ref, h2_ref, wg_ref, wu_ref, wd_ref, ln_ref, o_ref):
    for i in range(x1_ref.shape[0] // MLP_SLAB):
        rows = slice(i * MLP_SLAB, (i + 1) * MLP_SLAB)
        h2 = h2_ref[rows, :]
        g = jnp.dot(h2, wg_ref[...], preferred_element_type=F32)
        u = jnp.dot(h2, wu_ref[...], preferred_element_type=F32)
        act = (jax.nn.silu(g) * u).astype(BF16)
        f = jnp.dot(act, wd_ref[...], preferred_element_type=F32)
        o_ref[rows, :] = x1_ref[rows, :] + _rms(f, ln_ref[...])


def _ffn(x1, h2, w_g, w_u, w_d, ln_post, tm=MLP_TILE):
    n_tok, d = x1.shape
    return pl.pallas_call(
        _ffn_kernel,
        grid=(n_tok // tm,),
        in_specs=[pl.BlockSpec((tm, d), lambda i: (i, 0)), pl.BlockSpec((tm, d), lambda i: (i, 0)),
                  _resident(w_g.shape), _resident(w_u.shape), _resident(w_d.shape),
                  _resident((1, d))],
        out_specs=pl.BlockSpec((tm, d), lambda i: (i, 0)),
        out_shape=jax.ShapeDtypeStruct((n_tok, d), F32),
        compiler_params=_params(1),
        name="ffn",
    )(x1, h2, w_g, w_u, w_d, ln_post)


def kernel(x, mem, ln_mix_pre, ln_mix_post, ln_ffn_pre, ln_ffn_post, ln_mem, w_in, ln_v_gain, ln_v_bias,
           w_spatial, b_spatial, rel_bias, w_mem_kv, w_branch_a, w_branch_b, w_branch_c, w_out,
           w_ffn_gate, w_ffn_up, w_ffn_down):
    b, s, d = x.shape
    assert s % MOBA_BLOCK == 0 and s % TOKEN_TILE == 0 and d % V7X_LANES == 0
    depth = w_in.shape[0]
    cuts = [0, GMLP_WIDTH, 2 * GMLP_WIDTH, 2 * GMLP_WIDTH + MOBA_WIDTH, 2 * GMLP_WIDTH + 2 * MOBA_WIDTH,
            2 * GMLP_WIDTH + 3 * MOBA_WIDTH, 2 * GMLP_WIDTH + 3 * MOBA_WIDTH + MEM_WIDTH]
    rb_flat = rel_bias.astype(F32).reshape(-1)
    bias = _bias_tiles(rb_flat)
    far_lanes = jnp.broadcast_to((rel_bias[REL_BUCKETS - 1].astype(F32) * LOG2E)[:, None, None],
                                 (MOBA_HEADS, s // MOBA_BLOCK, V7X_LANES))
    row = lambda v: v.reshape(1, -1).astype(F32)
    for l in range(depth):
        wi = w_in[l]
        w_u, w_v, w_q, w_k, w_v2, w_cq = (wi[:, cuts[i]:cuts[i + 1]] for i in range(6))
        w_g = wi[:, cuts[6]:]
        x2 = x.reshape(b * s, d)

        w_qt_scaled = (w_q * (HEAD_DIM ** -0.5)).T
        h2, h_lo, hmean, qt, k, vt = _qkv(x2, row(ln_mix_pre[l]), b, s, (w_qt_scaled * LOG2E).astype(BF16),
                                          w_k.astype(BF16), w_v2.T.astype(BF16))
        gt = _gatew(hmean.reshape(-1, d), w_k, w_qt_scaled, b)
        qadd = _select(h2, h_lo, gt, far_lanes, b, s)
        b_out_t = _attn(qt, qadd, k, vt, bias)

        b_s_lanes = jnp.broadcast_to(b_spatial[l][:, :, None], (GMLP_GROUPS, CHUNK, GMLP_GROUP_DIM))
        a_out = _gmlp(h2, w_u.astype(BF16), w_v.astype(BF16), row(ln_v_gain[l]), row(ln_v_bias[l]),
                      w_spatial[l], b_s_lanes.astype(F32))

        wkv = w_mem_kv[l]
        kt_mem, v_mem = _memkv(mem, row(ln_mem[l]), wkv[:, :MEM_WIDTH].T.astype(BF16),
                               wkv[:, MEM_WIDTH:].astype(BF16))
        c_out = _memattn(h2, b, s, w_cq.astype(BF16), kt_mem, v_mem)

        x1, hn = _merge(x2, h2, a_out, b_out_t, c_out, w_g.astype(BF16), w_branch_a[l].astype(BF16),
                        w_branch_b[l].astype(BF16), w_branch_c[l].astype(BF16), w_out[l].astype(BF16),
                        row(ln_mix_post[l]), row(ln_ffn_pre[l]))
        out = _ffn(x1, hn, w_ffn_gate[l].astype(BF16), w_ffn_up[l].astype(BF16),
                   w_ffn_down[l].astype(BF16), row(ln_ffn_post[l]))
        x = out.reshape(b, s, d)
    return x
```

```python
import functools
import math

import jax
import jax.numpy as jnp
from jax import lax
from jax.experimental import pallas as pl
from jax.experimental.pallas import tpu as pltpu

F32 = jnp.float32
BF16 = jnp.bfloat16

EPS = 1e-6
NEG = -1e30
GMLP_GROUPS = 6
GMLP_GROUP_DIM = 128
GMLP_WIDTH = GMLP_GROUPS * GMLP_GROUP_DIM
CHUNK = 128
GMLP_SLAB = 256
MOBA_HEADS = 12
HEAD_DIM = 64
MOBA_WIDTH = MOBA_HEADS * HEAD_DIM
MOBA_BLOCK = 256
MOBA_TOPK = 3
REL_BUCKETS = 32
REL_MAX_DIST = 128
LOG2E = math.log2(math.e)
ONES_ROWS = 16
ATTN_HEADS = 2
SCORE_LOOKAHEAD = 3
SCORE_BUFFERS = SCORE_LOOKAHEAD + 1
GATE_BACK = 1
MEM_HEADS = 4
MEM_HEAD_DIM = 128
MEM_WIDTH = MEM_HEADS * MEM_HEAD_DIM
MEM_SLAB = 256
N_BRANCHES = 3

V7X_LANES = 128
V7X_VMEM_LIMIT = 56 * 1024 * 1024

TOKEN_TILE = 512
MLP_TILE = 1024
MLP_SLAB = 256
QKV_SLAB = 256
GATE_ROWS = 128


def _params(n_axes, vmem=V7X_VMEM_LIMIT):
    return pltpu.CompilerParams(
        dimension_semantics=("arbitrary",) * n_axes, vmem_limit_bytes=vmem)


def _resident(shape):
    zeros = (0,) * len(shape)
    return pl.BlockSpec(shape, lambda *_: zeros, pipeline_mode=pl.Buffered(1))


def _rms(x, g):
    return x * lax.rsqrt(jnp.mean(x * x, axis=-1, keepdims=True) + EPS) * g


def _split_bf16(x):
    hi = x.astype(BF16)
    return hi, (x - hi.astype(F32)).astype(BF16)


def _gatew_kernel(hm_ref, wk_ref, wqt_ref, g_ref, gt_scr, *, n_batch, n_blk):
    kmean = jnp.dot(hm_ref[...], wk_ref[...], precision=lax.Precision.HIGHEST,
                    preferred_element_type=F32)
    used = MOBA_HEADS * n_blk
    gt_scr[:, used:, :] = jnp.zeros((n_batch, GATE_ROWS - used, gt_scr.shape[-1]), F32)
    for h in range(MOBA_HEADS):
        sl = slice(h * HEAD_DIM, (h + 1) * HEAD_DIM)
        res = jnp.dot(kmean[:, sl], wqt_ref[sl, :], precision=lax.Precision.HIGHEST,
                      preferred_element_type=F32)
        for b in range(n_batch):
            gt_scr[b, h * n_blk:(h + 1) * n_blk, :] = res[b * n_blk:(b + 1) * n_blk, :]
    for b in range(n_batch):
        hi, lo = _split_bf16(gt_scr[b].T)
        g_ref[b, :, 0:GATE_ROWS] = hi
        g_ref[b, :, GATE_ROWS:] = lo


def _gatew(hmean, w_k, w_qt, n_batch):
    rows, d = hmean.shape
    n_blk = rows // n_batch
    assert MOBA_HEADS * n_blk <= GATE_ROWS
    return pl.pallas_call(
        functools.partial(_gatew_kernel, n_batch=n_batch, n_blk=n_blk),
        grid=(1,),
        in_specs=[_resident((rows, d)), _resident(w_k.shape), _resident(w_qt.shape)],
        out_specs=pl.BlockSpec((n_batch, d, 2 * GATE_ROWS), lambda i: (0, 0, 0)),
        out_shape=jax.ShapeDtypeStruct((n_batch, d, 2 * GATE_ROWS), BF16),
        scratch_shapes=[pltpu.VMEM((n_batch, GATE_ROWS, d), F32)],
        compiler_params=_params(1),
        name="gatew",
    )(hmean, w_k, w_qt)


def _select_kernel(hhi_ref, hlo_ref, gw_ref, far_ref, qadd_ref, *, n_blk, tq):
    far = jnp.concatenate([far_ref[...]] * (MOBA_BLOCK // V7X_LANES), axis=-1)
    for i in range(tq // MOBA_BLOCK):
        rows = slice(i * MOBA_BLOCK, (i + 1) * MOBA_BLOCK)
        cur = pl.program_id(1) * (tq // MOBA_BLOCK) + i
        res = jnp.dot(jnp.concatenate([hhi_ref[rows, :], hlo_ref[rows, :]], axis=0), gw_ref[0],
                      preferred_element_type=F32)
        hi, lo = res[:MOBA_BLOCK], res[MOBA_BLOCK:]
        gate = (hi[:, :GATE_ROWS] + (hi[:, GATE_ROWS:] + lo[:, :GATE_ROWS])) + lo[:, GATE_ROWS:]
        gate = gate.T[0:MOBA_HEADS * n_blk].reshape(MOBA_HEADS, n_blk, MOBA_BLOCK)
        blk = lax.broadcasted_iota(jnp.int32, gate.shape, 1)
        past = blk < cur
        gate = jnp.where(past, gate, NEG)
        rank = jnp.zeros(gate.shape, jnp.int32)
        for m in range(n_blk):
            gm = gate[:, m:m + 1, :]
            tie = jnp.where(blk > m, 1, 0)
            rank = rank + jnp.where(gm > gate, 1, jnp.where(gm == gate, tie, 0))
        mask = jnp.where(rank < min(MOBA_TOPK, n_blk), jnp.where(past, 0.0, NEG), NEG)
        add = jnp.where(blk < cur - 1, mask + far, jnp.where(blk == cur - 1, mask, 0.0))
        hi, lo = _split_bf16(add)
        qadd_ref[0, :, :, rows] = jnp.concatenate([hi.astype(F32), lo.astype(F32)], axis=1).astype(BF16)


def _select(h_hi, h_lo, gt, far_lanes, b, s, tq=2 * TOKEN_TILE):
    d = h_hi.shape[-1]
    n_blk = s // MOBA_BLOCK
    nt = s // tq
    return pl.pallas_call(
        functools.partial(_select_kernel, n_blk=n_blk, tq=tq),
        grid=(b, nt),
        in_specs=[pl.BlockSpec((tq, d), lambda i, j: (i * nt + j, 0)),
                  pl.BlockSpec((tq, d), lambda i, j: (i * nt + j, 0)),
                  pl.BlockSpec((1, d, 2 * GATE_ROWS), lambda i, j: (i, 0, 0)),
                  _resident(far_lanes.shape)],
        out_specs=pl.BlockSpec((1, MOBA_HEADS, 2 * n_blk, tq), lambda i, j: (i, 0, 0, j)),
        out_shape=jax.ShapeDtypeStruct((b, MOBA_HEADS, 2 * n_blk, s), BF16),
        compiler_params=_params(2),
        name="select",
    )(h_hi, h_lo, gt, far_lanes)


def _qkv_kernel(x_ref, g_ref, wqt_ref, wk_ref, wvt_ref, h_ref, hlo_ref, hmean_ref, qt_ref, k_ref, vt_ref,
                *, n_blk):
    nt = (((1,), (1,)), ((), ()))
    n_slabs = x_ref.shape[0] // QKV_SLAB
    lane =lax.broadcasted_iota(jnp.int32, (QKV_SLAB, V7X_LANES), 1)
    low = lane < HEAD_DIM
    for i in range(n_slabs):
        rows = slice(i * QKV_SLAB, (i + 1) * QKV_SLAB)
        blk = pl.program_id(1) * n_slabs + i
        onehot = jnp.where((lane == HEAD_DIM + blk) | (lane == HEAD_DIM + n_blk + blk), 1.0, 0.0)
        hf = _rms(x_ref[rows, :], g_ref[...])
        h, hlo_ref[rows, :] = _split_bf16(hf)
        h_ref[rows, :] = h
        for c in range(QKV_SLAB // MOBA_BLOCK):
            hmean_ref[i * (QKV_SLAB // MOBA_BLOCK) + c] = jnp.mean(
                hf[c * MOBA_BLOCK:(c + 1) * MOBA_BLOCK], axis=0, keepdims=True)
        qt_ref[0, :, rows] = lax.dot_general(wqt_ref[...], h, nt, preferred_element_type=F32).astype(BF16)
        vt_ref[0, :, rows] = lax.dot_general(wvt_ref[...], h, nt, preferred_element_type=F32).astype(BF16)
        kf = jnp.dot(h, wk_ref[...], preferred_element_type=F32)
        for p in range(MOBA_HEADS // 2):
            pair = kf[:, p * V7X_LANES:(p + 1) * V7X_LANES]
            even = jnp.where(low, pair, onehot)
            odd = jnp.where(low, pltpu.roll(pair, HEAD_DIM, 1), onehot)
            k_ref[0, rows, (2 * p) * V7X_LANES:(2 * p + 1) * V7X_LANES] = even.astype(BF16)
            k_ref[0, rows, (2 * p + 1) * V7X_LANES:(2 * p + 2) * V7X_LANES] = odd.astype(BF16)


def _qkv(x2, g, b, s, w_qt, w_k, w_vt, tm=2 * TOKEN_TILE):
    d = x2.shape[-1]
    nt = s // tm
    blocks = tm // MOBA_BLOCK
    n_blk = s // MOBA_BLOCK
    assert QKV_SLAB == MOBA_BLOCK and HEAD_DIM + 2 * n_blk <= V7X_LANES
    tok = lambda i, j: (i * nt + j, 0)
    return pl.pallas_call(
        functools.partial(_qkv_kernel, n_blk=n_blk),
        grid=(b, nt),
        in_specs=[pl.BlockSpec((tm, d), tok), _resident((1, d)),
                  _resident(w_qt.shape), _resident(w_k.shape), _resident(w_vt.shape)],
        out_specs=[pl.BlockSpec((tm, d), tok), pl.BlockSpec((tm, d), tok),
                   pl.BlockSpec((blocks, 1, d), lambda i, j: (i * nt + j, 0, 0)),
                   pl.BlockSpec((1, MOBA_WIDTH, tm), lambda i, j: (i, 0, j)),
                   pl.BlockSpec((1, tm, MOBA_HEADS * V7X_LANES), lambda i, j: (i, j, 0)),
                   pl.BlockSpec((1, MOBA_WIDTH, tm), lambda i, j: (i, 0, j))],
        out_shape=[jax.ShapeDtypeStruct((b * s, d), BF16), jax.ShapeDtypeStruct((b * s, d), BF16),
                   jax.ShapeDtypeStruct((b * s // MOBA_BLOCK, 1, d), F32),
                   jax.ShapeDtypeStruct((b, MOBA_WIDTH, s), BF16),
                   jax.ShapeDtypeStruct((b, s, MOBA_HEADS * V7X_LANES), BF16),
                   jax.ShapeDtypeStruct((b, MOBA_WIDTH, s), BF16)],
        compiler_params=_params(2),
        name="qkv",
    )(x2, g, w_qt, w_k, w_vt)


def _bias_kernel(rb_ref, out_ref):
    h = pl.program_id(0)
    ik = lax.broadcasted_iota(jnp.int32, (MOBA_BLOCK, MOBA_BLOCK), 0)
    iq = lax.broadcasted_iota(jnp.int32, (MOBA_BLOCK, MOBA_BLOCK), 1)
    max_exact = REL_BUCKETS // 2
    for t in range(2):
        dist = iq - ik + t * MOBA_BLOCK
        n = jnp.maximum(dist, 0)
        nf = jnp.maximum(n, 1).astype(F32)
        large = max_exact + (jnp.log(nf / max_exact) / math.log(REL_MAX_DIST / max_exact)
                             * (REL_BUCKETS - max_exact)).astype(jnp.int32)
        large = jnp.minimum(large, REL_BUCKETS - 1)
        bucket = jnp.where(n < max_exact, n, large)
        val = jnp.zeros((MOBA_BLOCK, MOBA_BLOCK), F32)
        for bk in range(REL_BUCKETS):
            val = jnp.where(bucket == bk, rb_ref[bk * MOBA_HEADS + h], val)
        val = val * LOG2E
        if t == 0:
            val = jnp.where(dist >= 0, val, NEG)
        out_ref[0, t] = val


def _bias_tiles(rb_flat):
    return pl.pallas_call(
        _bias_kernel,
        grid=(MOBA_HEADS,),
        in_specs=[pl.BlockSpec(memory_space=pltpu.SMEM)],
        out_specs=pl.BlockSpec((1, 2, MOBA_BLOCK, MOBA_BLOCK), lambda h: (h, 0, 0, 0)),
        out_shape=jax.ShapeDtypeStruct((MOBA_HEADS, 2, MOBA_BLOCK, MOBA_BLOCK), F32),
        compiler_params=_params(1),
        name="bias",
    )(rb_flat)


def _zero_after(x):
    bits = lax.bitcast_convert_type(x, jnp.uint32)
    sixteen = jnp.uint32(16)
    return lax.bitcast_convert_type(
        lax.shift_right_logical(lax.shift_right_logical(bits, sixteen), sixteen), F32)


def _attn_kernel(qt_ref, qadd_ref, k_ref, vt_ref, bias_ref, o_ref, s_scr, *, n_blk):
    pad = jnp.zeros((V7X_LANES - HEAD_DIM - 2 * n_blk, MOBA_BLOCK), BF16)

    def scores(hh, j, buf):
        cols = slice(j * MOBA_BLOCK, (j + 1) * MOBA_BLOCK)
        q = qt_ref[0, hh * HEAD_DIM:(hh + 1) * HEAD_DIM, cols]
        qz = jnp.concatenate([q, qadd_ref[0, hh, :, cols], pad], axis=0)
        k = k_ref[0, 0:(j + 1) * MOBA_BLOCK, hh * V7X_LANES:(hh + 1) * V7X_LANES]
        s = jnp.dot(k, qz, preferred_element_type=F32)
        gates = []
        for n in range(j + 1):
            rows = slice(n * MOBA_BLOCK, (n + 1) * MOBA_BLOCK)
            if n == j:
                sn = s[rows] + bias_ref[hh, 0]
            elif n == j - 1:
                sn = s[rows] + bias_ref[hh, 1]
            else:
                sn = s[rows]
            s_scr[buf, rows, :] = sn
            gates.append(_zero_after(sn[0:1, :]))
        return gates

    def finish(hh, j, buf, gates):
        cols = slice(j * MOBA_BLOCK, (j + 1) * MOBA_BLOCK)
        heads = slice(hh * HEAD_DIM, (hh + 1) * HEAD_DIM)
        m = jnp.max(s_scr[buf, 0:(j + 1) * MOBA_BLOCK, :], axis=0, keepdims=True)
        ones = jnp.ones((ONES_ROWS, MOBA_BLOCK), BF16)
        acc = None
        for n in range(j + 1):
            rows = slice(n * MOBA_BLOCK, (n + 1) * MOBA_BLOCK)
            mn = m + gates[min(n, len(gates) - 1)] if gates else m
            p = jnp.exp2(s_scr[buf, rows, :] - mn).astype(BF16)
            part = jnp.dot(jnp.concatenate([vt_ref[0, heads, rows], ones], axis=0), p,
                           preferred_element_type=F32)
            acc = part if acc is None else acc + part
        o_ref[0, heads, cols] = (acc[0:HEAD_DIM] / acc[HEAD_DIM:HEAD_DIM + 1]).astype(BF16)

    tasks = [(hh, j) for j in reversed(range(n_blk)) for hh in range(ATTN_HEADS)]
    pending, issued = [], []
    for i, (hh, j) in enumerate(tasks):
        buf = i % SCORE_BUFFERS
        issued.append(scores(hh, j, buf))
        pending.append((hh, j, buf))
        if len(pending) > SCORE_LOOKAHEAD:
            finish(*pending.pop(0), issued[i - GATE_BACK])
    for item in pending:
        finish(*item, None)


def _attn(qt, qadd, k, vt, bias):
    b, _, s = qt.shape
    n_blk = s // MOBA_BLOCK
    nh = ATTN_HEADS
    return pl.pallas_call(
        functools.partial(_attn_kernel, n_blk=n_blk),
        grid=(b, MOBA_HEADS // nh),
        in_specs=[pl.BlockSpec((1, nh * HEAD_DIM, s), lambda i, h: (i, h, 0)),
                  pl.BlockSpec((1, nh, 2 * n_blk, s), lambda i, h: (i, h, 0, 0)),
                  pl.BlockSpec((1, s, nh * V7X_LANES), lambda i, h: (i, 0, h)),
                  pl.BlockSpec((1, nh * HEAD_DIM, s), lambda i, h: (i, h, 0)),
                  pl.BlockSpec((nh, 2, MOBA_BLOCK, MOBA_BLOCK), lambda i, h: (h, 0, 0, 0))],
        out_specs=pl.BlockSpec((1, nh * HEAD_DIM, s), lambda i, h: (i, h, 0)),
        out_shape=jax.ShapeDtypeStruct((b, MOBA_WIDTH, s), BF16),
        scratch_shapes=[pltpu.VMEM((SCORE_BUFFERS, s, MOBA_BLOCK), F32)],
        compiler_params=_params(2),
        name="attn",
    )(qt, qadd, k, vt, bias)


def _gmlp_kernel(h_ref, wu_ref, wv_ref, lng_ref, lnb_ref, ws_ref, bs_ref, o_ref):
    causal = (lax.broadcasted_iota(jnp.int32, (CHUNK, CHUNK), 0)
              >= lax.broadcasted_iota(jnp.int32, (CHUNK, CHUNK), 1))
    ws = [jnp.where(causal, ws_ref[g], 0.0).astype(BF16) for g in range(GMLP_GROUPS)]
    slabs = [slice(i * GMLP_SLAB, (i + 1) * GMLP_SLAB) for i in range(h_ref.shape[0] // GMLP_SLAB)]
    vs = [jnp.dot(h_ref[rows, :], wv_ref[...], preferred_element_type=F32) for rows in slabs]
    us = [jnp.dot(h_ref[rows, :], wu_ref[...], preferred_element_type=F32) for rows in slabs]
    for rows, v, u in zip(slabs, vs, us):
        v = jax.nn.gelu(v)
        vc = v - jnp.mean(v, axis=-1, keepdims=True)
        vn = vc * lax.rsqrt(jnp.mean(vc * vc, axis=-1, keepdims=True) + EPS) * lng_ref[...] + lnb_ref[...]
        vn = vn.astype(BF16)
        u = jax.nn.gelu(u)
        for g in range(GMLP_GROUPS):
            lanes = slice(g * GMLP_GROUP_DIM, (g + 1) * GMLP_GROUP_DIM)
            for c in range(GMLP_SLAB // CHUNK):
                toks = slice(c * CHUNK, (c + 1) * CHUNK)
                mixed = jnp.dot(ws[g], vn[toks, lanes], preferred_element_type=F32) + bs_ref[g]
                o_ref[rows.start + c * CHUNK:rows.start + (c + 1) * CHUNK, lanes] = (
                    u[toks, lanes] * mixed).astype(BF16)


def _gmlp(h2, w_u, w_v, ln_g, ln_b, w_s, b_s_lanes, tm=2 * TOKEN_TILE):
    n_tok, d = h2.shape
    return pl.pallas_call(
        _gmlp_kernel,
        grid=(n_tok // tm,),
        in_specs=[pl.BlockSpec((tm, d), lambda i: (i, 0)),
                  _resident(w_u.shape), _resident(w_v.shape),
                  _resident(ln_g.shape), _resident(ln_b.shape),
                  _resident(w_s.shape), _resident(b_s_lanes.shape)],
        out_specs=pl.BlockSpec((tm, GMLP_WIDTH), lambda i: (i, 0)),
        out_shape=jax.ShapeDtypeStruct((n_tok, GMLP_WIDTH), BF16),
        compiler_params=_params(1),
        name="gmlp",
    )(h2, w_u, w_v, ln_g, ln_b, w_s, b_s_lanes)


def _branches_kernel(h_ref, hlo_ref, wu_ref, wv_ref, lng_ref, lnb_ref, ws_ref, bs_ref, wq_ref, kt_ref, vm_ref,
                     gw_ref, far_ref, a_ref, c_ref, qadd_ref, *, n_blk):
    causal = (lax.broadcasted_iota(jnp.int32, (CHUNK, CHUNK), 0)
              >= lax.broadcasted_iota(jnp.int32, (CHUNK, CHUNK), 1))
    ws = [jnp.where(causal, ws_ref[g], 0.0).astype(BF16) for g in range(GMLP_GROUPS)]
    ones = jnp.ones((vm_ref.shape[1], MEM_HEAD_DIM), BF16)
    far = jnp.concatenate([far_ref[...]] * (MOBA_BLOCK // V7X_LANES), axis=-1)
    n_slabs = h_ref.shape[0] // MOBA_BLOCK

    def project(i):
        rows = slice(i * MOBA_BLOCK, (i + 1) * MOBA_BLOCK)
        h = h_ref[rows, :]
        cq = jnp.dot(h, wq_ref[...], preferred_element_type=F32).astype(BF16)
        v = jnp.dot(h, wv_ref[...], preferred_element_type=F32)
        u = jnp.dot(h, wu_ref[...], preferred_element_type=F32)
        res = jnp.dot(jnp.concatenate([h, hlo_ref[rows, :]], axis=0), gw_ref[0], preferred_element_type=F32)
        ps = []
        for hd in range(MEM_HEADS):
            sl = slice(hd * MEM_HEAD_DIM, (hd + 1) * MEM_HEAD_DIM)
            s = jnp.dot(cq[:, sl], kt_ref[0, sl, :], preferred_element_type=F32) * (MEM_HEAD_DIM ** -0.5 * LOG2E)
            ps.append(jnp.exp2(s - jnp.max(s, axis=-1, keepdims=True)).astype(BF16))
        return rows, v, u, ps, res

    def finish(i, rows, v, u, ps, res):
        v = jax.nn.gelu(v)
        vc = v - jnp.mean(v, axis=-1, keepdims=True)
        vn = vc * lax.rsqrt(jnp.mean(vc * vc, axis=-1, keepdims=True) + EPS) * lng_ref[...] + lnb_ref[...]
        vn = vn.astype(BF16)
        u = jax.nn.gelu(u)
        for g in range(GMLP_GROUPS):
            lanes = slice(g * GMLP_GROUP_DIM, (g + 1) * GMLP_GROUP_DIM)
            for c in range(MOBA_BLOCK // CHUNK):
                toks = slice(c * CHUNK, (c + 1) * CHUNK)
                mixed = jnp.dot(ws[g], vn[toks, lanes], preferred_element_type=F32) + bs_ref[g]
                a_ref[rows.start + c * CHUNK:rows.start + (c + 1) * CHUNK, lanes] = (
                    u[toks, lanes] * mixed).astype(BF16)
        for hd, p in enumerate(ps):
            sl = slice(hd * MEM_HEAD_DIM, (hd + 1) * MEM_HEAD_DIM)
            acc = jnp.dot(p, jnp.concatenate([vm_ref[0, :, sl], ones], axis=1), preferred_element_type=F32)
            c_ref[rows, sl] = (acc[:, :MEM_HEAD_DIM] / acc[:, MEM_HEAD_DIM:]).astype(BF16)
        cur = pl.program_id(1) * n_slabs + i
        hi, lo = res[:MOBA_BLOCK], res[MOBA_BLOCK:]
        gate = (hi[:, :GATE_ROWS] + (hi[:, GATE_ROWS:] + lo[:, :GATE_ROWS])) + lo[:, GATE_ROWS:]
        gate = gate.T[0:MOBA_HEADS * n_blk].reshape(MOBA_HEADS, n_blk, MOBA_BLOCK)
        blk = lax.broadcasted_iota(jnp.int32, gate.shape, 1)
        past = blk < cur
        gate = jnp.where(past, gate, NEG)
        rank = jnp.zeros(gate.shape, jnp.int32)
        for m in range(n_blk):
            gm = gate[:, m:m + 1, :]
            tie = jnp.where(blk > m, 1, 0)
            rank = rank + jnp.where(gm > gate, 1, jnp.where(gm == gate, tie, 0))
        mask = jnp.where(rank < min(MOBA_TOPK, n_blk), jnp.where(past, 0.0, NEG), NEG)
        add = jnp.where(blk < cur - 1, mask + far, jnp.where(blk == cur - 1, mask, 0.0))
        add_hi, add_lo = _split_bf16(add)
        qadd_ref[0, :, :, rows] = jnp.concatenate(
            [add_hi.astype(F32), add_lo.astype(F32)], axis=1).astype(BF16)

    staged = project(0)
    for i in range(n_slabs):
        nxt = project(i + 1) if i + 1 < n_slabs else None
        finish(i, *staged)
        staged = nxt


def _branches(h_hi, h_lo, b, s, w_u, w_v, ln_g, ln_b, w_s, b_s_lanes, w_cq, kt, vm, gt, far_lanes,
              tm=2 * TOKEN_TILE):
    d = h_hi.shape[-1]
    n_blk = s // MOBA_BLOCK
    nt = s // tm
    m = kt.shape[-1]
    tok = lambda i, j: (i * nt + j, 0)
    return pl.pallas_call(
        functools.partial(_branches_kernel, n_blk=n_blk),
        grid=(b, nt),
        in_specs=[pl.BlockSpec((tm, d), tok), pl.BlockSpec((tm, d), tok),
                  _resident(w_u.shape), _resident(w_v.shape), _resident(ln_g.shape), _resident(ln_b.shape),
                  _resident(w_s.shape), _resident(b_s_lanes.shape), _resident(w_cq.shape),
                  pl.BlockSpec((1, MEM_WIDTH, m), lambda i, j: (i, 0, 0)),
                  pl.BlockSpec((1, m, MEM_WIDTH), lambda i, j: (i, 0, 0)),
                  pl.BlockSpec((1, d, 2 * GATE_ROWS), lambda i, j: (i, 0, 0)),
                  _resident(far_lanes.shape)],
        out_specs=[pl.BlockSpec((tm, GMLP_WIDTH), tok), pl.BlockSpec((tm, MEM_WIDTH), tok),
                   pl.BlockSpec((1, MOBA_HEADS, 2 * n_blk, tm), lambda i, j: (i, 0, 0, j))],
        out_shape=[jax.ShapeDtypeStruct((b * s, GMLP_WIDTH), BF16),
                   jax.ShapeDtypeStruct((b * s, MEM_WIDTH), BF16),
                   jax.ShapeDtypeStruct((b, MOBA_HEADS, 2 * n_blk, s), BF16)],
        compiler_params=_params(2),
        name="branches",
    )(h_hi, h_lo, w_u, w_v, ln_g, ln_b, w_s, b_s_lanes, w_cq, kt, vm, gt, far_lanes)


def _memkv_kernel(mem_ref, g_ref, wkt_ref, wv_ref, kt_ref, v_ref):
    mn = _rms(mem_ref[0], g_ref[...]).astype(BF16)
    kt_ref[0] = lax.dot_general(wkt_ref[...], mn, (((1,), (1,)), ((), ())),
                                preferred_element_type=F32).astype(BF16)
    v_ref[0] = jnp.dot(mn, wv_ref[...], preferred_element_type=F32).astype(BF16)


def _memkv(mem, g, w_kt, w_v):
    b, m, d = mem.shape
    return pl.pallas_call(
        _memkv_kernel,
        grid=(b,),
        in_specs=[pl.BlockSpec((1, m, d), lambda i: (i, 0, 0)), _resident((1, d)),
                  _resident(w_kt.shape), _resident(w_v.shape)],
        out_specs=[pl.BlockSpec((1, MEM_WIDTH, m), lambda i: (i, 0, 0)),
                   pl.BlockSpec((1, m, MEM_WIDTH), lambda i: (i, 0, 0))],
        out_shape=[jax.ShapeDtypeStruct((b, MEM_WIDTH, m), BF16),
                   jax.ShapeDtypeStruct((b, m, MEM_WIDTH), BF16)],
        compiler_params=_params(1),
        name="memkv",
    )(mem, g, w_kt, w_v)


def _memattn_kernel(h_ref, wq_ref, kt_ref, v_ref, o_ref):
    slabs = [slice(i * MEM_SLAB, (i + 1) * MEM_SLAB) for i in range(h_ref.shape[0] // MEM_SLAB)]
    cqs = [jnp.dot(h_ref[rows, :], wq_ref[...], preferred_element_type=F32).astype(BF16) for rows in slabs]
    ones = jnp.ones((v_ref.shape[1], MEM_HEAD_DIM), BF16)
    for rows, cq in zip(slabs, cqs):
        for hd in range(MEM_HEADS):
            sl = slice(hd * MEM_HEAD_DIM, (hd + 1) * MEM_HEAD_DIM)
            s = jnp.dot(cq[:, sl], kt_ref[0, sl, :], preferred_element_type=F32) * (MEM_HEAD_DIM ** -0.5 * LOG2E)
            p = jnp.exp2(s - jnp.max(s, axis=-1, keepdims=True)).astype(BF16)
            acc = jnp.dot(p, jnp.concatenate([v_ref[0, :, sl], ones], axis=1), preferred_element_type=F32)
            o_ref[rows, sl] = (acc[:, :MEM_HEAD_DIM] / acc[:, MEM_HEAD_DIM:]).astype(BF16)


def _memattn(h2, b, s, w_cq, kt, v, tm=TOKEN_TILE):
    d = h2.shape[-1]
    nt = s // tm
    m = kt.shape[-1]
    return pl.pallas_call(
        _memattn_kernel,
        grid=(b, nt),
        in_specs=[pl.BlockSpec((tm, d), lambda i, j: (i * nt + j, 0)),
                  _resident(w_cq.shape),
                  pl.BlockSpec((1, MEM_WIDTH, m), lambda i, j: (i, 0, 0)),
                  pl.BlockSpec((1, m, MEM_WIDTH), lambda i, j: (i, 0, 0))],
        out_specs=pl.BlockSpec((tm, MEM_WIDTH), lambda i, j: (i * nt + j, 0)),
        out_shape=jax.ShapeDtypeStruct((b * s, MEM_WIDTH), BF16),
        compiler_params=_params(2),
        name="memattn",
    )(h2, w_cq, kt, v)


def _merge_kernel(x_ref, h_ref, a_ref, bt_ref, c_ref, wg_ref, wa_ref, wb_ref, wc_ref, wo_ref,
                  lnpost_ref, lnpre_ref, x1_ref, h2_ref):
    d = x_ref.shape[-1]
    for i in range(x_ref.shape[0] // MLP_SLAB):
        rows = slice(i * MLP_SLAB, (i + 1) * MLP_SLAB)
        gates = jax.nn.sigmoid(jnp.dot(h_ref[rows, :], wg_ref[...], preferred_element_type=F32))
        pa = jnp.dot(a_ref[rows, :], wa_ref[...], preferred_element_type=F32)
        pb = lax.dot_general(bt_ref[0, :, rows], wb_ref[...], (((0,), (0,)), ((), ())),
                             preferred_element_type=F32)
        pc = jnp.dot(c_ref[rows, :], wc_ref[...], preferred_element_type=F32)
        merged = gates[:, :d] * pa + gates[:, d:2 * d] * pb + gates[:, 2 * d:] * pc
        mo = jnp.dot(merged.astype(BF16), wo_ref[...], preferred_element_type=F32)
        x1 = x_ref[rows, :] + _rms(mo, lnpost_ref[...])
        x1_ref[rows, :] = x1
        h2_ref[rows, :] = _rms(x1, lnpre_ref[...]).astype(BF16)


def _merge(x2, h2, a_out, b_out_t, c_out, w_g, w_a, w_b, w_c, w_o, ln_post, ln_pre, tm=MLP_TILE):
    n_tok, d = x2.shape
    b, _, s = b_out_t.shape
    nt = s // tm
    tok = lambda i, j: (i * nt + j, 0)
    return pl.pallas_call(
        _merge_kernel,
        grid=(b, nt),
        in_specs=[pl.BlockSpec((tm, d), tok), pl.BlockSpec((tm, d), tok),
                  pl.BlockSpec((tm, GMLP_WIDTH), tok),
                  pl.BlockSpec((1, MOBA_WIDTH, tm), lambda i, j: (i, 0, j)),
                  pl.BlockSpec((tm, MEM_WIDTH), tok),
                  _resident(w_g.shape), _resident(w_a.shape), _resident(w_b.shape),
                  _resident(w_c.shape), _resident(w_o.shape),
                  _resident((1, d)), _resident((1, d))],
        out_specs=[pl.BlockSpec((tm, d), tok), pl.BlockSpec((tm, d), tok)],
        out_shape=[jax.ShapeDtypeStruct((n_tok, d), F32), jax.ShapeDtypeStruct((n_tok, d), BF16)],
        compiler_params=_params(2),
        name="merge",
    )(x2, h2, a_out, b_out_t, c_out, w_g, w_a, w_b, w_c, w_o, ln_post, ln_pre)


def _ffn_kernel(x1_ref, h2_ref, wg_ref, wu_ref, wd_ref, ln_ref, o_ref):
    for i in range(x1_ref.shape[0] // MLP_SLAB):
        rows = slice(i * MLP_SLAB, (i + 1) * MLP_SLAB)
        h2 = h2_ref[rows, :]
        g = jnp.dot(h2, wg_ref[...], preferred_element_type=F32)
        u = jnp.dot(h2, wu_ref[...], preferred_element_type=F32)
        act = (jax.nn.silu(g) * u).astype(BF16)
        f = jnp.dot(act, wd_ref[...], preferred_element_type=F32)
        o_ref[rows, :] = x1_ref[rows, :] + _rms(f, ln_ref[...])


def _ffn(x1, h2, w_g, w_u, w_d, ln_post, tm=MLP_TILE):
    n_tok, d = x1.shape
    return pl.pallas_call(
        _ffn_kernel,
        grid=(n_tok // tm,),
        in_specs=[pl.BlockSpec((tm, d), lambda i: (i, 0)), pl.BlockSpec((tm, d), lambda i: (i, 0)),
                  _resident(w_g.shape), _resident(w_u.shape), _resident(w_d.shape),
                  _resident((1, d))],
        out_specs=pl.BlockSpec((tm, d), lambda i: (i, 0)),
        out_shape=jax.ShapeDtypeStruct((n_tok, d), F32),
        compiler_params=_params(1),
        name="ffn",
    )(x1, h2, w_g, w_u, w_d, ln_post)


def kernel(x, mem, ln_mix_pre, ln_mix_post, ln_ffn_pre, ln_ffn_post, ln_mem, w_in, ln_v_gain, ln_v_bias,
           w_spatial, b_spatial, rel_bias, w_mem_kv, w_branch_a, w_branch_b, w_branch_c, w_out,
           w_ffn_gate, w_ffn_up, w_ffn_down):
    b, s, d = x.shape
    assert s % MOBA_BLOCK == 0 and s % TOKEN_TILE == 0 and d % V7X_LANES == 0
    depth = w_in.shape[0]
    cuts = [0, GMLP_WIDTH, 2 * GMLP_WIDTH, 2 * GMLP_WIDTH + MOBA_WIDTH, 2 * GMLP_WIDTH + 2 * MOBA_WIDTH,
            2 * GMLP_WIDTH + 3 * MOBA_WIDTH, 2 * GMLP_WIDTH + 3 * MOBA_WIDTH + MEM_WIDTH]
    rb_flat = rel_bias.astype(F32).reshape(-1)
    bias = _bias_tiles(rb_flat)
    far_lanes = jnp.broadcast_to((rel_bias[REL_BUCKETS - 1].astype(F32) * LOG2E)[:, None, None],
                                 (MOBA_HEADS, s // MOBA_BLOCK, V7X_LANES))
    row = lambda v: v.reshape(1, -1).astype(F32)
    for l in range(depth):
        wi = w_in[l]
        w_u, w_v, w_q, w_k, w_v2, w_cq = (wi[:, cuts[i]:cuts[i + 1]] for i in range(6))
        w_g = wi[:, cuts[6]:]
        x2 = x.reshape(b * s, d)

        w_qt_scaled = (w_q * (HEAD_DIM ** -0.5)).T
        h2, h_lo, hmean, qt, k, vt = _qkv(x2, row(ln_mix_pre[l]), b, s, (w_qt_scaled * LOG2E).astype(BF16),
                                          w_k.astype(BF16), w_v2.T.astype(BF16))
        gt = _gatew(hmean.reshape(-1, d), w_k, w_qt_scaled, b)
        wkv = w_mem_kv[l]
        kt_mem, v_mem = _memkv(mem, row(ln_mem[l]), wkv[:, :MEM_WIDTH].T.astype(BF16),
                               wkv[:, MEM_WIDTH:].astype(BF16))
        b_s_lanes = jnp.broadcast_to(b_spatial[l][:, :, None], (GMLP_GROUPS, CHUNK, GMLP_GROUP_DIM))
        a_out, c_out, qadd = _branches(
            h2, h_lo, b, s, w_u.astype(BF16), w_v.astype(BF16), row(ln_v_gain[l]), row(ln_v_bias[l]),
            w_spatial[l], b_s_lanes.astype(F32), w_cq.astype(BF16), kt_mem, v_mem, gt, far_lanes)
        b_out_t = _attn(qt, qadd, k, vt, bias)

        x1, hn = _merge(x2, h2, a_out, b_out_t, c_out, w_g.astype(BF16), w_branch_a[l].astype(BF16),
                        w_branch_b[l].astype(BF16), w_branch_c[l].astype(BF16), w_out[l].astype(BF16),
                        row(ln_mix_post[l]), row(ln_ffn_pre[l]))
        out = _ffn(x1, hn, w_ffn_gate[l].astype(BF16), w_ffn_up[l].astype(BF16),
                   w_ffn_down[l].astype(BF16), row(ln_ffn_post[l]))
        x = out.reshape(b, s, d)
    return x
```

```python
import functools
import math

import jax
import jax.numpy as jnp
from jax import lax
from jax.experimental import pallas as pl
from jax.experimental.pallas import tpu as pltpu

F32 = jnp.float32
BF16 = jnp.bfloat16

EPS = 1e-6
NEG = -1e30
GMLP_GROUPS = 6
GMLP_GROUP_DIM = 128
GMLP_WIDTH = GMLP_GROUPS * GMLP_GROUP_DIM
CHUNK = 128
MOBA_HEADS = 12
HEAD_DIM = 64
MOBA_WIDTH = MOBA_HEADS * HEAD_DIM
MOBA_BLOCK = 256
MOBA_TOPK = 3
REL_BUCKETS = 32
REL_MAX_DIST = 128
LOG2E = math.log2(math.e)
ONES_ROWS = 16
ATTN_HEADS = 2
SCORE_LOOKAHEAD = 3
SCORE_BUFFERS = SCORE_LOOKAHEAD + 1
GATE_BACK = 1
MEM_HEADS = 4
MEM_HEAD_DIM = 128
MEM_WIDTH = MEM_HEADS * MEM_HEAD_DIM
N_BRANCHES = 3

V7X_LANES = 128
V7X_VMEM_LIMIT = 56 * 1024 * 1024

TOKEN_TILE = 512
MLP_TILE = 1024
MLP_SLAB = 256
QKV_SLAB = 256
GATE_ROWS = 128


def _params(n_axes, vmem=V7X_VMEM_LIMIT):
    return pltpu.CompilerParams(
        dimension_semantics=("arbitrary",) * n_axes, vmem_limit_bytes=vmem)


def _resident(shape):
    zeros = (0,) * len(shape)
    return pl.BlockSpec(shape, lambda *_: zeros, pipeline_mode=pl.Buffered(1))


def _rms(x, g):
    return x * lax.rsqrt(jnp.mean(x * x, axis=-1, keepdims=True) + EPS) * g


def _split_bf16(x):
    hi = x.astype(BF16)
    return hi, (x - hi.astype(F32)).astype(BF16)


def _gatew_kernel(hm_ref, wk_ref, wqt_ref, g_ref, gt_scr, *, n_batch, n_blk):
    kmean = jnp.dot(hm_ref[...], wk_ref[...], precision=lax.Precision.HIGHEST,
                    preferred_element_type=F32)
    used = MOBA_HEADS * n_blk
    gt_scr[:, used:, :] = jnp.zeros((n_batch, GATE_ROWS - used, gt_scr.shape[-1]), F32)
    for h in range(MOBA_HEADS):
        sl = slice(h * HEAD_DIM, (h + 1) * HEAD_DIM)
        res = jnp.dot(kmean[:, sl], wqt_ref[sl, :], precision=lax.Precision.HIGHEST,
                      preferred_element_type=F32)
        for b in range(n_batch):
            gt_scr[b, h * n_blk:(h + 1) * n_blk, :] = res[b * n_blk:(b + 1) * n_blk, :]
    for b in range(n_batch):
        hi, lo = _split_bf16(gt_scr[b].T)
        g_ref[b, :, 0:GATE_ROWS] = hi
        g_ref[b, :, GATE_ROWS:] = lo


def _gatew(hmean, w_k, w_qt, n_batch):
    rows, d = hmean.shape
    n_blk = rows // n_batch
    assert MOBA_HEADS * n_blk <= GATE_ROWS
    return pl.pallas_call(
        functools.partial(_gatew_kernel, n_batch=n_batch, n_blk=n_blk),
        grid=(1,),
        in_specs=[_resident((rows, d)), _resident(w_k.shape), _resident(w_qt.shape)],
        out_specs=pl.BlockSpec((n_batch, d, 2 * GATE_ROWS), lambda i: (0, 0, 0)),
        out_shape=jax.ShapeDtypeStruct((n_batch, d, 2 * GATE_ROWS), BF16),
        scratch_shapes=[pltpu.VMEM((n_batch, GATE_ROWS, d), F32)],
        compiler_params=_params(1),
        name="gatew",
    )(hmean, w_k, w_qt)


def _qkv_kernel(x_ref, g_ref, wqt_ref, wk_ref, wvt_ref, h_ref, hlo_ref, hmean_ref, qt_ref, k_ref, vt_ref,
                *, n_blk):
    nt = (((1,), (1,)), ((), ()))
    n_slabs = x_ref.shape[0] // QKV_SLAB
    lane =lax.broadcasted_iota(jnp.int32, (QKV_SLAB, V7X_LANES), 1)
    low = lane < HEAD_DIM
    for i in range(n_slabs):
        rows = slice(i * QKV_SLAB, (i + 1) * QKV_SLAB)
        blk = pl.program_id(1) * n_slabs + i
        onehot = jnp.where((lane == HEAD_DIM + blk) | (lane == HEAD_DIM + n_blk + blk), 1.0, 0.0)
        hf = _rms(x_ref[rows, :], g_ref[...])
        h, hlo_ref[rows, :] = _split_bf16(hf)
        h_ref[rows, :] = h
        for c in range(QKV_SLAB // MOBA_BLOCK):
            hmean_ref[i * (QKV_SLAB // MOBA_BLOCK) + c] = jnp.mean(
                hf[c * MOBA_BLOCK:(c + 1) * MOBA_BLOCK], axis=0, keepdims=True)
        qt_ref[0, :, rows] = lax.dot_general(wqt_ref[...], h, nt, preferred_element_type=F32).astype(BF16)
        vt_ref[0, :, rows] = lax.dot_general(wvt_ref[...], h, nt, preferred_element_type=F32).astype(BF16)
        kf = jnp.dot(h, wk_ref[...], preferred_element_type=F32)
        for p in range(MOBA_HEADS // 2):
            pair = kf[:, p * V7X_LANES:(p + 1) * V7X_LANES]
            even = jnp.where(low, pair, onehot)
            odd = jnp.where(low, pltpu.roll(pair, HEAD_DIM, 1), onehot)
            k_ref[0, rows, (2 * p) * V7X_LANES:(2 * p + 1) * V7X_LANES] = even.astype(BF16)
            k_ref[0, rows, (2 * p + 1) * V7X_LANES:(2 * p + 2) * V7X_LANES] = odd.astype(BF16)


def _qkv(x2, g, b, s, w_qt, w_k, w_vt, tm=2 * TOKEN_TILE):
    d = x2.shape[-1]
    nt = s // tm
    blocks = tm // MOBA_BLOCK
    n_blk = s // MOBA_BLOCK
    assert QKV_SLAB == MOBA_BLOCK and HEAD_DIM + 2 * n_blk <= V7X_LANES
    tok = lambda i, j: (i * nt + j, 0)
    return pl.pallas_call(
        functools.partial(_qkv_kernel, n_blk=n_blk),
        grid=(b, nt),
        in_specs=[pl.BlockSpec((tm, d), tok), _resident((1, d)),
                  _resident(w_qt.shape), _resident(w_k.shape), _resident(w_vt.shape)],
        out_specs=[pl.BlockSpec((tm, d), tok), pl.BlockSpec((tm, d), tok),
                   pl.BlockSpec((blocks, 1, d), lambda i, j: (i * nt + j, 0, 0)),
                   pl.BlockSpec((1, MOBA_WIDTH, tm), lambda i, j: (i, 0, j)),
                   pl.BlockSpec((1, tm, MOBA_HEADS * V7X_LANES), lambda i, j: (i, j, 0)),
                   pl.BlockSpec((1, MOBA_WIDTH, tm), lambda i, j: (i, 0, j))],
        out_shape=[jax.ShapeDtypeStruct((b * s, d), BF16), jax.ShapeDtypeStruct((b * s, d), BF16),
                   jax.ShapeDtypeStruct((b * s // MOBA_BLOCK, 1, d), F32),
                   jax.ShapeDtypeStruct((b, MOBA_WIDTH, s), BF16),
                   jax.ShapeDtypeStruct((b, s, MOBA_HEADS * V7X_LANES), BF16),
                   jax.ShapeDtypeStruct((b, MOBA_WIDTH, s), BF16)],
        compiler_params=_params(2),
        name="qkv",
    )(x2, g, w_qt, w_k, w_vt)


def _bias_kernel(rb_ref, out_ref):
    h = pl.program_id(0)
    ik = lax.broadcasted_iota(jnp.int32, (MOBA_BLOCK, MOBA_BLOCK), 0)
    iq = lax.broadcasted_iota(jnp.int32, (MOBA_BLOCK, MOBA_BLOCK), 1)
    max_exact = REL_BUCKETS // 2
    for t in range(2):
        dist = iq - ik + t * MOBA_BLOCK
        n = jnp.maximum(dist, 0)
        nf = jnp.maximum(n, 1).astype(F32)
        large = max_exact + (jnp.log(nf / max_exact) / math.log(REL_MAX_DIST / max_exact)
                             * (REL_BUCKETS - max_exact)).astype(jnp.int32)
        large = jnp.minimum(large, REL_BUCKETS - 1)
        bucket = jnp.where(n < max_exact, n, large)
        val = jnp.zeros((MOBA_BLOCK, MOBA_BLOCK), F32)
        for bk in range(REL_BUCKETS):
            val = jnp.where(bucket == bk, rb_ref[bk * MOBA_HEADS + h], val)
        val = val * LOG2E
        if t == 0:
            val = jnp.where(dist >= 0, val, NEG)
        out_ref[0, t] = val


def _bias_tiles(rb_flat):
    return pl.pallas_call(
        _bias_kernel,
        grid=(MOBA_HEADS,),
        in_specs=[pl.BlockSpec(memory_space=pltpu.SMEM)],
        out_specs=pl.BlockSpec((1, 2, MOBA_BLOCK, MOBA_BLOCK), lambda h: (h, 0, 0, 0)),
        out_shape=jax.ShapeDtypeStruct((MOBA_HEADS, 2, MOBA_BLOCK, MOBA_BLOCK), F32),
        compiler_params=_params(1),
        name="bias",
    )(rb_flat)


def _zero_after(x):
    bits = lax.bitcast_convert_type(x, jnp.uint32)
    sixteen = jnp.uint32(16)
    return lax.bitcast_convert_type(
        lax.shift_right_logical(lax.shift_right_logical(bits, sixteen), sixteen), F32)


def _attn_kernel(qt_ref, qadd_ref, k_ref, vt_ref, bias_ref, o_ref, s_scr, *, n_blk):
    pad = jnp.zeros((V7X_LANES - HEAD_DIM - 2 * n_blk, MOBA_BLOCK), BF16)

    def scores(hh, j, buf):
        cols = slice(j * MOBA_BLOCK, (j + 1) * MOBA_BLOCK)
        q = qt_ref[0, hh * HEAD_DIM:(hh + 1) * HEAD_DIM, cols]
        qz = jnp.concatenate([q, qadd_ref[0, hh, :, cols], pad], axis=0)
        k = k_ref[0, 0:(j + 1) * MOBA_BLOCK, hh * V7X_LANES:(hh + 1) * V7X_LANES]
        s = jnp.dot(k, qz, preferred_element_type=F32)
        gates = []
        for n in range(j + 1):
            rows = slice(n * MOBA_BLOCK, (n + 1) * MOBA_BLOCK)
            if n == j:
                sn = s[rows] + bias_ref[hh, 0]
            elif n == j - 1:
                sn = s[rows] + bias_ref[hh, 1]
            else:
                sn = s[rows]
            s_scr[buf, rows, :] = sn
            gates.append(_zero_after(sn[0:1, :]))
        return gates

    def finish(hh, j, buf, gates):
        cols = slice(j * MOBA_BLOCK, (j + 1) * MOBA_BLOCK)
        heads = slice(hh * HEAD_DIM, (hh + 1) * HEAD_DIM)
        m = jnp.max(s_scr[buf, 0:(j + 1) * MOBA_BLOCK, :], axis=0, keepdims=True)
        ones = jnp.ones((ONES_ROWS, MOBA_BLOCK), BF16)
        acc = None
        for n in range(j + 1):
            rows = slice(n * MOBA_BLOCK, (n + 1) * MOBA_BLOCK)
            mn = m + gates[min(n, len(gates) - 1)] if gates else m
            p = jnp.exp2(s_scr[buf, rows, :] - mn).astype(BF16)
            part = jnp.dot(jnp.concatenate([vt_ref[0, heads, rows], ones], axis=0), p,
                           preferred_element_type=F32)
            acc = part if acc is None else acc + part
        o_ref[0, heads, cols] = (acc[0:HEAD_DIM] / acc[HEAD_DIM:HEAD_DIM + 1]).astype(BF16)

    tasks = [(hh, j) for j in reversed(range(n_blk)) for hh in range(ATTN_HEADS)]
    pending, issued = [], []
    for i, (hh, j) in enumerate(tasks):
        buf = i % SCORE_BUFFERS
        issued.append(scores(hh, j, buf))
        pending.append((hh, j, buf))
        if len(pending) > SCORE_LOOKAHEAD:
            finish(*pending.pop(0), issued[i - GATE_BACK])
    for item in pending:
        finish(*item, None)


def _attn(qt, qadd, k, vt, bias):
    b, _, s = qt.shape
    n_blk = s // MOBA_BLOCK
    nh = ATTN_HEADS
    return pl.pallas_call(
        functools.partial(_attn_kernel, n_blk=n_blk),
        grid=(b, MOBA_HEADS // nh),
        in_specs=[pl.BlockSpec((1, nh * HEAD_DIM, s), lambda i, h: (i, h, 0)),
                  pl.BlockSpec((1, nh, 2 * n_blk, s), lambda i, h: (i, h, 0, 0)),
                  pl.BlockSpec((1, s, nh * V7X_LANES), lambda i, h: (i, 0, h)),
                  pl.BlockSpec((1, nh * HEAD_DIM, s), lambda i, h: (i, h, 0)),
                  pl.BlockSpec((nh, 2, MOBA_BLOCK, MOBA_BLOCK), lambda i, h: (h, 0, 0, 0))],
        out_specs=pl.BlockSpec((1, nh * HEAD_DIM, s), lambda i, h: (i, h, 0)),
        out_shape=jax.ShapeDtypeStruct((b, MOBA_WIDTH, s), BF16),
        scratch_shapes=[pltpu.VMEM((SCORE_BUFFERS, s, MOBA_BLOCK), F32)],
        compiler_params=_params(2),
        name="attn",
    )(qt, qadd, k, vt, bias)


def _branches_kernel(h_ref, hlo_ref, wu_ref, wv_ref, lng_ref, lnb_ref, ws_ref, bs_ref, wq_ref, kt_ref, vm_ref,
                     gw_ref, far_ref, a_ref, c_ref, qadd_ref, *, n_blk):
    causal = (lax.broadcasted_iota(jnp.int32, (CHUNK, CHUNK), 0)
              >= lax.broadcasted_iota(jnp.int32, (CHUNK, CHUNK), 1))
    ws = [jnp.where(causal, ws_ref[g], 0.0).astype(BF16) for g in range(GMLP_GROUPS)]
    ones = jnp.ones((vm_ref.shape[1], MEM_HEAD_DIM), BF16)
    far = jnp.concatenate([far_ref[...]] * (MOBA_BLOCK // V7X_LANES), axis=-1)
    n_slabs = h_ref.shape[0] // MOBA_BLOCK

    def project(i):
        rows = slice(i * MOBA_BLOCK, (i + 1) * MOBA_BLOCK)
        h = h_ref[rows, :]
        cq = jnp.dot(h, wq_ref[...], preferred_element_type=F32).astype(BF16)
        v = jnp.dot(h, wv_ref[...], preferred_element_type=F32)
        u = jnp.dot(h, wu_ref[...], preferred_element_type=F32)
        res = jnp.dot(jnp.concatenate([h, hlo_ref[rows, :]], axis=0), gw_ref[0], preferred_element_type=F32)
        ps = []
        for hd in range(MEM_HEADS):
            sl = slice(hd * MEM_HEAD_DIM, (hd + 1) * MEM_HEAD_DIM)
            s = jnp.dot(cq[:, sl], kt_ref[0, sl, :], preferred_element_type=F32) * (MEM_HEAD_DIM ** -0.5 * LOG2E)
            ps.append(jnp.exp2(s - jnp.max(s, axis=-1, keepdims=True)).astype(BF16))
        return rows, v, u, ps, res

    def finish(i, rows, v, u, ps, res):
        v = jax.nn.gelu(v)
        vc = v - jnp.mean(v, axis=-1, keepdims=True)
        vn = vc * lax.rsqrt(jnp.mean(vc * vc, axis=-1, keepdims=True) + EPS) * lng_ref[...] + lnb_ref[...]
        vn = vn.astype(BF16)
        u = jax.nn.gelu(u)
        chunks = [slice(c * CHUNK, (c + 1) * CHUNK) for c in range(MOBA_BLOCK // CHUNK)]
        for g in range(GMLP_GROUPS):
            lanes = slice(g * GMLP_GROUP_DIM, (g + 1) * GMLP_GROUP_DIM)
            mixed = jnp.dot(ws[g], jnp.concatenate([vn[toks, lanes] for toks in chunks], axis=1),
                            preferred_element_type=F32)
            for c, toks in enumerate(chunks):
                mc = mixed[:, c * GMLP_GROUP_DIM:(c + 1) * GMLP_GROUP_DIM] + bs_ref[g]
                a_ref[rows.start + c * CHUNK:rows.start + (c + 1) * CHUNK, lanes] = (
                    u[toks, lanes] * mc).astype(BF16)
        for hd, p in enumerate(ps):
            sl = slice(hd * MEM_HEAD_DIM, (hd + 1) * MEM_HEAD_DIM)
            acc = jnp.dot(p, jnp.concatenate([vm_ref[0, :, sl], ones], axis=1), preferred_element_type=F32)
            c_ref[rows, sl] = (acc[:, :MEM_HEAD_DIM] / acc[:, MEM_HEAD_DIM:]).astype(BF16)
        cur = pl.program_id(1) * n_slabs + i
        hi, lo = res[:MOBA_BLOCK], res[MOBA_BLOCK:]
        gate = (hi[:, :GATE_ROWS] + (hi[:, GATE_ROWS:] + lo[:, :GATE_ROWS])) + lo[:, GATE_ROWS:]
        gate = gate.T[0:MOBA_HEADS * n_blk].reshape(MOBA_HEADS, n_blk, MOBA_BLOCK)
        blk = lax.broadcasted_iota(jnp.int32, gate.shape, 1)
        past = blk < cur
        gate = jnp.where(past, gate, NEG)
        rank = jnp.zeros(gate.shape, jnp.int32)
        for m in range(n_blk):
            gm = gate[:, m:m + 1, :]
            tie = jnp.where(blk > m, 1, 0)
            rank = rank + jnp.where(gm > gate, 1, jnp.where(gm == gate, tie, 0))
        mask = jnp.where(rank < min(MOBA_TOPK, n_blk), jnp.where(past, 0.0, NEG), NEG)
        add = jnp.where(blk < cur - 1, mask + far, jnp.where(blk == cur - 1, mask, 0.0))
        add_hi, add_lo = _split_bf16(add)
        qadd_ref[0, :, :, rows] = jnp.concatenate(
            [add_hi.astype(F32), add_lo.astype(F32)], axis=1).astype(BF16)

    staged = project(0)
    for i in range(n_slabs):
        nxt = project(i + 1) if i + 1 < n_slabs else None
        finish(i, *staged)
        staged = nxt


def _branches(h_hi, h_lo, b, s, w_u, w_v, ln_g, ln_b, w_s, b_s_lanes, w_cq, kt, vm, gt, far_lanes,
              tm=2 * TOKEN_TILE):
    d = h_hi.shape[-1]
    n_blk = s // MOBA_BLOCK
    nt = s // tm
    m = kt.shape[-1]
    tok = lambda i, j: (i * nt + j, 0)
    return pl.pallas_call(
        functools.partial(_branches_kernel, n_blk=n_blk),
        grid=(b, nt),
        in_specs=[pl.BlockSpec((tm, d), tok), pl.BlockSpec((tm, d), tok),
                  _resident(w_u.shape), _resident(w_v.shape), _resident(ln_g.shape), _resident(ln_b.shape),
                  _resident(w_s.shape), _resident(b_s_lanes.shape), _resident(w_cq.shape),
                  pl.BlockSpec((1, MEM_WIDTH, m), lambda i, j: (i, 0, 0)),
                  pl.BlockSpec((1, m, MEM_WIDTH), lambda i, j: (i, 0, 0)),
                  pl.BlockSpec((1, d, 2 * GATE_ROWS), lambda i, j: (i, 0, 0)),
                  _resident(far_lanes.shape)],
        out_specs=[pl.BlockSpec((tm, GMLP_WIDTH), tok), pl.BlockSpec((tm, MEM_WIDTH), tok),
                   pl.BlockSpec((1, MOBA_HEADS, 2 * n_blk, tm), lambda i, j: (i, 0, 0, j))],
        out_shape=[jax.ShapeDtypeStruct((b * s, GMLP_WIDTH), BF16),
                   jax.ShapeDtypeStruct((b * s, MEM_WIDTH), BF16),
                   jax.ShapeDtypeStruct((b, MOBA_HEADS, 2 * n_blk, s), BF16)],
        compiler_params=_params(2),
        name="branches",
    )(h_hi, h_lo, w_u, w_v, ln_g, ln_b, w_s, b_s_lanes, w_cq, kt, vm, gt, far_lanes)


def _memkv_kernel(mem_ref, g_ref, wkt_ref, wv_ref, kt_ref, v_ref):
    mn = _rms(mem_ref[0], g_ref[...]).astype(BF16)
    kt_ref[0] = lax.dot_general(wkt_ref[...], mn, (((1,), (1,)), ((), ())),
                                preferred_element_type=F32).astype(BF16)
    v_ref[0] = jnp.dot(mn, wv_ref[...], preferred_element_type=F32).astype(BF16)


def _memkv(mem, g, w_kt, w_v):
    b, m, d = mem.shape
    return pl.pallas_call(
        _memkv_kernel,
        grid=(b,),
        in_specs=[pl.BlockSpec((1, m, d), lambda i: (i, 0, 0)), _resident((1, d)),
                  _resident(w_kt.shape), _resident(w_v.shape)],
        out_specs=[pl.BlockSpec((1, MEM_WIDTH, m), lambda i: (i, 0, 0)),
                   pl.BlockSpec((1, m, MEM_WIDTH), lambda i: (i, 0, 0))],
        out_shape=[jax.ShapeDtypeStruct((b, MEM_WIDTH, m), BF16),
                   jax.ShapeDtypeStruct((b, m, MEM_WIDTH), BF16)],
        compiler_params=_params(1),
        name="memkv",
    )(mem, g, w_kt, w_v)


def _merge_kernel(x_ref, h_ref, a_ref, bt_ref, c_ref, wg_ref, wa_ref, wb_ref, wc_ref, wo_ref,
                  lnpost_ref, lnpre_ref, x1_ref, h2_ref):
    d = x_ref.shape[-1]
    for i in range(x_ref.shape[0] // MLP_SLAB):
        rows = slice(i * MLP_SLAB, (i + 1) * MLP_SLAB)
        gates = jax.nn.sigmoid(jnp.dot(h_ref[rows, :], wg_ref[...], preferred_element_type=F32))
        pa = jnp.dot(a_ref[rows, :], wa_ref[...], preferred_element_type=F32)
        pb = lax.dot_general(bt_ref[0, :, rows], wb_ref[...], (((0,), (0,)), ((), ())),
                             preferred_element_type=F32)
        pc = jnp.dot(c_ref[rows, :], wc_ref[...], preferred_element_type=F32)
        merged = gates[:, :d] * pa + gates[:, d:2 * d] * pb + gates[:, 2 * d:] * pc
        mo = jnp.dot(merged.astype(BF16), wo_ref[...], preferred_element_type=F32)
        x1 = x_ref[rows, :] + _rms(mo, lnpost_ref[...])
        x1_ref[rows, :] = x1
        h2_ref[rows, :] = _rms(x1, lnpre_ref[...]).astype(BF16)


def _merge(x2, h2, a_out, b_out_t, c_out, w_g, w_a, w_b, w_c, w_o, ln_post, ln_pre, tm=MLP_TILE):
    n_tok, d = x2.shape
    b, _, s = b_out_t.shape
    nt = s // tm
    tok = lambda i, j: (i * nt + j, 0)
    return pl.pallas_call(
        _merge_kernel,
        grid=(b, nt),
        in_specs=[pl.BlockSpec((tm, d), tok), pl.BlockSpec((tm, d), tok),
                  pl.BlockSpec((tm, GMLP_WIDTH), tok),
                  pl.BlockSpec((1, MOBA_WIDTH, tm), lambda i, j: (i, 0, j)),
                  pl.BlockSpec((tm, MEM_WIDTH), tok),
                  _resident(w_g.shape), _resident(w_a.shape), _resident(w_b.shape),
                  _resident(w_c.shape), _resident(w_o.shape),
                  _resident((1, d)), _resident((1, d))],
        out_specs=[pl.BlockSpec((tm, d), tok), pl.BlockSpec((tm, d), tok)],
        out_shape=[jax.ShapeDtypeStruct((n_tok, d), F32), jax.ShapeDtypeStruct((n_tok, d), BF16)],
        compiler_params=_params(2),
        name="merge",
    )(x2, h2, a_out, b_out_t, c_out, w_g, w_a, w_b, w_c, w_o, ln_post, ln_pre)


def _ffn_kernel(x1_ref, h2_ref, wg_ref, wu_ref, wd_ref, ln_ref, o_ref):
    for i in range(x1_ref.shape[0] // MLP_SLAB):
        rows = slice(i * MLP_SLAB, (i + 1) * MLP_SLAB)
        h2 = h2_ref[rows, :]
        g = jnp.dot(h2, wg_ref[...], preferred_element_type=F32)
        u = jnp.dot(h2, wu_ref[...], preferred_element_type=F32)
        act = (jax.nn.silu(g) * u).astype(BF16)
        f = jnp.dot(act, wd_ref[...], preferred_element_type=F32)
        o_ref[rows, :] = x1_ref[rows, :] + _rms(f, ln_ref[...])


def _ffn(x1, h2, w_g, w_u, w_d, ln_post, tm=MLP_TILE):
    n_tok, d = x1.shape
    return pl.pallas_call(
        _ffn_kernel,
        grid=(n_tok // tm,),
        in_specs=[pl.BlockSpec((tm, d), lambda i: (i, 0)), pl.BlockSpec((tm, d), lambda i: (i, 0)),
                  _resident(w_g.shape), _resident(w_u.shape), _resident(w_d.shape),
                  _resident((1, d))],
        out_specs=pl.BlockSpec((tm, d), lambda i: (i, 0)),
        out_shape=jax.ShapeDtypeStruct((n_tok, d), F32),
        compiler_params=_params(1),
        name="ffn",
    )(x1, h2, w_g, w_u, w_d, ln_post)


def kernel(x, mem, ln_mix_pre, ln_mix_post, ln_ffn_pre, ln_ffn_post, ln_mem, w_in, ln_v_gain, ln_v_bias,
           w_spatial, b_spatial, rel_bias, w_mem_kv, w_branch_a, w_branch_b, w_branch_c, w_out,
           w_ffn_gate, w_ffn_up, w_ffn_down):
    b, s, d = x.shape
    assert s % MOBA_BLOCK == 0 and s % TOKEN_TILE == 0 and d % V7X_LANES == 0
    depth = w_in.shape[0]
    cuts = [0, GMLP_WIDTH, 2 * GMLP_WIDTH, 2 * GMLP_WIDTH + MOBA_WIDTH, 2 * GMLP_WIDTH + 2 * MOBA_WIDTH,
            2 * GMLP_WIDTH + 3 * MOBA_WIDTH, 2 * GMLP_WIDTH + 3 * MOBA_WIDTH + MEM_WIDTH]
    rb_flat = rel_bias.astype(F32).reshape(-1)
    bias = _bias_tiles(rb_flat)
    far_lanes = jnp.broadcast_to((rel_bias[REL_BUCKETS - 1].astype(F32) * LOG2E)[:, None, None],
                                 (MOBA_HEADS, s // MOBA_BLOCK, V7X_LANES))
    row = lambda v: v.reshape(1, -1).astype(F32)
    for l in range(depth):
        wi = w_in[l]
        w_u, w_v, w_q, w_k, w_v2, w_cq = (wi[:, cuts[i]:cuts[i + 1]] for i in range(6))
        w_g = wi[:, cuts[6]:]
        x2 = x.reshape(b * s, d)

        w_qt_scaled = (w_q * (HEAD_DIM ** -0.5)).T
        h2, h_lo, hmean, qt, k, vt = _qkv(x2, row(ln_mix_pre[l]), b, s, (w_qt_scaled * LOG2E).astype(BF16),
                                          w_k.astype(BF16), w_v2.T.astype(BF16))
        gt = _gatew(hmean.reshape(-1, d), w_k, w_qt_scaled, b)
        wkv = w_mem_kv[l]
        kt_mem, v_mem = _memkv(mem, row(ln_mem[l]), wkv[:, :MEM_WIDTH].T.astype(BF16),
                               wkv[:, MEM_WIDTH:].astype(BF16))
        b_s_lanes = jnp.broadcast_to(b_spatial[l][:, :, None], (GMLP_GROUPS, CHUNK, GMLP_GROUP_DIM))
        a_out, c_out, qadd = _branches(
            h2, h_lo, b, s, w_u.astype(BF16), w_v.astype(BF16), row(ln_v_gain[l]), row(ln_v_bias[l]),
            w_spatial[l], b_s_lanes.astype(F32), w_cq.astype(BF16), kt_mem, v_mem, gt, far_lanes)
        b_out_t = _attn(qt, qadd, k, vt, bias)

        x1, hn = _merge(x2, h2, a_out, b_out_t, c_out, w_g.astype(BF16), w_branch_a[l].astype(BF16),
                        w_branch_b[l].astype(BF16), w_branch_c[l].astype(BF16), w_out[l].astype(BF16),
                        row(ln_mix_post[l]), row(ln_ffn_pre[l]))
        out = _ffn(x1, hn, w_ffn_gate[l].astype(BF16), w_ffn_up[l].astype(BF16),
                   w_ffn_down[l].astype(BF16), row(ln_ffn_post[l]))
        x = out.reshape(b, s, d)
    return x
```

```python
import functools
import math

import jax
import jax.numpy as jnp
from jax import lax
from jax.experimental import pallas as pl
from jax.experimental.pallas import tpu as pltpu

F32 = jnp.float32
BF16 = jnp.bfloat16

EPS = 1e-6
NEG = -1e30
GMLP_GROUPS = 6
GMLP_GROUP_DIM = 128
GMLP_WIDTH = GMLP_GROUPS * GMLP_GROUP_DIM
CHUNK = 128
MOBA_HEADS = 12
HEAD_DIM = 64
MOBA_WIDTH = MOBA_HEADS * HEAD_DIM
MOBA_BLOCK = 256
MOBA_TOPK = 3
REL_BUCKETS = 32
REL_MAX_DIST = 128
LOG2E = math.log2(math.e)
ONES_ROWS = 16
ATTN_HEADS = 4
SCORE_LOOKAHEAD = 3
SCORE_BUFFERS = SCORE_LOOKAHEAD + 1
GATE_BACK = 1
MEM_HEADS = 4
MEM_HEAD_DIM = 128
MEM_WIDTH = MEM_HEADS * MEM_HEAD_DIM
N_BRANCHES = 3

V7X_LANES = 128
V7X_VMEM_LIMIT = 56 * 1024 * 1024

TOKEN_TILE = 512
MLP_TILE = 1024
MLP_SLAB = 256
QKV_SLAB = 256
GATE_ROWS = 128


def _params(n_axes, vmem=V7X_VMEM_LIMIT):
    return pltpu.CompilerParams(
        dimension_semantics=("arbitrary",) * n_axes, vmem_limit_bytes=vmem)


def _resident(shape):
    zeros = (0,) * len(shape)
    return pl.BlockSpec(shape, lambda *_: zeros, pipeline_mode=pl.Buffered(1))


def _rms(x, g):
    return x * lax.rsqrt(jnp.mean(x * x, axis=-1, keepdims=True) + EPS) * g


def _split_bf16(x):
    hi = x.astype(BF16)
    return hi, (x - hi.astype(F32)).astype(BF16)


def _gatew_kernel(hm_ref, wk_ref, wqt_ref, g_ref, gt_scr, *, n_batch, n_blk):
    kmean = jnp.dot(hm_ref[...], wk_ref[...], precision=lax.Precision.HIGHEST,
                    preferred_element_type=F32)
    used = MOBA_HEADS * n_blk
    gt_scr[:, used:, :] = jnp.zeros((n_batch, GATE_ROWS - used, gt_scr.shape[-1]), F32)
    for h in range(MOBA_HEADS):
        sl = slice(h * HEAD_DIM, (h + 1) * HEAD_DIM)
        res = jnp.dot(kmean[:, sl], wqt_ref[sl, :], precision=lax.Precision.HIGHEST,
                      preferred_element_type=F32)
        for b in range(n_batch):
            gt_scr[b, h * n_blk:(h + 1) * n_blk, :] = res[b * n_blk:(b + 1) * n_blk, :]
    for b in range(n_batch):
        hi, lo = _split_bf16(gt_scr[b].T)
        g_ref[b, :, 0:GATE_ROWS] = hi
        g_ref[b, :, GATE_ROWS:] = lo


def _gatew(hmean, w_k, w_qt, n_batch):
    rows, d = hmean.shape
    n_blk = rows // n_batch
    assert MOBA_HEADS * n_blk <= GATE_ROWS
    return pl.pallas_call(
        functools.partial(_gatew_kernel, n_batch=n_batch, n_blk=n_blk),
        grid=(1,),
        in_specs=[_resident((rows, d)), _resident(w_k.shape), _resident(w_qt.shape)],
        out_specs=pl.BlockSpec((n_batch, d, 2 * GATE_ROWS), lambda i: (0, 0, 0)),
        out_shape=jax.ShapeDtypeStruct((n_batch, d, 2 * GATE_ROWS), BF16),
        scratch_shapes=[pltpu.VMEM((n_batch, GATE_ROWS, d), F32)],
        compiler_params=_params(1),
        name="gatew",
    )(hmean, w_k, w_qt)


def _qkv_kernel(x_ref, g_ref, wqt_ref, wk_ref, wvt_ref, h_ref, hlo_ref, hmean_ref, qt_ref, k_ref, vt_ref,
                *, n_blk):
    nt = (((1,), (1,)), ((), ()))
    n_slabs = x_ref.shape[0] // QKV_SLAB
    lane =lax.broadcasted_iota(jnp.int32, (QKV_SLAB, V7X_LANES), 1)
    low = lane < HEAD_DIM
    for i in range(n_slabs):
        rows = slice(i * QKV_SLAB, (i + 1) * QKV_SLAB)
        blk = pl.program_id(1) * n_slabs + i
        onehot = jnp.where((lane == HEAD_DIM + blk) | (lane == HEAD_DIM + n_blk + blk), 1.0, 0.0)
        hf = _rms(x_ref[rows, :], g_ref[...])
        h, hlo_ref[rows, :] = _split_bf16(hf)
        h_ref[rows, :] = h
        for c in range(QKV_SLAB // MOBA_BLOCK):
            hmean_ref[i * (QKV_SLAB // MOBA_BLOCK) + c] = jnp.mean(
                hf[c * MOBA_BLOCK:(c + 1) * MOBA_BLOCK], axis=0, keepdims=True)
        qt_ref[0, :, rows] = lax.dot_general(wqt_ref[...], h, nt, preferred_element_type=F32).astype(BF16)
        vt_ref[0, :, rows] = lax.dot_general(wvt_ref[...], h, nt, preferred_element_type=F32).astype(BF16)
        kf = jnp.dot(h, wk_ref[...], preferred_element_type=F32)
        for p in range(MOBA_HEADS // 2):
            pair = kf[:, p * V7X_LANES:(p + 1) * V7X_LANES]
            even = jnp.where(low, pair, onehot)
            odd = jnp.where(low, pltpu.roll(pair, HEAD_DIM, 1), onehot)
            k_ref[0, rows, (2 * p) * V7X_LANES:(2 * p + 1) * V7X_LANES] = even.astype(BF16)
            k_ref[0, rows, (2 * p + 1) * V7X_LANES:(2 * p + 2) * V7X_LANES] = odd.astype(BF16)


def _qkv(x2, g, b, s, w_qt, w_k, w_vt, tm=2 * TOKEN_TILE):
    d = x2.shape[-1]
    nt = s // tm
    blocks = tm // MOBA_BLOCK
    n_blk = s // MOBA_BLOCK
    assert QKV_SLAB == MOBA_BLOCK and HEAD_DIM + 2 * n_blk <= V7X_LANES
    tok = lambda i, j: (i * nt + j, 0)
    return pl.pallas_call(
        functools.partial(_qkv_kernel, n_blk=n_blk),
        grid=(b, nt),
        in_specs=[pl.BlockSpec((tm, d), tok), _resident((1, d)),
                  _resident(w_qt.shape), _resident(w_k.shape), _resident(w_vt.shape)],
        out_specs=[pl.BlockSpec((tm, d), tok), pl.BlockSpec((tm, d), tok),
                   pl.BlockSpec((blocks, 1, d), lambda i, j: (i * nt + j, 0, 0)),
                   pl.BlockSpec((1, MOBA_WIDTH, tm), lambda i, j: (i, 0, j)),
                   pl.BlockSpec((1, tm, MOBA_HEADS * V7X_LANES), lambda i, j: (i, j, 0)),
                   pl.BlockSpec((1, MOBA_WIDTH, tm), lambda i, j: (i, 0, j))],
        out_shape=[jax.ShapeDtypeStruct((b * s, d), BF16), jax.ShapeDtypeStruct((b * s, d), BF16),
                   jax.ShapeDtypeStruct((b * s // MOBA_BLOCK, 1, d), F32),
                   jax.ShapeDtypeStruct((b, MOBA_WIDTH, s), BF16),
                   jax.ShapeDtypeStruct((b, s, MOBA_HEADS * V7X_LANES), BF16),
                   jax.ShapeDtypeStruct((b, MOBA_WIDTH, s), BF16)],
        compiler_params=_params(2),
        name="qkv",
    )(x2, g, w_qt, w_k, w_vt)


def _bias_kernel(rb_ref, out_ref):
    h = pl.program_id(0)
    ik = lax.broadcasted_iota(jnp.int32, (MOBA_BLOCK, MOBA_BLOCK), 0)
    iq = lax.broadcasted_iota(jnp.int32, (MOBA_BLOCK, MOBA_BLOCK), 1)
    max_exact = REL_BUCKETS // 2
    for t in range(2):
        dist = iq - ik + t * MOBA_BLOCK
        n = jnp.maximum(dist, 0)
        nf = jnp.maximum(n, 1).astype(F32)
        large = max_exact + (jnp.log(nf / max_exact) / math.log(REL_MAX_DIST / max_exact)
                             * (REL_BUCKETS - max_exact)).astype(jnp.int32)
        large = jnp.minimum(large, REL_BUCKETS - 1)
        bucket = jnp.where(n < max_exact, n, large)
        val = jnp.zeros((MOBA_BLOCK, MOBA_BLOCK), F32)
        for bk in range(REL_BUCKETS):
            val = jnp.where(bucket == bk, rb_ref[bk * MOBA_HEADS + h], val)
        val = val * LOG2E
        if t == 0:
            val = jnp.where(dist >= 0, val, NEG)
        out_ref[0, t] = val


def _bias_tiles(rb_flat):
    return pl.pallas_call(
        _bias_kernel,
        grid=(MOBA_HEADS,),
        in_specs=[pl.BlockSpec(memory_space=pltpu.SMEM)],
        out_specs=pl.BlockSpec((1, 2, MOBA_BLOCK, MOBA_BLOCK), lambda h: (h, 0, 0, 0)),
        out_shape=jax.ShapeDtypeStruct((MOBA_HEADS, 2, MOBA_BLOCK, MOBA_BLOCK), F32),
        compiler_params=_params(1),
        name="bias",
    )(rb_flat)


def _zero_after(x):
    bits = lax.bitcast_convert_type(x, jnp.uint32)
    sixteen = jnp.uint32(16)
    return lax.bitcast_convert_type(
        lax.shift_right_logical(lax.shift_right_logical(bits, sixteen), sixteen), F32)


def _attn_kernel(qt_ref, qadd_ref, k_ref, vt_ref, bias_ref, o_ref, s_scr, *, n_blk):
    pad = jnp.zeros((V7X_LANES - HEAD_DIM - 2 * n_blk, MOBA_BLOCK), BF16)

    def scores(hh, j, buf):
        cols = slice(j * MOBA_BLOCK, (j + 1) * MOBA_BLOCK)
        q = qt_ref[0, hh * HEAD_DIM:(hh + 1) * HEAD_DIM, cols]
        qz = jnp.concatenate([q, qadd_ref[0, hh, :, cols], pad], axis=0)
        k = k_ref[0, 0:(j + 1) * MOBA_BLOCK, hh * V7X_LANES:(hh + 1) * V7X_LANES]
        s = jnp.dot(k, qz, preferred_element_type=F32)
        gates = []
        for n in range(j + 1):
            rows = slice(n * MOBA_BLOCK, (n + 1) * MOBA_BLOCK)
            if n == j:
                sn = s[rows] + bias_ref[hh, 0]
            elif n == j - 1:
                sn = s[rows] + bias_ref[hh, 1]
            else:
                sn = s[rows]
            s_scr[buf, rows, :] = sn
            gates.append(_zero_after(sn[0:1, :]))
        return gates

    def finish(hh, j, buf, gates):
        cols = slice(j * MOBA_BLOCK, (j + 1) * MOBA_BLOCK)
        heads = slice(hh * HEAD_DIM, (hh + 1) * HEAD_DIM)
        m = jnp.max(s_scr[buf, 0:(j + 1) * MOBA_BLOCK, :], axis=0, keepdims=True)
        ones = jnp.ones((ONES_ROWS, MOBA_BLOCK), BF16)
        acc = None
        for n in range(j + 1):
            rows = slice(n * MOBA_BLOCK, (n + 1) * MOBA_BLOCK)
            mn = m + gates[min(n, len(gates) - 1)] if gates else m
            p = jnp.exp2(s_scr[buf, rows, :] - mn).astype(BF16)
            part = jnp.dot(jnp.concatenate([vt_ref[0, heads, rows], ones], axis=0), p,
                           preferred_element_type=F32)
            acc = part if acc is None else acc + part
        o_ref[0, heads, cols] = (acc[0:HEAD_DIM] / acc[HEAD_DIM:HEAD_DIM + 1]).astype(BF16)

    tasks = [(hh, j) for j in reversed(range(n_blk)) for hh in range(ATTN_HEADS)]
    pending, issued = [], []
    for i, (hh, j) in enumerate(tasks):
        buf = i % SCORE_BUFFERS
        issued.append(scores(hh, j, buf))
        pending.append((hh, j, buf))
        if len(pending) > SCORE_LOOKAHEAD:
            finish(*pending.pop(0), issued[i - GATE_BACK])
    for item in pending:
        finish(*item, None)


def _attn(qt, qadd, k, vt, bias):
    b, _, s = qt.shape
    n_blk = s // MOBA_BLOCK
    nh = ATTN_HEADS
    return pl.pallas_call(
        functools.partial(_attn_kernel, n_blk=n_blk),
        grid=(b, MOBA_HEADS // nh),
        in_specs=[pl.BlockSpec((1, nh * HEAD_DIM, s), lambda i, h: (i, h, 0)),
                  pl.BlockSpec((1, nh, 2 * n_blk, s), lambda i, h: (i, h, 0, 0)),
                  pl.BlockSpec((1, s, nh * V7X_LANES), lambda i, h: (i, 0, h)),
                  pl.BlockSpec((1, nh * HEAD_DIM, s), lambda i, h: (i, h, 0)),
                  pl.BlockSpec((nh, 2, MOBA_BLOCK, MOBA_BLOCK), lambda i, h: (h, 0, 0, 0))],
        out_specs=pl.BlockSpec((1, nh * HEAD_DIM, s), lambda i, h: (i, h, 0)),
        out_shape=jax.ShapeDtypeStruct((b, MOBA_WIDTH, s), BF16),
        scratch_shapes=[pltpu.VMEM((SCORE_BUFFERS, s, MOBA_BLOCK), F32)],
        compiler_params=_params(2),
        name="attn",
    )(qt, qadd, k, vt, bias)


def _branches_kernel(h_ref, hlo_ref, wu_ref, wv_ref, lng_ref, lnb_ref, ws_ref, bs_ref, wq_ref, kt_ref, vm_ref,
                     gw_ref, far_ref, a_ref, c_ref, qadd_ref, *, n_blk):
    causal = (lax.broadcasted_iota(jnp.int32, (CHUNK, CHUNK), 0)
              >= lax.broadcasted_iota(jnp.int32, (CHUNK, CHUNK), 1))
    ws = [jnp.where(causal, ws_ref[g], 0.0).astype(BF16) for g in range(GMLP_GROUPS)]
    ones = jnp.ones((vm_ref.shape[1], MEM_HEAD_DIM), BF16)
    far = jnp.concatenate([far_ref[...]] * (MOBA_BLOCK // V7X_LANES), axis=-1)
    n_slabs = h_ref.shape[0] // MOBA_BLOCK

    def project(i):
        rows = slice(i * MOBA_BLOCK, (i + 1) * MOBA_BLOCK)
        h = h_ref[rows, :]
        cq = jnp.dot(h, wq_ref[...], preferred_element_type=F32).astype(BF16)
        v = jnp.dot(h, wv_ref[...], preferred_element_type=F32)
        u = jnp.dot(h, wu_ref[...], preferred_element_type=F32)
        res = jnp.dot(jnp.concatenate([h, hlo_ref[rows, :]], axis=0), gw_ref[0], preferred_element_type=F32)
        ps = []
        for hd in range(MEM_HEADS):
            sl = slice(hd * MEM_HEAD_DIM, (hd + 1) * MEM_HEAD_DIM)
            s = jnp.dot(cq[:, sl], kt_ref[0, sl, :], preferred_element_type=F32) * (MEM_HEAD_DIM ** -0.5 * LOG2E)
            ps.append(jnp.exp2(s - jnp.max(s, axis=-1, keepdims=True)).astype(BF16))
        return rows, v, u, ps, res

    def finish(i, rows, v, u, ps, res):
        v = jax.nn.gelu(v)
        vc = v - jnp.mean(v, axis=-1, keepdims=True)
        vn = vc * lax.rsqrt(jnp.mean(vc * vc, axis=-1, keepdims=True) + EPS) * lng_ref[...] + lnb_ref[...]
        vn = vn.astype(BF16)
        u = jax.nn.gelu(u)
        chunks = [slice(c * CHUNK, (c + 1) * CHUNK) for c in range(MOBA_BLOCK // CHUNK)]
        for g in range(GMLP_GROUPS):
            lanes = slice(g * GMLP_GROUP_DIM, (g + 1) * GMLP_GROUP_DIM)
            mixed = jnp.dot(ws[g], jnp.concatenate([vn[toks, lanes] for toks in chunks], axis=1),
                            preferred_element_type=F32)
            for c, toks in enumerate(chunks):
                mc = mixed[:, c * GMLP_GROUP_DIM:(c + 1) * GMLP_GROUP_DIM] + bs_ref[g]
                a_ref[rows.start + c * CHUNK:rows.start + (c + 1) * CHUNK, lanes] = (
                    u[toks, lanes] * mc).astype(BF16)
        for hd, p in enumerate(ps):
            sl = slice(hd * MEM_HEAD_DIM, (hd + 1) * MEM_HEAD_DIM)
            acc = jnp.dot(p, jnp.concatenate([vm_ref[0, :, sl], ones], axis=1), preferred_element_type=F32)
            c_ref[rows, sl] = (acc[:, :MEM_HEAD_DIM] / acc[:, MEM_HEAD_DIM:]).astype(BF16)
        cur = pl.program_id(1) * n_slabs + i
        hi, lo = res[:MOBA_BLOCK], res[MOBA_BLOCK:]
        gate = (hi[:, :GATE_ROWS] + (hi[:, GATE_ROWS:] + lo[:, :GATE_ROWS])) + lo[:, GATE_ROWS:]
        gate = gate.T[0:MOBA_HEADS * n_blk].reshape(MOBA_HEADS, n_blk, MOBA_BLOCK)
        blk = lax.broadcasted_iota(jnp.int32, gate.shape, 1)
        past = blk < cur
        gate = jnp.where(past, gate, NEG)
        rank = jnp.zeros(gate.shape, jnp.int32)
        for m in range(n_blk):
            gm = gate[:, m:m + 1, :]
            tie = jnp.where(blk > m, 1, 0)
            rank = rank + jnp.where(gm > gate, 1, jnp.where(gm == gate, tie, 0))
        mask = jnp.where(rank < min(MOBA_TOPK, n_blk), jnp.where(past, 0.0, NEG), NEG)
        add = jnp.where(blk < cur - 1, mask + far, jnp.where(blk == cur - 1, mask, 0.0))
        add_hi, add_lo = _split_bf16(add)
        qadd_ref[0, :, :, rows] = jnp.concatenate(
            [add_hi.astype(F32), add_lo.astype(F32)], axis=1).astype(BF16)

    staged = project(0)
    for i in range(n_slabs):
        nxt = project(i + 1) if i + 1 < n_slabs else None
        finish(i, *staged)
        staged = nxt


def _branches(h_hi, h_lo, b, s, w_u, w_v, ln_g, ln_b, w_s, b_s_lanes, w_cq, kt, vm, gt, far_lanes,
              tm=2 * TOKEN_TILE):
    d = h_hi.shape[-1]
    n_blk = s // MOBA_BLOCK
    nt = s // tm
    m = kt.shape[-1]
    tok = lambda i, j: (i * nt + j, 0)
    return pl.pallas_call(
        functools.partial(_branches_kernel, n_blk=n_blk),
        grid=(b, nt),
        in_specs=[pl.BlockSpec((tm, d), tok), pl.BlockSpec((tm, d), tok),
                  _resident(w_u.shape), _resident(w_v.shape), _resident(ln_g.shape), _resident(ln_b.shape),
                  _resident(w_s.shape), _resident(b_s_lanes.shape), _resident(w_cq.shape),
                  pl.BlockSpec((1, MEM_WIDTH, m), lambda i, j: (i, 0, 0)),
                  pl.BlockSpec((1, m, MEM_WIDTH), lambda i, j: (i, 0, 0)),
                  pl.BlockSpec((1, d, 2 * GATE_ROWS), lambda i, j: (i, 0, 0)),
                  _resident(far_lanes.shape)],
        out_specs=[pl.BlockSpec((tm, GMLP_WIDTH), tok), pl.BlockSpec((tm, MEM_WIDTH), tok),
                   pl.BlockSpec((1, MOBA_HEADS, 2 * n_blk, tm), lambda i, j: (i, 0, 0, j))],
        out_shape=[jax.ShapeDtypeStruct((b * s, GMLP_WIDTH), BF16),
                   jax.ShapeDtypeStruct((b * s, MEM_WIDTH), BF16),
                   jax.ShapeDtypeStruct((b, MOBA_HEADS, 2 * n_blk, s), BF16)],
        compiler_params=_params(2),
        name="branches",
    )(h_hi, h_lo, w_u, w_v, ln_g, ln_b, w_s, b_s_lanes, w_cq, kt, vm, gt, far_lanes)


def _memkv_kernel(mem_ref, g_ref, wkt_ref, wv_ref, kt_ref, v_ref):
    mn = _rms(mem_ref[0], g_ref[...]).astype(BF16)
    kt_ref[0] = lax.dot_general(wkt_ref[...], mn, (((1,), (1,)), ((), ())),
                                preferred_element_type=F32).astype(BF16)
    v_ref[0] = jnp.dot(mn, wv_ref[...], preferred_element_type=F32).astype(BF16)


def _memkv(mem, g, w_kt, w_v):
    b, m, d = mem.shape
    return pl.pallas_call(
        _memkv_kernel,
        grid=(b,),
        in_specs=[pl.BlockSpec((1, m, d), lambda i: (i, 0, 0)), _resident((1, d)),
                  _resident(w_kt.shape), _resident(w_v.shape)],
        out_specs=[pl.BlockSpec((1, MEM_WIDTH, m), lambda i: (i, 0, 0)),
                   pl.BlockSpec((1, m, MEM_WIDTH), lambda i: (i, 0, 0))],
        out_shape=[jax.ShapeDtypeStruct((b, MEM_WIDTH, m), BF16),
                   jax.ShapeDtypeStruct((b, m, MEM_WIDTH), BF16)],
        compiler_params=_params(1),
        name="memkv",
    )(mem, g, w_kt, w_v)


def _merge_kernel(x_ref, h_ref, a_ref, bt_ref, c_ref, wg_ref, wa_ref, wb_ref, wc_ref, wo_ref,
                  lnpost_ref, lnpre_ref, x1_ref, h2_ref):
    d = x_ref.shape[-1]
    for i in range(x_ref.shape[0] // MLP_SLAB):
        rows = slice(i * MLP_SLAB, (i + 1) * MLP_SLAB)
        gates = jax.nn.sigmoid(jnp.dot(h_ref[rows, :], wg_ref[...], preferred_element_type=F32))
        pa = jnp.dot(a_ref[rows, :], wa_ref[...], preferred_element_type=F32)
        pb = lax.dot_general(bt_ref[0, :, rows], wb_ref[...], (((0,), (0,)), ((), ())),
                             preferred_element_type=F32)
        pc = jnp.dot(c_ref[rows, :], wc_ref[...], preferred_element_type=F32)
        merged = gates[:, :d] * pa + gates[:, d:2 * d] * pb + gates[:, 2 * d:] * pc
        mo = jnp.dot(merged.astype(BF16), wo_ref[...], preferred_element_type=F32)
        x1 = x_ref[rows, :] + _rms(mo, lnpost_ref[...])
        x1_ref[rows, :] = x1
        h2_ref[rows, :] = _rms(x1, lnpre_ref[...]).astype(BF16)


def _merge(x2, h2, a_out, b_out_t, c_out, w_g, w_a, w_b, w_c, w_o, ln_post, ln_pre, tm=MLP_TILE):
    n_tok, d = x2.shape
    b, _, s = b_out_t.shape
    nt = s // tm
    tok = lambda i, j: (i * nt + j, 0)
    return pl.pallas_call(
        _merge_kernel,
        grid=(b, nt),
        in_specs=[pl.BlockSpec((tm, d), tok), pl.BlockSpec((tm, d), tok),
                  pl.BlockSpec((tm, GMLP_WIDTH), tok),
                  pl.BlockSpec((1, MOBA_WIDTH, tm), lambda i, j: (i, 0, j)),
                  pl.BlockSpec((tm, MEM_WIDTH), tok),
                  _resident(w_g.shape), _resident(w_a.shape), _resident(w_b.shape),
                  _resident(w_c.shape), _resident(w_o.shape),
                  _resident((1, d)), _resident((1, d))],
        out_specs=[pl.BlockSpec((tm, d), tok), pl.BlockSpec((tm, d), tok)],
        out_shape=[jax.ShapeDtypeStruct((n_tok, d), F32), jax.ShapeDtypeStruct((n_tok, d), BF16)],
        compiler_params=_params(2),
        name="merge",
    )(x2, h2, a_out, b_out_t, c_out, w_g, w_a, w_b, w_c, w_o, ln_post, ln_pre)


def _ffn_kernel(x1_ref, h2_ref, wg_ref, wu_ref, wd_ref, ln_ref, o_ref):
    for i in range(x1_ref.shape[0] // MLP_SLAB):
        rows = slice(i * MLP_SLAB, (i + 1) * MLP_SLAB)
        h2 = h2_ref[rows, :]
        g = jnp.dot(h2, wg_ref[...], preferred_element_type=F32)
        u = jnp.dot(h2, wu_ref[...], preferred_element_type=F32)
        act = (jax.nn.silu(g) * u).astype(BF16)
        f = jnp.dot(act, wd_ref[...], preferred_element_type=F32)
        o_ref[rows, :] = x1_ref[rows, :] + _rms(f, ln_ref[...])


def _ffn(x1, h2, w_g, w_u, w_d, ln_post, tm=MLP_TILE):
    n_tok, d = x1.shape
    return pl.pallas_call(
        _ffn_kernel,
        grid=(n_tok // tm,),
        in_specs=[pl.BlockSpec((tm, d), lambda i: (i, 0)), pl.BlockSpec((tm, d), lambda i: (i, 0)),
                  _resident(w_g.shape), _resident(w_u.shape), _resident(w_d.shape),
                  _resident((1, d))],
        out_specs=pl.BlockSpec((tm, d), lambda i: (i, 0)),
        out_shape=jax.ShapeDtypeStruct((n_tok, d), F32),
        compiler_params=_params(1),
        name="ffn",
    )(x1, h2, w_g, w_u, w_d, ln_post)


def kernel(x, mem, ln_mix_pre, ln_mix_post, ln_ffn_pre, ln_ffn_post, ln_mem, w_in, ln_v_gain, ln_v_bias,
           w_spatial, b_spatial, rel_bias, w_mem_kv, w_branch_a, w_branch_b, w_branch_c, w_out,
           w_ffn_gate, w_ffn_up, w_ffn_down):
    b, s, d = x.shape
    assert s % MOBA_BLOCK == 0 and s % TOKEN_TILE == 0 and d % V7X_LANES == 0
    depth = w_in.shape[0]
    cuts = [0, GMLP_WIDTH, 2 * GMLP_WIDTH, 2 * GMLP_WIDTH + MOBA_WIDTH, 2 * GMLP_WIDTH + 2 * MOBA_WIDTH,
            2 * GMLP_WIDTH + 3 * MOBA_WIDTH, 2 * GMLP_WIDTH + 3 * MOBA_WIDTH + MEM_WIDTH]
    rb_flat = rel_bias.astype(F32).reshape(-1)
    bias = _bias_tiles(rb_flat)
    far_lanes = jnp.broadcast_to((rel_bias[REL_BUCKETS - 1].astype(F32) * LOG2E)[:, None, None],
                                 (MOBA_HEADS, s // MOBA_BLOCK, V7X_LANES))
    row = lambda v: v.reshape(1, -1).astype(F32)
    for l in range(depth):
        wi = w_in[l]
        w_u, w_v, w_q, w_k, w_v2, w_cq = (wi[:, cuts[i]:cuts[i + 1]] for i in range(6))
        w_g = wi[:, cuts[6]:]
        x2 = x.reshape(b * s, d)

        w_qt_scaled = (w_q * (HEAD_DIM ** -0.5)).T
        h2, h_lo, hmean, qt, k, vt = _qkv(x2, row(ln_mix_pre[l]), b, s, (w_qt_scaled * LOG2E).astype(BF16),
                                          w_k.astype(BF16), w_v2.T.astype(BF16))
        gt = _gatew(hmean.reshape(-1, d), w_k, w_qt_scaled, b)
        wkv = w_mem_kv[l]
        kt_mem, v_mem = _memkv(mem, row(ln_mem[l]), wkv[:, :MEM_WIDTH].T.astype(BF16),
                               wkv[:, MEM_WIDTH:].astype(BF16))
        b_s_lanes = jnp.broadcast_to(b_spatial[l][:, :, None], (GMLP_GROUPS, CHUNK, GMLP_GROUP_DIM))
        a_out, c_out, qadd = _branches(
            h2, h_lo, b, s, w_u.astype(BF16), w_v.astype(BF16), row(ln_v_gain[l]), row(ln_v_bias[l]),
            w_spatial[l], b_s_lanes.astype(F32), w_cq.astype(BF16), kt_mem, v_mem, gt, far_lanes)
        b_out_t = _attn(qt, qadd, k, vt, bias)

        x1, hn = _merge(x2, h2, a_out, b_out_t, c_out, w_g.astype(BF16), w_branch_a[l].astype(BF16),
                        w_branch_b[l].astype(BF16), w_branch_c[l].astype(BF16), w_out[l].astype(BF16),
                        row(ln_mix_post[l]), row(ln_ffn_pre[l]))
        out = _ffn(x1, hn, w_ffn_gate[l].astype(BF16), w_ffn_up[l].astype(BF16),
                   w_ffn_down[l].astype(BF16), row(ln_ffn_post[l]))
        x = out.reshape(b, s, d)
    return x
```

```python
import functools
import math

import jax
import jax.numpy as jnp
from jax import lax
from jax.experimental import pallas as pl
from jax.experimental.pallas import tpu as pltpu

F32 = jnp.float32
BF16 = jnp.bfloat16

EPS = 1e-6
NEG = -1e30
GMLP_GROUPS = 6
GMLP_GROUP_DIM = 128
GMLP_WIDTH = GMLP_GROUPS * GMLP_GROUP_DIM
CHUNK = 128
MOBA_HEADS = 12
HEAD_DIM = 64
MOBA_WIDTH = MOBA_HEADS * HEAD_DIM
MOBA_BLOCK = 256
MOBA_TOPK = 3
REL_BUCKETS = 32
REL_MAX_DIST = 128
LOG2E = math.log2(math.e)
ONES_ROWS = 16
ATTN_HEADS = 4
SCORE_LOOKAHEAD = 3
SCORE_BUFFERS = SCORE_LOOKAHEAD + 1
GATE_BACK = 1
MEM_HEADS = 4
MEM_HEAD_DIM = 128
MEM_WIDTH = MEM_HEADS * MEM_HEAD_DIM
MEM_BATCHES = 4
N_BRANCHES = 3

V7X_LANES = 128
V7X_VMEM_LIMIT = 56 * 1024 * 1024

TOKEN_TILE = 512
MLP_TILE = 1024
MLP_SLAB = 256
QKV_SLAB = 256
GATE_ROWS = 128


def _params(n_axes, vmem=V7X_VMEM_LIMIT):
    return pltpu.CompilerParams(
        dimension_semantics=("arbitrary",) * n_axes, vmem_limit_bytes=vmem)


def _resident(shape):
    zeros = (0,) * len(shape)
    return pl.BlockSpec(shape, lambda *_: zeros, pipeline_mode=pl.Buffered(1))


def _resident_cols(rows, width, index):
    return pl.BlockSpec((rows, width), lambda *_: (0, index), pipeline_mode=pl.Buffered(1))


def _rms(x, g):
    return x * lax.rsqrt(jnp.mean(x * x, axis=-1, keepdims=True) + EPS) * g


def _split_bf16(x):
    hi = x.astype(BF16)
    return hi, (x - hi.astype(F32)).astype(BF16)


def _gatew_kernel(hm_ref, wk_ref, wqt_ref, g_ref, gt_scr, *, n_batch, n_blk):
    kmean = jnp.dot(hm_ref[...], wk_ref[...], precision=lax.Precision.HIGHEST,
                    preferred_element_type=F32)
    used = MOBA_HEADS * n_blk
    gt_scr[:, used:, :] = jnp.zeros((n_batch, GATE_ROWS - used, gt_scr.shape[-1]), F32)
    for h in range(MOBA_HEADS):
        sl = slice(h * HEAD_DIM, (h + 1) * HEAD_DIM)
        res = jnp.dot(kmean[:, sl], wqt_ref[sl, :], precision=lax.Precision.HIGHEST,
                      preferred_element_type=F32)
        for b in range(n_batch):
            gt_scr[b, h * n_blk:(h + 1) * n_blk, :] = res[b * n_blk:(b + 1) * n_blk, :]
    for b in range(n_batch):
        hi, lo = _split_bf16(gt_scr[b].T)
        g_ref[b, :, 0:GATE_ROWS] = hi
        g_ref[b, :, GATE_ROWS:] = lo


def _gatew(hmean, w_k, w_qt, n_batch):
    rows, d = hmean.shape
    n_blk = rows // n_batch
    assert MOBA_HEADS * n_blk <= GATE_ROWS
    return pl.pallas_call(
        functools.partial(_gatew_kernel, n_batch=n_batch, n_blk=n_blk),
        grid=(1,),
        in_specs=[_resident((rows, d)), _resident(w_k.shape), _resident(w_qt.shape)],
        out_specs=pl.BlockSpec((n_batch, d, 2 * GATE_ROWS), lambda i: (0, 0, 0)),
        out_shape=jax.ShapeDtypeStruct((n_batch, d, 2 * GATE_ROWS), BF16),
        scratch_shapes=[pltpu.VMEM((n_batch, GATE_ROWS, d), F32)],
        compiler_params=_params(1),
        name="gatew",
    )(hmean, w_k, w_qt)


def _qkv_kernel(x_ref, g_ref, wqt_ref, wk_ref, wvt_ref, h_ref, hlo_ref, hmean_ref, qt_ref, k_ref, vt_ref,
                *, n_blk):
    nt = (((1,), (1,)), ((), ()))
    n_slabs = x_ref.shape[0] // QKV_SLAB
    lane =lax.broadcasted_iota(jnp.int32, (QKV_SLAB, V7X_LANES), 1)
    low = lane < HEAD_DIM
    for i in range(n_slabs):
        rows = slice(i * QKV_SLAB, (i + 1) * QKV_SLAB)
        blk = pl.program_id(1) * n_slabs + i
        onehot = jnp.where((lane == HEAD_DIM + blk) | (lane == HEAD_DIM + n_blk + blk), 1.0, 0.0)
        hf = _rms(x_ref[rows, :], g_ref[...])
        h, hlo_ref[rows, :] = _split_bf16(hf)
        h_ref[rows, :] = h
        for c in range(QKV_SLAB // MOBA_BLOCK):
            hmean_ref[i * (QKV_SLAB // MOBA_BLOCK) + c] = jnp.mean(
                hf[c * MOBA_BLOCK:(c + 1) * MOBA_BLOCK], axis=0, keepdims=True)
        qt_ref[0, :, rows] = lax.dot_general(wqt_ref[...], h, nt, preferred_element_type=F32).astype(BF16)
        vt_ref[0, :, rows] = lax.dot_general(wvt_ref[...], h, nt, preferred_element_type=F32).astype(BF16)
        kf = jnp.dot(h, wk_ref[...], preferred_element_type=F32)
        for p in range(MOBA_HEADS // 2):
            pair = kf[:, p * V7X_LANES:(p + 1) * V7X_LANES]
            even = jnp.where(low, pair, onehot)
            odd = jnp.where(low, pltpu.roll(pair, HEAD_DIM, 1), onehot)
            k_ref[0, rows, (2 * p) * V7X_LANES:(2 * p + 1) * V7X_LANES] = even.astype(BF16)
            k_ref[0, rows, (2 * p + 1) * V7X_LANES:(2 * p + 2) * V7X_LANES] = odd.astype(BF16)


def _qkv(x2, g, b, s, w_qt, w_in, k_col, w_vt, tm=2 * TOKEN_TILE):
    d = x2.shape[-1]
    assert k_col % MOBA_WIDTH == 0
    nt = s // tm
    blocks = tm // MOBA_BLOCK
    n_blk = s // MOBA_BLOCK
    assert QKV_SLAB == MOBA_BLOCK and HEAD_DIM + 2 * n_blk <= V7X_LANES
    tok = lambda i, j: (i * nt + j, 0)
    return pl.pallas_call(
        functools.partial(_qkv_kernel, n_blk=n_blk),
        grid=(b, nt),
        in_specs=[pl.BlockSpec((tm, d), tok), _resident((1, d)),
                  _resident(w_qt.shape), _resident_cols(d, MOBA_WIDTH, k_col // MOBA_WIDTH),
                  _resident(w_vt.shape)],
        out_specs=[pl.BlockSpec((tm, d), tok), pl.BlockSpec((tm, d), tok),
                   pl.BlockSpec((blocks, 1, d), lambda i, j: (i * nt + j, 0, 0)),
                   pl.BlockSpec((1, MOBA_WIDTH, tm), lambda i, j: (i, 0, j)),
                   pl.BlockSpec((1, tm, MOBA_HEADS * V7X_LANES), lambda i, j: (i, j, 0)),
                   pl.BlockSpec((1, MOBA_WIDTH, tm), lambda i, j: (i, 0, j))],
        out_shape=[jax.ShapeDtypeStruct((b * s, d), BF16), jax.ShapeDtypeStruct((b * s, d), BF16),
                   jax.ShapeDtypeStruct((b * s // MOBA_BLOCK, 1, d), F32),
                   jax.ShapeDtypeStruct((b, MOBA_WIDTH, s), BF16),
                   jax.ShapeDtypeStruct((b, s, MOBA_HEADS * V7X_LANES), BF16),
                   jax.ShapeDtypeStruct((b, MOBA_WIDTH, s), BF16)],
        compiler_params=_params(2),
        name="qkv",
    )(x2, g, w_qt, w_in, w_vt)


def _bias_kernel(rb_ref, out_ref):
    h = pl.program_id(0)
    ik = lax.broadcasted_iota(jnp.int32, (MOBA_BLOCK, MOBA_BLOCK), 0)
    iq = lax.broadcasted_iota(jnp.int32, (MOBA_BLOCK, MOBA_BLOCK), 1)
    max_exact = REL_BUCKETS // 2
    for t in range(2):
        dist = iq - ik + t * MOBA_BLOCK
        n = jnp.maximum(dist, 0)
        nf = jnp.maximum(n, 1).astype(F32)
        large = max_exact + (jnp.log(nf / max_exact) / math.log(REL_MAX_DIST / max_exact)
                             * (REL_BUCKETS - max_exact)).astype(jnp.int32)
        large = jnp.minimum(large, REL_BUCKETS - 1)
        bucket = jnp.where(n < max_exact, n, large)
        val = jnp.zeros((MOBA_BLOCK, MOBA_BLOCK), F32)
        for bk in range(REL_BUCKETS):
            val = jnp.where(bucket == bk, rb_ref[bk * MOBA_HEADS + h], val)
        val = val * LOG2E
        if t == 0:
            val = jnp.where(dist >= 0, val, NEG)
        out_ref[0, t] = val


def _bias_tiles(rb_flat):
    return pl.pallas_call(
        _bias_kernel,
        grid=(MOBA_HEADS,),
        in_specs=[pl.BlockSpec(memory_space=pltpu.SMEM)],
        out_specs=pl.BlockSpec((1, 2, MOBA_BLOCK, MOBA_BLOCK), lambda h: (h, 0, 0, 0)),
        out_shape=jax.ShapeDtypeStruct((MOBA_HEADS, 2, MOBA_BLOCK, MOBA_BLOCK), F32),
        compiler_params=_params(1),
        name="bias",
    )(rb_flat)


def _zero_after(x):
    bits = lax.bitcast_convert_type(x, jnp.uint32)
    sixteen = jnp.uint32(16)
    return lax.bitcast_convert_type(
        lax.shift_right_logical(lax.shift_right_logical(bits, sixteen), sixteen), F32)


def _attn_kernel(qt_ref, qadd_ref, k_ref, vt_ref, bias_ref, o_ref, s_scr, *, n_blk):
    pad = jnp.zeros((V7X_LANES - HEAD_DIM - 2 * n_blk, MOBA_BLOCK), BF16)

    def scores(hh, j, buf):
        cols = slice(j * MOBA_BLOCK, (j + 1) * MOBA_BLOCK)
        q = qt_ref[0, hh * HEAD_DIM:(hh + 1) * HEAD_DIM, cols]
        qz = jnp.concatenate([q, qadd_ref[0, hh, :, cols], pad], axis=0)
        k = k_ref[0, 0:(j + 1) * MOBA_BLOCK, hh * V7X_LANES:(hh + 1) * V7X_LANES]
        s = jnp.dot(k, qz, preferred_element_type=F32)
        gates = []
        for n in range(j + 1):
            rows = slice(n * MOBA_BLOCK, (n + 1) * MOBA_BLOCK)
            if n == j:
                sn = s[rows] + bias_ref[hh, 0]
            elif n == j - 1:
                sn = s[rows] + bias_ref[hh, 1]
            else:
                sn = s[rows]
            s_scr[buf, rows, :] = sn
            gates.append(_zero_after(sn[0:1, :]))
        return gates

    def finish(hh, j, buf, gates):
        cols = slice(j * MOBA_BLOCK, (j + 1) * MOBA_BLOCK)
        heads = slice(hh * HEAD_DIM, (hh + 1) * HEAD_DIM)
        m = jnp.max(s_scr[buf, 0:(j + 1) * MOBA_BLOCK, :], axis=0, keepdims=True)
        ones = jnp.ones((ONES_ROWS, MOBA_BLOCK), BF16)
        acc = None
        for n in range(j + 1):
            rows = slice(n * MOBA_BLOCK, (n + 1) * MOBA_BLOCK)
            mn = m + gates[min(n, len(gates) - 1)] if gates else m
            p = jnp.exp2(s_scr[buf, rows, :] - mn).astype(BF16)
            part = jnp.dot(jnp.concatenate([vt_ref[0, heads, rows], ones], axis=0), p,
                           preferred_element_type=F32)
            acc = part if acc is None else acc + part
        o_ref[0, heads, cols] = (acc[0:HEAD_DIM] / acc[HEAD_DIM:HEAD_DIM + 1]).astype(BF16)

    tasks = [(hh, j) for j in reversed(range(n_blk)) for hh in range(ATTN_HEADS)]
    pending, issued = [], []
    for i, (hh, j) in enumerate(tasks):
        buf = i % SCORE_BUFFERS
        issued.append(scores(hh, j, buf))
        pending.append((hh, j, buf))
        if len(pending) > SCORE_LOOKAHEAD:
            finish(*pending.pop(0), issued[i - GATE_BACK])
    for item in pending:
        finish(*item, None)


def _attn(qt, qadd, k, vt, bias):
    b, _, s = qt.shape
    n_blk = s // MOBA_BLOCK
    nh = ATTN_HEADS
    return pl.pallas_call(
        functools.partial(_attn_kernel, n_blk=n_blk),
        grid=(b, MOBA_HEADS // nh),
        in_specs=[pl.BlockSpec((1, nh * HEAD_DIM, s), lambda i, h: (i, h, 0)),
                  pl.BlockSpec((1, nh, 2 * n_blk, s), lambda i, h: (i, h, 0, 0)),
                  pl.BlockSpec((1, s, nh * V7X_LANES), lambda i, h: (i, 0, h)),
                  pl.BlockSpec((1, nh * HEAD_DIM, s), lambda i, h: (i, h, 0)),
                  pl.BlockSpec((nh, 2, MOBA_BLOCK, MOBA_BLOCK), lambda i, h: (h, 0, 0, 0))],
        out_specs=pl.BlockSpec((1, nh * HEAD_DIM, s), lambda i, h: (i, h, 0)),
        out_shape=jax.ShapeDtypeStruct((b, MOBA_WIDTH, s), BF16),
        scratch_shapes=[pltpu.VMEM((SCORE_BUFFERS, s, MOBA_BLOCK), F32)],
        compiler_params=_params(2),
        name="attn",
    )(qt, qadd, k, vt, bias)


def _branches_kernel(h_ref, hlo_ref, wu_ref, wv_ref, lng_ref, lnb_ref, ws_ref, bs_ref, wq_ref, kt_ref, vm_ref,
                     gw_ref, far_ref, a_ref, c_ref, qadd_ref, *, n_blk):
    causal = (lax.broadcasted_iota(jnp.int32, (CHUNK, CHUNK), 0)
              >= lax.broadcasted_iota(jnp.int32, (CHUNK, CHUNK), 1))
    ws = [jnp.where(causal, ws_ref[g], 0.0).astype(BF16) for g in range(GMLP_GROUPS)]
    ones = jnp.ones((vm_ref.shape[1], MEM_HEAD_DIM), BF16)
    far = jnp.concatenate([far_ref[...]] * (MOBA_BLOCK // V7X_LANES), axis=-1)
    n_slabs = h_ref.shape[0] // MOBA_BLOCK

    def project(i):
        rows = slice(i * MOBA_BLOCK, (i + 1) * MOBA_BLOCK)
        h = h_ref[rows, :]
        cq = jnp.dot(h, wq_ref[...], preferred_element_type=F32).astype(BF16)
        v = jnp.dot(h, wv_ref[...], preferred_element_type=F32)
        u = jnp.dot(h, wu_ref[...], preferred_element_type=F32)
        res = jnp.dot(jnp.concatenate([h, hlo_ref[rows, :]], axis=0), gw_ref[0], preferred_element_type=F32)
        ps = []
        for hd in range(MEM_HEADS):
            sl = slice(hd * MEM_HEAD_DIM, (hd + 1) * MEM_HEAD_DIM)
            s = jnp.dot(cq[:, sl], kt_ref[0, sl, :], preferred_element_type=F32) * (MEM_HEAD_DIM ** -0.5 * LOG2E)
            ps.append(jnp.exp2(s - jnp.max(s, axis=-1, keepdims=True)).astype(BF16))
        return rows, v, u, ps, res

    def finish(i, rows, v, u, ps, res):
        v = jax.nn.gelu(v)
        vc = v - jnp.mean(v, axis=-1, keepdims=True)
        vn = vc * lax.rsqrt(jnp.mean(vc * vc, axis=-1, keepdims=True) + EPS) * lng_ref[...] + lnb_ref[...]
        vn = vn.astype(BF16)
        u = jax.nn.gelu(u)
        chunks = [slice(c * CHUNK, (c + 1) * CHUNK) for c in range(MOBA_BLOCK // CHUNK)]
        for g in range(GMLP_GROUPS):
            lanes = slice(g * GMLP_GROUP_DIM, (g + 1) * GMLP_GROUP_DIM)
            mixed = jnp.dot(ws[g], jnp.concatenate([vn[toks, lanes] for toks in chunks], axis=1),
                            preferred_element_type=F32)
            for c, toks in enumerate(chunks):
                mc = mixed[:, c * GMLP_GROUP_DIM:(c + 1) * GMLP_GROUP_DIM] + bs_ref[g]
                a_ref[rows.start + c * CHUNK:rows.start + (c + 1) * CHUNK, lanes] = (
                    u[toks, lanes] * mc).astype(BF16)
        for hd, p in enumerate(ps):
            sl = slice(hd * MEM_HEAD_DIM, (hd + 1) * MEM_HEAD_DIM)
            acc = jnp.dot(p, jnp.concatenate([vm_ref[0, :, sl], ones], axis=1), preferred_element_type=F32)
            c_ref[rows, sl] = (acc[:, :MEM_HEAD_DIM] / acc[:, MEM_HEAD_DIM:]).astype(BF16)
        cur = pl.program_id(1) * n_slabs + i
        hi, lo = res[:MOBA_BLOCK], res[MOBA_BLOCK:]
        gate = (hi[:, :GATE_ROWS] + (hi[:, GATE_ROWS:] + lo[:, :GATE_ROWS])) + lo[:, GATE_ROWS:]
        gate = gate.T[0:MOBA_HEADS * n_blk].reshape(MOBA_HEADS, n_blk, MOBA_BLOCK)
        blk = lax.broadcasted_iota(jnp.int32, gate.shape, 1)
        past = blk < cur
        gate = jnp.where(past, gate, NEG)
        rank = jnp.zeros(gate.shape, jnp.int32)
        for m in range(n_blk):
            gm = gate[:, m:m + 1, :]
            tie = jnp.where(blk > m, 1, 0)
            rank = rank + jnp.where(gm > gate, 1, jnp.where(gm == gate, tie, 0))
        mask = jnp.where(rank < min(MOBA_TOPK, n_blk), jnp.where(past, 0.0, NEG), NEG)
        add = jnp.where(blk < cur - 1, mask + far, jnp.where(blk == cur - 1, mask, 0.0))
        add_hi, add_lo = _split_bf16(add)
        qadd_ref[0, :, :, rows] = jnp.concatenate(
            [add_hi.astype(F32), add_lo.astype(F32)], axis=1).astype(BF16)

    staged = project(0)
    for i in range(n_slabs):
        nxt = project(i + 1) if i + 1 < n_slabs else None
        finish(i, *staged)
        staged = nxt


def _branches(h_hi, h_lo, b, s, w_in, u_col, v_col, ln_g, ln_b, w_s, b_s_lanes, w_cq, kt, vm, gt, far_lanes,
              tm=2 * TOKEN_TILE):
    d = h_hi.shape[-1]
    assert u_col % GMLP_WIDTH == 0 and v_col % GMLP_WIDTH == 0
    n_blk = s // MOBA_BLOCK
    nt = s // tm
    m = kt.shape[-1]
    tok = lambda i, j: (i * nt + j, 0)
    return pl.pallas_call(
        functools.partial(_branches_kernel, n_blk=n_blk),
        grid=(b, nt),
        in_specs=[pl.BlockSpec((tm, d), tok), pl.BlockSpec((tm, d), tok),
                  _resident_cols(d, GMLP_WIDTH, u_col // GMLP_WIDTH),
                  _resident_cols(d, GMLP_WIDTH, v_col // GMLP_WIDTH),
                  _resident(ln_g.shape), _resident(ln_b.shape),
                  _resident(w_s.shape), _resident(b_s_lanes.shape), _resident(w_cq.shape),
                  pl.BlockSpec((1, MEM_WIDTH, m), lambda i, j: (i, 0, 0)),
                  pl.BlockSpec((1, m, MEM_WIDTH), lambda i, j: (i, 0, 0)),
                  pl.BlockSpec((1, d, 2 * GATE_ROWS), lambda i, j: (i, 0, 0)),
                  _resident(far_lanes.shape)],
        out_specs=[pl.BlockSpec((tm, GMLP_WIDTH), tok), pl.BlockSpec((tm, MEM_WIDTH), tok),
                   pl.BlockSpec((1, MOBA_HEADS, 2 * n_blk, tm), lambda i, j: (i, 0, 0, j))],
        out_shape=[jax.ShapeDtypeStruct((b * s, GMLP_WIDTH), BF16),
                   jax.ShapeDtypeStruct((b * s, MEM_WIDTH), BF16),
                   jax.ShapeDtypeStruct((b, MOBA_HEADS, 2 * n_blk, s), BF16)],
        compiler_params=_params(2),
        name="branches",
    )(h_hi, h_lo, w_in, w_in, ln_g, ln_b, w_s, b_s_lanes, w_cq, kt, vm, gt, far_lanes)


def _memkv_kernel(mem_ref, g_ref, wkt_ref, wv_ref, kt_ref, v_ref):
    for i in range(mem_ref.shape[0]):
        mn = _rms(mem_ref[i], g_ref[...]).astype(BF16)
        kt_ref[i] = lax.dot_general(wkt_ref[...], mn, (((1,), (1,)), ((), ())),
                                    preferred_element_type=F32).astype(BF16)
        v_ref[i] = jnp.dot(mn, wv_ref[...], preferred_element_type=F32).astype(BF16)


def _memkv(mem, g, w_kt, w_v):
    b, m, d = mem.shape
    nb = math.gcd(b, MEM_BATCHES)
    return pl.pallas_call(
        _memkv_kernel,
        grid=(b // nb,),
        in_specs=[pl.BlockSpec((nb, m, d), lambda i: (i, 0, 0)), _resident((1, d)),
                  _resident(w_kt.shape), _resident(w_v.shape)],
        out_specs=[pl.BlockSpec((nb, MEM_WIDTH, m), lambda i: (i, 0, 0)),
                   pl.BlockSpec((nb, m, MEM_WIDTH), lambda i: (i, 0, 0))],
        out_shape=[jax.ShapeDtypeStruct((b, MEM_WIDTH, m), BF16),
                   jax.ShapeDtypeStruct((b, m, MEM_WIDTH), BF16)],
        compiler_params=_params(1),
        name="memkv",
    )(mem, g, w_kt, w_v)


def _merge_kernel(x_ref, h_ref, a_ref, bt_ref, c_ref, wg_ref, wa_ref, wb_ref, wc_ref, wo_ref,
                  lnpost_ref, lnpre_ref, x1_ref, h2_ref):
    d = x_ref.shape[-1]
    for i in range(x_ref.shape[0] // MLP_SLAB):
        rows = slice(i * MLP_SLAB, (i + 1) * MLP_SLAB)
        gates = jax.nn.sigmoid(jnp.dot(h_ref[rows, :], wg_ref[...], preferred_element_type=F32))
        pa = jnp.dot(a_ref[rows, :], wa_ref[...], preferred_element_type=F32)
        pb = lax.dot_general(bt_ref[0, :, rows], wb_ref[...], (((0,), (0,)), ((), ())),
                             preferred_element_type=F32)
        pc = jnp.dot(c_ref[rows, :], wc_ref[...], preferred_element_type=F32)
        merged = gates[:, :d] * pa + gates[:, d:2 * d] * pb + gates[:, 2 * d:] * pc
        mo = jnp.dot(merged.astype(BF16), wo_ref[...], preferred_element_type=F32)
        x1 = x_ref[rows, :] + _rms(mo, lnpost_ref[...])
        x1_ref[rows, :] = x1
        h2_ref[rows, :] = _rms(x1, lnpre_ref[...]).astype(BF16)


def _merge(x2, h2, a_out, b_out_t, c_out, w_g, w_a, w_b, w_c, w_o, ln_post, ln_pre, tm=MLP_TILE):
    n_tok, d = x2.shape
    b, _, s = b_out_t.shape
    nt = s // tm
    tok = lambda i, j: (i * nt + j, 0)
    return pl.pallas_call(
        _merge_kernel,
        grid=(b, nt),
        in_specs=[pl.BlockSpec((tm, d), tok), pl.BlockSpec((tm, d), tok),
                  pl.BlockSpec((tm, GMLP_WIDTH), tok),
                  pl.BlockSpec((1, MOBA_WIDTH, tm), lambda i, j: (i, 0, j)),
                  pl.BlockSpec((tm, MEM_WIDTH), tok),
                  _resident(w_g.shape), _resident(w_a.shape), _resident(w_b.shape),
                  _resident(w_c.shape), _resident(w_o.shape),
                  _resident((1, d)), _resident((1, d))],
        out_specs=[pl.BlockSpec((tm, d), tok), pl.BlockSpec((tm, d), tok)],
        out_shape=[jax.ShapeDtypeStruct((n_tok, d), F32), jax.ShapeDtypeStruct((n_tok, d), BF16)],
        compiler_params=_params(2),
        name="merge",
    )(x2, h2, a_out, b_out_t, c_out, w_g, w_a, w_b, w_c, w_o, ln_post, ln_pre)


def _ffn_kernel(x1_ref, h2_ref, wg_ref, wu_ref, wd_ref, ln_ref, o_ref):
    for i in range(x1_ref.shape[0] // MLP_SLAB):
        rows = slice(i * MLP_SLAB, (i + 1) * MLP_SLAB)
        h2 = h2_ref[rows, :]
        g = jnp.dot(h2, wg_ref[...], preferred_element_type=F32)
        u = jnp.dot(h2, wu_ref[...], preferred_element_type=F32)
        act = (jax.nn.silu(g) * u).astype(BF16)
        f = jnp.dot(act, wd_ref[...], preferred_element_type=F32)
        o_ref[rows, :] = x1_ref[rows, :] + _rms(f, ln_ref[...])


def _ffn(x1, h2, w_g, w_u, w_d, ln_post, tm=MLP_TILE):
    n_tok, d = x1.shape
    return pl.pallas_call(
        _ffn_kernel,
        grid=(n_tok // tm,),
        in_specs=[pl.BlockSpec((tm, d), lambda i: (i, 0)), pl.BlockSpec((tm, d), lambda i: (i, 0)),
                  _resident(w_g.shape), _resident(w_u.shape), _resident(w_d.shape),
                  _resident((1, d))],
        out_specs=pl.BlockSpec((tm, d), lambda i: (i, 0)),
        out_shape=jax.ShapeDtypeStruct((n_tok, d), F32),
        compiler_params=_params(1),
        name="ffn",
    )(x1, h2, w_g, w_u, w_d, ln_post)


def kernel(x, mem, ln_mix_pre, ln_mix_post, ln_ffn_pre, ln_ffn_post, ln_mem, w_in, ln_v_gain, ln_v_bias,
           w_spatial, b_spatial, rel_bias, w_mem_kv, w_branch_a, w_branch_b, w_branch_c, w_out,
           w_ffn_gate, w_ffn_up, w_ffn_down):
    b, s, d = x.shape
    assert s % MOBA_BLOCK == 0 and s % TOKEN_TILE == 0 and d % V7X_LANES == 0
    depth = w_in.shape[0]
    cuts = [0, GMLP_WIDTH, 2 * GMLP_WIDTH, 2 * GMLP_WIDTH + MOBA_WIDTH, 2 * GMLP_WIDTH + 2 * MOBA_WIDTH,
            2 * GMLP_WIDTH + 3 * MOBA_WIDTH, 2 * GMLP_WIDTH + 3 * MOBA_WIDTH + MEM_WIDTH]
    rb_flat = rel_bias.astype(F32).reshape(-1)
    bias = _bias_tiles(rb_flat)
    far_lanes = jnp.broadcast_to((rel_bias[REL_BUCKETS - 1].astype(F32) * LOG2E)[:, None, None],
                                 (MOBA_HEADS, s // MOBA_BLOCK, V7X_LANES))
    row = lambda v: v.reshape(1, -1).astype(F32)
    for l in range(depth):
        wi = w_in[l]
        wi16 = wi.astype(BF16)
        w_q, w_k, w_v2 = (wi[:, cuts[i]:cuts[i + 1]] for i in (2, 3, 4))
        w_cq, w_g = wi16[:, cuts[5]:cuts[6]], wi16[:, cuts[6]:]
        x2 = x.reshape(b * s, d)

        w_qt_scaled = (w_q * (HEAD_DIM ** -0.5)).T
        h2, h_lo, hmean, qt, k, vt = _qkv(x2, row(ln_mix_pre[l]), b, s, (w_qt_scaled * LOG2E).astype(BF16),
                                          wi16, cuts[3], w_v2.T.astype(BF16))
        gt = _gatew(hmean.reshape(-1, d), w_k, w_qt_scaled, b)
        wkv = w_mem_kv[l]
        kt_mem, v_mem = _memkv(mem, row(ln_mem[l]), wkv[:, :MEM_WIDTH].T.astype(BF16),
                               wkv[:, MEM_WIDTH:].astype(BF16))
        b_s_lanes = jnp.broadcast_to(b_spatial[l][:, :, None], (GMLP_GROUPS, CHUNK, GMLP_GROUP_DIM))
        a_out, c_out, qadd = _branches(
            h2, h_lo, b, s, wi16, cuts[0], cuts[1], row(ln_v_gain[l]), row(ln_v_bias[l]),
            w_spatial[l], b_s_lanes.astype(F32), w_cq, kt_mem, v_mem, gt, far_lanes)
        b_out_t = _attn(qt, qadd, k, vt, bias)

        x1, hn = _merge(x2, h2, a_out, b_out_t, c_out, w_g, w_branch_a[l].astype(BF16),
                        w_branch_b[l].astype(BF16), w_branch_c[l].astype(BF16), w_out[l].astype(BF16),
                        row(ln_mix_post[l]), row(ln_ffn_pre[l]))
        out = _ffn(x1, hn, w_ffn_gate[l].astype(BF16), w_ffn_up[l].astype(BF16),
                   w_ffn_down[l].astype(BF16), row(ln_ffn_post[l]))
        x = out.reshape(b, s, d)
    return x
```

```python
import functools
import math

import jax
import jax.numpy as jnp
from jax import lax
from jax.experimental import pallas as pl
from jax.experimental.pallas import tpu as pltpu

F32 = jnp.float32
BF16 = jnp.bfloat16

EPS = 1e-6
NEG = -1e30
GMLP_GROUPS = 6
GMLP_GROUP_DIM = 128
GMLP_WIDTH = GMLP_GROUPS * GMLP_GROUP_DIM
CHUNK = 128
MOBA_HEADS = 12
HEAD_DIM = 64
MOBA_WIDTH = MOBA_HEADS * HEAD_DIM
MOBA_BLOCK = 256
MOBA_TOPK = 3
REL_BUCKETS = 32
REL_MAX_DIST = 128
LOG2E = math.log2(math.e)
ONES_ROWS = 16
ATTN_HEADS = 4
SCORE_LOOKAHEAD = 3
SCORE_BUFFERS = SCORE_LOOKAHEAD + 1
GATE_BACK = 1
MEM_HEADS = 4
MEM_HEAD_DIM = 128
MEM_WIDTH = MEM_HEADS * MEM_HEAD_DIM
MEM_BATCHES = 4
N_BRANCHES = 3

V7X_LANES = 128
V7X_VMEM_LIMIT = 56 * 1024 * 1024

TOKEN_TILE = 512
MLP_TILE = 1024
MLP_SLAB = 256
QKV_SLAB = 256
GATE_ROWS = 128


def _params(n_axes, vmem=V7X_VMEM_LIMIT):
    return pltpu.CompilerParams(
        dimension_semantics=("arbitrary",) * n_axes, vmem_limit_bytes=vmem)


def _resident(shape):
    zeros = (0,) * len(shape)
    return pl.BlockSpec(shape, lambda *_: zeros, pipeline_mode=pl.Buffered(1))


def _resident_cols(rows, width, index):
    return pl.BlockSpec((rows, width), lambda *_: (0, index), pipeline_mode=pl.Buffered(1))


def _rms(x, g):
    return x * lax.rsqrt(jnp.mean(x * x, axis=-1, keepdims=True) + EPS) * g


def _split_bf16(x):
    hi = x.astype(BF16)
    return hi, (x - hi.astype(F32)).astype(BF16)


def _gatew_kernel(hm_ref, wk_ref, wqt_ref, g_ref, gt_scr, *, n_batch, n_blk):
    kmean = jnp.dot(hm_ref[...], wk_ref[...], precision=lax.Precision.HIGHEST,
                    preferred_element_type=F32)
    used = MOBA_HEADS * n_blk
    gt_scr[:, used:, :] = jnp.zeros((n_batch, GATE_ROWS - used, gt_scr.shape[-1]), F32)
    for h in range(MOBA_HEADS):
        sl = slice(h * HEAD_DIM, (h + 1) * HEAD_DIM)
        res = jnp.dot(kmean[:, sl], wqt_ref[sl, :], precision=lax.Precision.HIGHEST,
                      preferred_element_type=F32)
        for b in range(n_batch):
            gt_scr[b, h * n_blk:(h + 1) * n_blk, :] = res[b * n_blk:(b + 1) * n_blk, :]
    for b in range(n_batch):
        hi, lo = _split_bf16(gt_scr[b].T)
        g_ref[b, :, 0:GATE_ROWS] = hi
        g_ref[b, :, GATE_ROWS:] = lo


def _gatew(hmean, w_k, w_qt, n_batch):
    rows, d = hmean.shape
    n_blk = rows // n_batch
    assert MOBA_HEADS * n_blk <= GATE_ROWS
    return pl.pallas_call(
        functools.partial(_gatew_kernel, n_batch=n_batch, n_blk=n_blk),
        grid=(1,),
        in_specs=[_resident((rows, d)), _resident(w_k.shape), _resident(w_qt.shape)],
        out_specs=pl.BlockSpec((n_batch, d, 2 * GATE_ROWS), lambda i: (0, 0, 0)),
        out_shape=jax.ShapeDtypeStruct((n_batch, d, 2 * GATE_ROWS), BF16),
        scratch_shapes=[pltpu.VMEM((n_batch, GATE_ROWS, d), F32)],
        compiler_params=_params(1),
        name="gatew",
    )(hmean, w_k, w_qt)


def _qkv_kernel(x_ref, g_ref, wqt_ref, wk_ref, wvt_ref, h_ref, hlo_ref, hmean_ref, qt_ref, k_ref, vt_ref,
                *, n_blk):
    nt = (((1,), (1,)), ((), ()))
    n_slabs = x_ref.shape[0] // QKV_SLAB
    lane =lax.broadcasted_iota(jnp.int32, (QKV_SLAB, V7X_LANES), 1)
    low = lane < HEAD_DIM
    for i in range(n_slabs):
        rows = slice(i * QKV_SLAB, (i + 1) * QKV_SLAB)
        blk = pl.program_id(1) * n_slabs + i
        onehot = jnp.where((lane == HEAD_DIM + blk) | (lane == HEAD_DIM + n_blk + blk), 1.0, 0.0)
        hf = _rms(x_ref[rows, :], g_ref[...])
        h, hlo_ref[rows, :] = _split_bf16(hf)
        h_ref[rows, :] = h
        for c in range(QKV_SLAB // MOBA_BLOCK):
            hmean_ref[i * (QKV_SLAB // MOBA_BLOCK) + c] = jnp.mean(
                hf[c * MOBA_BLOCK:(c + 1) * MOBA_BLOCK], axis=0, keepdims=True)
        qt_ref[0, :, rows] = lax.dot_general(wqt_ref[...], h, nt, preferred_element_type=F32).astype(BF16)
        vt_ref[0, :, rows] = lax.dot_general(wvt_ref[...], h, nt, preferred_element_type=F32).astype(BF16)
        kf = jnp.dot(h, wk_ref[...], preferred_element_type=F32)
        for p in range(MOBA_HEADS // 2):
            pair = kf[:, p * V7X_LANES:(p + 1) * V7X_LANES]
            even = jnp.where(low, pair, onehot)
            odd = jnp.where(low, pltpu.roll(pair, HEAD_DIM, 1), onehot)
            k_ref[0, rows, (2 * p) * V7X_LANES:(2 * p + 1) * V7X_LANES] = even.astype(BF16)
            k_ref[0, rows, (2 * p + 1) * V7X_LANES:(2 * p + 2) * V7X_LANES] = odd.astype(BF16)


def _qkv(x2, g, b, s, w_qt, w_in, k_col, w_vt, tm=2 * TOKEN_TILE):
    d = x2.shape[-1]
    assert k_col % MOBA_WIDTH == 0
    nt = s // tm
    blocks = tm // MOBA_BLOCK
    n_blk = s // MOBA_BLOCK
    assert QKV_SLAB == MOBA_BLOCK and HEAD_DIM + 2 * n_blk <= V7X_LANES
    tok = lambda i, j: (i * nt + j, 0)
    return pl.pallas_call(
        functools.partial(_qkv_kernel, n_blk=n_blk),
        grid=(b, nt),
        in_specs=[pl.BlockSpec((tm, d), tok), _resident((1, d)),
                  _resident(w_qt.shape), _resident_cols(d, MOBA_WIDTH, k_col // MOBA_WIDTH),
                  _resident(w_vt.shape)],
        out_specs=[pl.BlockSpec((tm, d), tok), pl.BlockSpec((tm, d), tok),
                   pl.BlockSpec((blocks, 1, d), lambda i, j: (i * nt + j, 0, 0)),
                   pl.BlockSpec((1, MOBA_WIDTH, tm), lambda i, j: (i, 0, j)),
                   pl.BlockSpec((1, tm, MOBA_HEADS * V7X_LANES), lambda i, j: (i, j, 0)),
                   pl.BlockSpec((1, MOBA_WIDTH, tm), lambda i, j: (i, 0, j))],
        out_shape=[jax.ShapeDtypeStruct((b * s, d), BF16), jax.ShapeDtypeStruct((b * s, d), BF16),
                   jax.ShapeDtypeStruct((b * s // MOBA_BLOCK, 1, d), F32),
                   jax.ShapeDtypeStruct((b, MOBA_WIDTH, s), BF16),
                   jax.ShapeDtypeStruct((b, s, MOBA_HEADS * V7X_LANES), BF16),
                   jax.ShapeDtypeStruct((b, MOBA_WIDTH, s), BF16)],
        compiler_params=_params(2),
        name="qkv",
    )(x2, g, w_qt, w_in, w_vt)


def _bias_kernel(rb_ref, out_ref):
    h = pl.program_id(0)
    span = 2 * MOBA_BLOCK
    u = lax.broadcasted_iota(jnp.int32, (8, span), 1)
    max_exact = REL_BUCKETS // 2
    for t in range(2):
        dist = u if t == 0 else jnp.where(u < MOBA_BLOCK, u + MOBA_BLOCK, u - MOBA_BLOCK)
        n = jnp.maximum(dist, 0)
        nf = jnp.maximum(n, 1).astype(F32)
        large = max_exact + (jnp.log(nf / max_exact) / math.log(REL_MAX_DIST / max_exact)
                             * (REL_BUCKETS - max_exact)).astype(jnp.int32)
        large = jnp.minimum(large, REL_BUCKETS - 1)
        bucket = jnp.where(n < max_exact, n, large)
        val = jnp.zeros((8, span), F32)
        for bk in range(REL_BUCKETS):
            val = jnp.where(bucket == bk, rb_ref[bk * MOBA_HEADS + h], val)
        val = val * LOG2E
        if t == 0:
            val = jnp.where(u < MOBA_BLOCK, val, NEG)
        table = jnp.concatenate([val] * (MOBA_BLOCK // 8), axis=0)
        tile = pltpu.roll(table, 0, 1, stride=1, stride_axis=0)
        out_ref[0, t] = tile[:, 0:MOBA_BLOCK]


def _bias_tiles(rb_flat):
    return pl.pallas_call(
        _bias_kernel,
        grid=(MOBA_HEADS,),
        in_specs=[pl.BlockSpec(memory_space=pltpu.SMEM)],
        out_specs=pl.BlockSpec((1, 2, MOBA_BLOCK, MOBA_BLOCK), lambda h: (h, 0, 0, 0)),
        out_shape=jax.ShapeDtypeStruct((MOBA_HEADS, 2, MOBA_BLOCK, MOBA_BLOCK), F32),
        compiler_params=_params(1),
        name="bias",
    )(rb_flat)


def _zero_after(x):
    bits = lax.bitcast_convert_type(x, jnp.uint32)
    sixteen = jnp.uint32(16)
    return lax.bitcast_convert_type(
        lax.shift_right_logical(lax.shift_right_logical(bits, sixteen), sixteen), F32)


def _attn_kernel(qt_ref, qadd_ref, k_ref, vt_ref, bias_ref, o_ref, s_scr, *, n_blk):
    pad = jnp.zeros((V7X_LANES - HEAD_DIM - 2 * n_blk, MOBA_BLOCK), BF16)

    def scores(hh, j, buf):
        cols = slice(j * MOBA_BLOCK, (j + 1) * MOBA_BLOCK)
        q = qt_ref[0, hh * HEAD_DIM:(hh + 1) * HEAD_DIM, cols]
        qz = jnp.concatenate([q, qadd_ref[0, hh, :, cols], pad], axis=0)
        k = k_ref[0, 0:(j + 1) * MOBA_BLOCK, hh * V7X_LANES:(hh + 1) * V7X_LANES]
        s = jnp.dot(k, qz, preferred_element_type=F32)
        gates = []
        for n in range(j + 1):
            rows = slice(n * MOBA_BLOCK, (n + 1) * MOBA_BLOCK)
            if n == j:
                sn = s[rows] + bias_ref[hh, 0]
            elif n == j - 1:
                sn = s[rows] + bias_ref[hh, 1]
            else:
                sn = s[rows]
            s_scr[buf, rows, :] = sn
            gates.append(_zero_after(sn[0:1, :]))
        return gates

    def finish(hh, j, buf, gates):
        cols = slice(j * MOBA_BLOCK, (j + 1) * MOBA_BLOCK)
        heads = slice(hh * HEAD_DIM, (hh + 1) * HEAD_DIM)
        m = jnp.max(s_scr[buf, 0:(j + 1) * MOBA_BLOCK, :], axis=0, keepdims=True)
        ones = jnp.ones((ONES_ROWS, MOBA_BLOCK), BF16)
        acc = None
        for n in range(j + 1):
            rows = slice(n * MOBA_BLOCK, (n + 1) * MOBA_BLOCK)
            mn = m + gates[min(n, len(gates) - 1)] if gates else m
            p = jnp.exp2(s_scr[buf, rows, :] - mn).astype(BF16)
            part = jnp.dot(jnp.concatenate([vt_ref[0, heads, rows], ones], axis=0), p,
                           preferred_element_type=F32)
            acc = part if acc is None else acc + part
        o_ref[0, heads, cols] = (acc[0:HEAD_DIM] / acc[HEAD_DIM:HEAD_DIM + 1]).astype(BF16)

    tasks = [(hh, j) for j in reversed(range(n_blk)) for hh in range(ATTN_HEADS)]
    pending, issued = [], []
    for i, (hh, j) in enumerate(tasks):
        buf = i % SCORE_BUFFERS
        issued.append(scores(hh, j, buf))
        pending.append((hh, j, buf))
        if len(pending) > SCORE_LOOKAHEAD:
            finish(*pending.pop(0), issued[i - GATE_BACK])
    for item in pending:
        finish(*item, None)


def _attn(qt, qadd, k, vt, bias):
    b, _, s = qt.shape
    n_blk = s // MOBA_BLOCK
    nh = ATTN_HEADS
    return pl.pallas_call(
        functools.partial(_attn_kernel, n_blk=n_blk),
        grid=(b, MOBA_HEADS // nh),
        in_specs=[pl.BlockSpec((1, nh * HEAD_DIM, s), lambda i, h: (i, h, 0)),
                  pl.BlockSpec((1, nh, 2 * n_blk, s), lambda i, h: (i, h, 0, 0)),
                  pl.BlockSpec((1, s, nh * V7X_LANES), lambda i, h: (i, 0, h)),
                  pl.BlockSpec((1, nh * HEAD_DIM, s), lambda i, h: (i, h, 0)),
                  pl.BlockSpec((nh, 2, MOBA_BLOCK, MOBA_BLOCK), lambda i, h: (h, 0, 0, 0))],
        out_specs=pl.BlockSpec((1, nh * HEAD_DIM, s), lambda i, h: (i, h, 0)),
        out_shape=jax.ShapeDtypeStruct((b, MOBA_WIDTH, s), BF16),
        scratch_shapes=[pltpu.VMEM((SCORE_BUFFERS, s, MOBA_BLOCK), F32)],
        compiler_params=_params(2),
        name="attn",
    )(qt, qadd, k, vt, bias)


def _branches_kernel(h_ref, hlo_ref, wu_ref, wv_ref, lng_ref, lnb_ref, ws_ref, bs_ref, wq_ref, kt_ref, vm_ref,
                     gw_ref, far_ref, a_ref, c_ref, qadd_ref, *, n_blk):
    causal = (lax.broadcasted_iota(jnp.int32, (CHUNK, CHUNK), 0)
              >= lax.broadcasted_iota(jnp.int32, (CHUNK, CHUNK), 1))
    ws = [jnp.where(causal, ws_ref[g], 0.0).astype(BF16) for g in range(GMLP_GROUPS)]
    ones = jnp.ones((vm_ref.shape[1], MEM_HEAD_DIM), BF16)
    far = jnp.concatenate([far_ref[...]] * (MOBA_BLOCK // V7X_LANES), axis=-1)
    n_slabs = h_ref.shape[0] // MOBA_BLOCK

    def project(i):
        rows = slice(i * MOBA_BLOCK, (i + 1) * MOBA_BLOCK)
        h = h_ref[rows, :]
        cq = jnp.dot(h, wq_ref[...], preferred_element_type=F32).astype(BF16)
        v = jnp.dot(h, wv_ref[...], preferred_element_type=F32)
        u = jnp.dot(h, wu_ref[...], preferred_element_type=F32)
        res = jnp.dot(jnp.concatenate([h, hlo_ref[rows, :]], axis=0), gw_ref[0], preferred_element_type=F32)
        ps = []
        for hd in range(MEM_HEADS):
            sl = slice(hd * MEM_HEAD_DIM, (hd + 1) * MEM_HEAD_DIM)
            s = jnp.dot(cq[:, sl], kt_ref[0, sl, :], preferred_element_type=F32) * (MEM_HEAD_DIM ** -0.5 * LOG2E)
            ps.append(jnp.exp2(s - jnp.max(s, axis=-1, keepdims=True)).astype(BF16))
        return rows, v, u, ps, res

    def finish(i, rows, v, u, ps, res):
        v = jax.nn.gelu(v)
        vc = v - jnp.mean(v, axis=-1, keepdims=True)
        vn = vc * lax.rsqrt(jnp.mean(vc * vc, axis=-1, keepdims=True) + EPS) * lng_ref[...] + lnb_ref[...]
        vn = vn.astype(BF16)
        u = jax.nn.gelu(u)
        chunks = [slice(c * CHUNK, (c + 1) * CHUNK) for c in range(MOBA_BLOCK // CHUNK)]
        for g in range(GMLP_GROUPS):
            lanes = slice(g * GMLP_GROUP_DIM, (g + 1) * GMLP_GROUP_DIM)
            mixed = jnp.dot(ws[g], jnp.concatenate([vn[toks, lanes] for toks in chunks], axis=1),
                            preferred_element_type=F32)
            for c, toks in enumerate(chunks):
                mc = mixed[:, c * GMLP_GROUP_DIM:(c + 1) * GMLP_GROUP_DIM] + bs_ref[g]
                a_ref[rows.start + c * CHUNK:rows.start + (c + 1) * CHUNK, lanes] = (
                    u[toks, lanes] * mc).astype(BF16)
        for hd, p in enumerate(ps):
            sl = slice(hd * MEM_HEAD_DIM, (hd + 1) * MEM_HEAD_DIM)
            acc = jnp.dot(p, jnp.concatenate([vm_ref[0, :, sl], ones], axis=1), preferred_element_type=F32)
            c_ref[rows, sl] = (acc[:, :MEM_HEAD_DIM] / acc[:, MEM_HEAD_DIM:]).astype(BF16)
        cur = pl.program_id(1) * n_slabs + i
        hi, lo = res[:MOBA_BLOCK], res[MOBA_BLOCK:]
        gate = (hi[:, :GATE_ROWS] + (hi[:, GATE_ROWS:] + lo[:, :GATE_ROWS])) + lo[:, GATE_ROWS:]
        gate = gate.T[0:MOBA_HEADS * n_blk].reshape(MOBA_HEADS, n_blk, MOBA_BLOCK)
        blk = lax.broadcasted_iota(jnp.int32, gate.shape, 1)
        past = blk < cur
        gate = jnp.where(past, gate, NEG)
        rank = jnp.zeros(gate.shape, jnp.int32)
        for m in range(n_blk):
            gm = gate[:, m:m + 1, :]
            tie = jnp.where(blk > m, 1, 0)
            rank = rank + jnp.where(gm > gate, 1, jnp.where(gm == gate, tie, 0))
        mask = jnp.where(rank < min(MOBA_TOPK, n_blk), jnp.where(past, 0.0, NEG), NEG)
        add = jnp.where(blk < cur - 1, mask + far, jnp.where(blk == cur - 1, mask, 0.0))
        add_hi, add_lo = _split_bf16(add)
        qadd_ref[0, :, :, rows] = jnp.concatenate(
            [add_hi.astype(F32), add_lo.astype(F32)], axis=1).astype(BF16)

    staged = project(0)
    for i in range(n_slabs):
        nxt = project(i + 1) if i + 1 < n_slabs else None
        finish(i, *staged)
        staged = nxt


def _branches(h_hi, h_lo, b, s, w_in, u_col, v_col, ln_g, ln_b, w_s, b_s_lanes, w_cq, kt, vm, gt, far_lanes,
              tm=2 * TOKEN_TILE):
    d = h_hi.shape[-1]
    assert u_col % GMLP_WIDTH == 0 and v_col % GMLP_WIDTH == 0
    n_blk = s // MOBA_BLOCK
    nt = s // tm
    m = kt.shape[-1]
    tok = lambda i, j: (i * nt + j, 0)
    return pl.pallas_call(
        functools.partial(_branches_kernel, n_blk=n_blk),
        grid=(b, nt),
        in_specs=[pl.BlockSpec((tm, d), tok), pl.BlockSpec((tm, d), tok),
                  _resident_cols(d, GMLP_WIDTH, u_col // GMLP_WIDTH),
                  _resident_cols(d, GMLP_WIDTH, v_col // GMLP_WIDTH),
                  _resident(ln_g.shape), _resident(ln_b.shape),
                  _resident(w_s.shape), _resident(b_s_lanes.shape), _resident(w_cq.shape),
                  pl.BlockSpec((1, MEM_WIDTH, m), lambda i, j: (i, 0, 0)),
                  pl.BlockSpec((1, m, MEM_WIDTH), lambda i, j: (i, 0, 0)),
                  pl.BlockSpec((1, d, 2 * GATE_ROWS), lambda i, j: (i, 0, 0)),
                  _resident(far_lanes.shape)],
        out_specs=[pl.BlockSpec((tm, GMLP_WIDTH), tok), pl.BlockSpec((tm, MEM_WIDTH), tok),
                   pl.BlockSpec((1, MOBA_HEADS, 2 * n_blk, tm), lambda i, j: (i, 0, 0, j))],
        out_shape=[jax.ShapeDtypeStruct((b * s, GMLP_WIDTH), BF16),
                   jax.ShapeDtypeStruct((b * s, MEM_WIDTH), BF16),
                   jax.ShapeDtypeStruct((b, MOBA_HEADS, 2 * n_blk, s), BF16)],
        compiler_params=_params(2),
        name="branches",
    )(h_hi, h_lo, w_in, w_in, ln_g, ln_b, w_s, b_s_lanes, w_cq, kt, vm, gt, far_lanes)


def _memkv_kernel(mem_ref, g_ref, wkt_ref, wv_ref, kt_ref, v_ref):
    for i in range(mem_ref.shape[0]):
        mn = _rms(mem_ref[i], g_ref[...]).astype(BF16)
        kt_ref[i] = lax.dot_general(wkt_ref[...], mn, (((1,), (1,)), ((), ())),
                                    preferred_element_type=F32).astype(BF16)
        v_ref[i] = jnp.dot(mn, wv_ref[...], preferred_element_type=F32).astype(BF16)


def _memkv(mem, g, w_kt, w_v):
    b, m, d = mem.shape
    nb = math.gcd(b, MEM_BATCHES)
    return pl.pallas_call(
        _memkv_kernel,
        grid=(b // nb,),
        in_specs=[pl.BlockSpec((nb, m, d), lambda i: (i, 0, 0)), _resident((1, d)),
                  _resident(w_kt.shape), _resident(w_v.shape)],
        out_specs=[pl.BlockSpec((nb, MEM_WIDTH, m), lambda i: (i, 0, 0)),
                   pl.BlockSpec((nb, m, MEM_WIDTH), lambda i: (i, 0, 0))],
        out_shape=[jax.ShapeDtypeStruct((b, MEM_WIDTH, m), BF16),
                   jax.ShapeDtypeStruct((b, m, MEM_WIDTH), BF16)],
        compiler_params=_params(1),
        name="memkv",
    )(mem, g, w_kt, w_v)


def _merge_kernel(x_ref, h_ref, a_ref, bt_ref, c_ref, wg_ref, wa_ref, wb_ref, wc_ref, wo_ref,
                  lnpost_ref, lnpre_ref, x1_ref, h2_ref):
    d = x_ref.shape[-1]
    for i in range(x_ref.shape[0] // MLP_SLAB):
        rows = slice(i * MLP_SLAB, (i + 1) * MLP_SLAB)
        gates = jax.nn.sigmoid(jnp.dot(h_ref[rows, :], wg_ref[...], preferred_element_type=F32))
        pa = jnp.dot(a_ref[rows, :], wa_ref[...], preferred_element_type=F32)
        pb = lax.dot_general(bt_ref[0, :, rows], wb_ref[...], (((0,), (0,)), ((), ())),
                             preferred_element_type=F32)
        pc = jnp.dot(c_ref[rows, :], wc_ref[...], preferred_element_type=F32)
        merged = gates[:, :d] * pa + gates[:, d:2 * d] * pb + gates[:, 2 * d:] * pc
        mo = jnp.dot(merged.astype(BF16), wo_ref[...], preferred_element_type=F32)
        x1 = x_ref[rows, :] + _rms(mo, lnpost_ref[...])
        x1_ref[rows, :] = x1
        h2_ref[rows, :] = _rms(x1, lnpre_ref[...]).astype(BF16)


def _merge(x2, h2, a_out, b_out_t, c_out, w_g, w_a, w_b, w_c, w_o, ln_post, ln_pre, tm=MLP_TILE):
    n_tok, d = x2.shape
    b, _, s = b_out_t.shape
    nt = s // tm
    tok = lambda i, j: (i * nt + j, 0)
    return pl.pallas_call(
        _merge_kernel,
        grid=(b, nt),
        in_specs=[pl.BlockSpec((tm, d), tok), pl.BlockSpec((tm, d), tok),
                  pl.BlockSpec((tm, GMLP_WIDTH), tok),
                  pl.BlockSpec((1, MOBA_WIDTH, tm), lambda i, j: (i, 0, j)),
                  pl.BlockSpec((tm, MEM_WIDTH), tok),
                  _resident(w_g.shape), _resident(w_a.shape), _resident(w_b.shape),
                  _resident(w_c.shape), _resident(w_o.shape),
                  _resident((1, d)), _resident((1, d))],
        out_specs=[pl.BlockSpec((tm, d), tok), pl.BlockSpec((tm, d), tok)],
        out_shape=[jax.ShapeDtypeStruct((n_tok, d), F32), jax.ShapeDtypeStruct((n_tok, d), BF16)],
        compiler_params=_params(2),
        name="merge",
    )(x2, h2, a_out, b_out_t, c_out, w_g, w_a, w_b, w_c, w_o, ln_post, ln_pre)


def _ffn_kernel(x1_ref, h2_ref, wg_ref, wu_ref, wd_ref, ln_ref, o_ref):
    for i in range(x1_ref.shape[0] // MLP_SLAB):
        rows = slice(i * MLP_SLAB, (i + 1) * MLP_SLAB)
        h2 = h2_ref[rows, :]
        g = jnp.dot(h2, wg_ref[...], preferred_element_type=F32)
        u = jnp.dot(h2, wu_ref[...], preferred_element_type=F32)
        act = (jax.nn.silu(g) * u).astype(BF16)
        f = jnp.dot(act, wd_ref[...], preferred_element_type=F32)
        o_ref[rows, :] = x1_ref[rows, :] + _rms(f, ln_ref[...])


def _ffn(x1, h2, w_g, w_u, w_d, ln_post, tm=MLP_TILE):
    n_tok, d = x1.shape
    return pl.pallas_call(
        _ffn_kernel,
        grid=(n_tok // tm,),
        in_specs=[pl.BlockSpec((tm, d), lambda i: (i, 0)), pl.BlockSpec((tm, d), lambda i: (i, 0)),
                  _resident(w_g.shape), _resident(w_u.shape), _resident(w_d.shape),
                  _resident((1, d))],
        out_specs=pl.BlockSpec((tm, d), lambda i: (i, 0)),
        out_shape=jax.ShapeDtypeStruct((n_tok, d), F32),
        compiler_params=_params(1),
        name="ffn",
    )(x1, h2, w_g, w_u, w_d, ln_post)


def kernel(x, mem, ln_mix_pre, ln_mix_post, ln_ffn_pre, ln_ffn_post, ln_mem, w_in, ln_v_gain, ln_v_bias,
           w_spatial, b_spatial, rel_bias, w_mem_kv, w_branch_a, w_branch_b, w_branch_c, w_out,
           w_ffn_gate, w_ffn_up, w_ffn_down):
    b, s, d = x.shape
    assert s % MOBA_BLOCK == 0 and s % TOKEN_TILE == 0 and d % V7X_LANES == 0
    depth = w_in.shape[0]
    cuts = [0, GMLP_WIDTH, 2 * GMLP_WIDTH, 2 * GMLP_WIDTH + MOBA_WIDTH, 2 * GMLP_WIDTH + 2 * MOBA_WIDTH,
            2 * GMLP_WIDTH + 3 * MOBA_WIDTH, 2 * GMLP_WIDTH + 3 * MOBA_WIDTH + MEM_WIDTH]
    rb_flat = rel_bias.astype(F32).reshape(-1)
    bias = _bias_tiles(rb_flat)
    far_lanes = jnp.broadcast_to((rel_bias[REL_BUCKETS - 1].astype(F32) * LOG2E)[:, None, None],
                                 (MOBA_HEADS, s // MOBA_BLOCK, V7X_LANES))
    row = lambda v: v.reshape(1, -1).astype(F32)
    for l in range(depth):
        wi = w_in[l]
        wi16 = wi.astype(BF16)
        w_q, w_k, w_v2 = (wi[:, cuts[i]:cuts[i + 1]] for i in (2, 3, 4))
        w_cq, w_g = wi16[:, cuts[5]:cuts[6]], wi16[:, cuts[6]:]
        x2 = x.reshape(b * s, d)

        w_qt_scaled = (w_q * (HEAD_DIM ** -0.5)).T
        h2, h_lo, hmean, qt, k, vt = _qkv(x2, row(ln_mix_pre[l]), b, s, (w_qt_scaled * LOG2E).astype(BF16),
                                          wi16, cuts[3], w_v2.T.astype(BF16))
        gt = _gatew(hmean.reshape(-1, d), w_k, w_qt_scaled, b)
        wkv = w_mem_kv[l]
        kt_mem, v_mem = _memkv(mem, row(ln_mem[l]), wkv[:, :MEM_WIDTH].T.astype(BF16),
                               wkv[:, MEM_WIDTH:].astype(BF16))
        b_s_lanes = jnp.broadcast_to(b_spatial[l][:, :, None], (GMLP_GROUPS, CHUNK, GMLP_GROUP_DIM))
        a_out, c_out, qadd = _branches(
            h2, h_lo, b, s, wi16, cuts[0], cuts[1], row(ln_v_gain[l]), row(ln_v_bias[l]),
            w_spatial[l], b_s_lanes.astype(F32), w_cq, kt_mem, v_mem, gt, far_lanes)
        b_out_t = _attn(qt, qadd, k, vt, bias)

        x1, hn = _merge(x2, h2, a_out, b_out_t, c_out, w_g, w_branch_a[l].astype(BF16),
                        w_branch_b[l].astype(BF16), w_branch_c[l].astype(BF16), w_out[l].astype(BF16),
                        row(ln_mix_post[l]), row(ln_ffn_pre[l]))
        out = _ffn(x1, hn, w_ffn_gate[l].astype(BF16), w_ffn_up[l].astype(BF16),
                   w_ffn_down[l].astype(BF16), row(ln_ffn_post[l]))
        x = out.reshape(b, s, d)
    return x
```

```python
import functools
import math

import jax
import jax.numpy as jnp
from jax import lax
from jax.experimental import pallas as pl
from jax.experimental.pallas import tpu as pltpu

F32 = jnp.float32
BF16 = jnp.bfloat16

EPS = 1e-6
NEG = -1e30
GMLP_GROUPS = 6
GMLP_GROUP_DIM = 128
GMLP_WIDTH = GMLP_GROUPS * GMLP_GROUP_DIM
CHUNK = 128
MOBA_HEADS = 12
HEAD_DIM = 64
MOBA_WIDTH = MOBA_HEADS * HEAD_DIM
MOBA_BLOCK = 256
MOBA_TOPK = 3
REL_BUCKETS = 32
REL_MAX_DIST = 128
LOG2E = math.log2(math.e)
ONES_ROWS = 16
ATTN_HEADS = 4
SCORE_LOOKAHEAD = 3
SCORE_BUFFERS = SCORE_LOOKAHEAD + 1
GATE_BACK = 1
MEM_HEADS = 4
MEM_HEAD_DIM = 128
MEM_WIDTH = MEM_HEADS * MEM_HEAD_DIM
MEM_BATCHES = 4
N_BRANCHES = 3

V7X_LANES = 128
V7X_VMEM_LIMIT = 56 * 1024 * 1024

TOKEN_TILE = 512
MLP_TILE = 1024
MLP_SLAB = 256
QKV_SLAB = 256
GATE_ROWS = 128


def _params(n_axes, vmem=V7X_VMEM_LIMIT):
    return pltpu.CompilerParams(
        dimension_semantics=("arbitrary",) * n_axes, vmem_limit_bytes=vmem)


def _resident(shape):
    zeros = (0,) * len(shape)
    return pl.BlockSpec(shape, lambda *_: zeros, pipeline_mode=pl.Buffered(1))


def _resident_cols(rows, width, index):
    return pl.BlockSpec((rows, width), lambda *_: (0, index), pipeline_mode=pl.Buffered(1))


def _rms(x, g):
    return x * lax.rsqrt(jnp.mean(x * x, axis=-1, keepdims=True) + EPS) * g


def _split_bf16(x):
    hi = x.astype(BF16)
    return hi, (x - hi.astype(F32)).astype(BF16)


def _gatew_kernel(hm_ref, wk_ref, wqt_ref, g_ref, gt_scr, *, n_batch, n_blk):
    kmean = jnp.dot(hm_ref[...], wk_ref[...], precision=lax.Precision.HIGHEST,
                    preferred_element_type=F32)
    used = MOBA_HEADS * n_blk
    gt_scr[:, used:, :] = jnp.zeros((n_batch, GATE_ROWS - used, gt_scr.shape[-1]), F32)
    for h in range(MOBA_HEADS):
        sl = slice(h * HEAD_DIM, (h + 1) * HEAD_DIM)
        res = jnp.dot(kmean[:, sl], wqt_ref[sl, :], precision=lax.Precision.HIGHEST,
                      preferred_element_type=F32)
        for b in range(n_batch):
            gt_scr[b, h * n_blk:(h + 1) * n_blk, :] = res[b * n_blk:(b + 1) * n_blk, :]
    for b in range(n_batch):
        hi, lo = _split_bf16(gt_scr[b].T)
        g_ref[b, :, 0:GATE_ROWS] = hi
        g_ref[b, :, GATE_ROWS:] = lo


def _gatew(hmean, w_k, w_qt, n_batch):
    rows, d = hmean.shape
    n_blk = rows // n_batch
    assert MOBA_HEADS * n_blk <= GATE_ROWS
    return pl.pallas_call(
        functools.partial(_gatew_kernel, n_batch=n_batch, n_blk=n_blk),
        grid=(1,),
        in_specs=[_resident((rows, d)), _resident(w_k.shape), _resident(w_qt.shape)],
        out_specs=pl.BlockSpec((n_batch, d, 2 * GATE_ROWS), lambda i: (0, 0, 0)),
        out_shape=jax.ShapeDtypeStruct((n_batch, d, 2 * GATE_ROWS), BF16),
        scratch_shapes=[pltpu.VMEM((n_batch, GATE_ROWS, d), F32)],
        compiler_params=_params(1),
        name="gatew",
    )(hmean, w_k, w_qt)


def _qkv_kernel(x_ref, g_ref, wqt_ref, wk_ref, wvt_ref, h_ref, hlo_ref, hmean_ref, qt_ref, k_ref, vt_ref,
                *, n_blk):
    nt = (((1,), (1,)), ((), ()))
    n_slabs = x_ref.shape[0] // QKV_SLAB
    lane =lax.broadcasted_iota(jnp.int32, (QKV_SLAB, V7X_LANES), 1)
    low = lane < HEAD_DIM
    for i in range(n_slabs):
        rows = slice(i * QKV_SLAB, (i + 1) * QKV_SLAB)
        blk = pl.program_id(1) * n_slabs + i
        onehot = jnp.where((lane == HEAD_DIM + blk) | (lane == HEAD_DIM + n_blk + blk), 1.0, 0.0)
        hf = _rms(x_ref[rows, :], g_ref[...])
        h, hlo_ref[rows, :] = _split_bf16(hf)
        h_ref[rows, :] = h
        for c in range(QKV_SLAB // MOBA_BLOCK):
            hmean_ref[i * (QKV_SLAB // MOBA_BLOCK) + c] = jnp.mean(
                hf[c * MOBA_BLOCK:(c + 1) * MOBA_BLOCK], axis=0, keepdims=True)
        qt_ref[0, :, rows] = lax.dot_general(wqt_ref[...], h, nt, preferred_element_type=F32).astype(BF16)
        vt_ref[0, :, rows] = lax.dot_general(wvt_ref[...], h, nt, preferred_element_type=F32).astype(BF16)
        kf = jnp.dot(h, wk_ref[...], preferred_element_type=F32)
        for p in range(MOBA_HEADS // 2):
            pair = kf[:, p * V7X_LANES:(p + 1) * V7X_LANES]
            even = jnp.where(low, pair, onehot)
            odd = jnp.where(low, pltpu.roll(pair, HEAD_DIM, 1), onehot)
            k_ref[0, rows, (2 * p) * V7X_LANES:(2 * p + 1) * V7X_LANES] = even.astype(BF16)
            k_ref[0, rows, (2 * p + 1) * V7X_LANES:(2 * p + 2) * V7X_LANES] = odd.astype(BF16)


def _qkv(x2, g, b, s, w_qt, w_in, k_col, w_vt, tm=2 * TOKEN_TILE):
    d = x2.shape[-1]
    assert k_col % MOBA_WIDTH == 0
    nt = s // tm
    blocks = tm // MOBA_BLOCK
    n_blk = s // MOBA_BLOCK
    assert QKV_SLAB == MOBA_BLOCK and HEAD_DIM + 2 * n_blk <= V7X_LANES
    tok = lambda i, j: (i * nt + j, 0)
    return pl.pallas_call(
        functools.partial(_qkv_kernel, n_blk=n_blk),
        grid=(b, nt),
        in_specs=[pl.BlockSpec((tm, d), tok), _resident((1, d)),
                  _resident(w_qt.shape), _resident_cols(d, MOBA_WIDTH, k_col // MOBA_WIDTH),
                  _resident(w_vt.shape)],
        out_specs=[pl.BlockSpec((tm, d), tok), pl.BlockSpec((tm, d), tok),
                   pl.BlockSpec((blocks, 1, d), lambda i, j: (i * nt + j, 0, 0)),
                   pl.BlockSpec((1, MOBA_WIDTH, tm), lambda i, j: (i, 0, j)),
                   pl.BlockSpec((1, tm, MOBA_HEADS * V7X_LANES), lambda i, j: (i, j, 0)),
                   pl.BlockSpec((1, MOBA_WIDTH, tm), lambda i, j: (i, 0, j))],
        out_shape=[jax.ShapeDtypeStruct((b * s, d), BF16), jax.ShapeDtypeStruct((b * s, d), BF16),
                   jax.ShapeDtypeStruct((b * s // MOBA_BLOCK, 1, d), F32),
                   jax.ShapeDtypeStruct((b, MOBA_WIDTH, s), BF16),
                   jax.ShapeDtypeStruct((b, s, MOBA_HEADS * V7X_LANES), BF16),
                   jax.ShapeDtypeStruct((b, MOBA_WIDTH, s), BF16)],
        compiler_params=_params(2),
        name="qkv",
    )(x2, g, w_qt, w_in, w_vt)


def _bias_kernel(rb_ref, out_ref):
    h = pl.program_id(0)
    span = 2 * MOBA_BLOCK
    u = lax.broadcasted_iota(jnp.int32, (8, span), 1)
    max_exact = REL_BUCKETS // 2
    for t in range(2):
        dist = u if t == 0 else jnp.where(u < MOBA_BLOCK, u + MOBA_BLOCK, u - MOBA_BLOCK)
        n = jnp.maximum(dist, 0)
        nf = jnp.maximum(n, 1).astype(F32)
        large = max_exact + (jnp.log(nf / max_exact) / math.log(REL_MAX_DIST / max_exact)
                             * (REL_BUCKETS - max_exact)).astype(jnp.int32)
        large = jnp.minimum(large, REL_BUCKETS - 1)
        bucket = jnp.where(n < max_exact, n, large)
        val = jnp.zeros((8, span), F32)
        for bk in range(REL_BUCKETS):
            val = jnp.where(bucket == bk, rb_ref[bk * MOBA_HEADS + h], val)
        val = val * LOG2E
        if t == 0:
            val = jnp.where(u < MOBA_BLOCK, val, NEG)
        table = jnp.concatenate([val] * (MOBA_BLOCK // 8), axis=0)
        tile = pltpu.roll(table, 0, 1, stride=1, stride_axis=0)
        out_ref[0, t] = tile[:, 0:MOBA_BLOCK]


def _bias_tiles(rb_flat):
    return pl.pallas_call(
        _bias_kernel,
        grid=(MOBA_HEADS,),
        in_specs=[pl.BlockSpec(memory_space=pltpu.SMEM)],
        out_specs=pl.BlockSpec((1, 2, MOBA_BLOCK, MOBA_BLOCK), lambda h: (h, 0, 0, 0)),
        out_shape=jax.ShapeDtypeStruct((MOBA_HEADS, 2, MOBA_BLOCK, MOBA_BLOCK), F32),
        compiler_params=_params(1),
        name="bias",
    )(rb_flat)


def _zero_after(x):
    bits = lax.bitcast_convert_type(x, jnp.uint32)
    sixteen = jnp.uint32(16)
    return lax.bitcast_convert_type(
        lax.shift_right_logical(lax.shift_right_logical(bits, sixteen), sixteen), F32)


def _attn_kernel(qt_ref, qadd_ref, k_ref, vt_ref, bias_ref, o_ref, s_scr, *, n_blk):
    pad = jnp.zeros((V7X_LANES - HEAD_DIM - 2 * n_blk, MOBA_BLOCK), BF16)

    def scores(hh, j, buf):
        cols = slice(j * MOBA_BLOCK, (j + 1) * MOBA_BLOCK)
        q = qt_ref[0, hh * HEAD_DIM:(hh + 1) * HEAD_DIM, cols]
        qz = jnp.concatenate([q, qadd_ref[0, hh, :, cols], pad], axis=0)
        k = k_ref[0, 0:(j + 1) * MOBA_BLOCK, hh * V7X_LANES:(hh + 1) * V7X_LANES]
        s = jnp.dot(k, qz, preferred_element_type=F32)
        gates = []
        for n in range(j + 1):
            rows = slice(n * MOBA_BLOCK, (n + 1) * MOBA_BLOCK)
            if n == j:
                sn = s[rows] + bias_ref[hh, 0]
            elif n == j - 1:
                sn = s[rows] + bias_ref[hh, 1]
            else:
                sn = s[rows]
            s_scr[buf, rows, :] = sn
            gates.append(_zero_after(sn[0:1, :]))
        return gates

    def finish(hh, j, buf, gates):
        cols = slice(j * MOBA_BLOCK, (j + 1) * MOBA_BLOCK)
        heads = slice(hh * HEAD_DIM, (hh + 1) * HEAD_DIM)
        m = jnp.max(s_scr[buf, 0:(j + 1) * MOBA_BLOCK, :], axis=0, keepdims=True)
        ones = jnp.ones((ONES_ROWS, MOBA_BLOCK), BF16)
        acc = None
        for n in range(j + 1):
            rows = slice(n * MOBA_BLOCK, (n + 1) * MOBA_BLOCK)
            mn = m + gates[min(n, len(gates) - 1)] if gates else m
            p = jnp.exp2(s_scr[buf, rows, :] - mn).astype(BF16)
            part = jnp.dot(jnp.concatenate([vt_ref[0, heads, rows], ones], axis=0), p,
                           preferred_element_type=F32)
            acc = part if acc is None else acc + part
        o_ref[0, heads, cols] = (acc[0:HEAD_DIM] / acc[HEAD_DIM:HEAD_DIM + 1]).astype(BF16)

    tasks = [(hh, j) for j in reversed(range(n_blk)) for hh in range(ATTN_HEADS)]
    pending, issued = [], []
    for i, (hh, j) in enumerate(tasks):
        buf = i % SCORE_BUFFERS
        issued.append(scores(hh, j, buf))
        pending.append((hh, j, buf))
        if len(pending) > SCORE_LOOKAHEAD:
            finish(*pending.pop(0), issued[i - GATE_BACK])
    for item in pending:
        finish(*item, None)


def _attn(qt, qadd, k, vt, bias):
    b, _, s = qt.shape
    n_blk = s // MOBA_BLOCK
    nh = ATTN_HEADS
    return pl.pallas_call(
        functools.partial(_attn_kernel, n_blk=n_blk),
        grid=(MOBA_HEADS // nh, b),
        in_specs=[pl.BlockSpec((1, nh * HEAD_DIM, s), lambda h, i: (i, h, 0)),
                  pl.BlockSpec((1, nh, 2 * n_blk, s), lambda h, i: (i, h, 0, 0)),
                  pl.BlockSpec((1, s, nh * V7X_LANES), lambda h, i: (i, 0, h)),
                  pl.BlockSpec((1, nh * HEAD_DIM, s), lambda h, i: (i, h, 0)),
                  pl.BlockSpec((nh, 2, MOBA_BLOCK, MOBA_BLOCK), lambda h, i: (h, 0, 0, 0))],
        out_specs=pl.BlockSpec((1, nh * HEAD_DIM, s), lambda h, i: (i, h, 0)),
        out_shape=jax.ShapeDtypeStruct((b, MOBA_WIDTH, s), BF16),
        scratch_shapes=[pltpu.VMEM((SCORE_BUFFERS, s, MOBA_BLOCK), F32)],
        compiler_params=_params(2),
        name="attn",
    )(qt, qadd, k, vt, bias)


def _branches_kernel(h_ref, hlo_ref, wu_ref, wv_ref, lng_ref, lnb_ref, ws_ref, bs_ref, wq_ref, kt_ref, vm_ref,
                     gw_ref, far_ref, a_ref, c_ref, qadd_ref, *, n_blk):
    causal = (lax.broadcasted_iota(jnp.int32, (CHUNK, CHUNK), 0)
              >= lax.broadcasted_iota(jnp.int32, (CHUNK, CHUNK), 1))
    ws = [jnp.where(causal, ws_ref[g], 0.0).astype(BF16) for g in range(GMLP_GROUPS)]
    ones = jnp.ones((vm_ref.shape[1], MEM_HEAD_DIM), BF16)
    far = jnp.concatenate([far_ref[...]] * (MOBA_BLOCK // V7X_LANES), axis=-1)
    n_slabs = h_ref.shape[0] // MOBA_BLOCK

    def project(i):
        rows = slice(i * MOBA_BLOCK, (i + 1) * MOBA_BLOCK)
        h = h_ref[rows, :]
        cq = jnp.dot(h, wq_ref[...], preferred_element_type=F32).astype(BF16)
        v = jnp.dot(h, wv_ref[...], preferred_element_type=F32)
        u = jnp.dot(h, wu_ref[...], preferred_element_type=F32)
        res = jnp.dot(jnp.concatenate([h, hlo_ref[rows, :]], axis=0), gw_ref[0], preferred_element_type=F32)
        ps = []
        for hd in range(MEM_HEADS):
            sl = slice(hd * MEM_HEAD_DIM, (hd + 1) * MEM_HEAD_DIM)
            s = jnp.dot(cq[:, sl], kt_ref[0, sl, :], preferred_element_type=F32) * (MEM_HEAD_DIM ** -0.5 * LOG2E)
            ps.append(jnp.exp2(s - jnp.max(s, axis=-1, keepdims=True)).astype(BF16))
        return rows, v, u, ps, res

    def finish(i, rows, v, u, ps, res):
        v = jax.nn.gelu(v)
        vc = v - jnp.mean(v, axis=-1, keepdims=True)
        vn = vc * lax.rsqrt(jnp.mean(vc * vc, axis=-1, keepdims=True) + EPS) * lng_ref[...] + lnb_ref[...]
        vn = vn.astype(BF16)
        u = jax.nn.gelu(u)
        chunks = [slice(c * CHUNK, (c + 1) * CHUNK) for c in range(MOBA_BLOCK // CHUNK)]
        for g in range(GMLP_GROUPS):
            lanes = slice(g * GMLP_GROUP_DIM, (g + 1) * GMLP_GROUP_DIM)
            mixed = jnp.dot(ws[g], jnp.concatenate([vn[toks, lanes] for toks in chunks], axis=1),
                            preferred_element_type=F32)
            for c, toks in enumerate(chunks):
                mc = mixed[:, c * GMLP_GROUP_DIM:(c + 1) * GMLP_GROUP_DIM] + bs_ref[g]
                a_ref[rows.start + c * CHUNK:rows.start + (c + 1) * CHUNK, lanes] = (
                    u[toks, lanes] * mc).astype(BF16)
        for hd, p in enumerate(ps):
            sl = slice(hd * MEM_HEAD_DIM, (hd + 1) * MEM_HEAD_DIM)
            acc = jnp.dot(p, jnp.concatenate([vm_ref[0, :, sl], ones], axis=1), preferred_element_type=F32)
            c_ref[rows, sl] = (acc[:, :MEM_HEAD_DIM] / acc[:, MEM_HEAD_DIM:]).astype(BF16)
        cur = pl.program_id(1) * n_slabs + i
        hi, lo = res[:MOBA_BLOCK], res[MOBA_BLOCK:]
        gate = (hi[:, :GATE_ROWS] + (hi[:, GATE_ROWS:] + lo[:, :GATE_ROWS])) + lo[:, GATE_ROWS:]
        gate = gate.T[0:MOBA_HEADS * n_blk].reshape(MOBA_HEADS, n_blk, MOBA_BLOCK)
        blk = lax.broadcasted_iota(jnp.int32, gate.shape, 1)
        past = blk < cur
        gate = jnp.where(past, gate, NEG)
        rank = jnp.zeros(gate.shape, jnp.int32)
        for m in range(n_blk):
            gm = gate[:, m:m + 1, :]
            tie = jnp.where(blk > m, 1, 0)
            rank = rank + jnp.where(gm > gate, 1, jnp.where(gm == gate, tie, 0))
        mask = jnp.where(rank < min(MOBA_TOPK, n_blk), jnp.where(past, 0.0, NEG), NEG)
        add = jnp.where(blk < cur - 1, mask + far, jnp.where(blk == cur - 1, mask, 0.0))
        add_hi, add_lo = _split_bf16(add)
        qadd_ref[0, :, :, rows] = jnp.concatenate(
            [add_hi.astype(F32), add_lo.astype(F32)], axis=1).astype(BF16)

    staged = project(0)
    for i in range(n_slabs):
        nxt = project(i + 1) if i + 1 < n_slabs else None
        finish(i, *staged)
        staged = nxt


def _branches(h_hi, h_lo, b, s, w_in, u_col, v_col, ln_g, ln_b, w_s, b_s_lanes, w_cq, kt, vm, gt, far_lanes,
              tm=2 * TOKEN_TILE):
    d = h_hi.shape[-1]
    assert u_col % GMLP_WIDTH == 0 and v_col % GMLP_WIDTH == 0
    n_blk = s // MOBA_BLOCK
    nt = s // tm
    m = kt.shape[-1]
    tok = lambda i, j: (i * nt + j, 0)
    return pl.pallas_call(
        functools.partial(_branches_kernel, n_blk=n_blk),
        grid=(b, nt),
        in_specs=[pl.BlockSpec((tm, d), tok), pl.BlockSpec((tm, d), tok),
                  _resident_cols(d, GMLP_WIDTH, u_col // GMLP_WIDTH),
                  _resident_cols(d, GMLP_WIDTH, v_col // GMLP_WIDTH),
                  _resident(ln_g.shape), _resident(ln_b.shape),
                  _resident(w_s.shape), _resident(b_s_lanes.shape), _resident(w_cq.shape),
                  pl.BlockSpec((1, MEM_WIDTH, m), lambda i, j: (i, 0, 0)),
                  pl.BlockSpec((1, m, MEM_WIDTH), lambda i, j: (i, 0, 0)),
                  pl.BlockSpec((1, d, 2 * GATE_ROWS), lambda i, j: (i, 0, 0)),
                  _resident(far_lanes.shape)],
        out_specs=[pl.BlockSpec((tm, GMLP_WIDTH), tok), pl.BlockSpec((tm, MEM_WIDTH), tok),
                   pl.BlockSpec((1, MOBA_HEADS, 2 * n_blk, tm), lambda i, j: (i, 0, 0, j))],
        out_shape=[jax.ShapeDtypeStruct((b * s, GMLP_WIDTH), BF16),
                   jax.ShapeDtypeStruct((b * s, MEM_WIDTH), BF16),
                   jax.ShapeDtypeStruct((b, MOBA_HEADS, 2 * n_blk, s), BF16)],
        compiler_params=_params(2),
        name="branches",
    )(h_hi, h_lo, w_in, w_in, ln_g, ln_b, w_s, b_s_lanes, w_cq, kt, vm, gt, far_lanes)


def _memkv_kernel(mem_ref, g_ref, wkt_ref, wv_ref, kt_ref, v_ref):
    for i in range(mem_ref.shape[0]):
        mn = _rms(mem_ref[i], g_ref[...]).astype(BF16)
        kt_ref[i] = lax.dot_general(wkt_ref[...], mn, (((1,), (1,)), ((), ())),
                                    preferred_element_type=F32).astype(BF16)
        v_ref[i] = jnp.dot(mn, wv_ref[...], preferred_element_type=F32).astype(BF16)


def _memkv(mem, g, w_kt, w_v):
    b, m, d = mem.shape
    nb = math.gcd(b, MEM_BATCHES)
    return pl.pallas_call(
        _memkv_kernel,
        grid=(b // nb,),
        in_specs=[pl.BlockSpec((nb, m, d), lambda i: (i, 0, 0)), _resident((1, d)),
                  _resident(w_kt.shape), _resident(w_v.shape)],
        out_specs=[pl.BlockSpec((nb, MEM_WIDTH, m), lambda i: (i, 0, 0)),
                   pl.BlockSpec((nb, m, MEM_WIDTH), lambda i: (i, 0, 0))],
        out_shape=[jax.ShapeDtypeStruct((b, MEM_WIDTH, m), BF16),
                   jax.ShapeDtypeStruct((b, m, MEM_WIDTH), BF16)],
        compiler_params=_params(1),
        name="memkv",
    )(mem, g, w_kt, w_v)


def _merge_kernel(x_ref, h_ref, a_ref, bt_ref, c_ref, wg_ref, wa_ref, wb_ref, wc_ref, wo_ref,
                  lnpost_ref, lnpre_ref, x1_ref, h2_ref):
    d = x_ref.shape[-1]
    for i in range(x_ref.shape[0] // MLP_SLAB):
        rows = slice(i * MLP_SLAB, (i + 1) * MLP_SLAB)
        gates = jax.nn.sigmoid(jnp.dot(h_ref[rows, :], wg_ref[...], preferred_element_type=F32))
        pa = jnp.dot(a_ref[rows, :], wa_ref[...], preferred_element_type=F32)
        pb = lax.dot_general(bt_ref[0, :, rows], wb_ref[...], (((0,), (0,)), ((), ())),
                             preferred_element_type=F32)
        pc = jnp.dot(c_ref[rows, :], wc_ref[...], preferred_element_type=F32)
        merged = gates[:, :d] * pa + gates[:, d:2 * d] * pb + gates[:, 2 * d:] * pc
        mo = jnp.dot(merged.astype(BF16), wo_ref[...], preferred_element_type=F32)
        x1 = x_ref[rows, :] + _rms(mo, lnpost_ref[...])
        x1_ref[rows, :] = x1
        h2_ref[rows, :] = _rms(x1, lnpre_ref[...]).astype(BF16)


def _merge(x2, h2, a_out, b_out_t, c_out, w_g, w_a, w_b, w_c, w_o, ln_post, ln_pre, tm=MLP_TILE):
    n_tok, d = x2.shape
    b, _, s = b_out_t.shape
    nt = s // tm
    tok = lambda i, j: (i * nt + j, 0)
    return pl.pallas_call(
        _merge_kernel,
        grid=(b, nt),
        in_specs=[pl.BlockSpec((tm, d), tok), pl.BlockSpec((tm, d), tok),
                  pl.BlockSpec((tm, GMLP_WIDTH), tok),
                  pl.BlockSpec((1, MOBA_WIDTH, tm), lambda i, j: (i, 0, j)),
                  pl.BlockSpec((tm, MEM_WIDTH), tok),
                  _resident(w_g.shape), _resident(w_a.shape), _resident(w_b.shape),
                  _resident(w_c.shape), _resident(w_o.shape),
                  _resident((1, d)), _resident((1, d))],
        out_specs=[pl.BlockSpec((tm, d), tok), pl.BlockSpec((tm, d), tok)],
        out_shape=[jax.ShapeDtypeStruct((n_tok, d), F32), jax.ShapeDtypeStruct((n_tok, d), BF16)],
        compiler_params=_params(2),
        name="merge",
    )(x2, h2, a_out, b_out_t, c_out, w_g, w_a, w_b, w_c, w_o, ln_post, ln_pre)


def _ffn_kernel(x1_ref, h2_ref, wg_ref, wu_ref, wd_ref, ln_ref, o_ref):
    for i in range(x1_ref.shape[0] // MLP_SLAB):
        rows = slice(i * MLP_SLAB, (i + 1) * MLP_SLAB)
        h2 = h2_ref[rows, :]
        g = jnp.dot(h2, wg_ref[...], preferred_element_type=F32)
        u = jnp.dot(h2, wu_ref[...], preferred_element_type=F32)
        act = (jax.nn.silu(g) * u).astype(BF16)
        f = jnp.dot(act, wd_ref[...], preferred_element_type=F32)
        o_ref[rows, :] = x1_ref[rows, :] + _rms(f, ln_ref[...])


def _ffn(x1, h2, w_g, w_u, w_d, ln_post, tm=MLP_TILE):
    n_tok, d = x1.shape
    return pl.pallas_call(
        _ffn_kernel,
        grid=(n_tok // tm,),
        in_specs=[pl.BlockSpec((tm, d), lambda i: (i, 0)), pl.BlockSpec((tm, d), lambda i: (i, 0)),
                  _resident(w_g.shape), _resident(w_u.shape), _resident(w_d.shape),
                  _resident((1, d))],
        out_specs=pl.BlockSpec((tm, d), lambda i: (i, 0)),
        out_shape=jax.ShapeDtypeStruct((n_tok, d), F32),
        compiler_params=_params(1),
        name="ffn",
    )(x1, h2, w_g, w_u, w_d, ln_post)


def kernel(x, mem, ln_mix_pre, ln_mix_post, ln_ffn_pre, ln_ffn_post, ln_mem, w_in, ln_v_gain, ln_v_bias,
           w_spatial, b_spatial, rel_bias, w_mem_kv, w_branch_a, w_branch_b, w_branch_c, w_out,
           w_ffn_gate, w_ffn_up, w_ffn_down):
    b, s, d = x.shape
    assert s % MOBA_BLOCK == 0 and s % TOKEN_TILE == 0 and d % V7X_LANES == 0
    depth = w_in.shape[0]
    cuts = [0, GMLP_WIDTH, 2 * GMLP_WIDTH, 2 * GMLP_WIDTH + MOBA_WIDTH, 2 * GMLP_WIDTH + 2 * MOBA_WIDTH,
            2 * GMLP_WIDTH + 3 * MOBA_WIDTH, 2 * GMLP_WIDTH + 3 * MOBA_WIDTH + MEM_WIDTH]
    rb_flat = rel_bias.astype(F32).reshape(-1)
    bias = _bias_tiles(rb_flat)
    far_lanes = jnp.broadcast_to((rel_bias[REL_BUCKETS - 1].astype(F32) * LOG2E)[:, None, None],
                                 (MOBA_HEADS, s // MOBA_BLOCK, V7X_LANES))
    row = lambda v: v.reshape(1, -1).astype(F32)
    for l in range(depth):
        wi = w_in[l]
        wi16 = wi.astype(BF16)
        w_q, w_k, w_v2 = (wi[:, cuts[i]:cuts[i + 1]] for i in (2, 3, 4))
        w_cq, w_g = wi16[:, cuts[5]:cuts[6]], wi16[:, cuts[6]:]
        x2 = x.reshape(b * s, d)

        w_qt_scaled = (w_q * (HEAD_DIM ** -0.5)).T
        h2, h_lo, hmean, qt, k, vt = _qkv(x2, row(ln_mix_pre[l]), b, s, (w_qt_scaled * LOG2E).astype(BF16),
                                          wi16, cuts[3], w_v2.T.astype(BF16))
        gt = _gatew(hmean.reshape(-1, d), w_k, w_qt_scaled, b)
        wkv = w_mem_kv[l]
        kt_mem, v_mem = _memkv(mem, row(ln_mem[l]), wkv[:, :MEM_WIDTH].T.astype(BF16),
                               wkv[:, MEM_WIDTH:].astype(BF16))
        b_s_lanes = jnp.broadcast_to(b_spatial[l][:, :, None], (GMLP_GROUPS, CHUNK, GMLP_GROUP_DIM))
        a_out, c_out, qadd = _branches(
            h2, h_lo, b, s, wi16, cuts[0], cuts[1], row(ln_v_gain[l]), row(ln_v_bias[l]),
            w_spatial[l], b_s_lanes.astype(F32), w_cq, kt_mem, v_mem, gt, far_lanes)
        b_out_t = _attn(qt, qadd, k, vt, bias)

        x1, hn = _merge(x2, h2, a_out, b_out_t, c_out, w_g, w_branch_a[l].astype(BF16),
                        w_branch_b[l].astype(BF16), w_branch_c[l].astype(BF16), w_out[l].astype(BF16),
                        row(ln_mix_post[l]), row(ln_ffn_pre[l]))
        out = _ffn(x1, hn, w_ffn_gate[l].astype(BF16), w_ffn_up[l].astype(BF16),
                   w_ffn_down[l].astype(BF16), row(ln_ffn_post[l]))
        x = out.reshape(b, s, d)
    return x
```

```python
import functools
import math

import jax
import jax.numpy as jnp
from jax import lax
from jax.experimental import pallas as pl
from jax.experimental.pallas import tpu as pltpu

F32 = jnp.float32
BF16 = jnp.bfloat16

EPS = 1e-6
NEG = -1e30
GMLP_GROUPS = 6
GMLP_GROUP_DIM = 128
GMLP_WIDTH = GMLP_GROUPS * GMLP_GROUP_DIM
CHUNK = 128
MOBA_HEADS = 12
HEAD_DIM = 64
MOBA_WIDTH = MOBA_HEADS * HEAD_DIM
MOBA_BLOCK = 256
MOBA_TOPK = 3
REL_BUCKETS = 32
REL_MAX_DIST = 128
LOG2E = math.log2(math.e)
ONES_ROWS = 16
ATTN_HEADS = 4
SCORE_LOOKAHEAD = 3
SCORE_BUFFERS = SCORE_LOOKAHEAD + 1
GATE_BACK = 1
MEM_HEADS = 4
MEM_HEAD_DIM = 128
MEM_WIDTH = MEM_HEADS * MEM_HEAD_DIM
MEM_BATCHES = 4
N_BRANCHES = 3

V7X_LANES = 128
V7X_VMEM_LIMIT = 56 * 1024 * 1024

TOKEN_TILE = 512
MLP_TILE = 1024
MLP_SLAB = 256
QKV_SLAB = 256
GATE_ROWS = 128


def _params(n_axes, vmem=V7X_VMEM_LIMIT):
    return pltpu.CompilerParams(
        dimension_semantics=("arbitrary",) * n_axes, vmem_limit_bytes=vmem)


def _resident(shape):
    zeros = (0,) * len(shape)
    return pl.BlockSpec(shape, lambda *_: zeros, pipeline_mode=pl.Buffered(1))


def _resident_cols(rows, width, index):
    return pl.BlockSpec((rows, width), lambda *_: (0, index), pipeline_mode=pl.Buffered(1))


def _rms(x, g):
    return x * lax.rsqrt(jnp.mean(x * x, axis=-1, keepdims=True) + EPS) * g


def _split_bf16(x):
    hi = x.astype(BF16)
    return hi, (x - hi.astype(F32)).astype(BF16)


def _gatew_kernel(hm_ref, wk_ref, wqt_ref, g_ref, gt_scr, *, n_batch, n_blk):
    kmean = jnp.dot(hm_ref[...], wk_ref[...], precision=lax.Precision.HIGHEST,
                    preferred_element_type=F32)
    used = MOBA_HEADS * n_blk
    gt_scr[:, used:, :] = jnp.zeros((n_batch, GATE_ROWS - used, gt_scr.shape[-1]), F32)
    for h in range(MOBA_HEADS):
        sl = slice(h * HEAD_DIM, (h + 1) * HEAD_DIM)
        res = jnp.dot(kmean[:, sl], wqt_ref[sl, :], precision=lax.Precision.HIGHEST,
                      preferred_element_type=F32)
        for b in range(n_batch):
            gt_scr[b, h * n_blk:(h + 1) * n_blk, :] = res[b * n_blk:(b + 1) * n_blk, :]
    for b in range(n_batch):
        hi, lo = _split_bf16(gt_scr[b].T)
        g_ref[b, :, 0:GATE_ROWS] = hi
        g_ref[b, :, GATE_ROWS:] = lo


def _gatew(hmean, w_k, w_qt, n_batch):
    rows, d = hmean.shape
    n_blk = rows // n_batch
    assert MOBA_HEADS * n_blk <= GATE_ROWS
    return pl.pallas_call(
        functools.partial(_gatew_kernel, n_batch=n_batch, n_blk=n_blk),
        grid=(1,),
        in_specs=[_resident((rows, d)), _resident(w_k.shape), _resident(w_qt.shape)],
        out_specs=pl.BlockSpec((n_batch, d, 2 * GATE_ROWS), lambda i: (0, 0, 0)),
        out_shape=jax.ShapeDtypeStruct((n_batch, d, 2 * GATE_ROWS), BF16),
        scratch_shapes=[pltpu.VMEM((n_batch, GATE_ROWS, d), F32)],
        compiler_params=_params(1),
        name="gatew",
    )(hmean, w_k, w_qt)


def _qkv_kernel(x_ref, g_ref, wqt_ref, wk_ref, wvt_ref, h_ref, hlo_ref, hmean_ref, qt_ref, k_ref, vt_ref,
                *, n_blk):
    nt = (((1,), (1,)), ((), ()))
    n_slabs = x_ref.shape[0] // QKV_SLAB
    lane =lax.broadcasted_iota(jnp.int32, (QKV_SLAB, V7X_LANES), 1)
    low = lane < HEAD_DIM
    for i in range(n_slabs):
        rows = slice(i * QKV_SLAB, (i + 1) * QKV_SLAB)
        blk = pl.program_id(1) * n_slabs + i
        onehot = jnp.where((lane == HEAD_DIM + blk) | (lane == HEAD_DIM + n_blk + blk), 1.0, 0.0)
        hf = _rms(x_ref[rows, :], g_ref[...])
        h, hlo_ref[rows, :] = _split_bf16(hf)
        h_ref[rows, :] = h
        for c in range(QKV_SLAB // MOBA_BLOCK):
            hmean_ref[i * (QKV_SLAB // MOBA_BLOCK) + c] = jnp.mean(
                hf[c * MOBA_BLOCK:(c + 1) * MOBA_BLOCK], axis=0, keepdims=True)
        qt_ref[0, :, rows] = lax.dot_general(wqt_ref[...], h, nt, preferred_element_type=F32).astype(BF16)
        vt_ref[0, :, rows] = lax.dot_general(wvt_ref[...], h, nt, preferred_element_type=F32).astype(BF16)
        kf = jnp.dot(h, wk_ref[...], preferred_element_type=F32)
        for p in range(MOBA_HEADS // 2):
            pair = kf[:, p * V7X_LANES:(p + 1) * V7X_LANES]
            even = jnp.where(low, pair, onehot)
            odd = jnp.where(low, pltpu.roll(pair, HEAD_DIM, 1), onehot)
            k_ref[0, rows, (2 * p) * V7X_LANES:(2 * p + 1) * V7X_LANES] = even.astype(BF16)
            k_ref[0, rows, (2 * p + 1) * V7X_LANES:(2 * p + 2) * V7X_LANES] = odd.astype(BF16)


def _qkv(x2, g, b, s, w_qt, w_in, k_col, w_vt, tm=2 * TOKEN_TILE):
    d = x2.shape[-1]
    assert k_col % MOBA_WIDTH == 0
    nt = s // tm
    blocks = tm // MOBA_BLOCK
    n_blk = s // MOBA_BLOCK
    assert QKV_SLAB == MOBA_BLOCK and HEAD_DIM + 2 * n_blk <= V7X_LANES
    tok = lambda i, j: (i * nt + j, 0)
    return pl.pallas_call(
        functools.partial(_qkv_kernel, n_blk=n_blk),
        grid=(b, nt),
        in_specs=[pl.BlockSpec((tm, d), tok), _resident((1, d)),
                  _resident(w_qt.shape), _resident_cols(d, MOBA_WIDTH, k_col // MOBA_WIDTH),
                  _resident(w_vt.shape)],
        out_specs=[pl.BlockSpec((tm, d), tok), pl.BlockSpec((tm, d), tok),
                   pl.BlockSpec((blocks, 1, d), lambda i, j: (i * nt + j, 0, 0)),
                   pl.BlockSpec((1, MOBA_WIDTH, tm), lambda i, j: (i, 0, j)),
                   pl.BlockSpec((1, tm, MOBA_HEADS * V7X_LANES), lambda i, j: (i, j, 0)),
                   pl.BlockSpec((1, MOBA_WIDTH, tm), lambda i, j: (i, 0, j))],
        out_shape=[jax.ShapeDtypeStruct((b * s, d), BF16), jax.ShapeDtypeStruct((b * s, d), BF16),
                   jax.ShapeDtypeStruct((b * s // MOBA_BLOCK, 1, d), F32),
                   jax.ShapeDtypeStruct((b, MOBA_WIDTH, s), BF16),
                   jax.ShapeDtypeStruct((b, s, MOBA_HEADS * V7X_LANES), BF16),
                   jax.ShapeDtypeStruct((b, MOBA_WIDTH, s), BF16)],
        compiler_params=_params(2),
        name="qkv",
    )(x2, g, w_qt, w_in, w_vt)


def _bias_kernel(rb_ref, out_ref):
    h = pl.program_id(0)
    span = 2 * MOBA_BLOCK
    u = lax.broadcasted_iota(jnp.int32, (8, span), 1)
    max_exact = REL_BUCKETS // 2
    for t in range(2):
        dist = u if t == 0 else jnp.where(u < MOBA_BLOCK, u + MOBA_BLOCK, u - MOBA_BLOCK)
        n = jnp.maximum(dist, 0)
        nf = jnp.maximum(n, 1).astype(F32)
        large = max_exact + (jnp.log(nf / max_exact) / math.log(REL_MAX_DIST / max_exact)
                             * (REL_BUCKETS - max_exact)).astype(jnp.int32)
        large = jnp.minimum(large, REL_BUCKETS - 1)
        bucket = jnp.where(n < max_exact, n, large)
        val = jnp.zeros((8, span), F32)
        for bk in range(REL_BUCKETS):
            val = jnp.where(bucket == bk, rb_ref[bk * MOBA_HEADS + h], val)
        val = val * LOG2E
        if t == 0:
            val = jnp.where(u < MOBA_BLOCK, val, NEG)
        table = jnp.concatenate([val] * (MOBA_BLOCK // 8), axis=0)
        tile = pltpu.roll(table, 0, 1, stride=1, stride_axis=0)
        out_ref[0, t] = tile[:, 0:MOBA_BLOCK]


def _bias_tiles(rb_flat):
    return pl.pallas_call(
        _bias_kernel,
        grid=(MOBA_HEADS,),
        in_specs=[pl.BlockSpec(memory_space=pltpu.SMEM)],
        out_specs=pl.BlockSpec((1, 2, MOBA_BLOCK, MOBA_BLOCK), lambda h: (h, 0, 0, 0)),
        out_shape=jax.ShapeDtypeStruct((MOBA_HEADS, 2, MOBA_BLOCK, MOBA_BLOCK), F32),
        compiler_params=_params(1),
        name="bias",
    )(rb_flat)


def _zero_after(x):
    bits = lax.bitcast_convert_type(x, jnp.uint32)
    sixteen = jnp.uint32(16)
    return lax.bitcast_convert_type(
        lax.shift_right_logical(lax.shift_right_logical(bits, sixteen), sixteen), F32)


def _attn_kernel(qt_ref, qadd_ref, k_ref, vt_ref, bias_ref, o_ref, s_scr, *, n_blk):
    pad = jnp.zeros((V7X_LANES - HEAD_DIM - 2 * n_blk, MOBA_BLOCK), BF16)

    def scores(hh, j, buf):
        cols = slice(j * MOBA_BLOCK, (j + 1) * MOBA_BLOCK)
        q = qt_ref[0, hh * HEAD_DIM:(hh + 1) * HEAD_DIM, cols]
        qz = jnp.concatenate([q, qadd_ref[0, hh, :, cols], pad], axis=0)
        k = k_ref[0, 0:(j + 1) * MOBA_BLOCK, hh * V7X_LANES:(hh + 1) * V7X_LANES]
        s = jnp.dot(k, qz, preferred_element_type=F32)
        gates, m8 = [], None
        for n in range(j + 1):
            rows = slice(n * MOBA_BLOCK, (n + 1) * MOBA_BLOCK)
            if n == j:
                sn = s[rows] + bias_ref[hh, 0]
            elif n == j - 1:
                sn = s[rows] + bias_ref[hh, 1]
            else:
                sn = s[rows]
            s_scr[buf, rows, :] = sn
            gates.append(_zero_after(sn[0:1, :]))
            part = jnp.max(sn.reshape(MOBA_BLOCK // 8, 8, MOBA_BLOCK), axis=0)
            m8 = part if m8 is None else jnp.maximum(m8, part)
        return gates, m8

    def finish(hh, j, buf, m8, gates):
        cols = slice(j * MOBA_BLOCK, (j + 1) * MOBA_BLOCK)
        heads = slice(hh * HEAD_DIM, (hh + 1) * HEAD_DIM)
        m = jnp.max(m8, axis=0, keepdims=True)
        ones = jnp.ones((ONES_ROWS, MOBA_BLOCK), BF16)
        acc = None
        for n in range(j + 1):
            rows = slice(n * MOBA_BLOCK, (n + 1) * MOBA_BLOCK)
            mn = m + gates[min(n, len(gates) - 1)] if gates else m
            p = jnp.exp2(s_scr[buf, rows, :] - mn).astype(BF16)
            part = jnp.dot(jnp.concatenate([vt_ref[0, heads, rows], ones], axis=0), p,
                           preferred_element_type=F32)
            acc = part if acc is None else acc + part
        o_ref[0, heads, cols] = (acc[0:HEAD_DIM] / acc[HEAD_DIM:HEAD_DIM + 1]).astype(BF16)

    tasks = [(hh, j) for j in reversed(range(n_blk)) for hh in range(ATTN_HEADS)]
    pending, issued = [], []
    for i, (hh, j) in enumerate(tasks):
        buf = i % SCORE_BUFFERS
        issued.append(scores(hh, j, buf))
        pending.append((hh, j, buf, issued[i][1]))
        if len(pending) > SCORE_LOOKAHEAD:
            finish(*pending.pop(0), issued[i - GATE_BACK][0])
    for item in pending:
        finish(*item, None)


def _attn(qt, qadd, k, vt, bias):
    b, _, s = qt.shape
    n_blk = s // MOBA_BLOCK
    nh = ATTN_HEADS
    return pl.pallas_call(
        functools.partial(_attn_kernel, n_blk=n_blk),
        grid=(MOBA_HEADS // nh, b),
        in_specs=[pl.BlockSpec((1, nh * HEAD_DIM, s), lambda h, i: (i, h, 0)),
                  pl.BlockSpec((1, nh, 2 * n_blk, s), lambda h, i: (i, h, 0, 0)),
                  pl.BlockSpec((1, s, nh * V7X_LANES), lambda h, i: (i, 0, h)),
                  pl.BlockSpec((1, nh * HEAD_DIM, s), lambda h, i: (i, h, 0)),
                  pl.BlockSpec((nh, 2, MOBA_BLOCK, MOBA_BLOCK), lambda h, i: (h, 0, 0, 0))],
        out_specs=pl.BlockSpec((1, nh * HEAD_DIM, s), lambda h, i: (i, h, 0)),
        out_shape=jax.ShapeDtypeStruct((b, MOBA_WIDTH, s), BF16),
        scratch_shapes=[pltpu.VMEM((SCORE_BUFFERS, s, MOBA_BLOCK), F32)],
        compiler_params=_params(2),
        name="attn",
    )(qt, qadd, k, vt, bias)


def _branches_kernel(h_ref, hlo_ref, wu_ref, wv_ref, lng_ref, lnb_ref, ws_ref, bs_ref, wq_ref, kt_ref, vm_ref,
                     gw_ref, far_ref, a_ref, c_ref, qadd_ref, *, n_blk):
    causal = (lax.broadcasted_iota(jnp.int32, (CHUNK, CHUNK), 0)
              >= lax.broadcasted_iota(jnp.int32, (CHUNK, CHUNK), 1))
    ws = [jnp.where(causal, ws_ref[g], 0.0).astype(BF16) for g in range(GMLP_GROUPS)]
    ones = jnp.ones((vm_ref.shape[1], MEM_HEAD_DIM), BF16)
    far = jnp.concatenate([far_ref[...]] * (MOBA_BLOCK // V7X_LANES), axis=-1)
    n_slabs = h_ref.shape[0] // MOBA_BLOCK

    def project(i):
        rows = slice(i * MOBA_BLOCK, (i + 1) * MOBA_BLOCK)
        h = h_ref[rows, :]
        cq = jnp.dot(h, wq_ref[...], preferred_element_type=F32).astype(BF16)
        v = jnp.dot(h, wv_ref[...], preferred_element_type=F32)
        u = jnp.dot(h, wu_ref[...], preferred_element_type=F32)
        res = jnp.dot(jnp.concatenate([h, hlo_ref[rows, :]], axis=0), gw_ref[0], preferred_element_type=F32)
        ps = []
        for hd in range(MEM_HEADS):
            sl = slice(hd * MEM_HEAD_DIM, (hd + 1) * MEM_HEAD_DIM)
            s = jnp.dot(cq[:, sl], kt_ref[0, sl, :], preferred_element_type=F32) * (MEM_HEAD_DIM ** -0.5 * LOG2E)
            ps.append(jnp.exp2(s - jnp.max(s, axis=-1, keepdims=True)).astype(BF16))
        return rows, v, u, ps, res

    def finish(i, rows, v, u, ps, res):
        v = jax.nn.gelu(v)
        vc = v - jnp.mean(v, axis=-1, keepdims=True)
        vn = vc * lax.rsqrt(jnp.mean(vc * vc, axis=-1, keepdims=True) + EPS) * lng_ref[...] + lnb_ref[...]
        vn = vn.astype(BF16)
        u = jax.nn.gelu(u)
        chunks = [slice(c * CHUNK, (c + 1) * CHUNK) for c in range(MOBA_BLOCK // CHUNK)]
        for g in range(GMLP_GROUPS):
            lanes = slice(g * GMLP_GROUP_DIM, (g + 1) * GMLP_GROUP_DIM)
            mixed = jnp.dot(ws[g], jnp.concatenate([vn[toks, lanes] for toks in chunks], axis=1),
                            preferred_element_type=F32)
            for c, toks in enumerate(chunks):
                mc = mixed[:, c * GMLP_GROUP_DIM:(c + 1) * GMLP_GROUP_DIM] + bs_ref[g]
                a_ref[rows.start + c * CHUNK:rows.start + (c + 1) * CHUNK, lanes] = (
                    u[toks, lanes] * mc).astype(BF16)
        for hd, p in enumerate(ps):
            sl = slice(hd * MEM_HEAD_DIM, (hd + 1) * MEM_HEAD_DIM)
            acc = jnp.dot(p, jnp.concatenate([vm_ref[0, :, sl], ones], axis=1), preferred_element_type=F32)
            c_ref[rows, sl] = (acc[:, :MEM_HEAD_DIM] / acc[:, MEM_HEAD_DIM:]).astype(BF16)
        cur = pl.program_id(1) * n_slabs + i
        hi, lo = res[:MOBA_BLOCK], res[MOBA_BLOCK:]
        gate = (hi[:, :GATE_ROWS] + (hi[:, GATE_ROWS:] + lo[:, :GATE_ROWS])) + lo[:, GATE_ROWS:]
        gate = gate.T[0:MOBA_HEADS * n_blk].reshape(MOBA_HEADS, n_blk, MOBA_BLOCK)
        blk = lax.broadcasted_iota(jnp.int32, gate.shape, 1)
        past = blk < cur
        gate = jnp.where(past, gate, NEG)
        rank = jnp.zeros(gate.shape, jnp.int32)
        for m in range(n_blk):
            gm = gate[:, m:m + 1, :]
            tie = jnp.where(blk > m, 1, 0)
            rank = rank + jnp.where(gm > gate, 1, jnp.where(gm == gate, tie, 0))
        mask = jnp.where(rank < min(MOBA_TOPK, n_blk), jnp.where(past, 0.0, NEG), NEG)
        add = jnp.where(blk < cur - 1, mask + far, jnp.where(blk == cur - 1, mask, 0.0))
        add_hi, add_lo = _split_bf16(add)
        qadd_ref[0, :, :, rows] = jnp.concatenate(
            [add_hi.astype(F32), add_lo.astype(F32)], axis=1).astype(BF16)

    staged = project(0)
    for i in range(n_slabs):
        nxt = project(i + 1) if i + 1 < n_slabs else None
        finish(i, *staged)
        staged = nxt


def _branches(h_hi, h_lo, b, s, w_in, u_col, v_col, ln_g, ln_b, w_s, b_s_lanes, w_cq, kt, vm, gt, far_lanes,
              tm=2 * TOKEN_TILE):
    d = h_hi.shape[-1]
    assert u_col % GMLP_WIDTH == 0 and v_col % GMLP_WIDTH == 0
    n_blk = s // MOBA_BLOCK
    nt = s // tm
    m = kt.shape[-1]
    tok = lambda i, j: (i * nt + j, 0)
    return pl.pallas_call(
        functools.partial(_branches_kernel, n_blk=n_blk),
        grid=(b, nt),
        in_specs=[pl.BlockSpec((tm, d), tok), pl.BlockSpec((tm, d), tok),
                  _resident_cols(d, GMLP_WIDTH, u_col // GMLP_WIDTH),
                  _resident_cols(d, GMLP_WIDTH, v_col // GMLP_WIDTH),
                  _resident(ln_g.shape), _resident(ln_b.shape),
                  _resident(w_s.shape), _resident(b_s_lanes.shape), _resident(w_cq.shape),
                  pl.BlockSpec((1, MEM_WIDTH, m), lambda i, j: (i, 0, 0)),
                  pl.BlockSpec((1, m, MEM_WIDTH), lambda i, j: (i, 0, 0)),
                  pl.BlockSpec((1, d, 2 * GATE_ROWS), lambda i, j: (i, 0, 0)),
                  _resident(far_lanes.shape)],
        out_specs=[pl.BlockSpec((tm, GMLP_WIDTH), tok), pl.BlockSpec((tm, MEM_WIDTH), tok),
                   pl.BlockSpec((1, MOBA_HEADS, 2 * n_blk, tm), lambda i, j: (i, 0, 0, j))],
        out_shape=[jax.ShapeDtypeStruct((b * s, GMLP_WIDTH), BF16),
                   jax.ShapeDtypeStruct((b * s, MEM_WIDTH), BF16),
                   jax.ShapeDtypeStruct((b, MOBA_HEADS, 2 * n_blk, s), BF16)],
        compiler_params=_params(2),
        name="branches",
    )(h_hi, h_lo, w_in, w_in, ln_g, ln_b, w_s, b_s_lanes, w_cq, kt, vm, gt, far_lanes)


def _memkv_kernel(mem_ref, g_ref, wkt_ref, wv_ref, kt_ref, v_ref):
    for i in range(mem_ref.shape[0]):
        mn = _rms(mem_ref[i], g_ref[...]).astype(BF16)
        kt_ref[i] = lax.dot_general(wkt_ref[...], mn, (((1,), (1,)), ((), ())),
                                    preferred_element_type=F32).astype(BF16)
        v_ref[i] = jnp.dot(mn, wv_ref[...], preferred_element_type=F32).astype(BF16)


def _memkv(mem, g, w_kt, w_v):
    b, m, d = mem.shape
    nb = math.gcd(b, MEM_BATCHES)
    return pl.pallas_call(
        _memkv_kernel,
        grid=(b // nb,),
        in_specs=[pl.BlockSpec((nb, m, d), lambda i: (i, 0, 0)), _resident((1, d)),
                  _resident(w_kt.shape), _resident(w_v.shape)],
        out_specs=[pl.BlockSpec((nb, MEM_WIDTH, m), lambda i: (i, 0, 0)),
                   pl.BlockSpec((nb, m, MEM_WIDTH), lambda i: (i, 0, 0))],
        out_shape=[jax.ShapeDtypeStruct((b, MEM_WIDTH, m), BF16),
                   jax.ShapeDtypeStruct((b, m, MEM_WIDTH), BF16)],
        compiler_params=_params(1),
        name="memkv",
    )(mem, g, w_kt, w_v)


def _merge_kernel(x_ref, h_ref, a_ref, bt_ref, c_ref, wg_ref, wa_ref, wb_ref, wc_ref, wo_ref,
                  lnpost_ref, lnpre_ref, x1_ref, h2_ref):
    d = x_ref.shape[-1]
    for i in range(x_ref.shape[0] // MLP_SLAB):
        rows = slice(i * MLP_SLAB, (i + 1) * MLP_SLAB)
        gates = jax.nn.sigmoid(jnp.dot(h_ref[rows, :], wg_ref[...], preferred_element_type=F32))
        pa = jnp.dot(a_ref[rows, :], wa_ref[...], preferred_element_type=F32)
        pb = lax.dot_general(bt_ref[0, :, rows], wb_ref[...], (((0,), (0,)), ((), ())),
                             preferred_element_type=F32)
        pc = jnp.dot(c_ref[rows, :], wc_ref[...], preferred_element_type=F32)
        merged = gates[:, :d] * pa + gates[:, d:2 * d] * pb + gates[:, 2 * d:] * pc
        mo = jnp.dot(merged.astype(BF16), wo_ref[...], preferred_element_type=F32)
        x1 = x_ref[rows, :] + _rms(mo, lnpost_ref[...])
        x1_ref[rows, :] = x1
        h2_ref[rows, :] = _rms(x1, lnpre_ref[...]).astype(BF16)


def _merge(x2, h2, a_out, b_out_t, c_out, w_g, w_a, w_b, w_c, w_o, ln_post, ln_pre, tm=MLP_TILE):
    n_tok, d = x2.shape
    b, _, s = b_out_t.shape
    nt = s // tm
    tok = lambda i, j: (i * nt + j, 0)
    return pl.pallas_call(
        _merge_kernel,
        grid=(b, nt),
        in_specs=[pl.BlockSpec((tm, d), tok), pl.BlockSpec((tm, d), tok),
                  pl.BlockSpec((tm, GMLP_WIDTH), tok),
                  pl.BlockSpec((1, MOBA_WIDTH, tm), lambda i, j: (i, 0, j)),
                  pl.BlockSpec((tm, MEM_WIDTH), tok),
                  _resident(w_g.shape), _resident(w_a.shape), _resident(w_b.shape),
                  _resident(w_c.shape), _resident(w_o.shape),
                  _resident((1, d)), _resident((1, d))],
        out_specs=[pl.BlockSpec((tm, d), tok), pl.BlockSpec((tm, d), tok)],
        out_shape=[jax.ShapeDtypeStruct((n_tok, d), F32), jax.ShapeDtypeStruct((n_tok, d), BF16)],
        compiler_params=_params(2),
        name="merge",
    )(x2, h2, a_out, b_out_t, c_out, w_g, w_a, w_b, w_c, w_o, ln_post, ln_pre)


def _ffn_kernel(x1_ref, h2_ref, wg_ref, wu_ref, wd_ref, ln_ref, o_ref):
    for i in range(x1_ref.shape[0] // MLP_SLAB):
        rows = slice(i * MLP_SLAB, (i + 1) * MLP_SLAB)
        h2 = h2_ref[rows, :]
        g = jnp.dot(h2, wg_ref[...], preferred_element_type=F32)
        u = jnp.dot(h2, wu_ref[...], preferred_element_type=F32)
        act = (jax.nn.silu(g) * u).astype(BF16)
        f = jnp.dot(act, wd_ref[...], preferred_element_type=F32)
        o_ref[rows, :] = x1_ref[rows, :] + _rms(f, ln_ref[...])


def _ffn(x1, h2, w_g, w_u, w_d, ln_post, tm=MLP_TILE):
    n_tok, d = x1.shape
    return pl.pallas_call(
        _ffn_kernel,
        grid=(n_tok // tm,),
        in_specs=[pl.BlockSpec((tm, d), lambda i: (i, 0)), pl.BlockSpec((tm, d), lambda i: (i, 0)),
                  _resident(w_g.shape), _resident(w_u.shape), _resident(w_d.shape),
                  _resident((1, d))],
        out_specs=pl.BlockSpec((tm, d), lambda i: (i, 0)),
        out_shape=jax.ShapeDtypeStruct((n_tok, d), F32),
        compiler_params=_params(1),
        name="ffn",
    )(x1, h2, w_g, w_u, w_d, ln_post)


def kernel(x, mem, ln_mix_pre, ln_mix_post, ln_ffn_pre, ln_ffn_post, ln_mem, w_in, ln_v_gain, ln_v_bias,
           w_spatial, b_spatial, rel_bias, w_mem_kv, w_branch_a, w_branch_b, w_branch_c, w_out,
           w_ffn_gate, w_ffn_up, w_ffn_down):
    b, s, d = x.shape
    assert s % MOBA_BLOCK == 0 and s % TOKEN_TILE == 0 and d % V7X_LANES == 0
    depth = w_in.shape[0]
    cuts = [0, GMLP_WIDTH, 2 * GMLP_WIDTH, 2 * GMLP_WIDTH + MOBA_WIDTH, 2 * GMLP_WIDTH + 2 * MOBA_WIDTH,
            2 * GMLP_WIDTH + 3 * MOBA_WIDTH, 2 * GMLP_WIDTH + 3 * MOBA_WIDTH + MEM_WIDTH]
    rb_flat = rel_bias.astype(F32).reshape(-1)
    bias = _bias_tiles(rb_flat)
    far_lanes = jnp.broadcast_to((rel_bias[REL_BUCKETS - 1].astype(F32) * LOG2E)[:, None, None],
                                 (MOBA_HEADS, s // MOBA_BLOCK, V7X_LANES))
    row = lambda v: v.reshape(1, -1).astype(F32)
    for l in range(depth):
        wi = w_in[l]
        wi16 = wi.astype(BF16)
        w_q, w_k, w_v2 = (wi[:, cuts[i]:cuts[i + 1]] for i in (2, 3, 4))
        w_cq, w_g = wi16[:, cuts[5]:cuts[6]], wi16[:, cuts[6]:]
        x2 = x.reshape(b * s, d)

        w_qt_scaled = (w_q * (HEAD_DIM ** -0.5)).T
        h2, h_lo, hmean, qt, k, vt = _qkv(x2, row(ln_mix_pre[l]), b, s, (w_qt_scaled * LOG2E).astype(BF16),
                                          wi16, cuts[3], w_v2.T.astype(BF16))
        gt = _gatew(hmean.reshape(-1, d), w_k, w_qt_scaled, b)
        wkv = w_mem_kv[l]
        kt_mem, v_mem = _memkv(mem, row(ln_mem[l]), wkv[:, :MEM_WIDTH].T.astype(BF16),
                               wkv[:, MEM_WIDTH:].astype(BF16))
        b_s_lanes = jnp.broadcast_to(b_spatial[l][:, :, None], (GMLP_GROUPS, CHUNK, GMLP_GROUP_DIM))
        a_out, c_out, qadd = _branches(
            h2, h_lo, b, s, wi16, cuts[0], cuts[1], row(ln_v_gain[l]), row(ln_v_bias[l]),
            w_spatial[l], b_s_lanes.astype(F32), w_cq, kt_mem, v_mem, gt, far_lanes)
        b_out_t = _attn(qt, qadd, k, vt, bias)

        x1, hn = _merge(x2, h2, a_out, b_out_t, c_out, w_g, w_branch_a[l].astype(BF16),
                        w_branch_b[l].astype(BF16), w_branch_c[l].astype(BF16), w_out[l].astype(BF16),
                        row(ln_mix_post[l]), row(ln_ffn_pre[l]))
        out = _ffn(x1, hn, w_ffn_gate[l].astype(BF16), w_ffn_up[l].astype(BF16),
                   w_ffn_down[l].astype(BF16), row(ln_ffn_post[l]))
        x = out.reshape(b, s, d)
    return x
```

```python
import functools
import math

import jax
import jax.numpy as jnp
from jax import lax
from jax.experimental import pallas as pl
from jax.experimental.pallas import tpu as pltpu

F32 = jnp.float32
BF16 = jnp.bfloat16

EPS = 1e-6
NEG = -1e30
GMLP_GROUPS = 6
GMLP_GROUP_DIM = 128
GMLP_WIDTH = GMLP_GROUPS * GMLP_GROUP_DIM
CHUNK = 128
MOBA_HEADS = 12
HEAD_DIM = 64
MOBA_WIDTH = MOBA_HEADS * HEAD_DIM
MOBA_BLOCK = 256
MOBA_TOPK = 3
REL_BUCKETS = 32
REL_MAX_DIST = 128
LOG2E = math.log2(math.e)
ONES_ROWS = 16
ATTN_HEADS = 6
SCORE_LOOKAHEAD = 3
SCORE_BUFFERS = SCORE_LOOKAHEAD + 1
GATE_BACK = 1
MEM_HEADS = 4
MEM_HEAD_DIM = 128
MEM_WIDTH = MEM_HEADS * MEM_HEAD_DIM
MEM_BATCHES = 4
N_BRANCHES = 3

V7X_LANES = 128
V7X_VMEM_LIMIT = 56 * 1024 * 1024

TOKEN_TILE = 512
MLP_TILE = 1024
MLP_SLAB = 256
QKV_SLAB = 256
GATE_ROWS = 128


def _params(n_axes, vmem=V7X_VMEM_LIMIT):
    return pltpu.CompilerParams(
        dimension_semantics=("arbitrary",) * n_axes, vmem_limit_bytes=vmem)


def _resident(shape):
    zeros = (0,) * len(shape)
    return pl.BlockSpec(shape, lambda *_: zeros, pipeline_mode=pl.Buffered(1))


def _resident_cols(rows, width, index):
    return pl.BlockSpec((rows, width), lambda *_: (0, index), pipeline_mode=pl.Buffered(1))


def _rms(x, g):
    return x * lax.rsqrt(jnp.mean(x * x, axis=-1, keepdims=True) + EPS) * g


def _split_bf16(x):
    hi = x.astype(BF16)
    return hi, (x - hi.astype(F32)).astype(BF16)


def _gatew_kernel(hm_ref, wk_ref, wqt_ref, g_ref, gt_scr, *, n_batch, n_blk):
    kmean = jnp.dot(hm_ref[...], wk_ref[...], precision=lax.Precision.HIGHEST,
                    preferred_element_type=F32)
    used = MOBA_HEADS * n_blk
    gt_scr[:, used:, :] = jnp.zeros((n_batch, GATE_ROWS - used, gt_scr.shape[-1]), F32)
    for h in range(MOBA_HEADS):
        sl = slice(h * HEAD_DIM, (h + 1) * HEAD_DIM)
        res = jnp.dot(kmean[:, sl], wqt_ref[sl, :], precision=lax.Precision.HIGHEST,
                      preferred_element_type=F32)
        for b in range(n_batch):
            gt_scr[b, h * n_blk:(h + 1) * n_blk, :] = res[b * n_blk:(b + 1) * n_blk, :]
    for b in range(n_batch):
        hi, lo = _split_bf16(gt_scr[b].T)
        g_ref[b, :, 0:GATE_ROWS] = hi
        g_ref[b, :, GATE_ROWS:] = lo


def _gatew(hmean, w_k, w_qt, n_batch):
    rows, d = hmean.shape
    n_blk = rows // n_batch
    assert MOBA_HEADS * n_blk <= GATE_ROWS
    return pl.pallas_call(
        functools.partial(_gatew_kernel, n_batch=n_batch, n_blk=n_blk),
        grid=(1,),
        in_specs=[_resident((rows, d)), _resident(w_k.shape), _resident(w_qt.shape)],
        out_specs=pl.BlockSpec((n_batch, d, 2 * GATE_ROWS), lambda i: (0, 0, 0)),
        out_shape=jax.ShapeDtypeStruct((n_batch, d, 2 * GATE_ROWS), BF16),
        scratch_shapes=[pltpu.VMEM((n_batch, GATE_ROWS, d), F32)],
        compiler_params=_params(1),
        name="gatew",
    )(hmean, w_k, w_qt)


def _qkv_kernel(x_ref, g_ref, wqt_ref, wk_ref, wvt_ref, h_ref, hlo_ref, hmean_ref, qt_ref, k_ref, vt_ref,
                *, n_blk):
    nt = (((1,), (1,)), ((), ()))
    n_slabs = x_ref.shape[0] // QKV_SLAB
    lane = lax.broadcasted_iota(jnp.int32, (QKV_SLAB, V7X_LANES), 1)
    low = lane < HEAD_DIM
    for i in range(n_slabs):
        rows = slice(i * QKV_SLAB, (i + 1) * QKV_SLAB)
        blk = pl.program_id(1) * n_slabs + i
        onehot = jnp.where((lane == HEAD_DIM + blk) | (lane == HEAD_DIM + n_blk + blk), 1.0, 0.0)
        hf = _rms(x_ref[rows, :], g_ref[...])
        h, hlo_ref[rows, :] = _split_bf16(hf)
        h_ref[rows, :] = h
        for c in range(QKV_SLAB // MOBA_BLOCK):
            hmean_ref[i * (QKV_SLAB // MOBA_BLOCK) + c] = jnp.mean(
                hf[c * MOBA_BLOCK:(c + 1) * MOBA_BLOCK], axis=0, keepdims=True)
        qt_ref[0, :, rows] = lax.dot_general(wqt_ref[...], h, nt, preferred_element_type=F32).astype(BF16)
        vt_ref[0, :, rows] = lax.dot_general(wvt_ref[...], h, nt, preferred_element_type=F32).astype(BF16)
        kf = jnp.dot(h, wk_ref[...], preferred_element_type=F32)
        for p in range(MOBA_HEADS // 2):
            pair = kf[:, p * V7X_LANES:(p + 1) * V7X_LANES]
            even = jnp.where(low, pair, onehot)
            odd = jnp.where(low, pltpu.roll(pair, HEAD_DIM, 1), onehot)
            k_ref[0, rows, (2 * p) * V7X_LANES:(2 * p + 1) * V7X_LANES] = even.astype(BF16)
            k_ref[0, rows, (2 * p + 1) * V7X_LANES:(2 * p + 2) * V7X_LANES] = odd.astype(BF16)


def _qkv(x2, g, b, s, w_qt, w_in, k_col, w_vt, tm=2 * TOKEN_TILE):
    d = x2.shape[-1]
    assert k_col % MOBA_WIDTH == 0
    nt = s // tm
    blocks = tm // MOBA_BLOCK
    n_blk = s // MOBA_BLOCK
    assert QKV_SLAB == MOBA_BLOCK and HEAD_DIM + 2 * n_blk <= V7X_LANES
    tok = lambda i, j: (i * nt + j, 0)
    return pl.pallas_call(
        functools.partial(_qkv_kernel, n_blk=n_blk),
        grid=(b, nt),
        in_specs=[pl.BlockSpec((tm, d), tok), _resident((1, d)),
                  _resident(w_qt.shape), _resident_cols(d, MOBA_WIDTH, k_col // MOBA_WIDTH),
                  _resident(w_vt.shape)],
        out_specs=[pl.BlockSpec((tm, d), tok), pl.BlockSpec((tm, d), tok),
                   pl.BlockSpec((blocks, 1, d), lambda i, j: (i * nt + j, 0, 0)),
                   pl.BlockSpec((1, MOBA_WIDTH, tm), lambda i, j: (i, 0, j)),
                   pl.BlockSpec((1, tm, MOBA_HEADS * V7X_LANES), lambda i, j: (i, j, 0)),
                   pl.BlockSpec((1, MOBA_WIDTH, tm), lambda i, j: (i, 0, j))],
        out_shape=[jax.ShapeDtypeStruct((b * s, d), BF16), jax.ShapeDtypeStruct((b * s, d), BF16),
                   jax.ShapeDtypeStruct((b * s // MOBA_BLOCK, 1, d), F32),
                   jax.ShapeDtypeStruct((b, MOBA_WIDTH, s), BF16),
                   jax.ShapeDtypeStruct((b, s, MOBA_HEADS * V7X_LANES), BF16),
                   jax.ShapeDtypeStruct((b, MOBA_WIDTH, s), BF16)],
        compiler_params=_params(2),
        name="qkv",
    )(x2, g, w_qt, w_in, w_vt)


def _bias_kernel(rb_ref, out_ref):
    h = pl.program_id(0)
    span = 2 * MOBA_BLOCK
    u = lax.broadcasted_iota(jnp.int32, (8, span), 1)
    max_exact = REL_BUCKETS // 2
    for t in range(2):
        dist = u if t == 0 else jnp.where(u < MOBA_BLOCK, u + MOBA_BLOCK, u - MOBA_BLOCK)
        n = jnp.maximum(dist, 0)
        nf = jnp.maximum(n, 1).astype(F32)
        large = max_exact + (jnp.log(nf / max_exact) / math.log(REL_MAX_DIST / max_exact)
                             * (REL_BUCKETS - max_exact)).astype(jnp.int32)
        large = jnp.minimum(large, REL_BUCKETS - 1)
        bucket = jnp.where(n < max_exact, n, large)
        val = jnp.zeros((8, span), F32)
        for bk in range(REL_BUCKETS):
            val = jnp.where(bucket == bk, rb_ref[bk * MOBA_HEADS + h], val)
        val = val * LOG2E
        if t == 0:
            val = jnp.where(u < MOBA_BLOCK, val, NEG)
        table = jnp.concatenate([val] * (MOBA_BLOCK // 8), axis=0)
        tile = pltpu.roll(table, 0, 1, stride=1, stride_axis=0)
        out_ref[0, t] = tile[:, 0:MOBA_BLOCK]


def _bias_tiles(rb_flat):
    return pl.pallas_call(
        _bias_kernel,
        grid=(MOBA_HEADS,),
        in_specs=[pl.BlockSpec(memory_space=pltpu.SMEM)],
        out_specs=pl.BlockSpec((1, 2, MOBA_BLOCK, MOBA_BLOCK), lambda h: (h, 0, 0, 0)),
        out_shape=jax.ShapeDtypeStruct((MOBA_HEADS, 2, MOBA_BLOCK, MOBA_BLOCK), F32),
        compiler_params=_params(1),
        name="bias",
    )(rb_flat)


def _zero_after(x):
    bits = lax.bitcast_convert_type(x, jnp.uint32)
    sixteen = jnp.uint32(16)
    return lax.bitcast_convert_type(
        lax.shift_right_logical(lax.shift_right_logical(bits, sixteen), sixteen), F32)


def _attn_kernel(qt_ref, qadd_ref, k_ref, vt_ref, bias_ref, o_ref, s_scr, *, n_blk):
    pad = jnp.zeros((V7X_LANES - HEAD_DIM - 2 * n_blk, MOBA_BLOCK), BF16)

    def scores(hh, j, buf):
        cols = slice(j * MOBA_BLOCK, (j + 1) * MOBA_BLOCK)
        q = qt_ref[0, hh * HEAD_DIM:(hh + 1) * HEAD_DIM, cols]
        qz = jnp.concatenate([q, qadd_ref[0, hh, :, cols], pad], axis=0)
        k = k_ref[0, 0:(j + 1) * MOBA_BLOCK, hh * V7X_LANES:(hh + 1) * V7X_LANES]
        s = jnp.dot(k, qz, preferred_element_type=F32)
        gates = []
        for n in range(j + 1):
            rows = slice(n * MOBA_BLOCK, (n + 1) * MOBA_BLOCK)
            if n == j:
                sn = s[rows] + bias_ref[hh, 0]
            elif n == j - 1:
                sn = s[rows] + bias_ref[hh, 1]
            else:
                sn = s[rows]
            s_scr[buf, rows, :] = sn
            gates.append(_zero_after(sn[0:1, :]))
        return gates

    def finish(hh, j, buf, gates):
        cols = slice(j * MOBA_BLOCK, (j + 1) * MOBA_BLOCK)
        heads = slice(hh * HEAD_DIM, (hh + 1) * HEAD_DIM)
        m = jnp.max(s_scr[buf, 0:(j + 1) * MOBA_BLOCK, :], axis=0, keepdims=True)
        ones = jnp.ones((ONES_ROWS, MOBA_BLOCK), BF16)
        acc = None
        for n in range(j + 1):
            rows = slice(n * MOBA_BLOCK, (n + 1) * MOBA_BLOCK)
            mn = m + gates[min(n, len(gates) - 1)] if gates else m
            p = jnp.exp2(s_scr[buf, rows, :] - mn).astype(BF16)
            part = jnp.dot(jnp.concatenate([vt_ref[0, heads, rows], ones], axis=0), p,
                           preferred_element_type=F32)
            acc = part if acc is None else acc + part
        o_ref[0, heads, cols] = (acc[0:HEAD_DIM] / acc[HEAD_DIM:HEAD_DIM + 1]).astype(BF16)

    tasks = [(hh, j) for j in reversed(range(n_blk)) for hh in range(ATTN_HEADS)]
    pending, issued = [], []
    for i, (hh, j) in enumerate(tasks):
        buf = i % SCORE_BUFFERS
        issued.append(scores(hh, j, buf))
        pending.append((hh, j, buf))
        if len(pending) > SCORE_LOOKAHEAD:
            finish(*pending.pop(0), issued[i - GATE_BACK])
    for item in pending:
        finish(*item, None)


def _attn(qt, qadd, k, vt, bias):
    b, _, s = qt.shape
    n_blk = s // MOBA_BLOCK
    nh = ATTN_HEADS
    return pl.pallas_call(
        functools.partial(_attn_kernel, n_blk=n_blk),
        grid=(MOBA_HEADS // nh, b),
        in_specs=[pl.BlockSpec((1, nh * HEAD_DIM, s), lambda h, i: (i, h, 0)),
                  pl.BlockSpec((1, nh, 2 * n_blk, s), lambda h, i: (i, h, 0, 0)),
                  pl.BlockSpec((1, s, nh * V7X_LANES), lambda h, i: (i, 0, h)),
                  pl.BlockSpec((1, nh * HEAD_DIM, s), lambda h, i: (i, h, 0)),
                  pl.BlockSpec((nh, 2, MOBA_BLOCK, MOBA_BLOCK), lambda h, i: (h, 0, 0, 0))],
        out_specs=pl.BlockSpec((1, nh * HEAD_DIM, s), lambda h, i: (i, h, 0)),
        out_shape=jax.ShapeDtypeStruct((b, MOBA_WIDTH, s), BF16),
        scratch_shapes=[pltpu.VMEM((SCORE_BUFFERS, s, MOBA_BLOCK), F32)],
        compiler_params=_params(2),
        name="attn",
    )(qt, qadd, k, vt, bias)


def _branches_kernel(h_ref, hlo_ref, wu_ref, wv_ref, lng_ref, lnb_ref, ws_ref, bs_ref, wq_ref, kt_ref, vm_ref,
                     gw_ref, far_ref, a_ref, c_ref, qadd_ref, *, n_blk):
    causal = (lax.broadcasted_iota(jnp.int32, (CHUNK, CHUNK), 0)
              >= lax.broadcasted_iota(jnp.int32, (CHUNK, CHUNK), 1))
    ws = [jnp.where(causal, ws_ref[g], 0.0).astype(BF16) for g in range(GMLP_GROUPS)]
    ones = jnp.ones((vm_ref.shape[1], MEM_HEAD_DIM), BF16)
    far = jnp.concatenate([far_ref[...]] * (MOBA_BLOCK // V7X_LANES), axis=-1)
    n_slabs = h_ref.shape[0] // MOBA_BLOCK

    def project(i):
        rows = slice(i * MOBA_BLOCK, (i + 1) * MOBA_BLOCK)
        h = h_ref[rows, :]
        cq = jnp.dot(h, wq_ref[...], preferred_element_type=F32).astype(BF16)
        v = jnp.dot(h, wv_ref[...], preferred_element_type=F32)
        u = jnp.dot(h, wu_ref[...], preferred_element_type=F32)
        res = jnp.dot(jnp.concatenate([h, hlo_ref[rows, :]], axis=0), gw_ref[0], preferred_element_type=F32)
        ps = []
        for hd in range(MEM_HEADS):
            sl = slice(hd * MEM_HEAD_DIM, (hd + 1) * MEM_HEAD_DIM)
            s = jnp.dot(cq[:, sl], kt_ref[0, sl, :], preferred_element_type=F32) * (MEM_HEAD_DIM ** -0.5 * LOG2E)
            ps.append(jnp.exp2(s - jnp.max(s, axis=-1, keepdims=True)).astype(BF16))
        return rows, v, u, ps, res

    def finish(i, rows, v, u, ps, res):
        v = jax.nn.gelu(v)
        vc = v - jnp.mean(v, axis=-1, keepdims=True)
        vn = vc * lax.rsqrt(jnp.mean(vc * vc, axis=-1, keepdims=True) + EPS) * lng_ref[...] + lnb_ref[...]
        vn = vn.astype(BF16)
        u = jax.nn.gelu(u)
        chunks = [slice(c * CHUNK, (c + 1) * CHUNK) for c in range(MOBA_BLOCK // CHUNK)]
        for g in range(GMLP_GROUPS):
            lanes = slice(g * GMLP_GROUP_DIM, (g + 1) * GMLP_GROUP_DIM)
            mixed = jnp.dot(ws[g], jnp.concatenate([vn[toks, lanes] for toks in chunks], axis=1),
                            preferred_element_type=F32)
            for c, toks in enumerate(chunks):
                mc = mixed[:, c * GMLP_GROUP_DIM:(c + 1) * GMLP_GROUP_DIM] + bs_ref[g]
                a_ref[rows.start + c * CHUNK:rows.start + (c + 1) * CHUNK, lanes] = (
                    u[toks, lanes] * mc).astype(BF16)
        for hd, p in enumerate(ps):
            sl = slice(hd * MEM_HEAD_DIM, (hd + 1) * MEM_HEAD_DIM)
            acc = jnp.dot(p, jnp.concatenate([vm_ref[0, :, sl], ones], axis=1), preferred_element_type=F32)
            c_ref[rows, sl] = (acc[:, :MEM_HEAD_DIM] / acc[:, MEM_HEAD_DIM:]).astype(BF16)
        cur = pl.program_id(1) * n_slabs + i
        hi, lo = res[:MOBA_BLOCK], res[MOBA_BLOCK:]
        gate = (hi[:, :GATE_ROWS] + (hi[:, GATE_ROWS:] + lo[:, :GATE_ROWS])) + lo[:, GATE_ROWS:]
        gate = gate.T[0:MOBA_HEADS * n_blk].reshape(MOBA_HEADS, n_blk, MOBA_BLOCK)
        blk = lax.broadcasted_iota(jnp.int32, gate.shape, 1)
        past = blk < cur
        gate = jnp.where(past, gate, NEG)
        rank = jnp.zeros(gate.shape, jnp.int32)
        for m in range(n_blk):
            gm = gate[:, m:m + 1, :]
            tie = jnp.where(blk > m, 1, 0)
            rank = rank + jnp.where(gm > gate, 1, jnp.where(gm == gate, tie, 0))
        mask = jnp.where(rank < min(MOBA_TOPK, n_blk), jnp.where(past, 0.0, NEG), NEG)
        add = jnp.where(blk < cur - 1, mask + far, jnp.where(blk == cur - 1, mask, 0.0))
        add_hi, add_lo = _split_bf16(add)
        qadd_ref[0, :, :, rows] = jnp.concatenate(
            [add_hi.astype(F32), add_lo.astype(F32)], axis=1).astype(BF16)

    staged = project(0)
    for i in range(n_slabs):
        nxt = project(i + 1) if i + 1 < n_slabs else None
        finish(i, *staged)
        staged = nxt


def _branches(h_hi, h_lo, b, s, w_in, u_col, v_col, ln_g, ln_b, w_s, b_s_lanes, w_cq, kt, vm, gt, far_lanes,
              tm=2 * TOKEN_TILE):
    d = h_hi.shape[-1]
    assert u_col % GMLP_WIDTH == 0 and v_col % GMLP_WIDTH == 0
    n_blk = s // MOBA_BLOCK
    nt = s // tm
    m = kt.shape[-1]
    tok = lambda i, j: (i * nt + j, 0)
    return pl.pallas_call(
        functools.partial(_branches_kernel, n_blk=n_blk),
        grid=(b, nt),
        in_specs=[pl.BlockSpec((tm, d), tok), pl.BlockSpec((tm, d), tok),
                  _resident_cols(d, GMLP_WIDTH, u_col // GMLP_WIDTH),
                  _resident_cols(d, GMLP_WIDTH, v_col // GMLP_WIDTH),
                  _resident(ln_g.shape), _resident(ln_b.shape),
                  _resident(w_s.shape), _resident(b_s_lanes.shape), _resident(w_cq.shape),
                  pl.BlockSpec((1, MEM_WIDTH, m), lambda i, j: (i, 0, 0)),
                  pl.BlockSpec((1, m, MEM_WIDTH), lambda i, j: (i, 0, 0)),
                  pl.BlockSpec((1, d, 2 * GATE_ROWS), lambda i, j: (i, 0, 0)),
                  _resident(far_lanes.shape)],
        out_specs=[pl.BlockSpec((tm, GMLP_WIDTH), tok), pl.BlockSpec((tm, MEM_WIDTH), tok),
                   pl.BlockSpec((1, MOBA_HEADS, 2 * n_blk, tm), lambda i, j: (i, 0, 0, j))],
        out_shape=[jax.ShapeDtypeStruct((b * s, GMLP_WIDTH), BF16),
                   jax.ShapeDtypeStruct((b * s, MEM_WIDTH), BF16),
                   jax.ShapeDtypeStruct((b, MOBA_HEADS, 2 * n_blk, s), BF16)],
        compiler_params=_params(2),
        name="branches",
    )(h_hi, h_lo, w_in, w_in, ln_g, ln_b, w_s, b_s_lanes, w_cq, kt, vm, gt, far_lanes)


def _memkv_kernel(mem_ref, g_ref, wkt_ref, wv_ref, kt_ref, v_ref):
    for i in range(mem_ref.shape[0]):
        mn = _rms(mem_ref[i], g_ref[...]).astype(BF16)
        kt_ref[i] = lax.dot_general(wkt_ref[...], mn, (((1,), (1,)), ((), ())),
                                    preferred_element_type=F32).astype(BF16)
        v_ref[i] = jnp.dot(mn, wv_ref[...], preferred_element_type=F32).astype(BF16)


def _memkv(mem, g, w_kt, w_v):
    b, m, d = mem.shape
    nb = math.gcd(b, MEM_BATCHES)
    return pl.pallas_call(
        _memkv_kernel,
        grid=(b // nb,),
        in_specs=[pl.BlockSpec((nb, m, d), lambda i: (i, 0, 0)), _resident((1, d)),
                  _resident(w_kt.shape), _resident(w_v.shape)],
        out_specs=[pl.BlockSpec((nb, MEM_WIDTH, m), lambda i: (i, 0, 0)),
                   pl.BlockSpec((nb, m, MEM_WIDTH), lambda i: (i, 0, 0))],
        out_shape=[jax.ShapeDtypeStruct((b, MEM_WIDTH, m), BF16),
                   jax.ShapeDtypeStruct((b, m, MEM_WIDTH), BF16)],
        compiler_params=_params(1),
        name="memkv",
    )(mem, g, w_kt, w_v)


def _merge_kernel(x_ref, h_ref, a_ref, bt_ref, c_ref, wg_ref, wa_ref, wb_ref, wc_ref, wo_ref,
                  lnpost_ref, lnpre_ref, x1_ref, h2_ref):
    d = x_ref.shape[-1]
    for i in range(x_ref.shape[0] // MLP_SLAB):
        rows = slice(i * MLP_SLAB, (i + 1) * MLP_SLAB)
        gates = jax.nn.sigmoid(jnp.dot(h_ref[rows, :], wg_ref[...], preferred_element_type=F32))
        pa = jnp.dot(a_ref[rows, :], wa_ref[...], preferred_element_type=F32)
        pb = lax.dot_general(bt_ref[0, :, rows], wb_ref[...], (((0,), (0,)), ((), ())),
                             preferred_element_type=F32)
        pc = jnp.dot(c_ref[rows, :], wc_ref[...], preferred_element_type=F32)
        merged = gates[:, :d] * pa + gates[:, d:2 * d] * pb + gates[:, 2 * d:] * pc
        mo = jnp.dot(merged.astype(BF16), wo_ref[...], preferred_element_type=F32)
        x1 = x_ref[rows, :] + _rms(mo, lnpost_ref[...])
        x1_ref[rows, :] = x1
        h2_ref[rows, :] = _rms(x1, lnpre_ref[...]).astype(BF16)


def _merge(x2, h2, a_out, b_out_t, c_out, w_g, w_a, w_b, w_c, w_o, ln_post, ln_pre, tm=MLP_TILE):
    n_tok, d = x2.shape
    b, _, s = b_out_t.shape
    nt = s // tm
    tok = lambda i, j: (i * nt + j, 0)
    return pl.pallas_call(
        _merge_kernel,
        grid=(b, nt),
        in_specs=[pl.BlockSpec((tm, d), tok), pl.BlockSpec((tm, d), tok),
                  pl.BlockSpec((tm, GMLP_WIDTH), tok),
                  pl.BlockSpec((1, MOBA_WIDTH, tm), lambda i, j: (i, 0, j)),
                  pl.BlockSpec((tm, MEM_WIDTH), tok),
                  _resident(w_g.shape), _resident(w_a.shape), _resident(w_b.shape),
                  _resident(w_c.shape), _resident(w_o.shape),
                  _resident((1, d)), _resident((1, d))],
        out_specs=[pl.BlockSpec((tm, d), tok), pl.BlockSpec((tm, d), tok)],
        out_shape=[jax.ShapeDtypeStruct((n_tok, d), F32), jax.ShapeDtypeStruct((n_tok, d), BF16)],
        compiler_params=_params(2),
        name="merge",
    )(x2, h2, a_out, b_out_t, c_out, w_g, w_a, w_b, w_c, w_o, ln_post, ln_pre)


def _ffn_kernel(x1_ref, h2_ref, wg_ref, wu_ref, wd_ref, ln_ref, o_ref):
    for i in range(x1_ref.shape[0] // MLP_SLAB):
        rows = slice(i * MLP_SLAB, (i + 1) * MLP_SLAB)
        h2 = h2_ref[rows, :]
        g = jnp.dot(h2, wg_ref[...], preferred_element_type=F32)
        u = jnp.dot(h2, wu_ref[...], preferred_element_type=F32)
        act = (jax.nn.silu(g) * u).astype(BF16)
        f = jnp.dot(act, wd_ref[...], preferred_element_type=F32)
        o_ref[rows, :] = x1_ref[rows, :] + _rms(f, ln_ref[...])


def _ffn(x1, h2, w_g, w_u, w_d, ln_post, tm=MLP_TILE):
    n_tok, d = x1.shape
    return pl.pallas_call(
        _ffn_kernel,
        grid=(n_tok // tm,),
        in_specs=[pl.BlockSpec((tm, d), lambda i: (i, 0)), pl.BlockSpec((tm, d), lambda i: (i, 0)),
                  _resident(w_g.shape), _resident(w_u.shape), _resident(w_d.shape),
                  _resident((1, d))],
        out_specs=pl.BlockSpec((tm, d), lambda i: (i, 0)),
        out_shape=jax.ShapeDtypeStruct((n_tok, d), F32),
        compiler_params=_params(1),
        name="ffn",
    )(x1, h2, w_g, w_u, w_d, ln_post)


def kernel(x, mem, ln_mix_pre, ln_mix_post, ln_ffn_pre, ln_ffn_post, ln_mem, w_in, ln_v_gain, ln_v_bias,
           w_spatial, b_spatial, rel_bias, w_mem_kv, w_branch_a, w_branch_b, w_branch_c, w_out,
           w_ffn_gate, w_ffn_up, w_ffn_down):
    b, s, d = x.shape
    assert s % MOBA_BLOCK == 0 and s % TOKEN_TILE == 0 and d % V7X_LANES == 0
    depth = w_in.shape[0]
    cuts = [0, GMLP_WIDTH, 2 * GMLP_WIDTH, 2 * GMLP_WIDTH + MOBA_WIDTH, 2 * GMLP_WIDTH + 2 * MOBA_WIDTH,
            2 * GMLP_WIDTH + 3 * MOBA_WIDTH, 2 * GMLP_WIDTH + 3 * MOBA_WIDTH + MEM_WIDTH]
    rb_flat = rel_bias.astype(F32).reshape(-1)
    bias = _bias_tiles(rb_flat)
    far_lanes = jnp.broadcast_to((rel_bias[REL_BUCKETS - 1].astype(F32) * LOG2E)[:, None, None],
                                 (MOBA_HEADS, s // MOBA_BLOCK, V7X_LANES))
    row = lambda v: v.reshape(1, -1).astype(F32)
    for l in range(depth):
        wi = w_in[l]
        wi16 = wi.astype(BF16)
        w_q, w_k, w_v2 = (wi[:, cuts[i]:cuts[i + 1]] for i in (2, 3, 4))
        w_cq, w_g = wi16[:, cuts[5]:cuts[6]], wi16[:, cuts[6]:]
        x2 = x.reshape(b * s, d)

        w_qt_scaled = (w_q * (HEAD_DIM ** -0.5)).T
        h2, h_lo, hmean, qt, k, vt = _qkv(x2, row(ln_mix_pre[l]), b, s, (w_qt_scaled * LOG2E).astype(BF16),
                                          wi16, cuts[3], w_v2.T.astype(BF16))
        gt = _gatew(hmean.reshape(-1, d), w_k, w_qt_scaled, b)
        wkv = w_mem_kv[l]
        kt_mem, v_mem = _memkv(mem, row(ln_mem[l]), wkv[:, :MEM_WIDTH].T.astype(BF16),
                               wkv[:, MEM_WIDTH:].astype(BF16))
        b_s_lanes = jnp.broadcast_to(b_spatial[l][:, :, None], (GMLP_GROUPS, CHUNK, GMLP_GROUP_DIM))
        a_out, c_out, qadd = _branches(
            h2, h_lo, b, s, wi16, cuts[0], cuts[1], row(ln_v_gain[l]), row(ln_v_bias[l]),
            w_spatial[l], b_s_lanes.astype(F32), w_cq, kt_mem, v_mem, gt, far_lanes)
        b_out_t = _attn(qt, qadd, k, vt, bias)

        x1, hn = _merge(x2, h2, a_out, b_out_t, c_out, w_g, w_branch_a[l].astype(BF16),
                        w_branch_b[l].astype(BF16), w_branch_c[l].astype(BF16), w_out[l].astype(BF16),
                        row(ln_mix_post[l]), row(ln_ffn_pre[l]))
        out = _ffn(x1, hn, w_ffn_gate[l].astype(BF16), w_ffn_up[l].astype(BF16),
                   w_ffn_down[l].astype(BF16), row(ln_ffn_post[l]))
        x = out.reshape(b, s, d)
    return x
```

```python
import functools
import math

import jax
import jax.numpy as jnp
from jax import lax
from jax.experimental import pallas as pl
from jax.experimental.pallas import tpu as pltpu

F32 = jnp.float32
BF16 = jnp.bfloat16

EPS = 1e-6
NEG = -1e30
GMLP_GROUPS = 6
GMLP_GROUP_DIM = 128
GMLP_WIDTH = GMLP_GROUPS * GMLP_GROUP_DIM
CHUNK = 128
MOBA_HEADS = 12
HEAD_DIM = 64
MOBA_WIDTH = MOBA_HEADS * HEAD_DIM
MOBA_BLOCK = 256
MOBA_TOPK = 3
REL_BUCKETS = 32
REL_MAX_DIST = 128
LOG2E = math.log2(math.e)
ONES_ROWS = 16
ATTN_HEADS = 6
SCORE_LOOKAHEAD = 3
SCORE_BUFFERS = SCORE_LOOKAHEAD + 1
GATE_BACK = 1
MEM_HEADS = 4
MEM_HEAD_DIM = 128
MEM_WIDTH = MEM_HEADS * MEM_HEAD_DIM
MEM_BATCHES = 4
N_BRANCHES = 3

V7X_LANES = 128
V7X_VMEM_LIMIT = 56 * 1024 * 1024

TOKEN_TILE = 512
MLP_TILE = 1024
MLP_SLAB = 256
QKV_SLAB = 256
GATE_ROWS = 128


def _params(n_axes, vmem=V7X_VMEM_LIMIT):
    return pltpu.CompilerParams(
        dimension_semantics=("arbitrary",) * n_axes, vmem_limit_bytes=vmem)


def _resident(shape):
    zeros = (0,) * len(shape)
    return pl.BlockSpec(shape, lambda *_: zeros, pipeline_mode=pl.Buffered(1))


def _resident_cols(rows, width, index):
    return pl.BlockSpec((rows, width), lambda *_: (0, index), pipeline_mode=pl.Buffered(1))


def _rms(x, g):
    return x * lax.rsqrt(jnp.mean(x * x, axis=-1, keepdims=True) + EPS) * g


def _split_bf16(x):
    hi = x.astype(BF16)
    return hi, (x - hi.astype(F32)).astype(BF16)


def _gatew_kernel(hm_ref, wk_ref, wqt_ref, g_ref, gt_scr, *, n_batch, n_blk):
    kmean = jnp.dot(hm_ref[...], wk_ref[...], precision=lax.Precision.HIGHEST,
                    preferred_element_type=F32)
    used = MOBA_HEADS * n_blk
    gt_scr[:, used:, :] = jnp.zeros((n_batch, GATE_ROWS - used, gt_scr.shape[-1]), F32)
    for h in range(MOBA_HEADS):
        sl = slice(h * HEAD_DIM, (h + 1) * HEAD_DIM)
        res = jnp.dot(kmean[:, sl], wqt_ref[sl, :], precision=lax.Precision.HIGHEST,
                      preferred_element_type=F32)
        for b in range(n_batch):
            gt_scr[b, h * n_blk:(h + 1) * n_blk, :] = res[b * n_blk:(b + 1) * n_blk, :]
    for b in range(n_batch):
        hi, lo = _split_bf16(gt_scr[b].T)
        g_ref[b, :, 0:GATE_ROWS] = hi
        g_ref[b, :, GATE_ROWS:] = lo


def _gatew(hmean, w_k, w_qt, n_batch):
    rows, d = hmean.shape
    n_blk = rows // n_batch
    assert MOBA_HEADS * n_blk <= GATE_ROWS
    return pl.pallas_call(
        functools.partial(_gatew_kernel, n_batch=n_batch, n_blk=n_blk),
        grid=(1,),
        in_specs=[_resident((rows, d)), _resident(w_k.shape), _resident(w_qt.shape)],
        out_specs=pl.BlockSpec((n_batch, d, 2 * GATE_ROWS), lambda i: (0, 0, 0)),
        out_shape=jax.ShapeDtypeStruct((n_batch, d, 2 * GATE_ROWS), BF16),
        scratch_shapes=[pltpu.VMEM((n_batch, GATE_ROWS, d), F32)],
        compiler_params=_params(1),
        name="gatew",
    )(hmean, w_k, w_qt)


def _qkv_kernel(x_ref, g_ref, wqt_ref, wk_ref, wvt_ref, h_ref, hlo_ref, hmean_ref, qt_ref, k_ref, vt_ref,
                *, n_blk):
    nt = (((1,), (1,)), ((), ()))
    n_slabs = x_ref.shape[0] // QKV_SLAB
    lane = lax.broadcasted_iota(jnp.int32, (QKV_SLAB, V7X_LANES), 1)
    low = lane < HEAD_DIM
    for i in range(n_slabs):
        rows = slice(i * QKV_SLAB, (i + 1) * QKV_SLAB)
        blk = pl.program_id(1) * n_slabs + i
        onehot = jnp.where((lane == HEAD_DIM + blk) | (lane == HEAD_DIM + n_blk + blk), 1.0, 0.0)
        hf = _rms(x_ref[rows, :], g_ref[...])
        h, hlo_ref[rows, :] = _split_bf16(hf)
        h_ref[rows, :] = h
        for c in range(QKV_SLAB // MOBA_BLOCK):
            hmean_ref[i * (QKV_SLAB // MOBA_BLOCK) + c] = jnp.mean(
                hf[c * MOBA_BLOCK:(c + 1) * MOBA_BLOCK], axis=0, keepdims=True)
        qt_ref[0, :, rows] = lax.dot_general(wqt_ref[...], h, nt, preferred_element_type=F32).astype(BF16)
        vt_ref[0, :, rows] = lax.dot_general(wvt_ref[...], h, nt, preferred_element_type=F32).astype(BF16)
        kf = jnp.dot(h, wk_ref[...], preferred_element_type=F32)
        for p in range(MOBA_HEADS // 2):
            pair = kf[:, p * V7X_LANES:(p + 1) * V7X_LANES]
            even = jnp.where(low, pair, onehot)
            odd = jnp.where(low, pltpu.roll(pair, HEAD_DIM, 1), onehot)
            k_ref[0, rows, (2 * p) * V7X_LANES:(2 * p + 1) * V7X_LANES] = even.astype(BF16)
            k_ref[0, rows, (2 * p + 1) * V7X_LANES:(2 * p + 2) * V7X_LANES] = odd.astype(BF16)


def _qkv(x2, g, b, s, w_qt, w_in, k_col, w_vt, tm=2 * TOKEN_TILE):
    d = x2.shape[-1]
    assert k_col % MOBA_WIDTH == 0
    nt = s // tm
    blocks = tm // MOBA_BLOCK
    n_blk = s // MOBA_BLOCK
    assert QKV_SLAB == MOBA_BLOCK and HEAD_DIM + 2 * n_blk <= V7X_LANES
    tok = lambda i, j: (i * nt + j, 0)
    return pl.pallas_call(
        functools.partial(_qkv_kernel, n_blk=n_blk),
        grid=(b, nt),
        in_specs=[pl.BlockSpec((tm, d), tok), _resident((1, d)),
                  _resident(w_qt.shape), _resident_cols(d, MOBA_WIDTH, k_col // MOBA_WIDTH),
                  _resident(w_vt.shape)],
        out_specs=[pl.BlockSpec((tm, d), tok), pl.BlockSpec((tm, d), tok),
                   pl.BlockSpec((blocks, 1, d), lambda i, j: (i * nt + j, 0, 0)),
                   pl.BlockSpec((1, MOBA_WIDTH, tm), lambda i, j: (i, 0, j)),
                   pl.BlockSpec((1, tm, MOBA_HEADS * V7X_LANES), lambda i, j: (i, j, 0)),
                   pl.BlockSpec((1, MOBA_WIDTH, tm), lambda i, j: (i, 0, j))],
        out_shape=[jax.ShapeDtypeStruct((b * s, d), BF16), jax.ShapeDtypeStruct((b * s, d), BF16),
                   jax.ShapeDtypeStruct((b * s // MOBA_BLOCK, 1, d), F32),
                   jax.ShapeDtypeStruct((b, MOBA_WIDTH, s), BF16),
                   jax.ShapeDtypeStruct((b, s, MOBA_HEADS * V7X_LANES), BF16),
                   jax.ShapeDtypeStruct((b, MOBA_WIDTH, s), BF16)],
        compiler_params=_params(2),
        name="qkv",
    )(x2, g, w_qt, w_in, w_vt)


def _bias_kernel(rb_ref, out_ref):
    h = pl.program_id(0)
    span = 2 * MOBA_BLOCK
    u = lax.broadcasted_iota(jnp.int32, (8, span), 1)
    max_exact = REL_BUCKETS // 2
    for t in range(2):
        dist = u if t == 0 else jnp.where(u < MOBA_BLOCK, u + MOBA_BLOCK, u - MOBA_BLOCK)
        n = jnp.maximum(dist, 0)
        nf = jnp.maximum(n, 1).astype(F32)
        large = max_exact + (jnp.log(nf / max_exact) / math.log(REL_MAX_DIST / max_exact)
                             * (REL_BUCKETS - max_exact)).astype(jnp.int32)
        large = jnp.minimum(large, REL_BUCKETS - 1)
        bucket = jnp.where(n < max_exact, n, large)
        val = jnp.zeros((8, span), F32)
        for bk in range(REL_BUCKETS):
            val = jnp.where(bucket == bk, rb_ref[bk * MOBA_HEADS + h], val)
        val = val * LOG2E
        if t == 0:
            val = jnp.where(u < MOBA_BLOCK, val, NEG)
        table = jnp.concatenate([val] * (MOBA_BLOCK // 8), axis=0)
        tile = pltpu.roll(table, 0, 1, stride=1, stride_axis=0)
        out_ref[0, t] = tile[:, 0:MOBA_BLOCK]


def _bias_tiles(rb_flat):
    return pl.pallas_call(
        _bias_kernel,
        grid=(MOBA_HEADS,),
        in_specs=[pl.BlockSpec(memory_space=pltpu.SMEM)],
        out_specs=pl.BlockSpec((1, 2, MOBA_BLOCK, MOBA_BLOCK), lambda h: (h, 0, 0, 0)),
        out_shape=jax.ShapeDtypeStruct((MOBA_HEADS, 2, MOBA_BLOCK, MOBA_BLOCK), F32),
        compiler_params=_params(1),
        name="bias",
    )(rb_flat)


def _zero_after(x):
    bits = lax.bitcast_convert_type(x, jnp.uint32)
    sixteen = jnp.uint32(16)
    return lax.bitcast_convert_type(
        lax.shift_right_logical(lax.shift_right_logical(bits, sixteen), sixteen), F32)


def _attn_kernel(qt_ref, qadd_ref, k_ref, vt_ref, bias_ref, o_ref, s_scr, *, n_blk):
    pad = jnp.zeros((V7X_LANES - HEAD_DIM - 2 * n_blk, MOBA_BLOCK), BF16)

    def scores(hh, j, buf):
        cols = slice(j * MOBA_BLOCK, (j + 1) * MOBA_BLOCK)
        q = qt_ref[0, hh * HEAD_DIM:(hh + 1) * HEAD_DIM, cols]
        qz = jnp.concatenate([q, qadd_ref[0, hh, :, cols], pad], axis=0)
        k = k_ref[0, 0:(j + 1) * MOBA_BLOCK, hh * V7X_LANES:(hh + 1) * V7X_LANES]
        s = jnp.dot(k, qz, preferred_element_type=F32)
        gates = []
        for n in range(j + 1):
            rows = slice(n * MOBA_BLOCK, (n + 1) * MOBA_BLOCK)
            if n == j:
                sn = s[rows] + bias_ref[hh, 0]
            elif n == j - 1:
                sn = s[rows] + bias_ref[hh, 1]
            else:
                sn = s[rows]
            s_scr[buf, rows, :] = sn
            gates.append(_zero_after(sn[0:1, :]))
        return gates

    def finish(hh, j, buf, gates):
        cols = slice(j * MOBA_BLOCK, (j + 1) * MOBA_BLOCK)
        heads = slice(hh * HEAD_DIM, (hh + 1) * HEAD_DIM)
        m = jnp.max(s_scr[buf, 0:(j + 1) * MOBA_BLOCK, :], axis=0, keepdims=True)
        ones = jnp.ones((ONES_ROWS, MOBA_BLOCK), BF16)
        acc = None
        for n in range(j + 1):
            rows = slice(n * MOBA_BLOCK, (n + 1) * MOBA_BLOCK)
            mn = m + gates[min(n, len(gates) - 1)] if gates else m
            p = jnp.exp2(s_scr[buf, rows, :] - mn).astype(BF16)
            part = jnp.dot(jnp.concatenate([vt_ref[0, heads, rows], ones], axis=0), p,
                           preferred_element_type=F32)
            acc = part if acc is None else acc + part
        o_ref[0, heads, cols] = (acc[0:HEAD_DIM] / acc[HEAD_DIM:HEAD_DIM + 1]).astype(BF16)

    tasks = [(hh, j) for j in reversed(range(n_blk)) for hh in range(ATTN_HEADS)]
    pending, issued = [], []
    for i, (hh, j) in enumerate(tasks):
        buf = i % SCORE_BUFFERS
        issued.append(scores(hh, j, buf))
        pending.append((hh, j, buf))
        if len(pending) > SCORE_LOOKAHEAD:
            finish(*pending.pop(0), issued[i - GATE_BACK])
    for item in pending:
        finish(*item, None)


def _attn(qt, qadd, k, vt, bias):
    b, _, s = qt.shape
    n_blk = s // MOBA_BLOCK
    nh = ATTN_HEADS
    return pl.pallas_call(
        functools.partial(_attn_kernel, n_blk=n_blk),
        grid=(MOBA_HEADS // nh, b),
        in_specs=[pl.BlockSpec((1, nh * HEAD_DIM, s), lambda h, i: (i, h, 0)),
                  pl.BlockSpec((1, nh, 2 * n_blk, s), lambda h, i: (i, h, 0, 0)),
                  pl.BlockSpec((1, s, nh * V7X_LANES), lambda h, i: (i, 0, h)),
                  pl.BlockSpec((1, nh * HEAD_DIM, s), lambda h, i: (i, h, 0)),
                  pl.BlockSpec((nh, 2, MOBA_BLOCK, MOBA_BLOCK), lambda h, i: (h, 0, 0, 0))],
        out_specs=pl.BlockSpec((1, nh * HEAD_DIM, s), lambda h, i: (i, h, 0)),
        out_shape=jax.ShapeDtypeStruct((b, MOBA_WIDTH, s), BF16),
        scratch_shapes=[pltpu.VMEM((SCORE_BUFFERS, s, MOBA_BLOCK), F32)],
        compiler_params=_params(2),
        name="attn",
    )(qt, qadd, k, vt, bias)


def _branches_kernel(h_ref, hlo_ref, wu_ref, wv_ref, lng_ref, lnb_ref, ws_ref, bs_ref, wq_ref, kt_ref, vm_ref,
                     gw_ref, far_ref, a_ref, c_ref, qadd_ref, *, n_blk):
    causal = (lax.broadcasted_iota(jnp.int32, (CHUNK, CHUNK), 0)
              >= lax.broadcasted_iota(jnp.int32, (CHUNK, CHUNK), 1))
    ws = [jnp.where(causal, ws_ref[g], 0.0).astype(BF16) for g in range(GMLP_GROUPS)]
    ones = jnp.ones((vm_ref.shape[1], MEM_HEAD_DIM), BF16)
    far = jnp.concatenate([far_ref[...]] * (MOBA_BLOCK // V7X_LANES), axis=-1)
    n_slabs = h_ref.shape[0] // MOBA_BLOCK

    def project(i):
        rows = slice(i * MOBA_BLOCK, (i + 1) * MOBA_BLOCK)
        h = h_ref[rows, :]
        cq = jnp.dot(h, wq_ref[...], preferred_element_type=F32).astype(BF16)
        v = jnp.dot(h, wv_ref[...], preferred_element_type=F32)
        u = jnp.dot(h, wu_ref[...], preferred_element_type=F32)
        res = jnp.dot(jnp.concatenate([h, hlo_ref[rows, :]], axis=0), gw_ref[0], preferred_element_type=F32)
        ps = []
        for hd in range(MEM_HEADS):
            sl = slice(hd * MEM_HEAD_DIM, (hd + 1) * MEM_HEAD_DIM)
            s = jnp.dot(cq[:, sl], kt_ref[0, sl, :], preferred_element_type=F32) * (MEM_HEAD_DIM ** -0.5 * LOG2E)
            ps.append(jnp.exp2(s - jnp.max(s, axis=-1, keepdims=True)).astype(BF16))
        return rows, v, u, ps, res

    def finish(i, rows, v, u, ps, res):
        v = jax.nn.gelu(v)
        vc = v - jnp.mean(v, axis=-1, keepdims=True)
        vn = vc * lax.rsqrt(jnp.mean(vc * vc, axis=-1, keepdims=True) + EPS) * lng_ref[...] + lnb_ref[...]
        vn = vn.astype(BF16)
        u = jax.nn.gelu(u)
        chunks = [slice(c * CHUNK, (c + 1) * CHUNK) for c in range(MOBA_BLOCK // CHUNK)]
        for g in range(GMLP_GROUPS):
            lanes = slice(g * GMLP_GROUP_DIM, (g + 1) * GMLP_GROUP_DIM)
            mixed = jnp.dot(ws[g], jnp.concatenate([vn[toks, lanes] for toks in chunks], axis=1),
                            preferred_element_type=F32)
            for c, toks in enumerate(chunks):
                mc = mixed[:, c * GMLP_GROUP_DIM:(c + 1) * GMLP_GROUP_DIM] + bs_ref[g]
                a_ref[rows.start + c * CHUNK:rows.start + (c + 1) * CHUNK, lanes] = (
                    u[toks, lanes] * mc).astype(BF16)
        for hd, p in enumerate(ps):
            sl = slice(hd * MEM_HEAD_DIM, (hd + 1) * MEM_HEAD_DIM)
            acc = jnp.dot(p, jnp.concatenate([vm_ref[0, :, sl], ones], axis=1), preferred_element_type=F32)
            c_ref[rows, sl] = (acc[:, :MEM_HEAD_DIM] / acc[:, MEM_HEAD_DIM:]).astype(BF16)
        cur = pl.program_id(1) * n_slabs + i
        hi, lo = res[:MOBA_BLOCK], res[MOBA_BLOCK:]
        gate = (hi[:, :GATE_ROWS] + (hi[:, GATE_ROWS:] + lo[:, :GATE_ROWS])) + lo[:, GATE_ROWS:]
        gate = gate.T[0:MOBA_HEADS * n_blk].reshape(MOBA_HEADS, n_blk, MOBA_BLOCK)
        blk = lax.broadcasted_iota(jnp.int32, gate.shape, 1)
        past = blk < cur
        gate = jnp.where(past, gate, NEG)
        rank = jnp.zeros(gate.shape, jnp.int32)
        for m in range(n_blk):
            gm = gate[:, m:m + 1, :]
            tie = jnp.where(blk > m, 1, 0)
            rank = rank + jnp.where(gm > gate, 1, jnp.where(gm == gate, tie, 0))
        mask = jnp.where(rank < min(MOBA_TOPK, n_blk), jnp.where(past, 0.0, NEG), NEG)
        add = jnp.where(blk < cur - 1, mask + far, jnp.where(blk == cur - 1, mask, 0.0))
        add_hi, add_lo = _split_bf16(add)
        qadd_ref[0, :, :, rows] = jnp.concatenate(
            [add_hi.astype(F32), add_lo.astype(F32)], axis=1).astype(BF16)

    staged = project(0)
    for i in range(n_slabs):
        nxt = project(i + 1) if i + 1 < n_slabs else None
        finish(i, *staged)
        staged = nxt


def _branches(h_hi, h_lo, b, s, w_in, u_col, v_col, ln_g, ln_b, w_s, b_s_lanes, w_cq, kt, vm, gt, far_lanes,
              tm=4 * TOKEN_TILE):
    d = h_hi.shape[-1]
    assert u_col % GMLP_WIDTH == 0 and v_col % GMLP_WIDTH == 0
    n_blk = s // MOBA_BLOCK
    nt = s // tm
    m = kt.shape[-1]
    tok = lambda i, j: (i * nt + j, 0)
    return pl.pallas_call(
        functools.partial(_branches_kernel, n_blk=n_blk),
        grid=(b, nt),
        in_specs=[pl.BlockSpec((tm, d), tok), pl.BlockSpec((tm, d), tok),
                  _resident_cols(d, GMLP_WIDTH, u_col // GMLP_WIDTH),
                  _resident_cols(d, GMLP_WIDTH, v_col // GMLP_WIDTH),
                  _resident(ln_g.shape), _resident(ln_b.shape),
                  _resident(w_s.shape), _resident(b_s_lanes.shape), _resident(w_cq.shape),
                  pl.BlockSpec((1, MEM_WIDTH, m), lambda i, j: (i, 0, 0)),
                  pl.BlockSpec((1, m, MEM_WIDTH), lambda i, j: (i, 0, 0)),
                  pl.BlockSpec((1, d, 2 * GATE_ROWS), lambda i, j: (i, 0, 0)),
                  _resident(far_lanes.shape)],
        out_specs=[pl.BlockSpec((tm, GMLP_WIDTH), tok), pl.BlockSpec((tm, MEM_WIDTH), tok),
                   pl.BlockSpec((1, MOBA_HEADS, 2 * n_blk, tm), lambda i, j: (i, 0, 0, j))],
        out_shape=[jax.ShapeDtypeStruct((b * s, GMLP_WIDTH), BF16),
                   jax.ShapeDtypeStruct((b * s, MEM_WIDTH), BF16),
                   jax.ShapeDtypeStruct((b, MOBA_HEADS, 2 * n_blk, s), BF16)],
        compiler_params=_params(2),
        name="branches",
    )(h_hi, h_lo, w_in, w_in, ln_g, ln_b, w_s, b_s_lanes, w_cq, kt, vm, gt, far_lanes)


def _memkv_kernel(mem_ref, g_ref, wkt_ref, wv_ref, kt_ref, v_ref):
    for i in range(mem_ref.shape[0]):
        mn = _rms(mem_ref[i], g_ref[...]).astype(BF16)
        kt_ref[i] = lax.dot_general(wkt_ref[...], mn, (((1,), (1,)), ((), ())),
                                    preferred_element_type=F32).astype(BF16)
        v_ref[i] = jnp.dot(mn, wv_ref[...], preferred_element_type=F32).astype(BF16)


def _memkv(mem, g, w_kt, w_v):
    b, m, d = mem.shape
    nb = math.gcd(b, MEM_BATCHES)
    return pl.pallas_call(
        _memkv_kernel,
        grid=(b // nb,),
        in_specs=[pl.BlockSpec((nb, m, d), lambda i: (i, 0, 0)), _resident((1, d)),
                  _resident(w_kt.shape), _resident(w_v.shape)],
        out_specs=[pl.BlockSpec((nb, MEM_WIDTH, m), lambda i: (i, 0, 0)),
                   pl.BlockSpec((nb, m, MEM_WIDTH), lambda i: (i, 0, 0))],
        out_shape=[jax.ShapeDtypeStruct((b, MEM_WIDTH, m), BF16),
                   jax.ShapeDtypeStruct((b, m, MEM_WIDTH), BF16)],
        compiler_params=_params(1),
        name="memkv",
    )(mem, g, w_kt, w_v)


def _merge_kernel(x_ref, h_ref, a_ref, bt_ref, c_ref, wg_ref, wa_ref, wb_ref, wc_ref, wo_ref,
                  lnpost_ref, lnpre_ref, x1_ref, h2_ref):
    d = x_ref.shape[-1]
    for i in range(x_ref.shape[0] // MLP_SLAB):
        rows = slice(i * MLP_SLAB, (i + 1) * MLP_SLAB)
        gates = jax.nn.sigmoid(jnp.dot(h_ref[rows, :], wg_ref[...], preferred_element_type=F32))
        pa = jnp.dot(a_ref[rows, :], wa_ref[...], preferred_element_type=F32)
        pb = lax.dot_general(bt_ref[0, :, rows], wb_ref[...], (((0,), (0,)), ((), ())),
                             preferred_element_type=F32)
        pc = jnp.dot(c_ref[rows, :], wc_ref[...], preferred_element_type=F32)
        merged = gates[:, :d] * pa + gates[:, d:2 * d] * pb + gates[:, 2 * d:] * pc
        mo = jnp.dot(merged.astype(BF16), wo_ref[...], preferred_element_type=F32)
        x1 = x_ref[rows, :] + _rms(mo, lnpost_ref[...])
        x1_ref[rows, :] = x1
        h2_ref[rows, :] = _rms(x1, lnpre_ref[...]).astype(BF16)


def _merge(x2, h2, a_out, b_out_t, c_out, w_g, w_a, w_b, w_c, w_o, ln_post, ln_pre, tm=MLP_TILE):
    n_tok, d = x2.shape
    b, _, s = b_out_t.shape
    nt = s // tm
    tok = lambda i, j: (i * nt + j, 0)
    return pl.pallas_call(
        _merge_kernel,
        grid=(b, nt),
        in_specs=[pl.BlockSpec((tm, d), tok), pl.BlockSpec((tm, d), tok),
                  pl.BlockSpec((tm, GMLP_WIDTH), tok),
                  pl.BlockSpec((1, MOBA_WIDTH, tm), lambda i, j: (i, 0, j)),
                  pl.BlockSpec((tm, MEM_WIDTH), tok),
                  _resident(w_g.shape), _resident(w_a.shape), _resident(w_b.shape),
                  _resident(w_c.shape), _resident(w_o.shape),
                  _resident((1, d)), _resident((1, d))],
        out_specs=[pl.BlockSpec((tm, d), tok), pl.BlockSpec((tm, d), tok)],
        out_shape=[jax.ShapeDtypeStruct((n_tok, d), F32), jax.ShapeDtypeStruct((n_tok, d), BF16)],
        compiler_params=_params(2),
        name="merge",
    )(x2, h2, a_out, b_out_t, c_out, w_g, w_a, w_b, w_c, w_o, ln_post, ln_pre)


def _ffn_kernel(x1_ref, h2_ref, wg_ref, wu_ref, wd_ref, ln_ref, o_ref):
    for i in range(x1_ref.shape[0] // MLP_SLAB):
        rows = slice(i * MLP_SLAB, (i + 1) * MLP_SLAB)
        h2 = h2_ref[rows, :]
        g = jnp.dot(h2, wg_ref[...], preferred_element_type=F32)
        u = jnp.dot(h2, wu_ref[...], preferred_element_type=F32)
        act = (jax.nn.silu(g) * u).astype(BF16)
        f = jnp.dot(act, wd_ref[...], preferred_element_type=F32)
        o_ref[rows, :] = x1_ref[rows, :] + _rms(f, ln_ref[...])


def _ffn(x1, h2, w_g, w_u, w_d, ln_post, tm=MLP_TILE):
    n_tok, d = x1.shape
    return pl.pallas_call(
        _ffn_kernel,
        grid=(n_tok // tm,),
        in_specs=[pl.BlockSpec((tm, d), lambda i: (i, 0)), pl.BlockSpec((tm, d), lambda i: (i, 0)),
                  _resident(w_g.shape), _resident(w_u.shape), _resident(w_d.shape),
                  _resident((1, d))],
        out_specs=pl.BlockSpec((tm, d), lambda i: (i, 0)),
        out_shape=jax.ShapeDtypeStruct((n_tok, d), F32),
        compiler_params=_params(1),
        name="ffn",
    )(x1, h2, w_g, w_u, w_d, ln_post)


def kernel(x, mem, ln_mix_pre, ln_mix_post, ln_ffn_pre, ln_ffn_post, ln_mem, w_in, ln_v_gain, ln_v_bias,
           w_spatial, b_spatial, rel_bias, w_mem_kv, w_branch_a, w_branch_b, w_branch_c, w_out,
           w_ffn_gate, w_ffn_up, w_ffn_down):
    b, s, d = x.shape
    assert s % MOBA_BLOCK == 0 and s % TOKEN_TILE == 0 and d % V7X_LANES == 0
    depth = w_in.shape[0]
    cuts = [0, GMLP_WIDTH, 2 * GMLP_WIDTH, 2 * GMLP_WIDTH + MOBA_WIDTH, 2 * GMLP_WIDTH + 2 * MOBA_WIDTH,
            2 * GMLP_WIDTH + 3 * MOBA_WIDTH, 2 * GMLP_WIDTH + 3 * MOBA_WIDTH + MEM_WIDTH]
    rb_flat = rel_bias.astype(F32).reshape(-1)
    bias = _bias_tiles(rb_flat)
    far_lanes = jnp.broadcast_to((rel_bias[REL_BUCKETS - 1].astype(F32) * LOG2E)[:, None, None],
                                 (MOBA_HEADS, s // MOBA_BLOCK, V7X_LANES))
    row = lambda v: v.reshape(1, -1).astype(F32)
    for l in range(depth):
        wi = w_in[l]
        wi16 = wi.astype(BF16)
        w_q, w_k, w_v2 = (wi[:, cuts[i]:cuts[i + 1]] for i in (2, 3, 4))
        w_cq, w_g = wi16[:, cuts[5]:cuts[6]], wi16[:, cuts[6]:]
        x2 = x.reshape(b * s, d)

        w_qt_scaled = (w_q * (HEAD_DIM ** -0.5)).T
        h2, h_lo, hmean, qt, k, vt = _qkv(x2, row(ln_mix_pre[l]), b, s, (w_qt_scaled * LOG2E).astype(BF16),
                                          wi16, cuts[3], w_v2.T.astype(BF16))
        gt = _gatew(hmean.reshape(-1, d), w_k, w_qt_scaled, b)
        wkv = w_mem_kv[l]
        kt_mem, v_mem = _memkv(mem, row(ln_mem[l]), wkv[:, :MEM_WIDTH].T.astype(BF16),
                               wkv[:, MEM_WIDTH:].astype(BF16))
        b_s_lanes = jnp.broadcast_to(b_spatial[l][:, :, None], (GMLP_GROUPS, CHUNK, GMLP_GROUP_DIM))
        a_out, c_out, qadd = _branches(
            h2, h_lo, b, s, wi16, cuts[0], cuts[1], row(ln_v_gain[l]), row(ln_v_bias[l]),
            w_spatial[l], b_s_lanes.astype(F32), w_cq, kt_mem, v_mem, gt, far_lanes)
        b_out_t = _attn(qt, qadd, k, vt, bias)

        x1, hn = _merge(x2, h2, a_out, b_out_t, c_out, w_g, w_branch_a[l].astype(BF16),
                        w_branch_b[l].astype(BF16), w_branch_c[l].astype(BF16), w_out[l].astype(BF16),
                        row(ln_mix_post[l]), row(ln_ffn_pre[l]))
        out = _ffn(x1, hn, w_ffn_gate[l].astype(BF16), w_ffn_up[l].astype(BF16),
                   w_ffn_down[l].astype(BF16), row(ln_ffn_post[l]))
        x = out.reshape(b, s, d)
    return x
```

```python
import functools
import math

import jax
import jax.numpy as jnp
from jax import lax
from jax.experimental import pallas as pl
from jax.experimental.pallas import tpu as pltpu

F32 = jnp.float32
BF16 = jnp.bfloat16

EPS = 1e-6
NEG = -1e30
GMLP_GROUPS = 6
GMLP_GROUP_DIM = 128
GMLP_WIDTH = GMLP_GROUPS * GMLP_GROUP_DIM
CHUNK = 128
MOBA_HEADS = 12
HEAD_DIM = 64
MOBA_WIDTH = MOBA_HEADS * HEAD_DIM
MOBA_BLOCK = 256
MOBA_TOPK = 3
REL_BUCKETS = 32
REL_MAX_DIST = 128
LOG2E = math.log2(math.e)
ONES_ROWS = 16
ATTN_HEADS = 6
SCORE_LOOKAHEAD = 3
SCORE_BUFFERS = SCORE_LOOKAHEAD + 1
GATE_BACK = 1
MEM_HEADS = 4
MEM_HEAD_DIM = 128
MEM_WIDTH = MEM_HEADS * MEM_HEAD_DIM
MEM_BATCHES = 4
N_BRANCHES = 3

V7X_LANES = 128
V7X_VMEM_LIMIT = 56 * 1024 * 1024

TOKEN_TILE = 512
MLP_TILE = 1024
MLP_SLAB = 256
QKV_SLAB = 256
GATE_ROWS = 128


def _params(n_axes, vmem=V7X_VMEM_LIMIT):
    return pltpu.CompilerParams(
        dimension_semantics=("arbitrary",) * n_axes, vmem_limit_bytes=vmem)


def _resident(shape):
    zeros = (0,) * len(shape)
    return pl.BlockSpec(shape, lambda *_: zeros, pipeline_mode=pl.Buffered(1))


def _resident_cols(rows, width, index):
    return pl.BlockSpec((rows, width), lambda *_: (0, index), pipeline_mode=pl.Buffered(1))


def _rms(x, g):
    return x * lax.rsqrt(jnp.mean(x * x, axis=-1, keepdims=True) + EPS) * g


def _split_bf16(x):
    hi = x.astype(BF16)
    return hi, (x - hi.astype(F32)).astype(BF16)


def _gatew_kernel(hm_ref, wk_ref, wqt_ref, g_ref, gt_scr, *, n_batch, n_blk):
    kmean = jnp.dot(hm_ref[...], wk_ref[...], precision=lax.Precision.HIGHEST,
                    preferred_element_type=F32)
    used = MOBA_HEADS * n_blk
    gt_scr[:, used:, :] = jnp.zeros((n_batch, GATE_ROWS - used, gt_scr.shape[-1]), F32)
    for h in range(MOBA_HEADS):
        sl = slice(h * HEAD_DIM, (h + 1) * HEAD_DIM)
        res = jnp.dot(kmean[:, sl], wqt_ref[sl, :], precision=lax.Precision.HIGHEST,
                      preferred_element_type=F32)
        for b in range(n_batch):
            gt_scr[b, h * n_blk:(h + 1) * n_blk, :] = res[b * n_blk:(b + 1) * n_blk, :]
    for b in range(n_batch):
        hi, lo = _split_bf16(gt_scr[b].T)
        g_ref[b, :, 0:GATE_ROWS] = hi
        g_ref[b, :, GATE_ROWS:] = lo


def _gatew(hmean, w_k, w_qt, n_batch):
    rows, d = hmean.shape
    n_blk = rows // n_batch
    assert MOBA_HEADS * n_blk <= GATE_ROWS
    return pl.pallas_call(
        functools.partial(_gatew_kernel, n_batch=n_batch, n_blk=n_blk),
        grid=(1,),
        in_specs=[_resident((rows, d)), _resident(w_k.shape), _resident(w_qt.shape)],
        out_specs=pl.BlockSpec((n_batch, d, 2 * GATE_ROWS), lambda i: (0, 0, 0)),
        out_shape=jax.ShapeDtypeStruct((n_batch, d, 2 * GATE_ROWS), BF16),
        scratch_shapes=[pltpu.VMEM((n_batch, GATE_ROWS, d), F32)],
        compiler_params=_params(1),
        name="gatew",
    )(hmean, w_k, w_qt)


def _qkv_kernel(x_ref, g_ref, wqt_ref, wk_ref, wvt_ref, h_ref, hlo_ref, hmean_ref, qt_ref, k_ref, vt_ref,
                *, n_blk):
    nt = (((1,), (1,)), ((), ()))
    n_slabs = x_ref.shape[0] // QKV_SLAB
    lane = lax.broadcasted_iota(jnp.int32, (QKV_SLAB, V7X_LANES), 1)
    low = lane < HEAD_DIM
    for i in range(n_slabs):
        rows = slice(i * QKV_SLAB, (i + 1) * QKV_SLAB)
        blk = pl.program_id(1) * n_slabs + i
        onehot = jnp.where((lane == HEAD_DIM + blk) | (lane == HEAD_DIM + n_blk + blk), 1.0, 0.0)
        hf = _rms(x_ref[rows, :], g_ref[...])
        h, hlo_ref[rows, :] = _split_bf16(hf)
        h_ref[rows, :] = h
        for c in range(QKV_SLAB // MOBA_BLOCK):
            hmean_ref[i * (QKV_SLAB // MOBA_BLOCK) + c] = jnp.mean(
                hf[c * MOBA_BLOCK:(c + 1) * MOBA_BLOCK], axis=0, keepdims=True)
        kf = jnp.dot(h, wk_ref[...], preferred_element_type=F32)
        qt_ref[0, :, rows] = lax.dot_general(wqt_ref[...], h, nt, preferred_element_type=F32).astype(BF16)
        vt_ref[0, :, rows] = lax.dot_general(wvt_ref[...], h, nt, preferred_element_type=F32).astype(BF16)
        for p in range(MOBA_HEADS // 2):
            pair = kf[:, p * V7X_LANES:(p + 1) * V7X_LANES]
            even = jnp.where(low, pair, onehot)
            odd = jnp.where(low, pltpu.roll(pair, HEAD_DIM, 1), onehot)
            k_ref[0, rows, (2 * p) * V7X_LANES:(2 * p + 1) * V7X_LANES] = even.astype(BF16)
            k_ref[0, rows, (2 * p + 1) * V7X_LANES:(2 * p + 2) * V7X_LANES] = odd.astype(BF16)


def _qkv(x2, g, b, s, w_qt, w_in, k_col, w_vt, tm=2 * TOKEN_TILE):
    d = x2.shape[-1]
    assert k_col % MOBA_WIDTH == 0
    nt = s // tm
    blocks = tm // MOBA_BLOCK
    n_blk = s // MOBA_BLOCK
    assert QKV_SLAB == MOBA_BLOCK and HEAD_DIM + 2 * n_blk <= V7X_LANES
    tok = lambda i, j: (i * nt + j, 0)
    return pl.pallas_call(
        functools.partial(_qkv_kernel, n_blk=n_blk),
        grid=(b, nt),
        in_specs=[pl.BlockSpec((tm, d), tok), _resident((1, d)),
                  _resident(w_qt.shape), _resident_cols(d, MOBA_WIDTH, k_col // MOBA_WIDTH),
                  _resident(w_vt.shape)],
        out_specs=[pl.BlockSpec((tm, d), tok), pl.BlockSpec((tm, d), tok),
                   pl.BlockSpec((blocks, 1, d), lambda i, j: (i * nt + j, 0, 0)),
                   pl.BlockSpec((1, MOBA_WIDTH, tm), lambda i, j: (i, 0, j)),
                   pl.BlockSpec((1, tm, MOBA_HEADS * V7X_LANES), lambda i, j: (i, j, 0)),
                   pl.BlockSpec((1, MOBA_WIDTH, tm), lambda i, j: (i, 0, j))],
        out_shape=[jax.ShapeDtypeStruct((b * s, d), BF16), jax.ShapeDtypeStruct((b * s, d), BF16),
                   jax.ShapeDtypeStruct((b * s // MOBA_BLOCK, 1, d), F32),
                   jax.ShapeDtypeStruct((b, MOBA_WIDTH, s), BF16),
                   jax.ShapeDtypeStruct((b, s, MOBA_HEADS * V7X_LANES), BF16),
                   jax.ShapeDtypeStruct((b, MOBA_WIDTH, s), BF16)],
        compiler_params=_params(2),
        name="qkv",
    )(x2, g, w_qt, w_in, w_vt)


def _bias_kernel(rb_ref, out_ref):
    h = pl.program_id(0)
    span = 2 * MOBA_BLOCK
    u = lax.broadcasted_iota(jnp.int32, (8, span), 1)
    max_exact = REL_BUCKETS // 2
    for t in range(2):
        dist = u if t == 0 else jnp.where(u < MOBA_BLOCK, u + MOBA_BLOCK, u - MOBA_BLOCK)
        n = jnp.maximum(dist, 0)
        nf = jnp.maximum(n, 1).astype(F32)
        large = max_exact + (jnp.log(nf / max_exact) / math.log(REL_MAX_DIST / max_exact)
                             * (REL_BUCKETS - max_exact)).astype(jnp.int32)
        large = jnp.minimum(large, REL_BUCKETS - 1)
        bucket = jnp.where(n < max_exact, n, large)
        val = jnp.zeros((8, span), F32)
        for bk in range(REL_BUCKETS):
            val = jnp.where(bucket == bk, rb_ref[bk * MOBA_HEADS + h], val)
        val = val * LOG2E
        if t == 0:
            val = jnp.where(u < MOBA_BLOCK, val, NEG)
        table = jnp.concatenate([val] * (MOBA_BLOCK // 8), axis=0)
        tile = pltpu.roll(table, 0, 1, stride=1, stride_axis=0)
        out_ref[0, t] = tile[:, 0:MOBA_BLOCK]


def _bias_tiles(rb_flat):
    return pl.pallas_call(
        _bias_kernel,
        grid=(MOBA_HEADS,),
        in_specs=[pl.BlockSpec(memory_space=pltpu.SMEM)],
        out_specs=pl.BlockSpec((1, 2, MOBA_BLOCK, MOBA_BLOCK), lambda h: (h, 0, 0, 0)),
        out_shape=jax.ShapeDtypeStruct((MOBA_HEADS, 2, MOBA_BLOCK, MOBA_BLOCK), F32),
        compiler_params=_params(1),
        name="bias",
    )(rb_flat)


def _zero_after(x):
    bits = lax.bitcast_convert_type(x, jnp.uint32)
    sixteen = jnp.uint32(16)
    return lax.bitcast_convert_type(
        lax.shift_right_logical(lax.shift_right_logical(bits, sixteen), sixteen), F32)


def _attn_kernel(qt_ref, qadd_ref, k_ref, vt_ref, bias_ref, o_ref, s_scr, *, n_blk):
    pad = jnp.zeros((V7X_LANES - HEAD_DIM - 2 * n_blk, MOBA_BLOCK), BF16)

    def scores(hh, j, buf):
        cols = slice(j * MOBA_BLOCK, (j + 1) * MOBA_BLOCK)
        q = qt_ref[0, hh * HEAD_DIM:(hh + 1) * HEAD_DIM, cols]
        qz = jnp.concatenate([q, qadd_ref[0, hh, :, cols], pad], axis=0)
        k = k_ref[0, 0:(j + 1) * MOBA_BLOCK, hh * V7X_LANES:(hh + 1) * V7X_LANES]
        s = jnp.dot(k, qz, preferred_element_type=F32)
        gates = []
        for n in range(j + 1):
            rows = slice(n * MOBA_BLOCK, (n + 1) * MOBA_BLOCK)
            if n == j:
                sn = s[rows] + bias_ref[hh, 0]
            elif n == j - 1:
                sn = s[rows] + bias_ref[hh, 1]
            else:
                sn = s[rows]
            s_scr[buf, rows, :] = sn
            gates.append(_zero_after(sn[0:1, :]))
        return gates

    def finish(hh, j, buf, gates):
        cols = slice(j * MOBA_BLOCK, (j + 1) * MOBA_BLOCK)
        heads = slice(hh * HEAD_DIM, (hh + 1) * HEAD_DIM)
        m = jnp.max(s_scr[buf, 0:(j + 1) * MOBA_BLOCK, :], axis=0, keepdims=True)
        ones = jnp.ones((ONES_ROWS, MOBA_BLOCK), BF16)
        acc = None
        for n in range(j + 1):
            rows = slice(n * MOBA_BLOCK, (n + 1) * MOBA_BLOCK)
            mn = m + gates[min(n, len(gates) - 1)] if gates else m
            p = jnp.exp2(s_scr[buf, rows, :] - mn).astype(BF16)
            part = jnp.dot(jnp.concatenate([vt_ref[0, heads, rows], ones], axis=0), p,
                           preferred_element_type=F32)
            acc = part if acc is None else acc + part
        o_ref[0, heads, cols] = (acc[0:HEAD_DIM] / acc[HEAD_DIM:HEAD_DIM + 1]).astype(BF16)

    tasks = [(hh, j) for j in reversed(range(n_blk)) for hh in range(ATTN_HEADS)]
    pending, issued = [], []
    for i, (hh, j) in enumerate(tasks):
        buf = i % SCORE_BUFFERS
        issued.append(scores(hh, j, buf))
        pending.append((hh, j, buf))
        if len(pending) > SCORE_LOOKAHEAD:
            finish(*pending.pop(0), issued[i - GATE_BACK])
    for item in pending:
        finish(*item, None)


def _attn(qt, qadd, k, vt, bias):
    b, _, s = qt.shape
    n_blk = s // MOBA_BLOCK
    nh = ATTN_HEADS
    return pl.pallas_call(
        functools.partial(_attn_kernel, n_blk=n_blk),
        grid=(MOBA_HEADS // nh, b),
        in_specs=[pl.BlockSpec((1, nh * HEAD_DIM, s), lambda h, i: (i, h, 0)),
                  pl.BlockSpec((1, nh, 2 * n_blk, s), lambda h, i: (i, h, 0, 0)),
                  pl.BlockSpec((1, s, nh * V7X_LANES), lambda h, i: (i, 0, h)),
                  pl.BlockSpec((1, nh * HEAD_DIM, s), lambda h, i: (i, h, 0)),
                  pl.BlockSpec((nh, 2, MOBA_BLOCK, MOBA_BLOCK), lambda h, i: (h, 0, 0, 0))],
        out_specs=pl.BlockSpec((1, nh * HEAD_DIM, s), lambda h, i: (i, h, 0)),
        out_shape=jax.ShapeDtypeStruct((b, MOBA_WIDTH, s), BF16),
        scratch_shapes=[pltpu.VMEM((SCORE_BUFFERS, s, MOBA_BLOCK), F32)],
        compiler_params=_params(2),
        name="attn",
    )(qt, qadd, k, vt, bias)


def _branches_kernel(h_ref, hlo_ref, wu_ref, wv_ref, lng_ref, lnb_ref, ws_ref, bs_ref, wq_ref, kt_ref, vm_ref,
                     gw_ref, far_ref, a_ref, c_ref, qadd_ref, *, n_blk):
    causal = (lax.broadcasted_iota(jnp.int32, (CHUNK, CHUNK), 0)
              >= lax.broadcasted_iota(jnp.int32, (CHUNK, CHUNK), 1))
    ws = [jnp.where(causal, ws_ref[g], 0.0).astype(BF16) for g in range(GMLP_GROUPS)]
    ones = jnp.ones((vm_ref.shape[1], MEM_HEAD_DIM), BF16)
    far = jnp.concatenate([far_ref[...]] * (MOBA_BLOCK // V7X_LANES), axis=-1)
    n_slabs = h_ref.shape[0] // MOBA_BLOCK

    def project(i):
        rows = slice(i * MOBA_BLOCK, (i + 1) * MOBA_BLOCK)
        h = h_ref[rows, :]
        cq = jnp.dot(h, wq_ref[...], preferred_element_type=F32).astype(BF16)
        v = jnp.dot(h, wv_ref[...], preferred_element_type=F32)
        u = jnp.dot(h, wu_ref[...], preferred_element_type=F32)
        res = jnp.dot(jnp.concatenate([h, hlo_ref[rows, :]], axis=0), gw_ref[0], preferred_element_type=F32)
        ps = []
        for hd in range(MEM_HEADS):
            sl = slice(hd * MEM_HEAD_DIM, (hd + 1) * MEM_HEAD_DIM)
            s = jnp.dot(cq[:, sl], kt_ref[0, sl, :], preferred_element_type=F32) * (MEM_HEAD_DIM ** -0.5 * LOG2E)
            ps.append(jnp.exp2(s - jnp.max(s, axis=-1, keepdims=True)).astype(BF16))
        return rows, v, u, ps, res

    def finish(i, rows, v, u, ps, res):
        v = jax.nn.gelu(v)
        vc = v - jnp.mean(v, axis=-1, keepdims=True)
        vn = vc * lax.rsqrt(jnp.mean(vc * vc, axis=-1, keepdims=True) + EPS) * lng_ref[...] + lnb_ref[...]
        vn = vn.astype(BF16)
        u = jax.nn.gelu(u)
        chunks = [slice(c * CHUNK, (c + 1) * CHUNK) for c in range(MOBA_BLOCK // CHUNK)]
        for g in range(GMLP_GROUPS):
            lanes = slice(g * GMLP_GROUP_DIM, (g + 1) * GMLP_GROUP_DIM)
            mixed = jnp.dot(ws[g], jnp.concatenate([vn[toks, lanes] for toks in chunks], axis=1),
                            preferred_element_type=F32)
            for c, toks in enumerate(chunks):
                mc = mixed[:, c * GMLP_GROUP_DIM:(c + 1) * GMLP_GROUP_DIM] + bs_ref[g]
                a_ref[rows.start + c * CHUNK:rows.start + (c + 1) * CHUNK, lanes] = (
                    u[toks, lanes] * mc).astype(BF16)
        for hd, p in enumerate(ps):
            sl = slice(hd * MEM_HEAD_DIM, (hd + 1) * MEM_HEAD_DIM)
            acc = jnp.dot(p, jnp.concatenate([vm_ref[0, :, sl], ones], axis=1), preferred_element_type=F32)
            c_ref[rows, sl] = (acc[:, :MEM_HEAD_DIM] / acc[:, MEM_HEAD_DIM:]).astype(BF16)
        cur = pl.program_id(1) * n_slabs + i
        hi, lo = res[:MOBA_BLOCK], res[MOBA_BLOCK:]
        gate = (hi[:, :GATE_ROWS] + (hi[:, GATE_ROWS:] + lo[:, :GATE_ROWS])) + lo[:, GATE_ROWS:]
        gate = gate.T[0:MOBA_HEADS * n_blk].reshape(MOBA_HEADS, n_blk, MOBA_BLOCK)
        blk = lax.broadcasted_iota(jnp.int32, gate.shape, 1)
        past = blk < cur
        gate = jnp.where(past, gate, NEG)
        rank = jnp.zeros(gate.shape, jnp.int32)
        for m in range(n_blk):
            gm = gate[:, m:m + 1, :]
            tie = jnp.where(blk > m, 1, 0)
            rank = rank + jnp.where(gm > gate, 1, jnp.where(gm == gate, tie, 0))
        mask = jnp.where(rank < min(MOBA_TOPK, n_blk), jnp.where(past, 0.0, NEG), NEG)
        add = jnp.where(blk < cur - 1, mask + far, jnp.where(blk == cur - 1, mask, 0.0))
        add_hi, add_lo = _split_bf16(add)
        qadd_ref[0, :, :, rows] = jnp.concatenate(
            [add_hi.astype(F32), add_lo.astype(F32)], axis=1).astype(BF16)

    staged = project(0)
    for i in range(n_slabs):
        nxt = project(i + 1) if i + 1 < n_slabs else None
        finish(i, *staged)
        staged = nxt


def _branches(h_hi, h_lo, b, s, w_in, u_col, v_col, ln_g, ln_b, w_s, b_s_lanes, w_cq, kt, vm, gt, far_lanes,
              tm=4 * TOKEN_TILE):
    d = h_hi.shape[-1]
    assert u_col % GMLP_WIDTH == 0 and v_col % GMLP_WIDTH == 0
    n_blk = s // MOBA_BLOCK
    nt = s // tm
    m = kt.shape[-1]
    tok = lambda i, j: (i * nt + j, 0)
    return pl.pallas_call(
        functools.partial(_branches_kernel, n_blk=n_blk),
        grid=(b, nt),
        in_specs=[pl.BlockSpec((tm, d), tok), pl.BlockSpec((tm, d), tok),
                  _resident_cols(d, GMLP_WIDTH, u_col // GMLP_WIDTH),
                  _resident_cols(d, GMLP_WIDTH, v_col // GMLP_WIDTH),
                  _resident(ln_g.shape), _resident(ln_b.shape),
                  _resident(w_s.shape), _resident(b_s_lanes.shape), _resident(w_cq.shape),
                  pl.BlockSpec((1, MEM_WIDTH, m), lambda i, j: (i, 0, 0)),
                  pl.BlockSpec((1, m, MEM_WIDTH), lambda i, j: (i, 0, 0)),
                  pl.BlockSpec((1, d, 2 * GATE_ROWS), lambda i, j: (i, 0, 0)),
                  _resident(far_lanes.shape)],
        out_specs=[pl.BlockSpec((tm, GMLP_WIDTH), tok), pl.BlockSpec((tm, MEM_WIDTH), tok),
                   pl.BlockSpec((1, MOBA_HEADS, 2 * n_blk, tm), lambda i, j: (i, 0, 0, j))],
        out_shape=[jax.ShapeDtypeStruct((b * s, GMLP_WIDTH), BF16),
                   jax.ShapeDtypeStruct((b * s, MEM_WIDTH), BF16),
                   jax.ShapeDtypeStruct((b, MOBA_HEADS, 2 * n_blk, s), BF16)],
        compiler_params=_params(2),
        name="branches",
    )(h_hi, h_lo, w_in, w_in, ln_g, ln_b, w_s, b_s_lanes, w_cq, kt, vm, gt, far_lanes)


def _memkv_kernel(mem_ref, g_ref, wkt_ref, wv_ref, kt_ref, v_ref):
    for i in range(mem_ref.shape[0]):
        mn = _rms(mem_ref[i], g_ref[...]).astype(BF16)
        kt_ref[i] = lax.dot_general(wkt_ref[...], mn, (((1,), (1,)), ((), ())),
                                    preferred_element_type=F32).astype(BF16)
        v_ref[i] = jnp.dot(mn, wv_ref[...], preferred_element_type=F32).astype(BF16)


def _memkv(mem, g, w_kt, w_v):
    b, m, d = mem.shape
    nb = math.gcd(b, MEM_BATCHES)
    return pl.pallas_call(
        _memkv_kernel,
        grid=(b // nb,),
        in_specs=[pl.BlockSpec((nb, m, d), lambda i: (i, 0, 0)), _resident((1, d)),
                  _resident(w_kt.shape), _resident(w_v.shape)],
        out_specs=[pl.BlockSpec((nb, MEM_WIDTH, m), lambda i: (i, 0, 0)),
                   pl.BlockSpec((nb, m, MEM_WIDTH), lambda i: (i, 0, 0))],
        out_shape=[jax.ShapeDtypeStruct((b, MEM_WIDTH, m), BF16),
                   jax.ShapeDtypeStruct((b, m, MEM_WIDTH), BF16)],
        compiler_params=_params(1),
        name="memkv",
    )(mem, g, w_kt, w_v)


def _merge_kernel(x_ref, h_ref, a_ref, bt_ref, c_ref, wg_ref, wa_ref, wb_ref, wc_ref, wo_ref,
                  lnpost_ref, lnpre_ref, x1_ref, h2_ref):
    d = x_ref.shape[-1]
    for i in range(x_ref.shape[0] // MLP_SLAB):
        rows = slice(i * MLP_SLAB, (i + 1) * MLP_SLAB)
        gates = jax.nn.sigmoid(jnp.dot(h_ref[rows, :], wg_ref[...], preferred_element_type=F32))
        pa = jnp.dot(a_ref[rows, :], wa_ref[...], preferred_element_type=F32)
        pb = lax.dot_general(bt_ref[0, :, rows], wb_ref[...], (((0,), (0,)), ((), ())),
                             preferred_element_type=F32)
        pc = jnp.dot(c_ref[rows, :], wc_ref[...], preferred_element_type=F32)
        merged = gates[:, :d] * pa + gates[:, d:2 * d] * pb + gates[:, 2 * d:] * pc
        mo = jnp.dot(merged.astype(BF16), wo_ref[...], preferred_element_type=F32)
        x1 = x_ref[rows, :] + _rms(mo, lnpost_ref[...])
        x1_ref[rows, :] = x1
        h2_ref[rows, :] = _rms(x1, lnpre_ref[...]).astype(BF16)


def _merge(x2, h2, a_out, b_out_t, c_out, w_g, w_a, w_b, w_c, w_o, ln_post, ln_pre, tm=MLP_TILE):
    n_tok, d = x2.shape
    b, _, s = b_out_t.shape
    nt = s // tm
    tok = lambda i, j: (i * nt + j, 0)
    return pl.pallas_call(
        _merge_kernel,
        grid=(b, nt),
        in_specs=[pl.BlockSpec((tm, d), tok), pl.BlockSpec((tm, d), tok),
                  pl.BlockSpec((tm, GMLP_WIDTH), tok),
                  pl.BlockSpec((1, MOBA_WIDTH, tm), lambda i, j: (i, 0, j)),
                  pl.BlockSpec((tm, MEM_WIDTH), tok),
                  _resident(w_g.shape), _resident(w_a.shape), _resident(w_b.shape),
                  _resident(w_c.shape), _resident(w_o.shape),
                  _resident((1, d)), _resident((1, d))],
        out_specs=[pl.BlockSpec((tm, d), tok), pl.BlockSpec((tm, d), tok)],
        out_shape=[jax.ShapeDtypeStruct((n_tok, d), F32), jax.ShapeDtypeStruct((n_tok, d), BF16)],
        compiler_params=_params(2),
        name="merge",
    )(x2, h2, a_out, b_out_t, c_out, w_g, w_a, w_b, w_c, w_o, ln_post, ln_pre)


def _ffn_kernel(x1_ref, h2_ref, wg_ref, wu_ref, wd_ref, ln_ref, o_ref):
    for i in range(x1_ref.shape[0] // MLP_SLAB):
        rows = slice(i * MLP_SLAB, (i + 1) * MLP_SLAB)
        h2 = h2_ref[rows, :]
        g = jnp.dot(h2, wg_ref[...], preferred_element_type=F32)
        u = jnp.dot(h2, wu_ref[...], preferred_element_type=F32)
        act = (jax.nn.silu(g) * u).astype(BF16)
        f = jnp.dot(act, wd_ref[...], preferred_element_type=F32)
        o_ref[rows, :] = x1_ref[rows, :] + _rms(f, ln_ref[...])


def _ffn(x1, h2, w_g, w_u, w_d, ln_post, tm=MLP_TILE):
    n_tok, d = x1.shape
    return pl.pallas_call(
        _ffn_kernel,
        grid=(n_tok // tm,),
        in_specs=[pl.BlockSpec((tm, d), lambda i: (i, 0)), pl.BlockSpec((tm, d), lambda i: (i, 0)),
                  _resident(w_g.shape), _resident(w_u.shape), _resident(w_d.shape),
                  _resident((1, d))],
        out_specs=pl.BlockSpec((tm, d), lambda i: (i, 0)),
        out_shape=jax.ShapeDtypeStruct((n_tok, d), F32),
        compiler_params=_params(1),
        name="ffn",
    )(x1, h2, w_g, w_u, w_d, ln_post)


def kernel(x, mem, ln_mix_pre, ln_mix_post, ln_ffn_pre, ln_ffn_post, ln_mem, w_in, ln_v_gain, ln_v_bias,
           w_spatial, b_spatial, rel_bias, w_mem_kv, w_branch_a, w_branch_b, w_branch_c, w_out,
           w_ffn_gate, w_ffn_up, w_ffn_down):
    b, s, d = x.shape
    assert s % MOBA_BLOCK == 0 and s % TOKEN_TILE == 0 and d % V7X_LANES == 0
    depth = w_in.shape[0]
    cuts = [0, GMLP_WIDTH, 2 * GMLP_WIDTH, 2 * GMLP_WIDTH + MOBA_WIDTH, 2 * GMLP_WIDTH + 2 * MOBA_WIDTH,
            2 * GMLP_WIDTH + 3 * MOBA_WIDTH, 2 * GMLP_WIDTH + 3 * MOBA_WIDTH + MEM_WIDTH]
    rb_flat = rel_bias.astype(F32).reshape(-1)
    bias = _bias_tiles(rb_flat)
    far_lanes = jnp.broadcast_to((rel_bias[REL_BUCKETS - 1].astype(F32) * LOG2E)[:, None, None],
                                 (MOBA_HEADS, s // MOBA_BLOCK, V7X_LANES))
    row = lambda v: v.reshape(1, -1).astype(F32)
    for l in range(depth):
        wi = w_in[l]
        wi16 = wi.astype(BF16)
        w_q, w_k, w_v2 = (wi[:, cuts[i]:cuts[i + 1]] for i in (2, 3, 4))
        w_cq, w_g = wi16[:, cuts[5]:cuts[6]], wi16[:, cuts[6]:]
        x2 = x.reshape(b * s, d)

        w_qt_scaled = (w_q * (HEAD_DIM ** -0.5)).T
        h2, h_lo, hmean, qt, k, vt = _qkv(x2, row(ln_mix_pre[l]), b, s, (w_qt_scaled * LOG2E).astype(BF16),
                                          wi16, cuts[3], w_v2.T.astype(BF16))
        gt = _gatew(hmean.reshape(-1, d), w_k, w_qt_scaled, b)
        wkv = w_mem_kv[l]
        kt_mem, v_mem = _memkv(mem, row(ln_mem[l]), wkv[:, :MEM_WIDTH].T.astype(BF16),
                               wkv[:, MEM_WIDTH:].astype(BF16))
        b_s_lanes = jnp.broadcast_to(b_spatial[l][:, :, None], (GMLP_GROUPS, CHUNK, GMLP_GROUP_DIM))
        a_out, c_out, qadd = _branches(
            h2, h_lo, b, s, wi16, cuts[0], cuts[1], row(ln_v_gain[l]), row(ln_v_bias[l]),
            w_spatial[l], b_s_lanes.astype(F32), w_cq, kt_mem, v_mem, gt, far_lanes)
        b_out_t = _attn(qt, qadd, k, vt, bias)

        x1, hn = _merge(x2, h2, a_out, b_out_t, c_out, w_g, w_branch_a[l].astype(BF16),
                        w_branch_b[l].astype(BF16), w_branch_c[l].astype(BF16), w_out[l].astype(BF16),
                        row(ln_mix_post[l]), row(ln_ffn_pre[l]))
        out = _ffn(x1, hn, w_ffn_gate[l].astype(BF16), w_ffn_up[l].astype(BF16),
                   w_ffn_down[l].astype(BF16), row(ln_ffn_post[l]))
        x = out.reshape(b, s, d)
    return x
```

```python
import functools
import math

import jax
import jax.numpy as jnp
from jax import lax
from jax.experimental import pallas as pl
from jax.experimental.pallas import tpu as pltpu

F32 = jnp.float32
BF16 = jnp.bfloat16

EPS = 1e-6
NEG = -1e30
GMLP_GROUPS = 6
GMLP_GROUP_DIM = 128
GMLP_WIDTH = GMLP_GROUPS * GMLP_GROUP_DIM
CHUNK = 128
MOBA_HEADS = 12
HEAD_DIM = 64
MOBA_WIDTH = MOBA_HEADS * HEAD_DIM
MOBA_BLOCK = 256
MOBA_TOPK = 3
REL_BUCKETS = 32
REL_MAX_DIST = 128
LOG2E = math.log2(math.e)
ONES_ROWS = 16
ATTN_HEADS = 6
SCORE_LOOKAHEAD = 3
SCORE_BUFFERS = SCORE_LOOKAHEAD + 1
GATE_BACK = 1
MEM_HEADS = 4
MEM_HEAD_DIM = 128
MEM_WIDTH = MEM_HEADS * MEM_HEAD_DIM
MEM_BATCHES = 4
N_BRANCHES = 3

V7X_LANES = 128
V7X_VMEM_LIMIT = 56 * 1024 * 1024

TOKEN_TILE = 512
MLP_TILE = 1024
FF_ALIGN = 256
MLP_SLAB = 256
QKV_SLAB = 256
GATE_ROWS = 128


def _params(n_axes, vmem=V7X_VMEM_LIMIT):
    return pltpu.CompilerParams(
        dimension_semantics=("arbitrary",) * n_axes, vmem_limit_bytes=vmem)


def _resident(shape):
    zeros = (0,) * len(shape)
    return pl.BlockSpec(shape, lambda *_: zeros, pipeline_mode=pl.Buffered(1))


def _resident_cols(rows, width, index):
    return pl.BlockSpec((rows, width), lambda *_: (0, index), pipeline_mode=pl.Buffered(1))


def _rms(x, g):
    return x * lax.rsqrt(jnp.mean(x * x, axis=-1, keepdims=True) + EPS) * g


def _split_bf16(x):
    hi = x.astype(BF16)
    return hi, (x - hi.astype(F32)).astype(BF16)


def _gatew_kernel(hm_ref, wk_ref, wqt_ref, g_ref, gt_scr, *, n_batch, n_blk):
    kmean = jnp.dot(hm_ref[...], wk_ref[...], precision=lax.Precision.HIGHEST,
                    preferred_element_type=F32)
    used = MOBA_HEADS * n_blk
    gt_scr[:, used:, :] = jnp.zeros((n_batch, GATE_ROWS - used, gt_scr.shape[-1]), F32)
    for h in range(MOBA_HEADS):
        sl = slice(h * HEAD_DIM, (h + 1) * HEAD_DIM)
        res = jnp.dot(kmean[:, sl], wqt_ref[sl, :], precision=lax.Precision.HIGHEST,
                      preferred_element_type=F32)
        for b in range(n_batch):
            gt_scr[b, h * n_blk:(h + 1) * n_blk, :] = res[b * n_blk:(b + 1) * n_blk, :]
    for b in range(n_batch):
        hi, lo = _split_bf16(gt_scr[b].T)
        g_ref[b, :, 0:GATE_ROWS] = hi
        g_ref[b, :, GATE_ROWS:] = lo


def _gatew(hmean, w_k, w_qt, n_batch):
    rows, d = hmean.shape
    n_blk = rows // n_batch
    assert MOBA_HEADS * n_blk <= GATE_ROWS
    return pl.pallas_call(
        functools.partial(_gatew_kernel, n_batch=n_batch, n_blk=n_blk),
        grid=(1,),
        in_specs=[_resident((rows, d)), _resident(w_k.shape), _resident(w_qt.shape)],
        out_specs=pl.BlockSpec((n_batch, d, 2 * GATE_ROWS), lambda i: (0, 0, 0)),
        out_shape=jax.ShapeDtypeStruct((n_batch, d, 2 * GATE_ROWS), BF16),
        scratch_shapes=[pltpu.VMEM((n_batch, GATE_ROWS, d), F32)],
        compiler_params=_params(1),
        name="gatew",
    )(hmean, w_k, w_qt)


def _qkv_kernel(x_ref, g_ref, wqt_ref, wk_ref, wvt_ref, h_ref, hlo_ref, hmean_ref, qt_ref, k_ref, vt_ref,
                *, n_blk):
    nt = (((1,), (1,)), ((), ()))
    n_slabs = x_ref.shape[0] // QKV_SLAB
    lane = lax.broadcasted_iota(jnp.int32, (QKV_SLAB, V7X_LANES), 1)
    low = lane < HEAD_DIM
    for i in range(n_slabs):
        rows = slice(i * QKV_SLAB, (i + 1) * QKV_SLAB)
        blk = pl.program_id(1) * n_slabs + i
        onehot = jnp.where((lane == HEAD_DIM + blk) | (lane == HEAD_DIM + n_blk + blk), 1.0, 0.0)
        hf = _rms(x_ref[rows, :], g_ref[...])
        h, hlo_ref[rows, :] = _split_bf16(hf)
        h_ref[rows, :] = h
        for c in range(QKV_SLAB // MOBA_BLOCK):
            hmean_ref[i * (QKV_SLAB // MOBA_BLOCK) + c] = jnp.mean(
                hf[c * MOBA_BLOCK:(c + 1) * MOBA_BLOCK], axis=0, keepdims=True)
        kf = jnp.dot(h, wk_ref[...], preferred_element_type=F32)
        qt_ref[0, :, rows] = lax.dot_general(wqt_ref[...], h, nt, preferred_element_type=F32).astype(BF16)
        vt_ref[0, :, rows] = lax.dot_general(wvt_ref[...], h, nt, preferred_element_type=F32).astype(BF16)
        for p in range(MOBA_HEADS // 2):
            pair = kf[:, p * V7X_LANES:(p + 1) * V7X_LANES]
            even = jnp.where(low, pair, onehot)
            odd = jnp.where(low, pltpu.roll(pair, HEAD_DIM, 1), onehot)
            k_ref[0, rows, (2 * p) * V7X_LANES:(2 * p + 1) * V7X_LANES] = even.astype(BF16)
            k_ref[0, rows, (2 * p + 1) * V7X_LANES:(2 * p + 2) * V7X_LANES] = odd.astype(BF16)


def _qkv(x2, g, b, s, w_qt, w_in, k_col, w_vt, tm=2 * TOKEN_TILE):
    d = x2.shape[-1]
    assert k_col % MOBA_WIDTH == 0
    nt = s // tm
    blocks = tm // MOBA_BLOCK
    n_blk = s // MOBA_BLOCK
    assert QKV_SLAB == MOBA_BLOCK and HEAD_DIM + 2 * n_blk <= V7X_LANES
    tok = lambda i, j: (i * nt + j, 0)
    return pl.pallas_call(
        functools.partial(_qkv_kernel, n_blk=n_blk),
        grid=(b, nt),
        in_specs=[pl.BlockSpec((tm, d), tok), _resident((1, d)),
                  _resident(w_qt.shape), _resident_cols(d, MOBA_WIDTH, k_col // MOBA_WIDTH),
                  _resident(w_vt.shape)],
        out_specs=[pl.BlockSpec((tm, d), tok), pl.BlockSpec((tm, d), tok),
                   pl.BlockSpec((blocks, 1, d), lambda i, j: (i * nt + j, 0, 0)),
                   pl.BlockSpec((1, MOBA_WIDTH, tm), lambda i, j: (i, 0, j)),
                   pl.BlockSpec((1, tm, MOBA_HEADS * V7X_LANES), lambda i, j: (i, j, 0)),
                   pl.BlockSpec((1, MOBA_WIDTH, tm), lambda i, j: (i, 0, j))],
        out_shape=[jax.ShapeDtypeStruct((b * s, d), BF16), jax.ShapeDtypeStruct((b * s, d), BF16),
                   jax.ShapeDtypeStruct((b * s // MOBA_BLOCK, 1, d), F32),
                   jax.ShapeDtypeStruct((b, MOBA_WIDTH, s), BF16),
                   jax.ShapeDtypeStruct((b, s, MOBA_HEADS * V7X_LANES), BF16),
                   jax.ShapeDtypeStruct((b, MOBA_WIDTH, s), BF16)],
        compiler_params=_params(2),
        name="qkv",
    )(x2, g, w_qt, w_in, w_vt)


def _bias_kernel(rb_ref, out_ref):
    h = pl.program_id(0)
    span = 2 * MOBA_BLOCK
    u = lax.broadcasted_iota(jnp.int32, (8, span), 1)
    max_exact = REL_BUCKETS // 2
    for t in range(2):
        dist = u if t == 0 else jnp.where(u < MOBA_BLOCK, u + MOBA_BLOCK, u - MOBA_BLOCK)
        n = jnp.maximum(dist, 0)
        nf = jnp.maximum(n, 1).astype(F32)
        large = max_exact + (jnp.log(nf / max_exact) / math.log(REL_MAX_DIST / max_exact)
                             * (REL_BUCKETS - max_exact)).astype(jnp.int32)
        large = jnp.minimum(large, REL_BUCKETS - 1)
        bucket = jnp.where(n < max_exact, n, large)
        val = jnp.zeros((8, span), F32)
        for bk in range(REL_BUCKETS):
            val = jnp.where(bucket == bk, rb_ref[bk * MOBA_HEADS + h], val)
        val = val * LOG2E
        if t == 0:
            val = jnp.where(u < MOBA_BLOCK, val, NEG)
        table = jnp.concatenate([val] * (MOBA_BLOCK // 8), axis=0)
        tile = pltpu.roll(table, 0, 1, stride=1, stride_axis=0)
        out_ref[0, t] = tile[:, 0:MOBA_BLOCK]


def _bias_tiles(rb_flat):
    return pl.pallas_call(
        _bias_kernel,
        grid=(MOBA_HEADS,),
        in_specs=[pl.BlockSpec(memory_space=pltpu.SMEM)],
        out_specs=pl.BlockSpec((1, 2, MOBA_BLOCK, MOBA_BLOCK), lambda h: (h, 0, 0, 0)),
        out_shape=jax.ShapeDtypeStruct((MOBA_HEADS, 2, MOBA_BLOCK, MOBA_BLOCK), F32),
        compiler_params=_params(1),
        name="bias",
    )(rb_flat)


def _zero_after(x):
    bits = lax.bitcast_convert_type(x, jnp.uint32)
    sixteen = jnp.uint32(16)
    return lax.bitcast_convert_type(
        lax.shift_right_logical(lax.shift_right_logical(bits, sixteen), sixteen), F32)


def _attn_kernel(qt_ref, qadd_ref, k_ref, vt_ref, bias_ref, o_ref, s_scr, *, n_blk):
    pad = jnp.zeros((V7X_LANES - HEAD_DIM - 2 * n_blk, MOBA_BLOCK), BF16)

    def scores(hh, j, buf):
        cols = slice(j * MOBA_BLOCK, (j + 1) * MOBA_BLOCK)
        q = qt_ref[0, hh * HEAD_DIM:(hh + 1) * HEAD_DIM, cols]
        qz = jnp.concatenate([q, qadd_ref[0, hh, :, cols], pad], axis=0)
        k = k_ref[0, 0:(j + 1) * MOBA_BLOCK, hh * V7X_LANES:(hh + 1) * V7X_LANES]
        s = jnp.dot(k, qz, preferred_element_type=F32)
        gates = []
        for n in range(j + 1):
            rows = slice(n * MOBA_BLOCK, (n + 1) * MOBA_BLOCK)
            if n == j:
                sn = s[rows] + bias_ref[hh, 0]
            elif n == j - 1:
                sn = s[rows] + bias_ref[hh, 1]
            else:
                sn = s[rows]
            s_scr[buf, rows, :] = sn
            gates.append(_zero_after(sn[0:1, :]))
        return gates

    def finish(hh, j, buf, gates):
        cols = slice(j * MOBA_BLOCK, (j + 1) * MOBA_BLOCK)
        heads = slice(hh * HEAD_DIM, (hh + 1) * HEAD_DIM)
        m = jnp.max(s_scr[buf, 0:(j + 1) * MOBA_BLOCK, :], axis=0, keepdims=True)
        ones = jnp.ones((ONES_ROWS, MOBA_BLOCK), BF16)
        acc = None
        for n in range(j + 1):
            rows = slice(n * MOBA_BLOCK, (n + 1) * MOBA_BLOCK)
            mn = m + gates[min(n, len(gates) - 1)] if gates else m
            p = jnp.exp2(s_scr[buf, rows, :] - mn).astype(BF16)
            part = jnp.dot(jnp.concatenate([vt_ref[0, heads, rows], ones], axis=0), p,
                           preferred_element_type=F32)
            acc = part if acc is None else acc + part
        o_ref[0, heads, cols] = (acc[0:HEAD_DIM] / acc[HEAD_DIM:HEAD_DIM + 1]).astype(BF16)

    tasks = [(hh, j) for j in reversed(range(n_blk)) for hh in range(ATTN_HEADS)]
    pending, issued = [], []
    for i, (hh, j) in enumerate(tasks):
        buf = i % SCORE_BUFFERS
        issued.append(scores(hh, j, buf))
        pending.append((hh, j, buf))
        if len(pending) > SCORE_LOOKAHEAD:
            finish(*pending.pop(0), issued[i - GATE_BACK])
    for item in pending:
        finish(*item, None)


def _attn(qt, qadd, k, vt, bias):
    b, _, s = qt.shape
    n_blk = s // MOBA_BLOCK
    nh = ATTN_HEADS
    return pl.pallas_call(
        functools.partial(_attn_kernel, n_blk=n_blk),
        grid=(MOBA_HEADS // nh, b),
        in_specs=[pl.BlockSpec((1, nh * HEAD_DIM, s), lambda h, i: (i, h, 0)),
                  pl.BlockSpec((1, nh, 2 * n_blk, s), lambda h, i: (i, h, 0, 0)),
                  pl.BlockSpec((1, s, nh * V7X_LANES), lambda h, i: (i, 0, h)),
                  pl.BlockSpec((1, nh * HEAD_DIM, s), lambda h, i: (i, h, 0)),
                  pl.BlockSpec((nh, 2, MOBA_BLOCK, MOBA_BLOCK), lambda h, i: (h, 0, 0, 0))],
        out_specs=pl.BlockSpec((1, nh * HEAD_DIM, s), lambda h, i: (i, h, 0)),
        out_shape=jax.ShapeDtypeStruct((b, MOBA_WIDTH, s), BF16),
        scratch_shapes=[pltpu.VMEM((SCORE_BUFFERS, s, MOBA_BLOCK), F32)],
        compiler_params=_params(2),
        name="attn",
    )(qt, qadd, k, vt, bias)


def _branches_kernel(h_ref, hlo_ref, wu_ref, wv_ref, lng_ref, lnb_ref, ws_ref, bs_ref, wq_ref, kt_ref, vm_ref,
                     gw_ref, far_ref, a_ref, c_ref, qadd_ref, *, n_blk):
    causal = (lax.broadcasted_iota(jnp.int32, (CHUNK, CHUNK), 0)
              >= lax.broadcasted_iota(jnp.int32, (CHUNK, CHUNK), 1))
    ws = [jnp.where(causal, ws_ref[g], 0.0).astype(BF16) for g in range(GMLP_GROUPS)]
    ones = jnp.ones((vm_ref.shape[1], MEM_HEAD_DIM), BF16)
    far = jnp.concatenate([far_ref[...]] * (MOBA_BLOCK // V7X_LANES), axis=-1)
    n_slabs = h_ref.shape[0] // MOBA_BLOCK

    def project(i):
        rows = slice(i * MOBA_BLOCK, (i + 1) * MOBA_BLOCK)
        h = h_ref[rows, :]
        cq = jnp.dot(h, wq_ref[...], preferred_element_type=F32).astype(BF16)
        v = jnp.dot(h, wv_ref[...], preferred_element_type=F32)
        u = jnp.dot(h, wu_ref[...], preferred_element_type=F32)
        res = jnp.dot(jnp.concatenate([h, hlo_ref[rows, :]], axis=0), gw_ref[0], preferred_element_type=F32)
        ps = []
        for hd in range(MEM_HEADS):
            sl = slice(hd * MEM_HEAD_DIM, (hd + 1) * MEM_HEAD_DIM)
            s = jnp.dot(cq[:, sl], kt_ref[0, sl, :], preferred_element_type=F32) * (MEM_HEAD_DIM ** -0.5 * LOG2E)
            ps.append(jnp.exp2(s - jnp.max(s, axis=-1, keepdims=True)).astype(BF16))
        return rows, v, u, ps, res

    def finish(i, rows, v, u, ps, res):
        v = jax.nn.gelu(v)
        vc = v - jnp.mean(v, axis=-1, keepdims=True)
        vn = vc * lax.rsqrt(jnp.mean(vc * vc, axis=-1, keepdims=True) + EPS) * lng_ref[...] + lnb_ref[...]
        vn = vn.astype(BF16)
        u = jax.nn.gelu(u)
        chunks = [slice(c * CHUNK, (c + 1) * CHUNK) for c in range(MOBA_BLOCK // CHUNK)]
        for g in range(GMLP_GROUPS):
            lanes = slice(g * GMLP_GROUP_DIM, (g + 1) * GMLP_GROUP_DIM)
            mixed = jnp.dot(ws[g], jnp.concatenate([vn[toks, lanes] for toks in chunks], axis=1),
                            preferred_element_type=F32)
            for c, toks in enumerate(chunks):
                mc = mixed[:, c * GMLP_GROUP_DIM:(c + 1) * GMLP_GROUP_DIM] + bs_ref[g]
                a_ref[rows.start + c * CHUNK:rows.start + (c + 1) * CHUNK, lanes] = (
                    u[toks, lanes] * mc).astype(BF16)
        for hd, p in enumerate(ps):
            sl = slice(hd * MEM_HEAD_DIM, (hd + 1) * MEM_HEAD_DIM)
            acc = jnp.dot(p, jnp.concatenate([vm_ref[0, :, sl], ones], axis=1), preferred_element_type=F32)
            c_ref[rows, sl] = (acc[:, :MEM_HEAD_DIM] / acc[:, MEM_HEAD_DIM:]).astype(BF16)
        cur = pl.program_id(1) * n_slabs + i
        hi, lo = res[:MOBA_BLOCK], res[MOBA_BLOCK:]
        gate = (hi[:, :GATE_ROWS] + (hi[:, GATE_ROWS:] + lo[:, :GATE_ROWS])) + lo[:, GATE_ROWS:]
        gate = gate.T[0:MOBA_HEADS * n_blk].reshape(MOBA_HEADS, n_blk, MOBA_BLOCK)
        blk = lax.broadcasted_iota(jnp.int32, gate.shape, 1)
        past = blk < cur
        gate = jnp.where(past, gate, NEG)
        rank = jnp.zeros(gate.shape, jnp.int32)
        for m in range(n_blk):
            gm = gate[:, m:m + 1, :]
            tie = jnp.where(blk > m, 1, 0)
            rank = rank + jnp.where(gm > gate, 1, jnp.where(gm == gate, tie, 0))
        mask = jnp.where(rank < min(MOBA_TOPK, n_blk), jnp.where(past, 0.0, NEG), NEG)
        add = jnp.where(blk < cur - 1, mask + far, jnp.where(blk == cur - 1, mask, 0.0))
        add_hi, add_lo = _split_bf16(add)
        qadd_ref[0, :, :, rows] = jnp.concatenate(
            [add_hi.astype(F32), add_lo.astype(F32)], axis=1).astype(BF16)

    staged = project(0)
    for i in range(n_slabs):
        nxt = project(i + 1) if i + 1 < n_slabs else None
        finish(i, *staged)
        staged = nxt


def _branches(h_hi, h_lo, b, s, w_in, u_col, v_col, ln_g, ln_b, w_s, b_s_lanes, w_cq, kt, vm, gt, far_lanes,
              tm=4 * TOKEN_TILE):
    d = h_hi.shape[-1]
    assert u_col % GMLP_WIDTH == 0 and v_col % GMLP_WIDTH == 0
    n_blk = s // MOBA_BLOCK
    nt = s // tm
    m = kt.shape[-1]
    tok = lambda i, j: (i * nt + j, 0)
    return pl.pallas_call(
        functools.partial(_branches_kernel, n_blk=n_blk),
        grid=(b, nt),
        in_specs=[pl.BlockSpec((tm, d), tok), pl.BlockSpec((tm, d), tok),
                  _resident_cols(d, GMLP_WIDTH, u_col // GMLP_WIDTH),
                  _resident_cols(d, GMLP_WIDTH, v_col // GMLP_WIDTH),
                  _resident(ln_g.shape), _resident(ln_b.shape),
                  _resident(w_s.shape), _resident(b_s_lanes.shape), _resident(w_cq.shape),
                  pl.BlockSpec((1, MEM_WIDTH, m), lambda i, j: (i, 0, 0)),
                  pl.BlockSpec((1, m, MEM_WIDTH), lambda i, j: (i, 0, 0)),
                  pl.BlockSpec((1, d, 2 * GATE_ROWS), lambda i, j: (i, 0, 0)),
                  _resident(far_lanes.shape)],
        out_specs=[pl.BlockSpec((tm, GMLP_WIDTH), tok), pl.BlockSpec((tm, MEM_WIDTH), tok),
                   pl.BlockSpec((1, MOBA_HEADS, 2 * n_blk, tm), lambda i, j: (i, 0, 0, j))],
        out_shape=[jax.ShapeDtypeStruct((b * s, GMLP_WIDTH), BF16),
                   jax.ShapeDtypeStruct((b * s, MEM_WIDTH), BF16),
                   jax.ShapeDtypeStruct((b, MOBA_HEADS, 2 * n_blk, s), BF16)],
        compiler_params=_params(2),
        name="branches",
    )(h_hi, h_lo, w_in, w_in, ln_g, ln_b, w_s, b_s_lanes, w_cq, kt, vm, gt, far_lanes)


def _memkv_kernel(mem_ref, g_ref, wkt_ref, wv_ref, kt_ref, v_ref):
    for i in range(mem_ref.shape[0]):
        mn = _rms(mem_ref[i], g_ref[...]).astype(BF16)
        kt_ref[i] = lax.dot_general(wkt_ref[...], mn, (((1,), (1,)), ((), ())),
                                    preferred_element_type=F32).astype(BF16)
        v_ref[i] = jnp.dot(mn, wv_ref[...], preferred_element_type=F32).astype(BF16)


def _memkv(mem, g, w_kt, w_v):
    b, m, d = mem.shape
    nb = math.gcd(b, MEM_BATCHES)
    return pl.pallas_call(
        _memkv_kernel,
        grid=(b // nb,),
        in_specs=[pl.BlockSpec((nb, m, d), lambda i: (i, 0, 0)), _resident((1, d)),
                  _resident(w_kt.shape), _resident(w_v.shape)],
        out_specs=[pl.BlockSpec((nb, MEM_WIDTH, m), lambda i: (i, 0, 0)),
                   pl.BlockSpec((nb, m, MEM_WIDTH), lambda i: (i, 0, 0))],
        out_shape=[jax.ShapeDtypeStruct((b, MEM_WIDTH, m), BF16),
                   jax.ShapeDtypeStruct((b, m, MEM_WIDTH), BF16)],
        compiler_params=_params(1),
        name="memkv",
    )(mem, g, w_kt, w_v)


def _merge_kernel(x_ref, h_ref, a_ref, bt_ref, c_ref, wg_ref, wa_ref, wb_ref, wc_ref, wo_ref,
                  lnpost_ref, lnpre_ref, x1_ref, h2_ref):
    d = x_ref.shape[-1]
    for i in range(x_ref.shape[0] // MLP_SLAB):
        rows = slice(i * MLP_SLAB, (i + 1) * MLP_SLAB)
        gates = jax.nn.sigmoid(jnp.dot(h_ref[rows, :], wg_ref[...], preferred_element_type=F32))
        pa = jnp.dot(a_ref[rows, :], wa_ref[...], preferred_element_type=F32)
        pb = lax.dot_general(bt_ref[0, :, rows], wb_ref[...], (((0,), (0,)), ((), ())),
                             preferred_element_type=F32)
        pc = jnp.dot(c_ref[rows, :], wc_ref[...], preferred_element_type=F32)
        merged = gates[:, :d] * pa + gates[:, d:2 * d] * pb + gates[:, 2 * d:] * pc
        mo = jnp.dot(merged.astype(BF16), wo_ref[...], preferred_element_type=F32)
        x1 = x_ref[rows, :] + _rms(mo, lnpost_ref[...])
        x1_ref[rows, :] = x1
        h2_ref[rows, :] = _rms(x1, lnpre_ref[...]).astype(BF16)


def _merge(x2, h2, a_out, b_out_t, c_out, w_g, w_a, w_b, w_c, w_o, ln_post, ln_pre, tm=MLP_TILE):
    n_tok, d = x2.shape
    b, _, s = b_out_t.shape
    nt = s // tm
    tok = lambda i, j: (i * nt + j, 0)
    return pl.pallas_call(
        _merge_kernel,
        grid=(b, nt),
        in_specs=[pl.BlockSpec((tm, d), tok), pl.BlockSpec((tm, d), tok),
                  pl.BlockSpec((tm, GMLP_WIDTH), tok),
                  pl.BlockSpec((1, MOBA_WIDTH, tm), lambda i, j: (i, 0, j)),
                  pl.BlockSpec((tm, MEM_WIDTH), tok),
                  _resident(w_g.shape), _resident(w_a.shape), _resident(w_b.shape),
                  _resident(w_c.shape), _resident(w_o.shape),
                  _resident((1, d)), _resident((1, d))],
        out_specs=[pl.BlockSpec((tm, d), tok), pl.BlockSpec((tm, d), tok)],
        out_shape=[jax.ShapeDtypeStruct((n_tok, d), F32), jax.ShapeDtypeStruct((n_tok, d), BF16)],
        compiler_params=_params(2),
        name="merge",
    )(x2, h2, a_out, b_out_t, c_out, w_g, w_a, w_b, w_c, w_o, ln_post, ln_pre)


def _ffn_kernel(x1_ref, h2_ref, wg_ref, wu_ref, wd_ref, ln_ref, o_ref):
    for i in range(x1_ref.shape[0] // MLP_SLAB):
        rows = slice(i * MLP_SLAB, (i + 1) * MLP_SLAB)
        h2 = h2_ref[rows, :]
        d_ff = wg_ref.shape[1]
        cut = (d_ff // 2 + FF_ALIGN - 1) // FF_ALIGN * FF_ALIGN
        f = None
        for cols in (slice(0, cut), slice(cut, d_ff)):
            g = jnp.dot(h2, wg_ref[:, cols], preferred_element_type=F32)
            u = jnp.dot(h2, wu_ref[:, cols], preferred_element_type=F32)
            act = (jax.nn.silu(g) * u).astype(BF16)
            part = jnp.dot(act, wd_ref[cols, :], preferred_element_type=F32)
            f = part if f is None else f + part
        o_ref[rows, :] = x1_ref[rows, :] + _rms(f, ln_ref[...])


def _ffn(x1, h2, w_g, w_u, w_d, ln_post, tm=MLP_TILE):
    n_tok, d = x1.shape
    return pl.pallas_call(
        _ffn_kernel,
        grid=(n_tok // tm,),
        in_specs=[pl.BlockSpec((tm, d), lambda i: (i, 0)), pl.BlockSpec((tm, d), lambda i: (i, 0)),
                  _resident(w_g.shape), _resident(w_u.shape), _resident(w_d.shape),
                  _resident((1, d))],
        out_specs=pl.BlockSpec((tm, d), lambda i: (i, 0)),
        out_shape=jax.ShapeDtypeStruct((n_tok, d), F32),
        compiler_params=_params(1),
        name="ffn",
    )(x1, h2, w_g, w_u, w_d, ln_post)


def kernel(x, mem, ln_mix_pre, ln_mix_post, ln_ffn_pre, ln_ffn_post, ln_mem, w_in, ln_v_gain, ln_v_bias,
           w_spatial, b_spatial, rel_bias, w_mem_kv, w_branch_a, w_branch_b, w_branch_c, w_out,
           w_ffn_gate, w_ffn_up, w_ffn_down):
    b, s, d = x.shape
    assert s % MOBA_BLOCK == 0 and s % TOKEN_TILE == 0 and d % V7X_LANES == 0
    depth = w_in.shape[0]
    cuts = [0, GMLP_WIDTH, 2 * GMLP_WIDTH, 2 * GMLP_WIDTH + MOBA_WIDTH, 2 * GMLP_WIDTH + 2 * MOBA_WIDTH,
            2 * GMLP_WIDTH + 3 * MOBA_WIDTH, 2 * GMLP_WIDTH + 3 * MOBA_WIDTH + MEM_WIDTH]
    rb_flat = rel_bias.astype(F32).reshape(-1)
    bias = _bias_tiles(rb_flat)
    far_lanes = jnp.broadcast_to((rel_bias[REL_BUCKETS - 1].astype(F32) * LOG2E)[:, None, None],
                                 (MOBA_HEADS, s // MOBA_BLOCK, V7X_LANES))
    row = lambda v: v.reshape(1, -1).astype(F32)
    for l in range(depth):
        wi = w_in[l]
        wi16 = wi.astype(BF16)
        w_q, w_k, w_v2 = (wi[:, cuts[i]:cuts[i + 1]] for i in (2, 3, 4))
        w_cq, w_g = wi16[:, cuts[5]:cuts[6]], wi16[:, cuts[6]:]
        x2 = x.reshape(b * s, d)

        w_qt_scaled = (w_q * (HEAD_DIM ** -0.5)).T
        h2, h_lo, hmean, qt, k, vt = _qkv(x2, row(ln_mix_pre[l]), b, s, (w_qt_scaled * LOG2E).astype(BF16),
                                          wi16, cuts[3], w_v2.T.astype(BF16))
        gt = _gatew(hmean.reshape(-1, d), w_k, w_qt_scaled, b)
        wkv = w_mem_kv[l]
        kt_mem, v_mem = _memkv(mem, row(ln_mem[l]), wkv[:, :MEM_WIDTH].T.astype(BF16),
                               wkv[:, MEM_WIDTH:].astype(BF16))
        b_s_lanes = jnp.broadcast_to(b_spatial[l][:, :, None], (GMLP_GROUPS, CHUNK, GMLP_GROUP_DIM))
        a_out, c_out, qadd = _branches(
            h2, h_lo, b, s, wi16, cuts[0], cuts[1], row(ln_v_gain[l]), row(ln_v_bias[l]),
            w_spatial[l], b_s_lanes.astype(F32), w_cq, kt_mem, v_mem, gt, far_lanes)
        b_out_t = _attn(qt, qadd, k, vt, bias)

        x1, hn = _merge(x2, h2, a_out, b_out_t, c_out, w_g, w_branch_a[l].astype(BF16),
                        w_branch_b[l].astype(BF16), w_branch_c[l].astype(BF16), w_out[l].astype(BF16),
                        row(ln_mix_post[l]), row(ln_ffn_pre[l]))
        out = _ffn(x1, hn, w_ffn_gate[l].astype(BF16), w_ffn_up[l].astype(BF16),
                   w_ffn_down[l].astype(BF16), row(ln_ffn_post[l]))
        x = out.reshape(b, s, d)
    return x
```
